```python
import math
import jax
import jax.numpy as jnp
from jax import lax
import numpy as np

D_MODEL = 1024
BATCH = 8
SEQ = 4096
DEPTH = 4

CTX_LEN = 256
GRID_W = 64
N_MIXERS = 3
N_MOD = 9
D_FF = 2816
NORM_EPS = 1e-6

S5_GROUP = 16
S5_GROUPS = D_MODEL // S5_GROUP
S5_STATE = 64
S5_BLOCK = 128
S5_DT_MIN = 1e-3
S5_DT_MAX = 1e-1

GLA_HEADS = 4
GLA_DK = D_MODEL // 2 // GLA_HEADS
GLA_DV = D_MODEL // GLA_HEADS
GLA_RANK = 16
GLA_TAU = 16.0

HGRN_DK = 128
HGRN_HEADS = D_MODEL // HGRN_DK
HGRN_DV = D_MODEL // HGRN_HEADS

LIN_CHUNK = 64

N_S5 = len(range(0, DEPTH, N_MIXERS))
N_GLA = len(range(1, DEPTH, N_MIXERS))
N_HGRN = len(range(2, DEPTH, N_MIXERS))

F32 = jnp.float32

kernel_name = 'hybrid_s5_gla_hgrn2_prefix_trunk'


def rmsnorm(x, w):
    xf = x.astype(F32)
    y = xf * lax.rsqrt(jnp.mean(xf * xf, axis=-1, keepdims=True) + NORM_EPS)
    return (y * w.astype(F32)).astype(x.dtype)


def swiglu_ffn(h, w_in, w_out):
    gu = jnp.einsum('bld,df->blf', h, w_in)
    g, u = jnp.split(gu, 2, axis=-1)
    return jnp.einsum('blf,fd->bld', jax.nn.silu(g) * u, w_out)


def ffn_half_step(x, w_norm, shift, scale, gate, w_in, w_out):
    h = rmsnorm(x, w_norm) * (1.0 + scale) + shift
    return x + 0.5 * gate * swiglu_ffn(h, w_in, w_out)


def grid_to_col_major(t, rows):
    b, l, e = t.shape
    return t.reshape(b, rows, GRID_W, e).transpose(0, 2, 1, 3).reshape(b, l, e)


def col_major_to_grid(t, rows):
    b, l, e = t.shape
    return t.reshape(b, GRID_W, rows, e).transpose(0, 2, 1, 3).reshape(b, l, e)


def s5_discretize(a_re, a_im, log_step, b_re, b_im):
    dt = jnp.exp(log_step.astype(F32))[..., None]
    a_re = a_re.astype(F32)
    a_im = a_im.astype(F32)
    mag = jnp.exp(a_re * dt)
    abar_re = mag * jnp.cos(a_im * dt)
    abar_im = mag * jnp.sin(a_im * dt)
    den = a_re * a_re + a_im * a_im
    zr = abar_re - 1.0
    zi = abar_im
    coef_re = ((zr * a_re + zi * a_im) / den)[..., None]
    coef_im = ((zi * a_re - zr * a_im) / den)[..., None]
    b_re = b_re.astype(F32)
    b_im = b_im.astype(F32)
    bbar_re = coef_re * b_re - coef_im * b_im
    bbar_im = coef_re * b_im + coef_im * b_re
    return abar_re, abar_im, bbar_re, bbar_im


def s5_combine(e1, e2):
    ar1, ai1, br1, bi1 = e1
    ar2, ai2, br2, bi2 = e2
    return (ar1 * ar2 - ai1 * ai2, ar1 * ai2 + ai1 * ar2,
            ar2 * br1 - ai2 * bi1 + br2, ar2 * bi1 + ai2 * br1 + bi2)


def s5_direction(u, abar_re, abar_im, bbar_re, bbar_im, c_re, c_im, h0_re, h0_im):
    bsz, length = u.shape[0], u.shape[1]
    n_blk = length // S5_BLOCK
    ub = jnp.swapaxes(u.reshape(bsz, n_blk, S5_BLOCK, S5_GROUPS, S5_GROUP), 0, 1)
    a_re = jnp.broadcast_to(abar_re, (bsz, S5_BLOCK, S5_GROUPS, S5_STATE))
    a_im = jnp.broadcast_to(abar_im, (bsz, S5_BLOCK, S5_GROUPS, S5_STATE))

    def step(carry, ublk):
        h_re, h_im = carry
        bu_re = jnp.einsum('blgc,gpc->blgp', ublk, bbar_re)
        bu_im = jnp.einsum('blgc,gpc->blgp', ublk, bbar_im)
        ca_re, ca_im, cb_re, cb_im = lax.associative_scan(s5_combine, (a_re, a_im, bu_re, bu_im), axis=1)
        hs_re = ca_re * h_re[:, None] - ca_im * h_im[:, None] + cb_re
        hs_im = ca_re * h_im[:, None] + ca_im * h_re[:, None] + cb_im
        y = jnp.einsum('blgp,gcp->blgc', hs_re, c_re) - jnp.einsum('blgp,gcp->blgc', hs_im, c_im)
        return (hs_re[:, -1], hs_im[:, -1]), y

    (h_re, h_im), y = lax.scan(step, (h0_re, h0_im), ub)
    y = jnp.swapaxes(y, 0, 1).reshape(bsz, length, S5_GROUPS, S5_GROUP)
    return y, h_re, h_im


def s5_mixer(hc, hx, a_re, a_im, log_step, b_re, b_im, c_re, c_im, d_skip, w_glu, b_glu, need_ctx):
    abar_re, abar_im, bbar_re, bbar_im = s5_discretize(a_re, a_im, log_step, b_re, b_im)
    c_re = c_re.astype(F32)
    c_im = c_im.astype(F32)

    def groups(h):
        return h.astype(F32).reshape(h.shape[0], h.shape[1], S5_GROUPS, S5_GROUP)

    uc, ux = groups(hc), groups(hx)
    h0 = jnp.zeros((hc.shape[0], S5_GROUPS, S5_STATE), F32)
    fwd = (abar_re[0], abar_im[0], bbar_re[0], bbar_im[0], c_re[0], c_im[0])
    bwd = (abar_re[1], abar_im[1], bbar_re[1], bbar_im[1], c_re[1], c_im[1])
    yc_f, hf_re, hf_im = s5_direction(uc, *fwd, h0, h0)
    yc_b, hb_re, hb_im = s5_direction(jnp.flip(uc, 1), *bwd, h0, h0)
    yx_f, _, _ = s5_direction(ux, *fwd, hf_re, hf_im)
    yx_b, _, _ = s5_direction(jnp.flip(ux, 1), *bwd, hb_re, hb_im)

    def readout(h, y_f, y_b_rev):
        bsz, length, _ = h.shape
        y = (y_f + jnp.flip(y_b_rev, 1)).reshape(bsz, length, D_MODEL) + d_skip.astype(F32) * h.astype(F32)
        z = jnp.einsum('bld,de->ble', jax.nn.gelu(y), w_glu.astype(F32)) + b_glu.astype(F32)
        val, gate = jnp.split(z, 2, axis=-1)
        return (val * jax.nn.sigmoid(gate)).astype(h.dtype)

    yx = readout(hx, yx_f, yx_b)
    yc = readout(hc, yc_f, yc_b) if need_ctx else None
    return yc, yx


def chunk_gated_scan(q, k, v, log_g, s0):
    bsz, length, n_heads, _ = q.shape
    dv = v.shape[-1]
    n_chunks = length // LIN_CHUNK

    def blocks(t):
        return jnp.moveaxis(t.reshape(bsz, n_chunks, LIN_CHUNK, *t.shape[2:]), 1, 0)

    lower = jnp.tril(jnp.ones((LIN_CHUNK, LIN_CHUNK), dtype=bool))[None, :, :, None, None]

    def step(state, blk):
        qb, kb, vb, gb = blk
        cum = jnp.cumsum(gb, axis=1)
        rel = jnp.exp(jnp.where(lower, cum[:, :, None] - cum[:, None, :], -jnp.inf))
        scores = jnp.einsum('bihk,bjhk,bijhk->bhij', qb, kb, rel)
        out = (jnp.einsum('bhij,bjhv->bihv', scores, vb)
               + jnp.einsum('bihk,bhkv->bihv', qb * jnp.exp(cum), state))
        last = cum[:, -1]
        k_dec = kb * jnp.exp(last[:, None] - cum)
        state = jnp.exp(last)[..., None] * state + jnp.einsum('bjhk,bjhv->bhkv', k_dec, vb)
        return state, out

    s_final, out = lax.scan(step, s0, (blocks(q), blocks(k), blocks(v), blocks(log_g)))
    out = jnp.moveaxis(out, 0, 1).reshape(bsz, length, n_heads, dv)
    return out, s_final


def bidir_gated(ctx_in, lat_in, need_ctx):
    qc, kfc, kbc, vc, gfc, gbc = ctx_in
    qx, kfx, kbx, vx, gfx, gbx = lat_in

    def flip(t):
        return jnp.flip(t, axis=1)

    s0 = jnp.zeros((qc.shape[0], qc.shape[2], qc.shape[3], vc.shape[3]), F32)
    oc_f, s_f = chunk_gated_scan(qc, kfc, vc, gfc, s0)
    oc_b, s_b = chunk_gated_scan(flip(qc), flip(kbc), flip(vc), flip(gbc), s0)
    ox_f, _ = chunk_gated_scan(qx, kfx, vx, gfx, s_f)
    ox_b, _ = chunk_gated_scan(flip(qx), flip(kbx), flip(vx), flip(gbx), s_b)
    ox = ox_f + flip(ox_b)
    oc = oc_f + flip(oc_b) if need_ctx else None
    return oc, ox


def gated_head_out(o, gate, norm_w, w_out, dtype):
    bsz, length = o.shape[0], o.shape[1]
    o = rmsnorm(o, norm_w).reshape(bsz, length, -1) * jax.nn.silu(gate)
    return jnp.einsum('ble,ed->bld', o, w_out.astype(F32)).astype(dtype)


def gla_mixer(hc, hx, rows, w_in, w_gate, b_gate, norm_w, w_out, need_ctx):
    dkt, dvt = GLA_HEADS * GLA_DK, GLA_HEADS * GLA_DV

    def project(h):
        bsz, length, _ = h.shape
        p = jnp.einsum('bld,de->ble', h, w_in).astype(F32)
        q, k, v, r, low = jnp.split(p, [dkt, 2 * dkt, 2 * dkt + dvt, 2 * dkt + 2 * dvt], axis=-1)
        low = low.reshape(bsz, length, 2, GLA_RANK)
        log_a = jax.nn.log_sigmoid(jnp.einsum('blzr,zrk->blzk', low, w_gate.astype(F32)) + b_gate.astype(F32)) / GLA_TAU
        q = q.reshape(bsz, length, GLA_HEADS, GLA_DK) * (GLA_DK ** -0.5)
        k = k.reshape(bsz, length, GLA_HEADS, GLA_DK)
        v = v.reshape(bsz, length, GLA_HEADS, GLA_DV)
        g_f = log_a[:, :, 0].reshape(bsz, length, GLA_HEADS, GLA_DK)
        g_b = log_a[:, :, 1].reshape(bsz, length, GLA_HEADS, GLA_DK)
        return (q, k, k, v, g_f, g_b), r

    ctx_in, r_c = project(hc)
    lat_in, r_x = project(grid_to_col_major(hx, rows))
    oc, ox = bidir_gated(ctx_in, lat_in, need_ctx)
    yx = col_major_to_grid(gated_head_out(ox, r_x, norm_w, w_out, hx.dtype), rows)
    yc = gated_head_out(oc, r_c, norm_w, w_out, hc.dtype) if need_ctx else None
    return yc, yx


def hgrn_mixer(hc, hx, w_in, lower_bound, norm_w, w_out, need_ctx):
    lb = lower_bound.astype(F32).reshape(2, HGRN_HEADS, HGRN_DK)

    def project(h):
        bsz, length, _ = h.shape
        p = jnp.einsum('bld,de->ble', h, w_in).astype(F32)
        q, i, z_f, z_b, g = jnp.split(p, 5, axis=-1)

        def heads(t):
            return t.reshape(bsz, length, HGRN_HEADS, HGRN_DK)

        z_f, z_b = heads(z_f), heads(z_b)
        log_f_f = jnp.log(lb[0] + (1.0 - lb[0]) * jax.nn.sigmoid(z_f))
        log_f_b = jnp.log(lb[1] + (1.0 - lb[1]) * jax.nn.sigmoid(z_b))
        k_f = (1.0 - lb[0]) * jax.nn.sigmoid(-z_f)
        k_b = (1.0 - lb[1]) * jax.nn.sigmoid(-z_b)
        v = i.reshape(bsz, length, HGRN_HEADS, HGRN_DV)
        return (heads(jax.nn.silu(q)), k_f, k_b, v, log_f_f, log_f_b), g

    ctx_in, g_c = project(hc)
    lat_in, g_x = project(hx)
    oc, ox = bidir_gated(ctx_in, lat_in, need_ctx)
    yx = gated_head_out(ox, g_x, norm_w, w_out, hx.dtype)
    yc = gated_head_out(oc, g_c, norm_w, w_out, hc.dtype) if need_ctx else None
    return yc, yx


def setup_inputs(seed: int = 0) -> dict:
    key = jax.random.key(seed)
    ks = iter(jax.random.split(key, 32))

    def normal(shape, scale):
        return jax.random.normal(next(ks), shape, jnp.float32) * scale

    d = D_MODEL
    dkt, dvt = GLA_HEADS * GLA_DK, GLA_HEADS * GLA_DV
    n_idx = jnp.arange(S5_STATE, dtype=jnp.float32)
    s5_a = (N_S5, 2, S5_GROUPS, S5_STATE)
    s5_b = (N_S5, 2, S5_GROUPS, S5_STATE, S5_GROUP)
    s5_c = (N_S5, 2, S5_GROUPS, S5_GROUP, S5_STATE)
    return {
        'x': normal((BATCH, SEQ, d), 1.0),
        'c': normal((BATCH, d), 1.0),
        'ctx': normal((BATCH, CTX_LEN, d), 1.0),
        'c_ctx': normal((d,), 1.0),
        'ada_w': normal((DEPTH, d, N_MOD * d), 0.5 * d ** -0.5),
        'ada_b': normal((DEPTH, N_MOD * d), 0.02),
        'norm_w': 1.0 + normal((DEPTH, 3, d), 0.02),
        'ffn_w_in': normal((DEPTH, 2, d, 2 * D_FF), d ** -0.5),
        'ffn_w_out': normal((DEPTH, 2, D_FF, d), D_FF ** -0.5),
        's5_a_re': -0.5 + normal(s5_a, 0.01),
        's5_a_im': math.pi * n_idx + normal(s5_a, 0.01),
        's5_log_step': jax.random.uniform(next(ks), (N_S5, 2, S5_GROUPS), jnp.float32,
                                          math.log(S5_DT_MIN), math.log(S5_DT_MAX)),
        's5_b_re': normal(s5_b, (2 * S5_GROUP) ** -0.5),
        's5_b_im': normal(s5_b, (2 * S5_GROUP) ** -0.5),
        's5_c_re': normal(s5_c, S5_STATE ** -0.5),
        's5_c_im': normal(s5_c, S5_STATE ** -0.5),
        's5_d': normal((N_S5, d), 0.5),
        's5_w_glu': normal((N_S5, d, 2 * d), d ** -0.5),
        's5_b_glu': normal((N_S5, 2 * d), 0.02),
        'gla_w_in': normal((N_GLA, d, 2 * dkt + 2 * dvt + 2 * GLA_RANK), d ** -0.5),
        'gla_w_gate': normal((N_GLA, 2, GLA_RANK, dkt), GLA_RANK ** -0.5),
        'gla_b_gate': normal((N_GLA, 2, dkt), 0.02),
        'gla_norm_w': 1.0 + normal((N_GLA, GLA_DV), 0.02),
        'gla_w_out': normal((N_GLA, dvt, d), dvt ** -0.5),
        'hgrn_w_in': normal((N_HGRN, d, 5 * d), d ** -0.5),
        'hgrn_lb_logits': normal((DEPTH, 2, d), 0.1),
        'hgrn_norm_w': 1.0 + normal((N_HGRN, HGRN_DV), 0.02),
        'hgrn_w_out': normal((N_HGRN, d, d), d ** -0.5),
        'final_norm_w': 1.0 + normal((d,), 0.02),
    }


def reference(x, c, ctx, c_ctx, ada_w, ada_b, norm_w, ffn_w_in, ffn_w_out,
              s5_a_re, s5_a_im, s5_log_step, s5_b_re, s5_b_im, s5_c_re, s5_c_im, s5_d,
              s5_w_glu, s5_b_glu, gla_w_in, gla_w_gate, gla_b_gate, gla_norm_w, gla_w_out,
              hgrn_w_in, hgrn_lb_logits, hgrn_norm_w, hgrn_w_out, final_norm_w):
    rows = x.shape[1] // GRID_W
    lb_soft = jax.nn.softmax(hgrn_lb_logits.astype(F32), axis=0)
    lb_all = jnp.cumsum(lb_soft, axis=0) - lb_soft[0]
    xc = ctx
    for i in range(DEPTH):
        last = i == DEPTH - 1
        mod_x = (jnp.einsum('bd,de->be', jax.nn.silu(c), ada_w[i]) + ada_b[i])[:, None, :]
        mod_c = (jnp.einsum('d,de->e', jax.nn.silu(c_ctx), ada_w[i]) + ada_b[i])[None, None, :]
        sx = jnp.split(mod_x, N_MOD, axis=-1)
        sc = jnp.split(mod_c, N_MOD, axis=-1)

        x = ffn_half_step(x, norm_w[i, 0], sx[0], sx[1], sx[2], ffn_w_in[i, 0], ffn_w_out[i, 0])
        xc = ffn_half_step(xc, norm_w[i, 0], sc[0], sc[1], sc[2], ffn_w_in[i, 0], ffn_w_out[i, 0])

        hx = rmsnorm(x, norm_w[i, 1]) * (1.0 + sx[4]) + sx[3]
        hc = rmsnorm(xc, norm_w[i, 1]) * (1.0 + sc[4]) + sc[3]
        kind, j = i % N_MIXERS, i // N_MIXERS
        if kind == 0:
            yc, yx = s5_mixer(hc, hx, s5_a_re[j], s5_a_im[j], s5_log_step[j], s5_b_re[j], s5_b_im[j],
                              s5_c_re[j], s5_c_im[j], s5_d[j], s5_w_glu[j], s5_b_glu[j], not last)
        elif kind == 1:
            yc, yx = gla_mixer(hc, hx, rows, gla_w_in[j], gla_w_gate[j], gla_b_gate[j],
                               gla_norm_w[j], gla_w_out[j], not last)
        else:
            yc, yx = hgrn_mixer(hc, hx, hgrn_w_in[j], lb_all[i], hgrn_norm_w[j], hgrn_w_out[j], not last)
        x = x + sx[5] * yx

        x = ffn_half_step(x, norm_w[i, 2], sx[6], sx[7], sx[8], ffn_w_in[i, 1], ffn_w_out[i, 1])
        if not last:
            xc = xc + sc[5] * yc
            xc = ffn_half_step(xc, norm_w[i, 2], sc[6], sc[7], sc[8], ffn_w_in[i, 1], ffn_w_out[i, 1])
    return rmsnorm(x, final_norm_w)
```

```python
import functools
import math

import numpy as np
import jax
import jax.numpy as jnp
from jax import lax
from jax.experimental import pallas as pl
from jax.experimental.pallas import tpu as pltpu

F32 = jnp.float32
BF16 = jnp.bfloat16

NORM_EPS = 1e-6
GRID_W = 64
N_MOD = 9
N_MIXERS = 3

S5_GROUP = 16
S5_STATE = 64
S5_LANE_GROUPS = 8
S5_TB = 64

GLA_HEADS = 4
GLA_RANK = 16
GLA_TAU = 16.0
HGRN_DK = 128

SUBLANES = 8
LANES = 128
CHUNK = 128
SUB = CHUNK // SUBLANES
N_FINE = 4
N_LEVELS = 7

VMEM_LIMIT_BYTES = 56 * 1024 * 1024


def _cparams(*sem):
    return pltpu.CompilerParams(dimension_semantics=sem, vmem_limit_bytes=VMEM_LIMIT_BYTES)


def _dot(a, b):
    return jnp.dot(a, b, preferred_element_type=F32)


def _dot_nt(a, b):
    return lax.dot_general(a, b, (((1,), (1,)), ((), ())), preferred_element_type=F32)


def _dot_tn(a, b):
    return lax.dot_general(a, b, (((0,), (0,)), ((), ())), preferred_element_type=F32)


def _sigmoid(x):
    return 1.0 / (1.0 + jnp.exp(-x))


def _silu(x):
    return x * _sigmoid(x)


def _gelu_tanh(x):
    c = math.sqrt(2.0 / math.pi)
    return 0.5 * x * (1.0 + jnp.tanh(c * (x + 0.044715 * (x * x * x))))


def _log_sigmoid(x):
    return jnp.minimum(x, 0.0) - jnp.log1p(jnp.exp(-jnp.abs(x)))


def _rms(x, w):
    ms = jnp.mean(x * x, axis=-1, keepdims=True)
    return (x * lax.rsqrt(ms + NORM_EPS)) * w


def _norm_mod(x, w, shift, scale):
    return _rms(x, w) * (1.0 + scale) + shift


def _ada_kernel(cc_ref, w_ref, b_ref, o_ref):
    a = _silu(cc_ref[...]).astype(BF16)
    o_ref[...] = _dot(a, w_ref[...].astype(BF16)) + b_ref[...]


def _ada(c, c_ctx, ada_w, ada_b):
    depth, d, nd = ada_w.shape
    bsz = c.shape[0]
    rows = ((bsz + 1 + SUBLANES - 1) // SUBLANES) * SUBLANES
    cc = jnp.zeros((rows, d), F32).at[:bsz].set(c).at[bsz].set(c_ctx)
    nblk = nd // d
    mod = pl.pallas_call(
        _ada_kernel,
        grid=(depth, nblk),
        in_specs=[
            pl.BlockSpec((rows, d), lambda i, n: (0, 0)),
            pl.BlockSpec((None, d, d), lambda i, n: (i, 0, n)),
            pl.BlockSpec((None, 1, d), lambda i, n: (i, 0, n)),
        ],
        out_specs=pl.BlockSpec((None, rows, d), lambda i, n: (i, 0, n)),
        out_shape=jax.ShapeDtypeStruct((depth, rows, nd), F32),
        compiler_params=_cparams("arbitrary", "arbitrary"),
        name="ada_mod",
    )(cc, ada_w, ada_b.reshape(depth, 1, nd))
    mx = mod[:, :bsz].reshape(depth, bsz, 3, 3, d)
    mc = jnp.broadcast_to(mod[:, bsz:bsz + 1], (depth, bsz, nd)).reshape(depth, bsz, 3, 3, d)
    return (mx, jnp.transpose(mx, (0, 2, 3, 1, 4))), (mc, jnp.transpose(mc, (0, 2, 3, 1, 4)))


def _ffn_kernel(nff, final, x_ref, mod_ref, nw_ref, wg_ref, wu_ref, wo_ref, *rest):
    if final:
        fw_ref, o_ref, h_ref, acc_ref = rest
    else:
        o_ref, h_ref, acc_ref = rest
    c = pl.program_id(2)

    @pl.when(c == 0)
    def _():
        h_ref[...] = _norm_mod(x_ref[...], nw_ref[...], mod_ref[0:1, :], mod_ref[1:2, :]).astype(BF16)

    h = h_ref[...]
    g = _dot(h, wg_ref[...])
    u = _dot(h, wu_ref[...])
    part = _dot((_silu(g) * u).astype(BF16), wo_ref[...])

    @pl.when(c == 0)
    def _():
        acc_ref[...] = part

    @pl.when(c > 0)
    def _():
        acc_ref[...] += part

    @pl.when(c == nff - 1)
    def _():
        y = x_ref[...] + (0.5 * mod_ref[2:3, :]) * acc_ref[...]
        if final:
            y = _rms(y, fw_ref[...])
        o_ref[...] = y


def _ffn(x, mod_a, norm_w4, w_in, w_out, layer, stage, bsz, *, in_bld=False, final_w=None):
    d = norm_w4.shape[-1]
    seq = x.shape[1] if in_bld else x.shape[0]
    ff = w_out.shape[2]
    nff = 2
    fc = ff // nff
    tm = 512 if seq % 512 == 0 else 256
    s = 0 if stage == 0 else 1
    final = final_w is not None
    if in_bld:
        x_spec = pl.BlockSpec((None, tm, d), lambda t, b, c: (b, t, 0))
    else:
        x_spec = pl.BlockSpec((tm, d), lambda t, b, c: (t, b))
    in_specs = [
        x_spec,
        pl.BlockSpec((None, None, None, 3, d), lambda t, b, c: (layer, b, stage, 0, 0)),
        pl.BlockSpec((None, None, 1, d), lambda t, b, c: (layer, stage, 0, 0)),
        pl.BlockSpec((None, None, d, fc), lambda t, b, c: (layer, s, 0, c)),
        pl.BlockSpec((None, None, d, fc), lambda t, b, c: (layer, s, 0, c + nff)),
        pl.BlockSpec((None, None, fc, d), lambda t, b, c: (layer, s, c, 0)),
    ]
    args = [x, mod_a, norm_w4, w_in, w_in, w_out]
    if final:
        in_specs.append(pl.BlockSpec((1, d), lambda t, b, c: (0, 0)))
        args.append(final_w)
        out_spec = pl.BlockSpec((None, tm, d), lambda t, b, c: (b, t, 0))
        out_shape = jax.ShapeDtypeStruct((bsz, seq, d), F32)
    else:
        out_spec = pl.BlockSpec((tm, d), lambda t, b, c: (t, b))
        out_shape = jax.ShapeDtypeStruct((seq, bsz * d), F32)
    return pl.pallas_call(
        functools.partial(_ffn_kernel, nff, final),
        grid=(seq // tm, bsz, nff),
        in_specs=in_specs,
        out_specs=out_spec,
        out_shape=out_shape,
        scratch_shapes=[pltpu.VMEM((tm, d), BF16), pltpu.VMEM((tm, d), F32)],
        compiler_params=_cparams("parallel", "parallel", "arbitrary"),
        name="ffn_half",
    )(*args)


def _s5_prep(a_re, a_im, log_step, b_re, b_im, c_re, c_im, bsz):
    dt = jnp.exp(log_step.astype(F32))[..., None]
    a_re = a_re.astype(F32)
    a_im = a_im.astype(F32)
    mag = jnp.exp(a_re * dt)
    abar_re = mag * jnp.cos(a_im * dt)
    abar_im = mag * jnp.sin(a_im * dt)
    den = a_re * a_re + a_im * a_im
    zr = abar_re - 1.0
    zi = abar_im
    coef_re = ((zr * a_re + zi * a_im) / den)[..., None]
    coef_im = ((zi * a_re - zr * a_im) / den)[..., None]
    b_re = b_re.astype(F32)
    b_im = b_im.astype(F32)
    bbar_re = coef_re * b_re - coef_im * b_im
    bbar_im = coef_re * b_im + coef_im * b_re
    n_dir, g, p = abar_re.shape
    gc = b_re.shape[-1]
    nj = g // S5_LANE_GROUPS
    lg = S5_LANE_GROUPS
    are = jnp.broadcast_to(abar_re.reshape(n_dir, 1, g * p), (n_dir, bsz, g * p))
    aim = jnp.broadcast_to(abar_im.reshape(n_dir, 1, g * p), (n_dir, bsz, g * p))
    eye = jnp.eye(lg, dtype=F32)
    bb = jnp.stack([bbar_re, bbar_im], axis=1).reshape(n_dir, 2, nj, lg, p, gc)
    bb = jnp.transpose(bb, (0, 2, 3, 5, 1, 4))
    wb = bb[:, :, :, :, :, None, :] * eye[None, None, :, None, None, :, None]
    wb = wb.reshape(n_dir, nj, lg * gc, 2 * lg * p).astype(BF16)
    cc = jnp.stack([c_re.astype(F32), -c_im.astype(F32)], axis=1).reshape(n_dir, 2, nj, lg, gc, p)
    cc = jnp.transpose(cc, (0, 2, 1, 3, 5, 4))
    wc = cc[:, :, :, :, :, None, :] * eye[None, None, None, :, None, :, None]
    wc = wc.reshape(n_dir, nj, 2 * lg * p, lg * gc).astype(BF16)
    return are, aim, wb, wc


def _s5_kernel(tb, rev, finish, x_ref, mod_ref, nw_ref, are_ref, aim_ref, wb_ref, wc_ref,
               h0re_ref, h0im_ref, *rest):
    if finish:
        (yf_ref, dsk_ref, wglu_ref, bglu_ref, o_ref, hre_ref, him_ref,
         sre, sim, hs_re, hs_im, y_s, h_s) = rest
    else:
        o_ref, hre_ref, him_ref, sre, sim, hs_re, hs_im, y_s = rest
    bsz = x_ref.shape[1]
    d = x_ref.shape[2]
    nj = wb_ref.shape[0]
    sw = wb_ref.shape[2] // 2

    @pl.when(pl.program_id(0) == 0)
    def _():
        hs_re[...] = h0re_ref[...]
        hs_im[...] = h0im_ref[...]

    x3 = x_ref[...]
    ms = jnp.mean(x3 * x3, axis=-1, keepdims=True)
    h3 = (x3 * lax.rsqrt(ms + NORM_EPS)) * nw_ref[...] * (1.0 + mod_ref[1]) + mod_ref[0]
    h2 = h3.reshape(tb * bsz, d)
    if finish:
        h_s[...] = h2
    hb = h2.astype(BF16)

    for j in range(nj):
        r = _dot(hb[:, j * LANES:(j + 1) * LANES], wb_ref[j])
        sre[...] = r[:, :sw]
        sim[...] = r[:, sw:]
        ar = are_ref[:, j * sw:(j + 1) * sw]
        ai = aim_ref[:, j * sw:(j + 1) * sw]

        def body(t, carry, ar=ar, ai=ai):
            hr, hi = carry
            tt = (tb - 1 - t) if rev else t
            sl = pl.ds(pl.multiple_of(tt * bsz, bsz), bsz)
            nr = ar * hr - ai * hi + sre[sl, :]
            ni = ar * hi + ai * hr + sim[sl, :]
            sre[sl, :] = nr
            sim[sl, :] = ni
            return nr, ni

        hr, hi = lax.fori_loop(
            0, tb, body, (hs_re[:, j * sw:(j + 1) * sw], hs_im[:, j * sw:(j + 1) * sw]), unroll=4)
        hs_re[:, j * sw:(j + 1) * sw] = hr
        hs_im[:, j * sw:(j + 1) * sw] = hi
        yj = _dot(sre[...].astype(BF16), wc_ref[j, :sw, :]) + _dot(sim[...].astype(BF16), wc_ref[j, sw:, :])
        y_s[:, j * LANES:(j + 1) * LANES] = yj

    hre_ref[...] = hs_re[...]
    him_ref[...] = hs_im[...]
    if not finish:
        o_ref[...] = y_s[...].reshape(tb, bsz, d)
    else:
        y = y_s[...] + yf_ref[...].reshape(tb * bsz, d) + dsk_ref[...] * h_s[...]
        z = _dot(_gelu_tanh(y).astype(BF16), wglu_ref[...]) + bglu_ref[...]
        out = z[:, :d] * _sigmoid(z[:, d:])
        o_ref[...] = x3 + mod_ref[2] * out.reshape(tb, bsz, d)


def _s5_pass(x2, mod_b, norm_w4, layer, are, aim, wb, wc, h0, bsz, *, rev, fin=None):
    seq = x2.shape[0]
    d = x2.shape[1] // bsz
    tb = S5_TB
    nblk = seq // tb
    dirn = 1 if rev else 0
    nstate = are.shape[-1]
    sw = wb.shape[3] // 2
    x3 = x2.reshape(seq, bsz, d)
    blk = (lambda i: (nblk - 1 - i, 0, 0)) if rev else (lambda i: (i, 0, 0))
    x_spec = pl.BlockSpec((tb, bsz, d), blk)
    const2 = lambda i: (0, 0)
    in_specs = [
        x_spec,
        pl.BlockSpec((None, None, 3, bsz, d), lambda i: (layer, 1, 0, 0, 0)),
        pl.BlockSpec((None, None, 1, d), lambda i: (layer, 1, 0, 0)),
        pl.BlockSpec((None, bsz, nstate), lambda i: (dirn, 0, 0)),
        pl.BlockSpec((None, bsz, nstate), lambda i: (dirn, 0, 0)),
        pl.BlockSpec((None,) + wb.shape[1:], lambda i: (dirn, 0, 0, 0)),
        pl.BlockSpec((None,) + wc.shape[1:], lambda i: (dirn, 0, 0, 0)),
        pl.BlockSpec((bsz, nstate), const2),
        pl.BlockSpec((bsz, nstate), const2),
    ]
    args = [x3, mod_b, norm_w4, are, aim, wb, wc, h0[0], h0[1]]
    scratch = [
        pltpu.VMEM((tb * bsz, sw), F32), pltpu.VMEM((tb * bsz, sw), F32),
        pltpu.VMEM((bsz, nstate), F32), pltpu.VMEM((bsz, nstate), F32),
        pltpu.VMEM((tb * bsz, d), F32),
    ]
    finish = fin is not None
    if finish:
        y_fwd, d_skip, w_glu, b_glu = fin
        in_specs += [
            x_spec,
            pl.BlockSpec((1, d), const2),
            pl.BlockSpec(w_glu.shape, const2),
            pl.BlockSpec((1, 2 * d), const2),
        ]
        args += [y_fwd.reshape(seq, bsz, d), d_skip.reshape(1, d), w_glu, b_glu.reshape(1, 2 * d)]
        scratch.append(pltpu.VMEM((tb * bsz, d), F32))
    out, hre, him = pl.pallas_call(
        functools.partial(_s5_kernel, tb, rev, finish),
        grid=(nblk,),
        in_specs=in_specs,
        out_specs=[x_spec, pl.BlockSpec((bsz, nstate), const2), pl.BlockSpec((bsz, nstate), const2)],
        out_shape=[jax.ShapeDtypeStruct((seq, bsz, d), F32),
                   jax.ShapeDtypeStruct((bsz, nstate), F32),
                   jax.ShapeDtypeStruct((bsz, nstate), F32)],
        scratch_shapes=scratch,
        compiler_params=_cparams("arbitrary"),
        name="s5_bwd_finish" if finish else "s5_fwd",
    )(*args)
    return out.reshape(seq, bsz * d), (hre, him)


def _s5_mixer(xc2, x2, mods_b, norm_w4, layer, prm, bsz, need_ctx):
    (modc_b, modx_b) = mods_b
    are, aim, wb, wc, d_skip, w_glu, b_glu = prm
    nstate = are.shape[-1]
    zero = (jnp.zeros((bsz, nstate), F32), jnp.zeros((bsz, nstate), F32))
    ycf, hf = _s5_pass(xc2, modc_b, norm_w4, layer, are, aim, wb, wc, zero, bsz, rev=False)
    yxf, _ = _s5_pass(x2, modx_b, norm_w4, layer, are, aim, wb, wc, hf, bsz, rev=False)
    xc_new, hb = _s5_pass(xc2, modc_b, norm_w4, layer, are, aim, wb, wc, zero, bsz, rev=True,
                          fin=(ycf, d_skip, w_glu, b_glu))
    x_new, _ = _s5_pass(x2, modx_b, norm_w4, layer, are, aim, wb, wc, hb, bsz, rev=True,
                        fin=(yxf, d_skip, w_glu, b_glu))
    return (xc_new if need_ctx else None), x_new


def _level_matrix(rev):
    p = np.arange(CHUNK)
    a, r = p // SUBLANES, p % SUBLANES
    tau = SUB * r + a
    if rev:
        tau = CHUNK - 1 - tau
    ti, tj = tau[:, None], tau[None, :]
    x = ti ^ tj
    lvl = np.zeros((CHUNK, CHUNK), np.int32)
    for b in range(N_LEVELS):
        lvl = np.where((x >> b) & 1, b + 1, lvl)
    lvl = np.where(tj > ti, -1, lvl)
    return jnp.asarray(lvl.astype(np.int32))


def _scan_chunk_head(q, k, v, g, st_t, lv, rev):
    dk = q.shape[1]
    ea = (lambda e: SUB - 1 - e) if rev else (lambda e: e)
    ksub = (lambda e: SUBLANES - 1 - e) if rev else (lambda e: e)

    def vregs(x):
        return [x[SUBLANES * ea(e):SUBLANES * ea(e) + SUBLANES, :] for e in range(SUB)]

    def assemble(vs):
        return jnp.concatenate([vs[ea(a)] for a in range(SUB)], axis=0)

    qv, kv, gv = vregs(q), vregs(k), vregs(g)
    zero = jnp.zeros((SUBLANES, dk), F32)

    pf = [gv[0]]
    for e in range(1, SUB):
        pf.append(pf[-1] + gv[e])
    tot = pf[SUB - 1]

    sub_i = lax.broadcasted_iota(jnp.int32, (SUBLANES, dk), 0)
    re = (SUBLANES - 1 - sub_i) if rev else sub_i

    def row(x, e):
        kk = ksub(e)
        return jnp.broadcast_to(x[kk:kk + 1, :], (SUBLANES, dk))

    ct = zero
    for e in range(SUBLANES):
        ct = ct + jnp.where(re >= e, row(tot, e), 0.0)
    xcl = ct - tot
    ltot = row(ct, SUBLANES - 1)

    q_lv, k_lv = [], []
    for lvl in range(1, N_FINE + 1):
        s = 1 << lvl
        qs, ks = [zero] * SUB, [zero] * SUB
        for bs in range(0, SUB, s):
            m = bs + s // 2
            ref = pf[m - 1]
            for e in range(bs, m - 1):
                ks[e] = kv[e] * jnp.exp(ref - pf[e])
            ks[m - 1] = kv[m - 1]
            for e in range(m, bs + s):
                qs[e] = qv[e] * jnp.exp(pf[e] - ref)
        q_lv.append(assemble(qs))
        k_lv.append(assemble(ks))
    for lvl in range(N_FINE + 1, N_LEVELS + 1):
        w = 1 << (lvl - N_FINE)
        xm = zero
        for bs in range(0, SUBLANES, w):
            inblk = (re >= bs) & (re < bs + w)
            xm = xm + jnp.where(inblk, row(xcl, bs + w // 2), 0.0)
        right = (re & (w // 2)) != 0
        dr = xcl - xm
        dl = tot + xm - ct
        qs, ks = [], []
        for e in range(SUB):
            ex = jnp.exp(jnp.where(right, pf[e] + dr, dl - pf[e]))
            qs.append(jnp.where(right, qv[e] * ex, 0.0))
            ks.append(jnp.where(right, 0.0, kv[e] * ex))
        q_lv.append(assemble(qs))
        k_lv.append(assemble(ks))

    sc = jnp.where(lv == 0, jnp.sum(q * k, axis=-1, keepdims=True), 0.0)
    for lvl in range(1, N_LEVELS + 1):
        s_l = _dot_nt(q_lv[lvl - 1].astype(BF16), k_lv[lvl - 1].astype(BF16))
        sc = jnp.where(lv == lvl, s_l, sc)

    vb = v.astype(BF16)
    cum = [pf[e] + xcl for e in range(SUB)]
    qc = assemble([qv[e] * jnp.exp(cum[e]) for e in range(SUB)])
    kd = assemble([kv[e] * jnp.exp(ltot - cum[e]) for e in range(SUB)])
    o = _dot(sc.astype(BF16), vb) + _dot_nt(qc.astype(BF16), st_t.astype(BF16))
    st_new = jnp.exp(ltot[0:1, :]) * st_t + _dot_tn(vb, kd.astype(BF16))
    return o, st_new


def _scan_kernel(n_heads, dk, dv, rev, q_ref, k_ref, v_ref, g_ref, lv_ref, s0_ref, o_ref, sf_ref, st_ref):
    @pl.when(pl.program_id(1) == 0)
    def _():
        st_ref[...] = s0_ref[...]

    lv = lv_ref[...]
    for h in range(n_heads):
        ks = slice(h * dk, (h + 1) * dk)
        vs = slice(h * dv, (h + 1) * dv)
        o, st_new = _scan_chunk_head(q_ref[:, ks], k_ref[:, ks], v_ref[:, vs], g_ref[:, ks],
                                     st_ref[vs, :], lv, rev)
        o_ref[:, vs] = o
        st_ref[vs, :] = st_new
    sf_ref[...] = st_ref[...]


def _gated_scan(q, k, v, g, s0, n_heads, bsz, *, rev):
    seq = q.shape[0]
    wk = q.shape[1] // bsz
    wv = v.shape[1] // bsz
    dk, dv = wk // n_heads, wv // n_heads
    nchunk = seq // CHUNK
    blk = (lambda b, n: (nchunk - 1 - n, b)) if rev else (lambda b, n: (n, b))
    kspec = pl.BlockSpec((CHUNK, wk), blk)
    vspec = pl.BlockSpec((CHUNK, wv), blk)
    sspec = pl.BlockSpec((None, wv, dk), lambda b, n: (b, 0, 0))
    return pl.pallas_call(
        functools.partial(_scan_kernel, n_heads, dk, dv, rev),
        grid=(bsz, nchunk),
        in_specs=[kspec, kspec, vspec, kspec,
                  pl.BlockSpec((CHUNK, CHUNK), lambda b, n: (0, 0)), sspec],
        out_specs=[vspec, sspec],
        out_shape=[jax.ShapeDtypeStruct((seq, bsz * wv), F32),
                   jax.ShapeDtypeStruct((bsz, wv, dk), F32)],
        scratch_shapes=[pltpu.VMEM((wv, dk), F32)],
        compiler_params=_cparams("parallel", "arbitrary"),
        name="gated_scan_bwd" if rev else "gated_scan_fwd",
    )(q, k, v, g, _level_matrix(rev), s0)


def _bidir_scan(ctx_in, lat_in, n_heads, bsz, need_ctx):
    qc, kfc, kbc, vc, gfc, gbc = ctx_in
    qx, kfx, kbx, vx, gfx, gbx = lat_in
    wk = qc.shape[1] // bsz
    wv = vc.shape[1] // bsz
    s0 = jnp.zeros((bsz, wv, wk // n_heads), F32)
    oc_f, s_f = _gated_scan(qc, kfc, vc, gfc, s0, n_heads, bsz, rev=False)
    oc_b, s_b = _gated_scan(qc, kbc, vc, gbc, s0, n_heads, bsz, rev=True)
    ox_f, _ = _gated_scan(qx, kfx, vx, gfx, s_f, n_heads, bsz, rev=False)
    ox_b, _ = _gated_scan(qx, kbx, vx, gbx, s_b, n_heads, bsz, rev=True)
    return ((oc_f, oc_b) if need_ctx else None), (ox_f, ox_b)


def _stage_rows(src_ref, stage_ref, src_row0, nrows, src_lane0, stage_row0):
    for jl in range(stage_ref.shape[0]):
        stage_ref[jl, stage_row0:stage_row0 + nrows, :] = (
            src_ref[src_row0:src_row0 + nrows, src_lane0 + jl * LANES:src_lane0 + (jl + 1) * LANES])


def _unstage_rows(stage_ref, dst_ref, stage_row0, nrows, dst_row0, dst_lane0):
    for jl in range(stage_ref.shape[0]):
        dst_ref[dst_row0:dst_row0 + nrows, dst_lane0 + jl * LANES:dst_lane0 + (jl + 1) * LANES] = (
            stage_ref[jl, stage_row0:stage_row0 + nrows, :])


def _permute_chunk_in(stage_ref, dst_ref, base):
    for jl in range(stage_ref.shape[0]):
        for a in range(SUB):
            dst_ref[base + SUBLANES * a:base + SUBLANES * (a + 1), jl * LANES:(jl + 1) * LANES] = (
                stage_ref[jl, pl.ds(base + a, SUBLANES, stride=SUB), :])


def _permute_chunk_out(val, stage_ref, base):
    for jl in range(stage_ref.shape[0]):
        for a in range(SUB):
            stage_ref[jl, pl.ds(base + a, SUBLANES, stride=SUB), :] = (
                val[SUBLANES * a:SUBLANES * (a + 1), jl * LANES:(jl + 1) * LANES])


def _hgrn_epilogue(p, lb, d):
    q = _silu(p[:, :d])
    v = p[:, d:2 * d]
    zf = p[:, 2 * d:3 * d]
    zb = p[:, 3 * d:4 * d]
    og = p[:, 4 * d:]
    lf, lbk = lb[0:1, :], lb[1:2, :]
    gf = jnp.log(lf + (1.0 - lf) * _sigmoid(zf))
    gb = jnp.log(lbk + (1.0 - lbk) * _sigmoid(zb))
    kf = (1.0 - lf) * _sigmoid(-zf)
    kb = (1.0 - lbk) * _sigmoid(-zb)
    return q, kf, kb, v, gf, gb, og


def _hgrn_proj_kernel(nchunk, x_ref, mod_ref, nw_ref, w_ref, lb_ref,
                      q_ref, kf_ref, kb_ref, v_ref, gf_ref, gb_ref, og_ref, stage_ref, xs_ref):
    d = x_ref.shape[1]
    _stage_rows(x_ref, stage_ref, 0, nchunk * CHUNK, 0, 0)
    for c in range(nchunk):
        _permute_chunk_in(stage_ref, xs_ref, c * CHUNK)
    h = _norm_mod(xs_ref[...], nw_ref[...], mod_ref[0:1, :], mod_ref[1:2, :]).astype(BF16)
    outs = _hgrn_epilogue(_dot(h, w_ref[...]), lb_ref[...], d)
    for ref, val in zip((q_ref, kf_ref, kb_ref, v_ref, gf_ref, gb_ref, og_ref), outs):
        ref[...] = val


def _hgrn_project(x2, mod_a, norm_w4, layer, w_in, lb, bsz):
    seq = x2.shape[0]
    d = x2.shape[1] // bsz
    nchunk = 2
    tm = nchunk * CHUNK
    row = pl.BlockSpec((tm, d), lambda t, b: (t, b))
    return pl.pallas_call(
        functools.partial(_hgrn_proj_kernel, nchunk),
        grid=(seq // tm, bsz),
        in_specs=[
            row,
            pl.BlockSpec((None, None, None, 3, d), lambda t, b: (layer, b, 1, 0, 0)),
            pl.BlockSpec((None, None, 1, d), lambda t, b: (layer, 1, 0, 0)),
            pl.BlockSpec(w_in.shape, lambda t, b: (0, 0)),
            pl.BlockSpec(lb.shape, lambda t, b: (0, 0)),
        ],
        out_specs=[row] * 7,
        out_shape=[jax.ShapeDtypeStruct((seq, bsz * d), F32)] * 7,
        scratch_shapes=[pltpu.VMEM((d // LANES, tm, LANES), F32), pltpu.VMEM((tm, d), F32)],
        compiler_params=_cparams("parallel", "parallel"),
        name="hgrn_project",
    )(x2, mod_a, norm_w4, w_in, lb)


def _gla_epilogue(p, wgate, bgate, dkt, dvt):
    q = p[:, :dkt] * ((dkt // GLA_HEADS) ** -0.5)
    k = p[:, dkt:2 * dkt]
    v = p[:, 2 * dkt:2 * dkt + dvt]
    og = p[:, 2 * dkt + dvt:2 * dkt + 2 * dvt]
    low = p[:, 2 * dkt + 2 * dvt:]
    lg = _log_sigmoid(_dot(low.astype(BF16), wgate) + bgate) * (1.0 / GLA_TAU)
    return q, k, v, og, lg[:, :dkt], lg[:, dkt:]


def _gla_proj_kernel(nchunk, x_ref, mod_ref, nw_ref, w_ref, wg_ref, bg_ref,
                     q_ref, k_ref, v_ref, og_ref, gf_ref, gb_ref, stage_ref, xs_ref):
    dkt = q_ref.shape[1]
    dvt = v_ref.shape[1]
    _stage_rows(x_ref, stage_ref, 0, nchunk * CHUNK, 0, 0)
    for c in range(nchunk):
        _permute_chunk_in(stage_ref, xs_ref, c * CHUNK)
    h = _norm_mod(xs_ref[...], nw_ref[...], mod_ref[0:1, :], mod_ref[1:2, :]).astype(BF16)
    outs = _gla_epilogue(_dot(h, w_ref[...]), wg_ref[...], bg_ref[...], dkt, dvt)
    for ref, val in zip((q_ref, k_ref, v_ref, og_ref, gf_ref, gb_ref), outs):
        ref[...] = val


def _gla_proj_colmajor_kernel(bsz, x_ref, mod_ref, nw_ref, w_ref, wg_ref, bg_ref,
                              q_ref, k_ref, v_ref, og_ref, gf_ref, gb_ref, stage_ref, xs_ref):
    rows = x_ref.shape[0]
    d = nw_ref.shape[1]
    dkt = q_ref.shape[1] // bsz
    dvt = v_ref.shape[1] // bsz
    ncol = CHUNK // rows
    for b in range(bsz):
        for c in range(ncol):
            _stage_rows(x_ref, stage_ref, 0, rows, (c * bsz + b) * d, b * CHUNK + c * rows)
        _permute_chunk_in(stage_ref, xs_ref, b * CHUNK)
    for b in range(bsz):
        xb = xs_ref[b * CHUNK:(b + 1) * CHUNK, :]
        h = _norm_mod(xb, nw_ref[...], mod_ref[0, b:b + 1, :], mod_ref[1, b:b + 1, :]).astype(BF16)
        outs = _gla_epilogue(_dot(h, w_ref[...]), wg_ref[...], bg_ref[...], dkt, dvt)
        for ref, val, wdt in zip((q_ref, k_ref, v_ref, og_ref, gf_ref, gb_ref), outs,
                                 (dkt, dkt, dvt, dvt, dkt, dkt)):
            ref[:, b * wdt:(b + 1) * wdt] = val


def _gla_project(x2, mod, norm_w4, layer, w_in, wgate, bgate, bsz, dkt, dvt, *, colmajor):
    seq = x2.shape[0]
    d = x2.shape[1] // bsz
    widths = (dkt, dkt, dvt, dvt, dkt, dkt)
    out_shape = [jax.ShapeDtypeStruct((seq, bsz * w), F32) for w in widths]
    wspecs = lambda im: [pl.BlockSpec(w_in.shape, im), pl.BlockSpec(wgate.shape, im),
                         pl.BlockSpec(bgate.shape, im)]
    if not colmajor:
        nchunk = 2
        tm = nchunk * CHUNK
        return pl.pallas_call(
            functools.partial(_gla_proj_kernel, nchunk),
            grid=(seq // tm, bsz),
            in_specs=[
                pl.BlockSpec((tm, d), lambda t, b: (t, b)),
                pl.BlockSpec((None, None, None, 3, d), lambda t, b: (layer, b, 1, 0, 0)),
                pl.BlockSpec((None, None, 1, d), lambda t, b: (layer, 1, 0, 0)),
            ] + wspecs(lambda t, b: (0, 0)),
            out_specs=[pl.BlockSpec((tm, w), lambda t, b: (t, b)) for w in widths],
            out_shape=out_shape,
            scratch_shapes=[pltpu.VMEM((d // LANES, tm, LANES), F32), pltpu.VMEM((tm, d), F32)],
            compiler_params=_cparams("parallel", "parallel"),
            name="gla_project",
        )(x2, mod, norm_w4, w_in, wgate, bgate)
    rows = seq // GRID_W
    ncol = CHUNK // rows
    xg = x2.reshape(rows, GRID_W * bsz * d)
    return pl.pallas_call(
        functools.partial(_gla_proj_colmajor_kernel, bsz),
        grid=(seq // CHUNK,),
        in_specs=[
            pl.BlockSpec((rows, ncol * bsz * d), lambda n: (0, n)),
            pl.BlockSpec((None, None, 3, bsz, d), lambda n: (layer, 1, 0, 0, 0)),
            pl.BlockSpec((None, None, 1, d), lambda n: (layer, 1, 0, 0)),
        ] + wspecs(lambda n: (0, 0)),
        out_specs=[pl.BlockSpec((CHUNK, bsz * w), lambda n: (n, 0)) for w in widths],
        out_shape=out_shape,
        scratch_shapes=[pltpu.VMEM((d // LANES, bsz * CHUNK, LANES), F32), pltpu.VMEM((bsz * CHUNK, d), F32)],
        compiler_params=_cparams("parallel"),
        name="gla_project_colmajor",
    )(xg, mod, norm_w4, w_in, wgate, bgate)


def _head_out(of, ob, og, gnw, w_out, n_heads, zs_ref):
    dv = of.shape[1] // n_heads
    o = of + ob
    for h in range(n_heads):
        sl = slice(h * dv, (h + 1) * dv)
        zs_ref[:, sl] = (_rms(o[:, sl], gnw) * _silu(og[:, sl])).astype(BF16)
    return _dot(zs_ref[...], w_out)


def _mix_out_kernel(n_heads, nchunk, of_ref, ob_ref, og_ref, x_ref, mod_ref, gnw_ref, w_ref,
                    o_ref, stage_ref, xs_ref, zs_ref):
    _stage_rows(x_ref, stage_ref, 0, nchunk * CHUNK, 0, 0)
    for c in range(nchunk):
        _permute_chunk_in(stage_ref, xs_ref, c * CHUNK)
    y = _head_out(of_ref[...], ob_ref[...], og_ref[...], gnw_ref[...], w_ref[...], n_heads, zs_ref)
    xn = xs_ref[...] + mod_ref[2:3, :] * y
    for c in range(nchunk):
        _permute_chunk_out(xn[c * CHUNK:(c + 1) * CHUNK, :], stage_ref, c * CHUNK)
    _unstage_rows(stage_ref, o_ref, 0, nchunk * CHUNK, 0, 0)


def _mix_out_colmajor_kernel(n_heads, bsz, of_ref, ob_ref, og_ref, x_ref, mod_ref, gnw_ref, w_ref,
                             o_ref, stage_ref, xs_ref, zs_ref):
    rows = x_ref.shape[0]
    d = w_ref.shape[1]
    wv = of_ref.shape[1] // bsz
    ncol = CHUNK // rows
    for b in range(bsz):
        for c in range(ncol):
            _stage_rows(x_ref, stage_ref, 0, rows, (c * bsz + b) * d, b * CHUNK + c * rows)
        _permute_chunk_in(stage_ref, xs_ref, b * CHUNK)
    for b in range(bsz):
        ws = slice(b * wv, (b + 1) * wv)
        y = _head_out(of_ref[:, ws], ob_ref[:, ws], og_ref[:, ws], gnw_ref[...], w_ref[...],
                      n_heads, zs_ref)
        xn = xs_ref[b * CHUNK:(b + 1) * CHUNK, :] + mod_ref[2, b:b + 1, :] * y
        _permute_chunk_out(xn, stage_ref, b * CHUNK)
    for b in range(bsz):
        for c in range(ncol):
            _unstage_rows(stage_ref, o_ref, b * CHUNK + c * rows, rows, 0, (c * bsz + b) * d)


def _mix_out(of, ob, og, x2, mod, layer, gnw, w_out, n_heads, bsz, *, colmajor):
    seq = x2.shape[0]
    d = x2.shape[1] // bsz
    wv = of.shape[1] // bsz
    gnw = gnw.reshape(1, -1)
    if not colmajor:
        nchunk = 2
        tm = nchunk * CHUNK
        ospec = pl.BlockSpec((tm, wv), lambda t, b: (t, b))
        xspec = pl.BlockSpec((tm, d), lambda t, b: (t, b))
        return pl.pallas_call(
            functools.partial(_mix_out_kernel, n_heads, nchunk),
            grid=(seq // tm, bsz),
            in_specs=[ospec, ospec, ospec, xspec,
                      pl.BlockSpec((None, None, None, 3, d), lambda t, b: (layer, b, 1, 0, 0)),
                      pl.BlockSpec(gnw.shape, lambda t, b: (0, 0)),
                      pl.BlockSpec(w_out.shape, lambda t, b: (0, 0))],
            out_specs=xspec,
            out_shape=jax.ShapeDtypeStruct((seq, bsz * d), F32),
            scratch_shapes=[pltpu.VMEM((d // LANES, tm, LANES), F32), pltpu.VMEM((tm, d), F32),
                            pltpu.VMEM((tm, wv), BF16)],
            compiler_params=_cparams("parallel", "parallel"),
            name="mix_out",
        )(of, ob, og, x2, mod, gnw, w_out)
    rows = seq // GRID_W
    ncol = CHUNK // rows
    xg = x2.reshape(rows, GRID_W * bsz * d)
    ospec = pl.BlockSpec((CHUNK, bsz * wv), lambda n: (n, 0))
    xspec = pl.BlockSpec((rows, ncol * bsz * d), lambda n: (0, n))
    out = pl.pallas_call(
        functools.partial(_mix_out_colmajor_kernel, n_heads, bsz),
        grid=(seq // CHUNK,),
        in_specs=[ospec, ospec, ospec, xspec,
                  pl.BlockSpec((None, None, 3, bsz, d), lambda n: (layer, 1, 0, 0, 0)),
                  pl.BlockSpec(gnw.shape, lambda n: (0, 0)),
                  pl.BlockSpec(w_out.shape, lambda n: (0, 0))],
        out_specs=xspec,
        out_shape=jax.ShapeDtypeStruct(xg.shape, F32),
        scratch_shapes=[pltpu.VMEM((d // LANES, bsz * CHUNK, LANES), F32), pltpu.VMEM((bsz * CHUNK, d), F32),
                        pltpu.VMEM((CHUNK, wv), BF16)],
        compiler_params=_cparams("parallel"),
        name="mix_out_colmajor",
    )(of, ob, og, xg, mod, gnw, w_out)
    return out.reshape(seq, bsz * d)


def _gla_mixer(xc2, x2, mods, norm_w4, layer, prm, bsz, need_ctx):
    (modc_a, modc_b), (modx_a, modx_b) = mods
    w_in, wgate, bgate, gnw, w_out, dkt, dvt = prm
    qc, kc, vc, ogc, gfc, gbc = _gla_project(xc2, modc_a, norm_w4, layer, w_in, wgate, bgate, bsz,
                                             dkt, dvt, colmajor=False)
    qx, kx, vx, ogx, gfx, gbx = _gla_project(x2, modx_b, norm_w4, layer, w_in, wgate, bgate, bsz,
                                             dkt, dvt, colmajor=True)
    oc, ox = _bidir_scan((qc, kc, kc, vc, gfc, gbc), (qx, kx, kx, vx, gfx, gbx), GLA_HEADS, bsz, need_ctx)
    x_new = _mix_out(ox[0], ox[1], ogx, x2, modx_b, layer, gnw, w_out, GLA_HEADS, bsz, colmajor=True)
    xc_new = None
    if need_ctx:
        xc_new = _mix_out(oc[0], oc[1], ogc, xc2, modc_a, layer, gnw, w_out, GLA_HEADS, bsz, colmajor=False)
    return xc_new, x_new


def _hgrn_mixer(xc2, x2, mods, norm_w4, layer, prm, bsz, need_ctx):
    (modc_a, _), (modx_a, _) = mods
    w_in, lb, gnw, w_out, n_heads = prm
    qc, kfc, kbc, vc, gfc, gbc, ogc = _hgrn_project(xc2, modc_a, norm_w4, layer, w_in, lb, bsz)
    qx, kfx, kbx, vx, gfx, gbx, ogx = _hgrn_project(x2, modx_a, norm_w4, layer, w_in, lb, bsz)
    oc, ox = _bidir_scan((qc, kfc, kbc, vc, gfc, gbc), (qx, kfx, kbx, vx, gfx, gbx), n_heads, bsz, need_ctx)
    x_new = _mix_out(ox[0], ox[1], ogx, x2, modx_a, layer, gnw, w_out, n_heads, bsz, colmajor=False)
    xc_new = None
    if need_ctx:
        xc_new = _mix_out(oc[0], oc[1], ogc, xc2, modc_a, layer, gnw, w_out, n_heads, bsz, colmajor=False)
    return xc_new, x_new


def _forward(x, c, ctx, c_ctx, ada_w, ada_b, norm_w, ffn_w_in, ffn_w_out,
             s5_a_re, s5_a_im, s5_log_step, s5_b_re, s5_b_im, s5_c_re, s5_c_im, s5_d,
             s5_w_glu, s5_b_glu, gla_w_in, gla_w_gate, gla_b_gate, gla_norm_w, gla_w_out,
             hgrn_w_in, hgrn_lb_logits, hgrn_norm_w, hgrn_w_out, final_norm_w, depth=None):
    depth = ada_w.shape[0] if depth is None else depth
    bsz, seq, d = x.shape
    mods_x, mods_c = _ada(c, c_ctx, ada_w, ada_b)
    norm_w4 = norm_w.reshape(norm_w.shape[0], 3, 1, d)
    w_in = ffn_w_in.astype(BF16)
    w_out = ffn_w_out.astype(BF16)
    lb_soft = jax.nn.softmax(hgrn_lb_logits.astype(F32), axis=0)
    lb_all = jnp.cumsum(lb_soft, axis=0) - lb_soft[0]

    xc2 = jnp.transpose(ctx, (1, 0, 2)).reshape(ctx.shape[1], bsz * d)
    x2 = x
    for i in range(depth):
        last = i == depth - 1
        kind, j = i % N_MIXERS, i // N_MIXERS
        x2 = _ffn(x2, mods_x[0], norm_w4, w_in, w_out, i, 0, bsz, in_bld=(i == 0))
        xc2 = _ffn(xc2, mods_c[0], norm_w4, w_in, w_out, i, 0, bsz)
        mods = (mods_c, mods_x)
        if kind == 0:
            are, aim, wb, wc = _s5_prep(s5_a_re[j], s5_a_im[j], s5_log_step[j], s5_b_re[j], s5_b_im[j],
                                        s5_c_re[j], s5_c_im[j], bsz)
            prm = (are, aim, wb, wc, s5_d[j], s5_w_glu[j].astype(BF16), s5_b_glu[j])
            yc, x2 = _s5_mixer(xc2, x2, (mods_c[1], mods_x[1]), norm_w4, i, prm, bsz, not last)
        elif kind == 1:
            dkt = d // 2
            dvt = d
            n_low = 2 * GLA_RANK
            pad = LANES - n_low
            wi = jnp.pad(gla_w_in[j], ((0, 0), (0, pad))).astype(BF16)
            wg = jnp.zeros((LANES, 2 * dkt), F32)
            wg = wg.at[:GLA_RANK, :dkt].set(gla_w_gate[j, 0]).at[GLA_RANK:n_low, dkt:].set(gla_w_gate[j, 1])
            prm = (wi, wg.astype(BF16), gla_b_gate[j].reshape(1, 2 * dkt), gla_norm_w[j],
                   gla_w_out[j].astype(BF16), dkt, dvt)
            yc, x2 = _gla_mixer(xc2, x2, mods, norm_w4, i, prm, bsz, not last)
        else:
            prm = (hgrn_w_in[j].astype(BF16), lb_all[i], hgrn_norm_w[j], hgrn_w_out[j].astype(BF16),
                   d // HGRN_DK)
            yc, x2 = _hgrn_mixer(xc2, x2, mods, norm_w4, i, prm, bsz, not last)
        x2 = _ffn(x2, mods_x[0], norm_w4, w_in, w_out, i, 2, bsz,
                  final_w=final_norm_w.reshape(1, d) if last else None)
        if not last:
            xc2 = _ffn(yc, mods_c[0], norm_w4, w_in, w_out, i, 2, bsz)
    return x2


def kernel(x, c, ctx, c_ctx, ada_w, ada_b, norm_w, ffn_w_in, ffn_w_out, s5_a_re, s5_a_im, s5_log_step,
           s5_b_re, s5_b_im, s5_c_re, s5_c_im, s5_d, s5_w_glu, s5_b_glu, gla_w_in, gla_w_gate,
           gla_b_gate, gla_norm_w, gla_w_out, hgrn_w_in, hgrn_lb_logits, hgrn_norm_w, hgrn_w_out,
           final_norm_w):
    return _forward(x, c, ctx, c_ctx, ada_w, ada_b, norm_w, ffn_w_in, ffn_w_out, s5_a_re, s5_a_im,
                    s5_log_step, s5_b_re, s5_b_im, s5_c_re, s5_c_im, s5_d, s5_w_glu, s5_b_glu,
                    gla_w_in, gla_w_gate, gla_b_gate, gla_norm_w, gla_w_out, hgrn_w_in,
                    hgrn_lb_logits, hgrn_norm_w, hgrn_w_out, final_norm_w)
```

```python
import functools
import math

import numpy as np
import jax
import jax.numpy as jnp
from jax import lax
from jax.experimental import pallas as pl
from jax.experimental.pallas import tpu as pltpu

F32 = jnp.float32
BF16 = jnp.bfloat16

NORM_EPS = 1e-6
GRID_W = 64
N_MOD = 9
N_MIXERS = 3

S5_GROUP = 16
S5_STATE = 64
S5_LANE_GROUPS = 8
S5_TB = 64
S5_TS = 64

GLA_HEADS = 4
GLA_RANK = 16
GLA_TAU = 16.0
HGRN_DK = 128

SUBLANES = 8
LANES = 128
CHUNK = 128
SUB = CHUNK // SUBLANES
N_FINE = 4
N_LEVELS = 7

FFN_CHUNK = 768

VMEM_LIMIT_BYTES = 56 * 1024 * 1024


def _cparams(*sem):
    return pltpu.CompilerParams(dimension_semantics=sem, vmem_limit_bytes=VMEM_LIMIT_BYTES)


def _dot(a, b):
    return jnp.dot(a, b, preferred_element_type=F32)


def _dot_nt(a, b):
    return lax.dot_general(a, b, (((1,), (1,)), ((), ())), preferred_element_type=F32)


def _dot_tn(a, b):
    return lax.dot_general(a, b, (((0,), (0,)), ((), ())), preferred_element_type=F32)


def _sigmoid(x):
    return 1.0 / (1.0 + jnp.exp(-x))


def _silu(x):
    return x * _sigmoid(x)


def _gelu_tanh(x):
    c = math.sqrt(2.0 / math.pi)
    return 0.5 * x * (1.0 + jnp.tanh(c * (x + 0.044715 * (x * x * x))))


def _log_sigmoid(x):
    return jnp.minimum(x, 0.0) - jnp.log1p(jnp.exp(-jnp.abs(x)))


def _rms(x, w):
    ms = jnp.mean(x * x, axis=-1, keepdims=True)
    return (x * lax.rsqrt(ms + NORM_EPS)) * w


def _norm_mod(x, w, shift, scale):
    return _rms(x, w) * (1.0 + scale) + shift


def _ada_kernel(cc_ref, w_ref, b_ref, o_ref):
    a = _silu(cc_ref[...]).astype(BF16)
    o_ref[...] = _dot(a, w_ref[...].astype(BF16)) + b_ref[...]


def _ada(c, c_ctx, ada_w, ada_b):
    depth, d, nd = ada_w.shape
    bsz = c.shape[0]
    rows = ((bsz + 1 + SUBLANES - 1) // SUBLANES) * SUBLANES
    cc = jnp.zeros((rows, d), F32).at[:bsz].set(c).at[bsz].set(c_ctx)
    nblk = nd // d
    mod = pl.pallas_call(
        _ada_kernel,
        grid=(depth, nblk),
        in_specs=[
            pl.BlockSpec((rows, d), lambda i, n: (0, 0)),
            pl.BlockSpec((None, d, d), lambda i, n: (i, 0, n)),
            pl.BlockSpec((None, 1, d), lambda i, n: (i, 0, n)),
        ],
        out_specs=pl.BlockSpec((None, rows, d), lambda i, n: (i, 0, n)),
        out_shape=jax.ShapeDtypeStruct((depth, rows, nd), F32),
        compiler_params=_cparams("arbitrary", "arbitrary"),
        name="ada_mod",
    )(cc, ada_w, ada_b.reshape(depth, 1, nd))
    mx = mod[:, :bsz].reshape(depth, bsz, 3, 3, d)
    mc = jnp.broadcast_to(mod[:, bsz:bsz + 1], (depth, bsz, nd)).reshape(depth, bsz, 3, 3, d)
    return (mx, jnp.transpose(mx, (0, 2, 3, 1, 4))), (mc, jnp.transpose(mc, (0, 2, 3, 1, 4)))


def _ffn_kernel(chunks, final, x_ref, mod_ref, nw_ref, wi_ref, wo_ref, *rest):
    if final:
        fw_ref, o_ref = rest
    else:
        (o_ref,) = rest
    ff = wo_ref.shape[0]
    x = x_ref[...]
    h = _norm_mod(x, nw_ref[...], mod_ref[0:1, :], mod_ref[1:2, :]).astype(BF16)
    acc = None
    for lo, hi in chunks:
        g = _dot(h, wi_ref[:, lo:hi])
        u = _dot(h, wi_ref[:, ff + lo:ff + hi])
        part = _dot((_silu(g) * u).astype(BF16), wo_ref[lo:hi, :])
        acc = part if acc is None else acc + part
    y = x + (0.5 * mod_ref[2:3, :]) * acc
    if final:
        y = _rms(y, fw_ref[...])
    o_ref[...] = y


def _ff_chunks(ff, width):
    assert ff % LANES == 0 and width % LANES == 0
    return tuple((lo, min(lo + width, ff)) for lo in range(0, ff, width))


def _ffn(x, mod_a, norm_w4, w_in, w_out, layer, stage, bsz, *, in_bld=False, final_w=None):
    d = norm_w4.shape[-1]
    seq = x.shape[1] if in_bld else x.shape[0]
    ff = w_out.shape[2]
    tm = 512 if seq % 512 == 0 else 256
    s = 0 if stage == 0 else 1
    final = final_w is not None
    resident = pl.Buffered(1)
    if in_bld:
        x_spec = pl.BlockSpec((None, tm, d), lambda t, b: (b, t, 0))
    else:
        x_spec = pl.BlockSpec((tm, d), lambda t, b: (t, b))
    in_specs = [
        x_spec,
        pl.BlockSpec((None, None, None, 3, d), lambda t, b: (layer, b, stage, 0, 0)),
        pl.BlockSpec((None, None, 1, d), lambda t, b: (layer, stage, 0, 0)),
        pl.BlockSpec((None, None, d, 2 * ff), lambda t, b: (layer, s, 0, 0), pipeline_mode=resident),
        pl.BlockSpec((None, None, ff, d), lambda t, b: (layer, s, 0, 0), pipeline_mode=resident),
    ]
    args = [x, mod_a, norm_w4, w_in, w_out]
    if final:
        in_specs.append(pl.BlockSpec((1, d), lambda t, b: (0, 0)))
        args.append(final_w)
        out_spec = pl.BlockSpec((None, tm, d), lambda t, b: (b, t, 0))
        out_shape = jax.ShapeDtypeStruct((bsz, seq, d), F32)
    else:
        out_spec = pl.BlockSpec((tm, d), lambda t, b: (t, b))
        out_shape = jax.ShapeDtypeStruct((seq, bsz * d), F32)
    return pl.pallas_call(
        functools.partial(_ffn_kernel, _ff_chunks(ff, FFN_CHUNK), final),
        grid=(seq // tm, bsz),
        in_specs=in_specs,
        out_specs=out_spec,
        out_shape=out_shape,
        compiler_params=_cparams("parallel", "parallel"),
        name="ffn_half",
    )(*args)


def _s5_prep(a_re, a_im, log_step, b_re, b_im, c_re, c_im, bsz):
    dt = jnp.exp(log_step.astype(F32))[..., None]
    a_re = a_re.astype(F32)
    a_im = a_im.astype(F32)
    mag = jnp.exp(a_re * dt)
    abar_re = mag * jnp.cos(a_im * dt)
    abar_im = mag * jnp.sin(a_im * dt)
    den = a_re * a_re + a_im * a_im
    zr = abar_re - 1.0
    zi = abar_im
    coef_re = ((zr * a_re + zi * a_im) / den)[..., None]
    coef_im = ((zi * a_re - zr * a_im) / den)[..., None]
    b_re = b_re.astype(F32)
    b_im = b_im.astype(F32)
    bbar_re = coef_re * b_re - coef_im * b_im
    bbar_im = coef_re * b_im + coef_im * b_re
    n_dir, g, p = abar_re.shape
    gc = b_re.shape[-1]
    nj = g // S5_LANE_GROUPS
    lg = S5_LANE_GROUPS
    are = jnp.broadcast_to(abar_re.reshape(n_dir, 1, g * p), (n_dir, bsz, g * p))
    aim = jnp.broadcast_to(abar_im.reshape(n_dir, 1, g * p), (n_dir, bsz, g * p))
    eye = jnp.eye(lg, dtype=F32)
    bb = jnp.stack([bbar_re, bbar_im], axis=1).reshape(n_dir, 2, nj, lg, p, gc)
    bb = jnp.transpose(bb, (0, 2, 3, 5, 1, 4))
    wb = bb[:, :, :, :, :, None, :] * eye[None, None, :, None, None, :, None]
    wb = wb.reshape(n_dir, nj, lg * gc, 2 * lg * p).astype(BF16)
    cc = jnp.stack([c_re.astype(F32), -c_im.astype(F32)], axis=1).reshape(n_dir, 2, nj, lg, gc, p)
    cc = jnp.transpose(cc, (0, 2, 1, 3, 5, 4))
    wc = cc[:, :, :, :, :, None, :] * eye[None, None, None, :, None, :, None]
    wc = wc.reshape(n_dir, nj, 2 * lg * p, lg * gc).astype(BF16)
    return are, aim, wb, wc


def _s5_kernel(tb, rev, finish, x_ref, mod_ref, nw_ref, are_ref, aim_ref, wb_ref, wc_ref,
               h0re_ref, h0im_ref, *rest):
    n_io = 7 if finish else 3
    io, scratch = rest[:n_io], rest[n_io:]
    if finish:
        yf_ref, dsk_ref, wglu_ref, bglu_ref, o_ref, hre_ref, him_ref = io
        h_s = scratch[12]
    else:
        o_ref, hre_ref, him_ref = io
    bu = (scratch[0:2], scratch[2:4])
    hs = (scratch[4:6], scratch[6:8])
    hs_re, hs_im, y_s, hb_s = scratch[8:12]
    bsz = x_ref.shape[1]
    d = x_ref.shape[2]
    nj = wb_ref.shape[0]
    sw = wb_ref.shape[2] // 2
    ri = S5_TS * bsz
    nit = tb // S5_TS

    @pl.when(pl.program_id(0) == 0)
    def _():
        hs_re[...] = h0re_ref[...]
        hs_im[...] = h0im_ref[...]

    x3 = x_ref[...]
    ms = jnp.mean(x3 * x3, axis=-1, keepdims=True)
    h3 = (x3 * lax.rsqrt(ms + NORM_EPS)) * nw_ref[...] * (1.0 + mod_ref[1]) + mod_ref[0]
    h2 = h3.reshape(tb * bsz, d)
    if finish:
        h_s[...] = h2
    hb_s[...] = h2.astype(BF16)

    def lanes(j):
        return slice(j * LANES, (j + 1) * LANES)

    def drive(j, rows):
        r = _dot(hb_s[rows, lanes(j)], wb_ref[j])
        bu[j % 2][0][rows, :] = r[:, :sw]
        bu[j % 2][1][rows, :] = r[:, sw:]

    def readout(j, rows):
        y_s[rows, lanes(j)] = (_dot(hs[j % 2][0][rows, :].astype(BF16), wc_ref[j, :sw, :])
                               + _dot(hs[j % 2][1][rows, :].astype(BF16), wc_ref[j, sw:, :]))

    drive(0, slice(None))
    for j in range(nj):
        cur = j % 2
        ar = are_ref[:, j * sw:(j + 1) * sw]
        ai = aim_ref[:, j * sw:(j + 1) * sw]

        def body(i, carry, j=j, cur=cur, ar=ar, ai=ai):
            hr, hi = carry
            rows = pl.ds(pl.multiple_of(i * ri, ri), ri)
            if j >= 1:
                readout(j - 1, rows)
            if j + 1 < nj:
                drive(j + 1, rows)
            for s in range(S5_TS):
                t = i * S5_TS + s
                tt = (tb - 1 - t) if rev else t
                sl = pl.ds(pl.multiple_of(tt * bsz, bsz), bsz)
                hr, hi = (ar * hr - ai * hi + bu[cur][0][sl, :], ar * hi + ai * hr + bu[cur][1][sl, :])
                hs[cur][0][sl, :] = hr
                hs[cur][1][sl, :] = hi
            return hr, hi

        hr, hi = lax.fori_loop(
            0, nit, body, (hs_re[:, j * sw:(j + 1) * sw], hs_im[:, j * sw:(j + 1) * sw]))
        hs_re[:, j * sw:(j + 1) * sw] = hr
        hs_im[:, j * sw:(j + 1) * sw] = hi
    readout(nj - 1, slice(None))

    hre_ref[...] = hs_re[...]
    him_ref[...] = hs_im[...]
    if not finish:
        o_ref[...] = y_s[...].reshape(tb, bsz, d)
    else:
        y = y_s[...] + yf_ref[...].reshape(tb * bsz, d) + dsk_ref[...] * h_s[...]
        z = _dot(_gelu_tanh(y).astype(BF16), wglu_ref[...]) + bglu_ref[...]
        out = z[:, :d] * _sigmoid(z[:, d:])
        o_ref[...] = x3 + mod_ref[2] * out.reshape(tb, bsz, d)


def _s5_pass(x2, mod_b, norm_w4, layer, are, aim, wb, wc, h0, bsz, *, rev, fin=None):
    seq = x2.shape[0]
    d = x2.shape[1] // bsz
    tb = S5_TB
    nblk = seq // tb
    dirn = 1 if rev else 0
    nstate = are.shape[-1]
    sw = wb.shape[3] // 2
    x3 = x2.reshape(seq, bsz, d)
    blk = (lambda i: (nblk - 1 - i, 0, 0)) if rev else (lambda i: (i, 0, 0))
    x_spec = pl.BlockSpec((tb, bsz, d), blk)
    const2 = lambda i: (0, 0)
    in_specs = [
        x_spec,
        pl.BlockSpec((None, None, 3, bsz, d), lambda i: (layer, 1, 0, 0, 0)),
        pl.BlockSpec((None, None, 1, d), lambda i: (layer, 1, 0, 0)),
        pl.BlockSpec((None, bsz, nstate), lambda i: (dirn, 0, 0)),
        pl.BlockSpec((None, bsz, nstate), lambda i: (dirn, 0, 0)),
        pl.BlockSpec((None,) + wb.shape[1:], lambda i: (dirn, 0, 0, 0)),
        pl.BlockSpec((None,) + wc.shape[1:], lambda i: (dirn, 0, 0, 0)),
        pl.BlockSpec((bsz, nstate), const2),
        pl.BlockSpec((bsz, nstate), const2),
    ]
    args = [x3, mod_b, norm_w4, are, aim, wb, wc, h0[0], h0[1]]
    scratch = [pltpu.VMEM((tb * bsz, sw), F32) for _ in range(8)] + [
        pltpu.VMEM((bsz, nstate), F32), pltpu.VMEM((bsz, nstate), F32),
        pltpu.VMEM((tb * bsz, d), F32), pltpu.VMEM((tb * bsz, d), BF16),
    ]
    finish = fin is not None
    if finish:
        y_fwd, d_skip, w_glu, b_glu = fin
        in_specs += [
            x_spec,
            pl.BlockSpec((1, d), const2),
            pl.BlockSpec(w_glu.shape, const2),
            pl.BlockSpec((1, 2 * d), const2),
        ]
        args += [y_fwd.reshape(seq, bsz, d), d_skip.reshape(1, d), w_glu, b_glu.reshape(1, 2 * d)]
        scratch.append(pltpu.VMEM((tb * bsz, d), F32))
    out, hre, him = pl.pallas_call(
        functools.partial(_s5_kernel, tb, rev, finish),
        grid=(nblk,),
        in_specs=in_specs,
        out_specs=[x_spec, pl.BlockSpec((bsz, nstate), const2), pl.BlockSpec((bsz, nstate), const2)],
        out_shape=[jax.ShapeDtypeStruct((seq, bsz, d), F32),
                   jax.ShapeDtypeStruct((bsz, nstate), F32),
                   jax.ShapeDtypeStruct((bsz, nstate), F32)],
        scratch_shapes=scratch,
        compiler_params=_cparams("arbitrary"),
        name="s5_bwd_finish" if finish else "s5_fwd",
    )(*args)
    return out.reshape(seq, bsz * d), (hre, him)


def _s5_mixer(xc2, x2, mods_b, norm_w4, layer, prm, bsz, need_ctx):
    (modc_b, modx_b) = mods_b
    are, aim, wb, wc, d_skip, w_glu, b_glu = prm
    nstate = are.shape[-1]
    zero = (jnp.zeros((bsz, nstate), F32), jnp.zeros((bsz, nstate), F32))
    ycf, hf = _s5_pass(xc2, modc_b, norm_w4, layer, are, aim, wb, wc, zero, bsz, rev=False)
    yxf, _ = _s5_pass(x2, modx_b, norm_w4, layer, are, aim, wb, wc, hf, bsz, rev=False)
    xc_new, hb = _s5_pass(xc2, modc_b, norm_w4, layer, are, aim, wb, wc, zero, bsz, rev=True,
                          fin=(ycf, d_skip, w_glu, b_glu))
    x_new, _ = _s5_pass(x2, modx_b, norm_w4, layer, are, aim, wb, wc, hb, bsz, rev=True,
                        fin=(yxf, d_skip, w_glu, b_glu))
    return (xc_new if need_ctx else None), x_new


def _level_matrix(rev):
    p = np.arange(CHUNK)
    a, r = p // SUBLANES, p % SUBLANES
    tau = SUB * r + a
    if rev:
        tau = CHUNK - 1 - tau
    ti, tj = tau[:, None], tau[None, :]
    x = ti ^ tj
    lvl = np.zeros((CHUNK, CHUNK), np.int32)
    for b in range(N_LEVELS):
        lvl = np.where((x >> b) & 1, b + 1, lvl)
    lvl = np.where(tj > ti, -1, lvl)
    return lvl.astype(np.int32)


def _scan_chunk_head(q, k, vb, g, st_t, masks, rev):
    dk = q.shape[1]
    ea = (lambda e: SUB - 1 - e) if rev else (lambda e: e)
    ksub = (lambda e: SUBLANES - 1 - e) if rev else (lambda e: e)

    def vregs(x):
        return [x[SUBLANES * ea(e):SUBLANES * ea(e) + SUBLANES, :] for e in range(SUB)]

    def assemble(vs):
        return jnp.concatenate([vs[ea(a)] for a in range(SUB)], axis=0)

    qv, kv, gv = vregs(q), vregs(k), vregs(g)
    zero = jnp.zeros((SUBLANES, dk), F32)

    pf = [gv[0]]
    for e in range(1, SUB):
        pf.append(pf[-1] + gv[e])
    tot = pf[SUB - 1]

    sub_i = lax.broadcasted_iota(jnp.int32, (SUBLANES, dk), 0)
    re = (SUBLANES - 1 - sub_i) if rev else sub_i

    def row(x, e):
        kk = ksub(e)
        return jnp.broadcast_to(x[kk:kk + 1, :], (SUBLANES, dk))

    ct = zero
    for e in range(SUBLANES):
        ct = ct + jnp.where(re >= e, row(tot, e), 0.0)
    xcl = ct - tot
    ltot = row(ct, SUBLANES - 1)

    q_lv, k_lv = [], []
    for lvl in range(1, N_FINE + 1):
        s = 1 << lvl
        qs, ks = [zero] * SUB, [zero] * SUB
        for bs in range(0, SUB, s):
            m = bs + s // 2
            ref = pf[m - 1]
            for e in range(bs, m - 1):
                ks[e] = kv[e] * jnp.exp(ref - pf[e])
            ks[m - 1] = kv[m - 1]
            for e in range(m, bs + s):
                qs[e] = qv[e] * jnp.exp(pf[e] - ref)
        q_lv.append(assemble(qs))
        k_lv.append(assemble(ks))
    for lvl in range(N_FINE + 1, N_LEVELS + 1):
        w = 1 << (lvl - N_FINE)
        xm = zero
        for bs in range(0, SUBLANES, w):
            inblk = (re >= bs) & (re < bs + w)
            xm = xm + jnp.where(inblk, row(xcl, bs + w // 2), 0.0)
        right = (re & (w // 2)) != 0
        dr = xcl - xm
        dl = tot + xm - ct
        qs, ks = [], []
        for e in range(SUB):
            ex = jnp.exp(jnp.where(right, pf[e] + dr, dl - pf[e]))
            qs.append(jnp.where(right, qv[e] * ex, 0.0))
            ks.append(jnp.where(right, 0.0, kv[e] * ex))
        q_lv.append(assemble(qs))
        k_lv.append(assemble(ks))

    sc = jnp.where(masks[0], jnp.sum(q * k, axis=-1, keepdims=True), 0.0)
    for lvl in range(1, N_LEVELS + 1):
        s_l = _dot_nt(q_lv[lvl - 1].astype(BF16), k_lv[lvl - 1].astype(BF16))
        sc = jnp.where(masks[lvl], s_l, sc)

    cum = [pf[e] + xcl for e in range(SUB)]
    qc = assemble([qv[e] * jnp.exp(cum[e]) for e in range(SUB)])
    kd = assemble([kv[e] * jnp.exp(ltot - cum[e]) for e in range(SUB)])
    o = _dot(sc.astype(BF16), vb) + _dot_nt(qc.astype(BF16), st_t.astype(BF16))
    st_new = jnp.exp(ltot[0:1, :]) * st_t + _dot_tn(vb, kd.astype(BF16))
    return o, st_new


def _scan_kernel(n_heads, dk, dv, qf_ref, kf_ref, vf_ref, gf_ref, qb_ref, kb_ref, vb_ref, gb_ref,
                 lv_ref, s0_ref, of_ref, ob_ref, sf_ref, st_ref):
    @pl.when(pl.program_id(1) == 0)
    def _():
        st_ref[...] = s0_ref[...]

    dirs = ((qf_ref, kf_ref, vf_ref, gf_ref, of_ref, False), (qb_ref, kb_ref, vb_ref, gb_ref, ob_ref, True))
    masks = [[lv_ref[d] == lvl for lvl in range(N_LEVELS + 1)] for d in range(2)]
    for h in range(n_heads):
        ks = slice(h * dk, (h + 1) * dk)
        vs = slice(h * dv, (h + 1) * dv)
        for d, (q_ref, k_ref, v_ref, g_ref, o_ref, rev) in enumerate(dirs):
            o, st_new = _scan_chunk_head(q_ref[:, ks], k_ref[:, ks], v_ref[:, vs], g_ref[:, ks],
                                         st_ref[d, vs, :], masks[d], rev)
            o_ref[:, vs] = o
            st_ref[d, vs, :] = st_new
    sf_ref[...] = st_ref[...]


def _gated_scan(q, kf, kb, v, gf, gb, s0, n_heads, bsz):
    seq = q.shape[0]
    wk = q.shape[1] // bsz
    wv = v.shape[1] // bsz
    dk, dv = wk // n_heads, wv // n_heads
    nchunk = seq // CHUNK
    fwd = lambda b, n: (n, b)
    bwd = lambda b, n: (nchunk - 1 - n, b)
    kspec = lambda im: pl.BlockSpec((CHUNK, wk), im)
    vspec = lambda im: pl.BlockSpec((CHUNK, wv), im)
    sspec = pl.BlockSpec((2, None, wv, dk), lambda b, n: (0, b, 0, 0))
    lv = jnp.asarray(np.stack([_level_matrix(False), _level_matrix(True)]))
    return pl.pallas_call(
        functools.partial(_scan_kernel, n_heads, dk, dv),
        grid=(bsz, nchunk),
        in_specs=[kspec(fwd), kspec(fwd), vspec(fwd), kspec(fwd),
                  kspec(bwd), kspec(bwd), vspec(bwd), kspec(bwd),
                  pl.BlockSpec((2, CHUNK, CHUNK), lambda b, n: (0, 0, 0)), sspec],
        out_specs=[vspec(fwd), vspec(bwd), sspec],
        out_shape=[jax.ShapeDtypeStruct((seq, bsz * wv), F32),
                   jax.ShapeDtypeStruct((seq, bsz * wv), F32),
                   jax.ShapeDtypeStruct((2, bsz, wv, dk), F32)],
        scratch_shapes=[pltpu.VMEM((2, wv, dk), F32)],
        compiler_params=_cparams("parallel", "arbitrary"),
        name="gated_scan",
    )(q, kf, v, gf, q, kb, v, gb, lv, s0)


def _bidir_scan(ctx_in, lat_in, n_heads, bsz, need_ctx):
    qc, kfc, kbc, vc, gfc, gbc = ctx_in
    qx, kfx, kbx, vx, gfx, gbx = lat_in
    wk = qc.shape[1] // bsz
    wv = vc.shape[1] // bsz
    s0 = jnp.zeros((2, bsz, wv, wk // n_heads), F32)
    oc_f, oc_b, s_c = _gated_scan(qc, kfc, kbc, vc, gfc, gbc, s0, n_heads, bsz)
    ox_f, ox_b, _ = _gated_scan(qx, kfx, kbx, vx, gfx, gbx, s_c, n_heads, bsz)
    return ((oc_f, oc_b) if need_ctx else None), (ox_f, ox_b)


def _stage_rows(src_ref, stage_ref, src_row0, nrows, src_lane0, stage_row0):
    for jl in range(stage_ref.shape[0]):
        stage_ref[jl, stage_row0:stage_row0 + nrows, :] = (
            src_ref[src_row0:src_row0 + nrows, src_lane0 + jl * LANES:src_lane0 + (jl + 1) * LANES])


def _unstage_rows(stage_ref, dst_ref, stage_row0, nrows, dst_row0, dst_lane0):
    for jl in range(stage_ref.shape[0]):
        dst_ref[dst_row0:dst_row0 + nrows, dst_lane0 + jl * LANES:dst_lane0 + (jl + 1) * LANES] = (
            stage_ref[jl, stage_row0:stage_row0 + nrows, :])


def _permute_chunk_in(stage_ref, dst_ref, base):
    for jl in range(stage_ref.shape[0]):
        for a in range(SUB):
            dst_ref[base + SUBLANES * a:base + SUBLANES * (a + 1), jl * LANES:(jl + 1) * LANES] = (
                stage_ref[jl, pl.ds(base + a, SUBLANES, stride=SUB), :])


def _permute_chunk_out(val, stage_ref, base):
    for jl in range(stage_ref.shape[0]):
        for a in range(SUB):
            stage_ref[jl, pl.ds(base + a, SUBLANES, stride=SUB), :] = (
                val[SUBLANES * a:SUBLANES * (a + 1), jl * LANES:(jl + 1) * LANES])


def _hgrn_epilogue(p, lb, d):
    q = _silu(p[:, :d])
    v = p[:, d:2 * d]
    zf = p[:, 2 * d:3 * d]
    zb = p[:, 3 * d:4 * d]
    og = p[:, 4 * d:]
    lf, lbk = lb[0:1, :], lb[1:2, :]
    gf = jnp.log(lf + (1.0 - lf) * _sigmoid(zf))
    gb = jnp.log(lbk + (1.0 - lbk) * _sigmoid(zb))
    kf = (1.0 - lf) * _sigmoid(-zf)
    kb = (1.0 - lbk) * _sigmoid(-zb)
    return q, kf, kb, v, gf, gb, og


def _hgrn_proj_kernel(nchunk, x_ref, mod_ref, nw_ref, w_ref, lb_ref,
                      q_ref, kf_ref, kb_ref, v_ref, gf_ref, gb_ref, og_ref, stage_ref, xs_ref):
    d = x_ref.shape[1]
    _stage_rows(x_ref, stage_ref, 0, nchunk * CHUNK, 0, 0)
    for c in range(nchunk):
        _permute_chunk_in(stage_ref, xs_ref, c * CHUNK)
    h = _norm_mod(xs_ref[...], nw_ref[...], mod_ref[0:1, :], mod_ref[1:2, :]).astype(BF16)
    outs = _hgrn_epilogue(_dot(h, w_ref[...]), lb_ref[...], d)
    for ref, val in zip((q_ref, kf_ref, kb_ref, v_ref, gf_ref, gb_ref, og_ref), outs):
        ref[...] = val.astype(ref.dtype)


def _hgrn_project(x2, mod_a, norm_w4, layer, w_in, lb, bsz):
    seq = x2.shape[0]
    d = x2.shape[1] // bsz
    nchunk = 2
    tm = nchunk * CHUNK
    row = pl.BlockSpec((tm, d), lambda t, b: (t, b))
    return pl.pallas_call(
        functools.partial(_hgrn_proj_kernel, nchunk),
        grid=(seq // tm, bsz),
        in_specs=[
            row,
            pl.BlockSpec((None, None, None, 3, d), lambda t, b: (layer, b, 1, 0, 0)),
            pl.BlockSpec((None, None, 1, d), lambda t, b: (layer, 1, 0, 0)),
            pl.BlockSpec(w_in.shape, lambda t, b: (0, 0)),
            pl.BlockSpec(lb.shape, lambda t, b: (0, 0)),
        ],
        out_specs=[row] * 7,
        out_shape=[jax.ShapeDtypeStruct((seq, bsz * d), BF16 if i == 3 else F32) for i in range(7)],
        scratch_shapes=[pltpu.VMEM((d // LANES, tm, LANES), F32), pltpu.VMEM((tm, d), F32)],
        compiler_params=_cparams("parallel", "parallel"),
        name="hgrn_project",
    )(x2, mod_a, norm_w4, w_in, lb)


def _gla_epilogue(p, wgate, bgate, dkt, dvt):
    q = p[:, :dkt] * ((dkt // GLA_HEADS) ** -0.5)
    k = p[:, dkt:2 * dkt]
    v = p[:, 2 * dkt:2 * dkt + dvt]
    og = p[:, 2 * dkt + dvt:2 * dkt + 2 * dvt]
    low = p[:, 2 * dkt + 2 * dvt:]
    lg = _log_sigmoid(_dot(low.astype(BF16), wgate) + bgate) * (1.0 / GLA_TAU)
    return q, k, v, og, lg[:, :dkt], lg[:, dkt:]


def _gla_proj_kernel(nchunk, x_ref, mod_ref, nw_ref, w_ref, wg_ref, bg_ref,
                     q_ref, k_ref, v_ref, og_ref, gf_ref, gb_ref, stage_ref, xs_ref):
    dkt = q_ref.shape[1]
    dvt = v_ref.shape[1]
    _stage_rows(x_ref, stage_ref, 0, nchunk * CHUNK, 0, 0)
    for c in range(nchunk):
        _permute_chunk_in(stage_ref, xs_ref, c * CHUNK)
    h = _norm_mod(xs_ref[...], nw_ref[...], mod_ref[0:1, :], mod_ref[1:2, :]).astype(BF16)
    outs = _gla_epilogue(_dot(h, w_ref[...]), wg_ref[...], bg_ref[...], dkt, dvt)
    for ref, val in zip((q_ref, k_ref, v_ref, og_ref, gf_ref, gb_ref), outs):
        ref[...] = val.astype(ref.dtype)


def _gla_proj_colmajor_kernel(bsz, x_ref, mod_ref, nw_ref, w_ref, wg_ref, bg_ref,
                              q_ref, k_ref, v_ref, og_ref, gf_ref, gb_ref, stage_ref, xs_ref):
    rows = x_ref.shape[0]
    d = nw_ref.shape[1]
    dkt = q_ref.shape[1] // bsz
    dvt = v_ref.shape[1] // bsz
    ncol = CHUNK // rows
    for b in range(bsz):
        for c in range(ncol):
            _stage_rows(x_ref, stage_ref, 0, rows, (c * bsz + b) * d, b * CHUNK + c * rows)
        _permute_chunk_in(stage_ref, xs_ref, b * CHUNK)
    for b in range(bsz):
        xb = xs_ref[b * CHUNK:(b + 1) * CHUNK, :]
        h = _norm_mod(xb, nw_ref[...], mod_ref[0, b:b + 1, :], mod_ref[1, b:b + 1, :]).astype(BF16)
        outs = _gla_epilogue(_dot(h, w_ref[...]), wg_ref[...], bg_ref[...], dkt, dvt)
        for ref, val, wdt in zip((q_ref, k_ref, v_ref, og_ref, gf_ref, gb_ref), outs,
                                 (dkt, dkt, dvt, dvt, dkt, dkt)):
            ref[:, b * wdt:(b + 1) * wdt] = val.astype(ref.dtype)


def _gla_project(x2, mod, norm_w4, layer, w_in, wgate, bgate, bsz, dkt, dvt, *, colmajor):
    seq = x2.shape[0]
    d = x2.shape[1] // bsz
    widths = (dkt, dkt, dvt, dvt, dkt, dkt)
    out_shape = [jax.ShapeDtypeStruct((seq, bsz * w), BF16 if i == 2 else F32)
                 for i, w in enumerate(widths)]
    wspecs = lambda im: [pl.BlockSpec(w_in.shape, im), pl.BlockSpec(wgate.shape, im),
                         pl.BlockSpec(bgate.shape, im)]
    if not colmajor:
        nchunk = 2
        tm = nchunk * CHUNK
        return pl.pallas_call(
            functools.partial(_gla_proj_kernel, nchunk),
            grid=(seq // tm, bsz),
            in_specs=[
                pl.BlockSpec((tm, d), lambda t, b: (t, b)),
                pl.BlockSpec((None, None, None, 3, d), lambda t, b: (layer, b, 1, 0, 0)),
                pl.BlockSpec((None, None, 1, d), lambda t, b: (layer, 1, 0, 0)),
            ] + wspecs(lambda t, b: (0, 0)),
            out_specs=[pl.BlockSpec((tm, w), lambda t, b: (t, b)) for w in widths],
            out_shape=out_shape,
            scratch_shapes=[pltpu.VMEM((d // LANES, tm, LANES), F32), pltpu.VMEM((tm, d), F32)],
            compiler_params=_cparams("parallel", "parallel"),
            name="gla_project",
        )(x2, mod, norm_w4, w_in, wgate, bgate)
    rows = seq // GRID_W
    ncol = CHUNK // rows
    xg = x2.reshape(rows, GRID_W * bsz * d)
    return pl.pallas_call(
        functools.partial(_gla_proj_colmajor_kernel, bsz),
        grid=(seq // CHUNK,),
        in_specs=[
            pl.BlockSpec((rows, ncol * bsz * d), lambda n: (0, n)),
            pl.BlockSpec((None, None, 3, bsz, d), lambda n: (layer, 1, 0, 0, 0)),
            pl.BlockSpec((None, None, 1, d), lambda n: (layer, 1, 0, 0)),
        ] + wspecs(lambda n: (0, 0)),
        out_specs=[pl.BlockSpec((CHUNK, bsz * w), lambda n: (n, 0)) for w in widths],
        out_shape=out_shape,
        scratch_shapes=[pltpu.VMEM((d // LANES, bsz * CHUNK, LANES), F32), pltpu.VMEM((bsz * CHUNK, d), F32)],
        compiler_params=_cparams("parallel"),
        name="gla_project_colmajor",
    )(xg, mod, norm_w4, w_in, wgate, bgate)


def _head_out(of, ob, og, gnw, w_out, n_heads, zs_ref):
    dv = of.shape[1] // n_heads
    o = of + ob
    for h in range(n_heads):
        sl = slice(h * dv, (h + 1) * dv)
        zs_ref[:, sl] = (_rms(o[:, sl], gnw) * _silu(og[:, sl])).astype(BF16)
    return _dot(zs_ref[...], w_out)


def _mix_out_kernel(n_heads, nchunk, of_ref, ob_ref, og_ref, x_ref, mod_ref, gnw_ref, w_ref,
                    o_ref, stage_ref, xs_ref, zs_ref):
    _stage_rows(x_ref, stage_ref, 0, nchunk * CHUNK, 0, 0)
    for c in range(nchunk):
        _permute_chunk_in(stage_ref, xs_ref, c * CHUNK)
    y = _head_out(of_ref[...], ob_ref[...], og_ref[...], gnw_ref[...], w_ref[...], n_heads, zs_ref)
    xn = xs_ref[...] + mod_ref[2:3, :] * y
    for c in range(nchunk):
        _permute_chunk_out(xn[c * CHUNK:(c + 1) * CHUNK, :], stage_ref, c * CHUNK)
    _unstage_rows(stage_ref, o_ref, 0, nchunk * CHUNK, 0, 0)


def _mix_out_colmajor_kernel(n_heads, bsz, of_ref, ob_ref, og_ref, x_ref, mod_ref, gnw_ref, w_ref,
                             o_ref, stage_ref, xs_ref, zs_ref):
    rows = x_ref.shape[0]
    d = w_ref.shape[1]
    wv = of_ref.shape[1] // bsz
    ncol = CHUNK // rows
    for b in range(bsz):
        for c in range(ncol):
            _stage_rows(x_ref, stage_ref, 0, rows, (c * bsz + b) * d, b * CHUNK + c * rows)
        _permute_chunk_in(stage_ref, xs_ref, b * CHUNK)
    for b in range(bsz):
        ws = slice(b * wv, (b + 1) * wv)
        y = _head_out(of_ref[:, ws], ob_ref[:, ws], og_ref[:, ws], gnw_ref[...], w_ref[...],
                      n_heads, zs_ref)
        xn = xs_ref[b * CHUNK:(b + 1) * CHUNK, :] + mod_ref[2, b:b + 1, :] * y
        _permute_chunk_out(xn, stage_ref, b * CHUNK)
    for b in range(bsz):
        for c in range(ncol):
            _unstage_rows(stage_ref, o_ref, b * CHUNK + c * rows, rows, 0, (c * bsz + b) * d)


def _mix_out(of, ob, og, x2, mod, layer, gnw, w_out, n_heads, bsz, *, colmajor):
    seq = x2.shape[0]
    d = x2.shape[1] // bsz
    wv = of.shape[1] // bsz
    gnw = gnw.reshape(1, -1)
    if not colmajor:
        nchunk = 2
        tm = nchunk * CHUNK
        ospec = pl.BlockSpec((tm, wv), lambda t, b: (t, b))
        xspec = pl.BlockSpec((tm, d), lambda t, b: (t, b))
        return pl.pallas_call(
            functools.partial(_mix_out_kernel, n_heads, nchunk),
            grid=(seq // tm, bsz),
            in_specs=[ospec, ospec, ospec, xspec,
                      pl.BlockSpec((None, None, None, 3, d), lambda t, b: (layer, b, 1, 0, 0)),
                      pl.BlockSpec(gnw.shape, lambda t, b: (0, 0)),
                      pl.BlockSpec(w_out.shape, lambda t, b: (0, 0))],
            out_specs=xspec,
            out_shape=jax.ShapeDtypeStruct((seq, bsz * d), F32),
            scratch_shapes=[pltpu.VMEM((d // LANES, tm, LANES), F32), pltpu.VMEM((tm, d), F32),
                            pltpu.VMEM((tm, wv), BF16)],
            compiler_params=_cparams("parallel", "parallel"),
            name="mix_out",
        )(of, ob, og, x2, mod, gnw, w_out)
    rows = seq // GRID_W
    ncol = CHUNK // rows
    xg = x2.reshape(rows, GRID_W * bsz * d)
    ospec = pl.BlockSpec((CHUNK, bsz * wv), lambda n: (n, 0))
    xspec = pl.BlockSpec((rows, ncol * bsz * d), lambda n: (0, n))
    out = pl.pallas_call(
        functools.partial(_mix_out_colmajor_kernel, n_heads, bsz),
        grid=(seq // CHUNK,),
        in_specs=[ospec, ospec, ospec, xspec,
                  pl.BlockSpec((None, None, 3, bsz, d), lambda n: (layer, 1, 0, 0, 0)),
                  pl.BlockSpec(gnw.shape, lambda n: (0, 0)),
                  pl.BlockSpec(w_out.shape, lambda n: (0, 0))],
        out_specs=xspec,
        out_shape=jax.ShapeDtypeStruct(xg.shape, F32),
        scratch_shapes=[pltpu.VMEM((d // LANES, bsz * CHUNK, LANES), F32), pltpu.VMEM((bsz * CHUNK, d), F32),
                        pltpu.VMEM((CHUNK, wv), BF16)],
        compiler_params=_cparams("parallel"),
        name="mix_out_colmajor",
    )(of, ob, og, xg, mod, gnw, w_out)
    return out.reshape(seq, bsz * d)


def _gla_mixer(xc2, x2, mods, norm_w4, layer, prm, bsz, need_ctx):
    (modc_a, modc_b), (modx_a, modx_b) = mods
    w_in, wgate, bgate, gnw, w_out, dkt, dvt = prm
    qc, kc, vc, ogc, gfc, gbc = _gla_project(xc2, modc_a, norm_w4, layer, w_in, wgate, bgate, bsz,
                                             dkt, dvt, colmajor=False)
    qx, kx, vx, ogx, gfx, gbx = _gla_project(x2, modx_b, norm_w4, layer, w_in, wgate, bgate, bsz,
                                             dkt, dvt, colmajor=True)
    oc, ox = _bidir_scan((qc, kc, kc, vc, gfc, gbc), (qx, kx, kx, vx, gfx, gbx), GLA_HEADS, bsz, need_ctx)
    x_new = _mix_out(ox[0], ox[1], ogx, x2, modx_b, layer, gnw, w_out, GLA_HEADS, bsz, colmajor=True)
    xc_new = None
    if need_ctx:
        xc_new = _mix_out(oc[0], oc[1], ogc, xc2, modc_a, layer, gnw, w_out, GLA_HEADS, bsz, colmajor=False)
    return xc_new, x_new


def _hgrn_mixer(xc2, x2, mods, norm_w4, layer, prm, bsz, need_ctx):
    (modc_a, _), (modx_a, _) = mods
    w_in, lb, gnw, w_out, n_heads = prm
    qc, kfc, kbc, vc, gfc, gbc, ogc = _hgrn_project(xc2, modc_a, norm_w4, layer, w_in, lb, bsz)
    qx, kfx, kbx, vx, gfx, gbx, ogx = _hgrn_project(x2, modx_a, norm_w4, layer, w_in, lb, bsz)
    oc, ox = _bidir_scan((qc, kfc, kbc, vc, gfc, gbc), (qx, kfx, kbx, vx, gfx, gbx), n_heads, bsz, need_ctx)
    x_new = _mix_out(ox[0], ox[1], ogx, x2, modx_a, layer, gnw, w_out, n_heads, bsz, colmajor=False)
    xc_new = None
    if need_ctx:
        xc_new = _mix_out(oc[0], oc[1], ogc, xc2, modc_a, layer, gnw, w_out, n_heads, bsz, colmajor=False)
    return xc_new, x_new


def _forward(x, c, ctx, c_ctx, ada_w, ada_b, norm_w, ffn_w_in, ffn_w_out,
             s5_a_re, s5_a_im, s5_log_step, s5_b_re, s5_b_im, s5_c_re, s5_c_im, s5_d,
             s5_w_glu, s5_b_glu, gla_w_in, gla_w_gate, gla_b_gate, gla_norm_w, gla_w_out,
             hgrn_w_in, hgrn_lb_logits, hgrn_norm_w, hgrn_w_out, final_norm_w, depth=None):
    depth = ada_w.shape[0] if depth is None else depth
    bsz, seq, d = x.shape
    mods_x, mods_c = _ada(c, c_ctx, ada_w, ada_b)
    norm_w4 = norm_w.reshape(norm_w.shape[0], 3, 1, d)
    w_in = ffn_w_in.astype(BF16)
    w_out = ffn_w_out.astype(BF16)
    lb_soft = jax.nn.softmax(hgrn_lb_logits.astype(F32), axis=0)
    lb_all = jnp.cumsum(lb_soft, axis=0) - lb_soft[0]

    xc2 = jnp.transpose(ctx, (1, 0, 2)).reshape(ctx.shape[1], bsz * d)
    x2 = x
    for i in range(depth):
        last = i == depth - 1
        kind, j = i % N_MIXERS, i // N_MIXERS
        x2 = _ffn(x2, mods_x[0], norm_w4, w_in, w_out, i, 0, bsz, in_bld=(i == 0))
        xc2 = _ffn(xc2, mods_c[0], norm_w4, w_in, w_out, i, 0, bsz)
        mods = (mods_c, mods_x)
        if kind == 0:
            are, aim, wb, wc = _s5_prep(s5_a_re[j], s5_a_im[j], s5_log_step[j], s5_b_re[j], s5_b_im[j],
                                        s5_c_re[j], s5_c_im[j], bsz)
            prm = (are, aim, wb, wc, s5_d[j], s5_w_glu[j].astype(BF16), s5_b_glu[j])
            yc, x2 = _s5_mixer(xc2, x2, (mods_c[1], mods_x[1]), norm_w4, i, prm, bsz, not last)
        elif kind == 1:
            dkt = d // 2
            dvt = d
            n_low = 2 * GLA_RANK
            pad = LANES - n_low
            wi = jnp.pad(gla_w_in[j], ((0, 0), (0, pad))).astype(BF16)
            wg = jnp.zeros((LANES, 2 * dkt), F32)
            wg = wg.at[:GLA_RANK, :dkt].set(gla_w_gate[j, 0]).at[GLA_RANK:n_low, dkt:].set(gla_w_gate[j, 1])
            prm = (wi, wg.astype(BF16), gla_b_gate[j].reshape(1, 2 * dkt), gla_norm_w[j],
                   gla_w_out[j].astype(BF16), dkt, dvt)
            yc, x2 = _gla_mixer(xc2, x2, mods, norm_w4, i, prm, bsz, not last)
        else:
            prm = (hgrn_w_in[j].astype(BF16), lb_all[i], hgrn_norm_w[j], hgrn_w_out[j].astype(BF16),
                   d // HGRN_DK)
            yc, x2 = _hgrn_mixer(xc2, x2, mods, norm_w4, i, prm, bsz, not last)
        x2 = _ffn(x2, mods_x[0], norm_w4, w_in, w_out, i, 2, bsz,
                  final_w=final_norm_w.reshape(1, d) if last else None)
        if not last:
            xc2 = _ffn(yc, mods_c[0], norm_w4, w_in, w_out, i, 2, bsz)
    return x2


def kernel(x, c, ctx, c_ctx, ada_w, ada_b, norm_w, ffn_w_in, ffn_w_out, s5_a_re, s5_a_im, s5_log_step,
           s5_b_re, s5_b_im, s5_c_re, s5_c_im, s5_d, s5_w_glu, s5_b_glu, gla_w_in, gla_w_gate,
           gla_b_gate, gla_norm_w, gla_w_out, hgrn_w_in, hgrn_lb_logits, hgrn_norm_w, hgrn_w_out,
           final_norm_w):
    return _forward(x, c, ctx, c_ctx, ada_w, ada_b, norm_w, ffn_w_in, ffn_w_out, s5_a_re, s5_a_im,
                    s5_log_step, s5_b_re, s5_b_im, s5_c_re, s5_c_im, s5_d, s5_w_glu, s5_b_glu,
                    gla_w_in, gla_w_gate, gla_b_gate, gla_norm_w, gla_w_out, hgrn_w_in,
                    hgrn_lb_logits, hgrn_norm_w, hgrn_w_out, final_norm_w)
```

```python
import functools
import math

import numpy as np
import jax
import jax.numpy as jnp
from jax import lax
from jax.experimental import pallas as pl
from jax.experimental.pallas import tpu as pltpu

F32 = jnp.float32
BF16 = jnp.bfloat16

NORM_EPS = 1e-6
GRID_W = 64
N_MOD = 9
N_MIXERS = 3

S5_GROUP = 16
S5_STATE = 64
S5_LANE_GROUPS = 8
S5_TB = 64
S5_TS = 64

GLA_HEADS = 4
GLA_RANK = 16
GLA_TAU = 16.0
HGRN_DK = 128

SUBLANES = 8
LANES = 128
CHUNK = 128
GROUP = SUBLANES * SUBLANES
N_GROUPS = CHUNK // GROUP
SUB = CHUNK // SUBLANES
N_FINE = 3
N_COARSE = 3
N_LEVELS = 7
LOG2E = 1.4426950408889634

FFN_CHUNK = 768

VMEM_LIMIT_BYTES = 56 * 1024 * 1024


def _cparams(*sem):
    return pltpu.CompilerParams(dimension_semantics=sem, vmem_limit_bytes=VMEM_LIMIT_BYTES)


def _dot(a, b):
    return jnp.dot(a, b, preferred_element_type=F32)


def _dot_nt(a, b):
    return lax.dot_general(a, b, (((1,), (1,)), ((), ())), preferred_element_type=F32)


def _dot_tn(a, b):
    return lax.dot_general(a, b, (((0,), (0,)), ((), ())), preferred_element_type=F32)


def _sigmoid(x):
    return 1.0 / (1.0 + jnp.exp(-x))


def _silu(x):
    return x * _sigmoid(x)


def _gelu_tanh(x):
    c = math.sqrt(2.0 / math.pi)
    return 0.5 * x * (1.0 + jnp.tanh(c * (x + 0.044715 * (x * x * x))))


def _log_sigmoid(x):
    return jnp.minimum(x, 0.0) - jnp.log1p(jnp.exp(-jnp.abs(x)))


def _rms(x, w):
    ms = jnp.mean(x * x, axis=-1, keepdims=True)
    return (x * lax.rsqrt(ms + NORM_EPS)) * w


def _norm_mod(x, w, shift, scale):
    return _rms(x, w) * (1.0 + scale) + shift


def _ada_kernel(cc_ref, w_ref, b_ref, o_ref):
    a = _silu(cc_ref[...]).astype(BF16)
    o_ref[...] = _dot(a, w_ref[...].astype(BF16)) + b_ref[...]


def _ada(c, c_ctx, ada_w, ada_b):
    depth, d, nd = ada_w.shape
    bsz = c.shape[0]
    rows = ((bsz + 1 + SUBLANES - 1) // SUBLANES) * SUBLANES
    cc = jnp.zeros((rows, d), F32).at[:bsz].set(c).at[bsz].set(c_ctx)
    nblk = nd // d
    mod = pl.pallas_call(
        _ada_kernel,
        grid=(depth, nblk),
        in_specs=[
            pl.BlockSpec((rows, d), lambda i, n: (0, 0)),
            pl.BlockSpec((None, d, d), lambda i, n: (i, 0, n)),
            pl.BlockSpec((None, 1, d), lambda i, n: (i, 0, n)),
        ],
        out_specs=pl.BlockSpec((None, rows, d), lambda i, n: (i, 0, n)),
        out_shape=jax.ShapeDtypeStruct((depth, rows, nd), F32),
        compiler_params=_cparams("arbitrary", "arbitrary"),
        name="ada_mod",
    )(cc, ada_w, ada_b.reshape(depth, 1, nd))
    mx = mod[:, :bsz].reshape(depth, bsz, 3, 3, d)
    mc = jnp.broadcast_to(mod[:, bsz:bsz + 1], (depth, bsz, nd)).reshape(depth, bsz, 3, 3, d)
    return (mx, jnp.transpose(mx, (0, 2, 3, 1, 4))), (mc, jnp.transpose(mc, (0, 2, 3, 1, 4)))


def _ffn_kernel(chunks, final, x_ref, mod_ref, nw_ref, wi_ref, wo_ref, *rest):
    if final:
        fw_ref, o_ref = rest
    else:
        (o_ref,) = rest
    ff = wo_ref.shape[0]
    x = x_ref[...]
    h = _norm_mod(x, nw_ref[...], mod_ref[0:1, :], mod_ref[1:2, :]).astype(BF16)
    acc = None
    for lo, hi in chunks:
        g = _dot(h, wi_ref[:, lo:hi])
        u = _dot(h, wi_ref[:, ff + lo:ff + hi])
        part = _dot((_silu(g) * u).astype(BF16), wo_ref[lo:hi, :])
        acc = part if acc is None else acc + part
    y = x + (0.5 * mod_ref[2:3, :]) * acc
    if final:
        y = _rms(y, fw_ref[...])
    o_ref[...] = y


def _ff_chunks(ff, width):
    assert ff % LANES == 0 and width % LANES == 0
    return tuple((lo, min(lo + width, ff)) for lo in range(0, ff, width))


def _ffn(x, mod_a, norm_w4, w_in, w_out, layer, stage, bsz, *, in_bld=False, final_w=None):
    d = norm_w4.shape[-1]
    seq = x.shape[1] if in_bld else x.shape[0]
    ff = w_out.shape[2]
    tm = 512 if seq % 512 == 0 else 256
    s = 0 if stage == 0 else 1
    final = final_w is not None
    resident = pl.Buffered(1)
    if in_bld:
        x_spec = pl.BlockSpec((None, tm, d), lambda t, b: (b, t, 0))
    else:
        x_spec = pl.BlockSpec((tm, d), lambda t, b: (t, b))
    in_specs = [
        x_spec,
        pl.BlockSpec((None, None, None, 3, d), lambda t, b: (layer, b, stage, 0, 0)),
        pl.BlockSpec((None, None, 1, d), lambda t, b: (layer, stage, 0, 0)),
        pl.BlockSpec((None, None, d, 2 * ff), lambda t, b: (layer, s, 0, 0), pipeline_mode=resident),
        pl.BlockSpec((None, None, ff, d), lambda t, b: (layer, s, 0, 0), pipeline_mode=resident),
    ]
    args = [x, mod_a, norm_w4, w_in, w_out]
    if final:
        in_specs.append(pl.BlockSpec((1, d), lambda t, b: (0, 0)))
        args.append(final_w)
        out_spec = pl.BlockSpec((None, tm, d), lambda t, b: (b, t, 0))
        out_shape = jax.ShapeDtypeStruct((bsz, seq, d), F32)
    else:
        out_spec = pl.BlockSpec((tm, d), lambda t, b: (t, b))
        out_shape = jax.ShapeDtypeStruct((seq, bsz * d), F32)
    return pl.pallas_call(
        functools.partial(_ffn_kernel, _ff_chunks(ff, FFN_CHUNK), final),
        grid=(seq // tm, bsz),
        in_specs=in_specs,
        out_specs=out_spec,
        out_shape=out_shape,
        compiler_params=_cparams("parallel", "parallel"),
        name="ffn_half",
    )(*args)


def _s5_prep(a_re, a_im, log_step, b_re, b_im, c_re, c_im, bsz):
    dt = jnp.exp(log_step.astype(F32))[..., None]
    a_re = a_re.astype(F32)
    a_im = a_im.astype(F32)
    mag = jnp.exp(a_re * dt)
    abar_re = mag * jnp.cos(a_im * dt)
    abar_im = mag * jnp.sin(a_im * dt)
    den = a_re * a_re + a_im * a_im
    zr = abar_re - 1.0
    zi = abar_im
    coef_re = ((zr * a_re + zi * a_im) / den)[..., None]
    coef_im = ((zi * a_re - zr * a_im) / den)[..., None]
    b_re = b_re.astype(F32)
    b_im = b_im.astype(F32)
    bbar_re = coef_re * b_re - coef_im * b_im
    bbar_im = coef_re * b_im + coef_im * b_re
    n_dir, g, p = abar_re.shape
    gc = b_re.shape[-1]
    nj = g // S5_LANE_GROUPS
    lg = S5_LANE_GROUPS
    are = jnp.broadcast_to(abar_re.reshape(n_dir, 1, g * p), (n_dir, bsz, g * p))
    aim = jnp.broadcast_to(abar_im.reshape(n_dir, 1, g * p), (n_dir, bsz, g * p))
    eye = jnp.eye(lg, dtype=F32)
    bb = jnp.stack([bbar_re, bbar_im], axis=1).reshape(n_dir, 2, nj, lg, p, gc)
    bb = jnp.transpose(bb, (0, 2, 3, 5, 1, 4))
    wb = bb[:, :, :, :, :, None, :] * eye[None, None, :, None, None, :, None]
    wb = wb.reshape(n_dir, nj, lg * gc, 2 * lg * p).astype(BF16)
    cc = jnp.stack([c_re.astype(F32), -c_im.astype(F32)], axis=1).reshape(n_dir, 2, nj, lg, gc, p)
    cc = jnp.transpose(cc, (0, 2, 1, 3, 5, 4))
    wc = cc[:, :, :, :, :, None, :] * eye[None, None, None, :, None, :, None]
    wc = wc.reshape(n_dir, nj, 2 * lg * p, lg * gc).astype(BF16)
    return are, aim, wb, wc


def _s5_kernel(tb, rev, finish, x_ref, mod_ref, nw_ref, are_ref, aim_ref, wb_ref, wc_ref,
               h0re_ref, h0im_ref, *rest):
    n_io = 7 if finish else 3
    io, scratch = rest[:n_io], rest[n_io:]
    if finish:
        yf_ref, dsk_ref, wglu_ref, bglu_ref, o_ref, hre_ref, him_ref = io
        h_s = scratch[12]
    else:
        o_ref, hre_ref, him_ref = io
    bu = (scratch[0:2], scratch[2:4])
    hs = (scratch[4:6], scratch[6:8])
    hs_re, hs_im, y_s, hb_s = scratch[8:12]
    bsz = x_ref.shape[1]
    d = x_ref.shape[2]
    nj = wb_ref.shape[0]
    sw = wb_ref.shape[2] // 2
    ri = S5_TS * bsz
    nit = tb // S5_TS

    @pl.when(pl.program_id(0) == 0)
    def _():
        hs_re[...] = h0re_ref[...]
        hs_im[...] = h0im_ref[...]

    x3 = x_ref[...]
    ms = jnp.mean(x3 * x3, axis=-1, keepdims=True)
    h3 = (x3 * lax.rsqrt(ms + NORM_EPS)) * nw_ref[...] * (1.0 + mod_ref[1]) + mod_ref[0]
    h2 = h3.reshape(tb * bsz, d)
    if finish:
        h_s[...] = h2
    hb_s[...] = h2.astype(BF16)

    def lanes(j):
        return slice(j * LANES, (j + 1) * LANES)

    def drive(j, rows):
        r = _dot(hb_s[rows, lanes(j)], wb_ref[j])
        bu[j % 2][0][rows, :] = r[:, :sw]
        bu[j % 2][1][rows, :] = r[:, sw:]

    def readout(j, rows):
        y_s[rows, lanes(j)] = (_dot(hs[j % 2][0][rows, :].astype(BF16), wc_ref[j, :sw, :])
                               + _dot(hs[j % 2][1][rows, :].astype(BF16), wc_ref[j, sw:, :]))

    drive(0, slice(None))
    for j in range(nj):
        cur = j % 2
        ar = are_ref[:, j * sw:(j + 1) * sw]
        ai = aim_ref[:, j * sw:(j + 1) * sw]

        def body(i, carry, j=j, cur=cur, ar=ar, ai=ai):
            hr, hi = carry
            rows = pl.ds(pl.multiple_of(i * ri, ri), ri)
            if j >= 1:
                readout(j - 1, rows)
            if j + 1 < nj:
                drive(j + 1, rows)
            for s in range(S5_TS):
                t = i * S5_TS + s
                tt = (tb - 1 - t) if rev else t
                sl = pl.ds(pl.multiple_of(tt * bsz, bsz), bsz)
                hr, hi = (ar * hr - ai * hi + bu[cur][0][sl, :], ar * hi + ai * hr + bu[cur][1][sl, :])
                hs[cur][0][sl, :] = hr
                hs[cur][1][sl, :] = hi
            return hr, hi

        hr, hi = lax.fori_loop(
            0, nit, body, (hs_re[:, j * sw:(j + 1) * sw], hs_im[:, j * sw:(j + 1) * sw]))
        hs_re[:, j * sw:(j + 1) * sw] = hr
        hs_im[:, j * sw:(j + 1) * sw] = hi
    readout(nj - 1, slice(None))

    hre_ref[...] = hs_re[...]
    him_ref[...] = hs_im[...]
    if not finish:
        o_ref[...] = y_s[...].reshape(tb, bsz, d)
    else:
        y = y_s[...] + yf_ref[...].reshape(tb * bsz, d) + dsk_ref[...] * h_s[...]
        z = _dot(_gelu_tanh(y).astype(BF16), wglu_ref[...]) + bglu_ref[...]
        out = z[:, :d] * _sigmoid(z[:, d:])
        o_ref[...] = x3 + mod_ref[2] * out.reshape(tb, bsz, d)


def _s5_pass(x2, mod_b, norm_w4, layer, are, aim, wb, wc, h0, bsz, *, rev, fin=None):
    seq = x2.shape[0]
    d = x2.shape[1] // bsz
    tb = S5_TB
    nblk = seq // tb
    dirn = 1 if rev else 0
    nstate = are.shape[-1]
    sw = wb.shape[3] // 2
    x3 = x2.reshape(seq, bsz, d)
    blk = (lambda i: (nblk - 1 - i, 0, 0)) if rev else (lambda i: (i, 0, 0))
    x_spec = pl.BlockSpec((tb, bsz, d), blk)
    const2 = lambda i: (0, 0)
    in_specs = [
        x_spec,
        pl.BlockSpec((None, None, 3, bsz, d), lambda i: (layer, 1, 0, 0, 0)),
        pl.BlockSpec((None, None, 1, d), lambda i: (layer, 1, 0, 0)),
        pl.BlockSpec((None, bsz, nstate), lambda i: (dirn, 0, 0)),
        pl.BlockSpec((None, bsz, nstate), lambda i: (dirn, 0, 0)),
        pl.BlockSpec((None,) + wb.shape[1:], lambda i: (dirn, 0, 0, 0)),
        pl.BlockSpec((None,) + wc.shape[1:], lambda i: (dirn, 0, 0, 0)),
        pl.BlockSpec((bsz, nstate), const2),
        pl.BlockSpec((bsz, nstate), const2),
    ]
    args = [x3, mod_b, norm_w4, are, aim, wb, wc, h0[0], h0[1]]
    scratch = [pltpu.VMEM((tb * bsz, sw), F32) for _ in range(8)] + [
        pltpu.VMEM((bsz, nstate), F32), pltpu.VMEM((bsz, nstate), F32),
        pltpu.VMEM((tb * bsz, d), F32), pltpu.VMEM((tb * bsz, d), BF16),
    ]
    finish = fin is not None
    if finish:
        y_fwd, d_skip, w_glu, b_glu = fin
        in_specs += [
            x_spec,
            pl.BlockSpec((1, d), const2),
            pl.BlockSpec(w_glu.shape, const2),
            pl.BlockSpec((1, 2 * d), const2),
        ]
        args += [y_fwd.reshape(seq, bsz, d), d_skip.reshape(1, d), w_glu, b_glu.reshape(1, 2 * d)]
        scratch.append(pltpu.VMEM((tb * bsz, d), F32))
    out, hre, him = pl.pallas_call(
        functools.partial(_s5_kernel, tb, rev, finish),
        grid=(nblk,),
        in_specs=in_specs,
        out_specs=[x_spec, pl.BlockSpec((bsz, nstate), const2), pl.BlockSpec((bsz, nstate), const2)],
        out_shape=[jax.ShapeDtypeStruct((seq, bsz, d), F32),
                   jax.ShapeDtypeStruct((bsz, nstate), F32),
                   jax.ShapeDtypeStruct((bsz, nstate), F32)],
        scratch_shapes=scratch,
        compiler_params=_cparams("arbitrary"),
        name="s5_bwd_finish" if finish else "s5_fwd",
    )(*args)
    return out.reshape(seq, bsz * d), (hre, him)


def _s5_mixer(xc2, x2, mods_b, norm_w4, layer, prm, bsz, need_ctx):
    (modc_b, modx_b) = mods_b
    are, aim, wb, wc, d_skip, w_glu, b_glu = prm
    nstate = are.shape[-1]
    zero = (jnp.zeros((bsz, nstate), F32), jnp.zeros((bsz, nstate), F32))
    ycf, hf = _s5_pass(xc2, modc_b, norm_w4, layer, are, aim, wb, wc, zero, bsz, rev=False)
    yxf, _ = _s5_pass(x2, modx_b, norm_w4, layer, are, aim, wb, wc, hf, bsz, rev=False)
    xc_new, hb = _s5_pass(xc2, modc_b, norm_w4, layer, are, aim, wb, wc, zero, bsz, rev=True,
                          fin=(ycf, d_skip, w_glu, b_glu))
    x_new, _ = _s5_pass(x2, modx_b, norm_w4, layer, are, aim, wb, wc, hb, bsz, rev=True,
                        fin=(yxf, d_skip, w_glu, b_glu))
    return (xc_new if need_ctx else None), x_new


def _level_matrix(rev):
    p = np.arange(CHUNK)
    v, r = p // SUBLANES, p % SUBLANES
    tau = GROUP * (v // SUBLANES) + SUBLANES * r + v % SUBLANES
    if rev:
        tau = CHUNK - 1 - tau
    ti, tj = tau[:, None], tau[None, :]
    x = ti ^ tj
    lvl = np.zeros((CHUNK, CHUNK), np.int32)
    for b in range(N_LEVELS):
        lvl = np.where((x >> b) & 1, b + 1, lvl)
    lvl = np.where(tj > ti, -1, lvl)
    return lvl.astype(np.int32)


def _scan_chunk_head(q, k, vb, g, st_t, masks, rev):
    dk = q.shape[1]
    ng, nv = N_GROUPS, SUBLANES
    ea = (lambda e: SUB - 1 - e) if rev else (lambda e: e)
    ksub = (lambda e: SUBLANES - 1 - e) if rev else (lambda e: e)

    def vregs(x, scale=None):
        out = [x[SUBLANES * ea(e):SUBLANES * ea(e) + SUBLANES, :] for e in range(SUB)]
        if scale is not None:
            out = [o * scale for o in out]
        return [out[nv * gi:nv * (gi + 1)] for gi in range(ng)]

    def assemble(vs):
        flat = [vs[gi][a] for gi in range(ng) for a in range(nv)]
        return jnp.concatenate([flat[ea(a)] for a in range(SUB)], axis=0)

    def zeros():
        return [[zero] * nv for _ in range(ng)]

    qv, kv, gv = vregs(q), vregs(k), vregs(g, LOG2E)
    zero = jnp.zeros((SUBLANES, dk), F32)
    ninf = jnp.full((SUBLANES, dk), -jnp.inf, F32)

    sub_i = lax.broadcasted_iota(jnp.int32, (SUBLANES, dk), 0)
    re = (SUBLANES - 1 - sub_i) if rev else sub_i

    def row(x, e):
        kk = ksub(e)
        return jnp.broadcast_to(x[kk:kk + 1, :], (SUBLANES, dk))

    pf, tot, ct, xcl, gtot = [], [], [], [], []
    for gi in range(ng):
        p = [gv[gi][0]]
        for a in range(1, nv):
            p.append(p[-1] + gv[gi][a])
        t = p[nv - 1]
        c = zero
        for e in range(SUBLANES):
            c = c + jnp.where(re >= e, row(t, e), 0.0)
        pf.append(p)
        tot.append(t)
        ct.append(c)
        xcl.append(c - t)
        gtot.append(row(c, SUBLANES - 1))

    q_lv, k_lv = [], []
    for lvl in range(1, N_FINE + 1):
        s = 1 << lvl
        qs, ks = zeros(), zeros()
        for gi in range(ng):
            for bs in range(0, nv, s):
                m = bs + s // 2
                ref = pf[gi][m - 1]
                for a in range(bs, m - 1):
                    ks[gi][a] = kv[gi][a] * jnp.exp2(ref - pf[gi][a])
                ks[gi][m - 1] = kv[gi][m - 1]
                for a in range(m, bs + s):
                    qs[gi][a] = qv[gi][a] * jnp.exp2(pf[gi][a] - ref)
        q_lv.append(assemble(qs))
        k_lv.append(assemble(ks))
    for lvl in range(N_FINE + 1, N_FINE + N_COARSE + 1):
        w = 1 << (lvl - N_FINE)
        right = (re & (w // 2)) != 0
        qs, ks = zeros(), zeros()
        for gi in range(ng):
            xm = zero
            for bs in range(0, SUBLANES, w):
                inblk = (re >= bs) & (re < bs + w)
                xm = xm + jnp.where(inblk, row(xcl[gi], bs + w // 2), 0.0)
            drq = jnp.where(right, xcl[gi] - xm, ninf)
            dlk = jnp.where(right, ninf, tot[gi] + xm - ct[gi])
            for a in range(nv):
                qs[gi][a] = qv[gi][a] * jnp.exp2(pf[gi][a] + drq)
                ks[gi][a] = kv[gi][a] * jnp.exp2(dlk - pf[gi][a])
        q_lv.append(assemble(qs))
        k_lv.append(assemble(ks))
    qs, ks = zeros(), zeros()
    for a in range(nv):
        ks[0][a] = kv[0][a] * jnp.exp2(tot[0] + (gtot[0] - ct[0]) - pf[0][a])
        qs[1][a] = qv[1][a] * jnp.exp2(pf[1][a] + xcl[1])
    q_lv.append(assemble(qs))
    k_lv.append(assemble(ks))

    sc = jnp.where(masks[0], jnp.sum(q * k, axis=-1, keepdims=True), 0.0)
    for lvl in range(1, N_LEVELS + 1):
        s_l = _dot_nt(q_lv[lvl - 1].astype(BF16), k_lv[lvl - 1].astype(BF16))
        sc = jnp.where(masks[lvl], s_l, sc)

    e0, e1 = jnp.exp2(gtot[0]), jnp.exp2(gtot[1])
    qc = zeros()
    kd = zeros()
    for a in range(nv):
        qc[0][a] = qv[0][a] * jnp.exp2(pf[0][a] + xcl[0])
        qc[1][a] = qs[1][a] * e0
        kd[0][a] = ks[0][a] * e1
        kd[1][a] = kv[1][a] * jnp.exp2(tot[1] + (gtot[1] - ct[1]) - pf[1][a])
    o = _dot(sc.astype(BF16), vb) + _dot_nt(assemble(qc).astype(BF16), st_t.astype(BF16))
    st_new = (e0 * e1)[0:1, :] * st_t + _dot_tn(vb, assemble(kd).astype(BF16))
    return o, st_new


def _scan_kernel(n_heads, dk, dv, qf_ref, kf_ref, vf_ref, gf_ref, qb_ref, kb_ref, vb_ref, gb_ref,
                 lv_ref, s0_ref, of_ref, ob_ref, sf_ref, st_ref):
    @pl.when(pl.program_id(1) == 0)
    def _():
        st_ref[...] = s0_ref[...]

    dirs = ((qf_ref, kf_ref, vf_ref, gf_ref, of_ref, False), (qb_ref, kb_ref, vb_ref, gb_ref, ob_ref, True))
    masks = [[lv_ref[d] == lvl for lvl in range(N_LEVELS + 1)] for d in range(2)]
    for h in range(n_heads):
        ks = slice(h * dk, (h + 1) * dk)
        vs = slice(h * dv, (h + 1) * dv)
        for d, (q_ref, k_ref, v_ref, g_ref, o_ref, rev) in enumerate(dirs):
            o, st_new = _scan_chunk_head(q_ref[:, ks], k_ref[:, ks], v_ref[:, vs], g_ref[:, ks],
                                         st_ref[d, vs, :], masks[d], rev)
            o_ref[:, vs] = o
            st_ref[d, vs, :] = st_new
    sf_ref[...] = st_ref[...]


def _gated_scan(q, kf, kb, v, gf, gb, s0, n_heads, bsz):
    seq = q.shape[0]
    wk = q.shape[1] // bsz
    wv = v.shape[1] // bsz
    dk, dv = wk // n_heads, wv // n_heads
    nchunk = seq // CHUNK
    fwd = lambda b, n: (n, b)
    bwd = lambda b, n: (nchunk - 1 - n, b)
    kspec = lambda im: pl.BlockSpec((CHUNK, wk), im)
    vspec = lambda im: pl.BlockSpec((CHUNK, wv), im)
    sspec = pl.BlockSpec((2, None, wv, dk), lambda b, n: (0, b, 0, 0))
    lv = jnp.asarray(np.stack([_level_matrix(False), _level_matrix(True)]))
    return pl.pallas_call(
        functools.partial(_scan_kernel, n_heads, dk, dv),
        grid=(bsz, nchunk),
        in_specs=[kspec(fwd), kspec(fwd), vspec(fwd), kspec(fwd),
                  kspec(bwd), kspec(bwd), vspec(bwd), kspec(bwd),
                  pl.BlockSpec((2, CHUNK, CHUNK), lambda b, n: (0, 0, 0)), sspec],
        out_specs=[vspec(fwd), vspec(bwd), sspec],
        out_shape=[jax.ShapeDtypeStruct((seq, bsz * wv), F32),
                   jax.ShapeDtypeStruct((seq, bsz * wv), F32),
                   jax.ShapeDtypeStruct((2, bsz, wv, dk), F32)],
        scratch_shapes=[pltpu.VMEM((2, wv, dk), F32)],
        compiler_params=_cparams("parallel", "arbitrary"),
        name="gated_scan",
    )(q, kf, v, gf, q, kb, v, gb, lv, s0)


def _bidir_scan(ctx_in, lat_in, n_heads, bsz, need_ctx):
    qc, kfc, kbc, vc, gfc, gbc = ctx_in
    qx, kfx, kbx, vx, gfx, gbx = lat_in
    wk = qc.shape[1] // bsz
    wv = vc.shape[1] // bsz
    s0 = jnp.zeros((2, bsz, wv, wk // n_heads), F32)
    oc_f, oc_b, s_c = _gated_scan(qc, kfc, kbc, vc, gfc, gbc, s0, n_heads, bsz)
    ox_f, ox_b, _ = _gated_scan(qx, kfx, kbx, vx, gfx, gbx, s_c, n_heads, bsz)
    return ((oc_f, oc_b) if need_ctx else None), (ox_f, ox_b)


def _chunk_rows(c, colmajor):
    if colmajor:
        return lambda v: (SUBLANES * (v % SUBLANES) + N_GROUPS * c + v // SUBLANES, GROUP)
    return lambda v: (c * CHUNK + GROUP * (v // SUBLANES) + v % SUBLANES, SUBLANES)


def _permute_chunk_in(stage_ref, dst_ref, c, colmajor):
    rows = _chunk_rows(c, colmajor)
    for jl in range(stage_ref.shape[0]):
        for v in range(SUB):
            start, stride = rows(v)
            dst_ref[c * CHUNK + SUBLANES * v:c * CHUNK + SUBLANES * (v + 1), jl * LANES:(jl + 1) * LANES] = (
                stage_ref[jl, pl.ds(start, SUBLANES, stride=stride), :])


def _permute_chunk_out(val, stage_ref, c, colmajor):
    rows = _chunk_rows(c, colmajor)
    for jl in range(stage_ref.shape[0]):
        for v in range(SUB):
            start, stride = rows(v)
            stage_ref[jl, pl.ds(start, SUBLANES, stride=stride), :] = (
                val[SUBLANES * v:SUBLANES * (v + 1), jl * LANES:(jl + 1) * LANES])


def _stage_block(x_ref, stage_ref):
    n = stage_ref.shape[1]
    for jl in range(stage_ref.shape[0]):
        blk = x_ref[..., jl * LANES:(jl + 1) * LANES]
        stage_ref[jl] = blk.reshape(n, LANES)


def _unstage_block(stage_ref, o_ref):
    for jl in range(stage_ref.shape[0]):
        o_ref[..., jl * LANES:(jl + 1) * LANES] = stage_ref[jl].reshape(o_ref.shape[:-1] + (LANES,))


def _hgrn_epilogue(p, lb, d):
    q = _silu(p[:, :d])
    v = p[:, d:2 * d]
    zf = p[:, 2 * d:3 * d]
    zb = p[:, 3 * d:4 * d]
    og = p[:, 4 * d:]
    lf, lbk = lb[0:1, :], lb[1:2, :]
    gf = jnp.log(lf + (1.0 - lf) * _sigmoid(zf))
    gb = jnp.log(lbk + (1.0 - lbk) * _sigmoid(zb))
    kf = (1.0 - lf) * _sigmoid(-zf)
    kb = (1.0 - lbk) * _sigmoid(-zb)
    return q, kf, kb, v, gf, gb, og


def _x_stream(x2, bsz, colmajor):
    seq, d = x2.shape[0], x2.shape[1] // bsz
    if colmajor:
        rows = seq // GRID_W
        assert rows == GROUP, "column-major chunks assume one grid column per 64-step group"
        return (x2.reshape(rows, GRID_W, bsz * d),
                pl.BlockSpec((rows, SUBLANES, d), lambda t, b: (0, t, b)), SUBLANES * rows)
    tm = 2 * CHUNK
    return x2, pl.BlockSpec((tm, d), lambda t, b: (t, b)), tm


def _load_chunks(x_ref, stage_ref, xs_ref, colmajor):
    _stage_block(x_ref, stage_ref)
    for c in range(xs_ref.shape[0] // CHUNK):
        _permute_chunk_in(stage_ref, xs_ref, c, colmajor)


def _store_chunks(xn, stage_ref, o_ref, colmajor):
    for c in range(xn.shape[0] // CHUNK):
        _permute_chunk_out(xn[c * CHUNK:(c + 1) * CHUNK, :], stage_ref, c, colmajor)
    _unstage_block(stage_ref, o_ref)


def _chunk_scratch(tm, d):
    return [pltpu.VMEM((d // LANES, tm, LANES), F32), pltpu.VMEM((tm, d), F32)]


def _hgrn_proj_kernel(x_ref, mod_ref, nw_ref, w_ref, lb_ref,
                      q_ref, kf_ref, kb_ref, v_ref, gf_ref, gb_ref, og_ref, stage_ref, xs_ref):
    d = xs_ref.shape[1]
    _load_chunks(x_ref, stage_ref, xs_ref, False)
    h = _norm_mod(xs_ref[...], nw_ref[...], mod_ref[0:1, :], mod_ref[1:2, :]).astype(BF16)
    outs = _hgrn_epilogue(_dot(h, w_ref[...]), lb_ref[...], d)
    for ref, val in zip((q_ref, kf_ref, kb_ref, v_ref, gf_ref, gb_ref, og_ref), outs):
        ref[...] = val.astype(ref.dtype)


def _hgrn_project(x2, mod_a, norm_w4, layer, w_in, lb, bsz):
    seq = x2.shape[0]
    d = x2.shape[1] // bsz
    xv, xspec, tm = _x_stream(x2, bsz, False)
    row = pl.BlockSpec((tm, d), lambda t, b: (t, b))
    return pl.pallas_call(
        _hgrn_proj_kernel,
        grid=(seq // tm, bsz),
        in_specs=[
            xspec,
            pl.BlockSpec((None, None, None, 3, d), lambda t, b: (layer, b, 1, 0, 0)),
            pl.BlockSpec((None, None, 1, d), lambda t, b: (layer, 1, 0, 0)),
            pl.BlockSpec(w_in.shape, lambda t, b: (0, 0), pipeline_mode=pl.Buffered(1)),
            pl.BlockSpec(lb.shape, lambda t, b: (0, 0)),
        ],
        out_specs=[row] * 7,
        out_shape=[jax.ShapeDtypeStruct((seq, bsz * d), BF16 if i == 3 else F32) for i in range(7)],
        scratch_shapes=_chunk_scratch(tm, d),
        compiler_params=_cparams("parallel", "parallel"),
        name="hgrn_project",
    )(xv, mod_a, norm_w4, w_in, lb)


def _gla_epilogue(p, wgate, bgate, dkt, dvt):
    q = p[:, :dkt] * ((dkt // GLA_HEADS) ** -0.5)
    k = p[:, dkt:2 * dkt]
    v = p[:, 2 * dkt:2 * dkt + dvt]
    og = p[:, 2 * dkt + dvt:2 * dkt + 2 * dvt]
    low = p[:, 2 * dkt + 2 * dvt:]
    lg = _log_sigmoid(_dot(low.astype(BF16), wgate) + bgate) * (1.0 / GLA_TAU)
    return q, k, v, og, lg[:, :dkt], lg[:, dkt:]


def _gla_proj_kernel(colmajor, x_ref, mod_ref, nw_ref, w_ref, wg_ref, bg_ref,
                     q_ref, k_ref, v_ref, og_ref, gf_ref, gb_ref, stage_ref, xs_ref):
    dkt = q_ref.shape[1]
    dvt = v_ref.shape[1]
    _load_chunks(x_ref, stage_ref, xs_ref, colmajor)
    h = _norm_mod(xs_ref[...], nw_ref[...], mod_ref[0:1, :], mod_ref[1:2, :]).astype(BF16)
    outs = _gla_epilogue(_dot(h, w_ref[...]), wg_ref[...], bg_ref[...], dkt, dvt)
    for ref, val in zip((q_ref, k_ref, v_ref, og_ref, gf_ref, gb_ref), outs):
        ref[...] = val.astype(ref.dtype)


def _gla_project(x2, mod_a, norm_w4, layer, w_in, wgate, bgate, bsz, dkt, dvt, *, colmajor):
    seq = x2.shape[0]
    d = x2.shape[1] // bsz
    widths = (dkt, dkt, dvt, dvt, dkt, dkt)
    xv, xspec, tm = _x_stream(x2, bsz, colmajor)
    const = lambda t, b: (0, 0)
    return pl.pallas_call(
        functools.partial(_gla_proj_kernel, colmajor),
        grid=(seq // tm, bsz),
        in_specs=[
            xspec,
            pl.BlockSpec((None, None, None, 3, d), lambda t, b: (layer, b, 1, 0, 0)),
            pl.BlockSpec((None, None, 1, d), lambda t, b: (layer, 1, 0, 0)),
            pl.BlockSpec(w_in.shape, const, pipeline_mode=pl.Buffered(1)),
            pl.BlockSpec(wgate.shape, const),
            pl.BlockSpec(bgate.shape, const),
        ],
        out_specs=[pl.BlockSpec((tm, w), lambda t, b: (t, b)) for w in widths],
        out_shape=[jax.ShapeDtypeStruct((seq, bsz * w), BF16 if i == 2 else F32)
                   for i, w in enumerate(widths)],
        scratch_shapes=_chunk_scratch(tm, d),
        compiler_params=_cparams("parallel", "parallel"),
        name="gla_project",
    )(xv, mod_a, norm_w4, w_in, wgate, bgate)


def _head_out(of, ob, og, gnw, w_out, n_heads, zs_ref):
    dv = of.shape[1] // n_heads
    o = of + ob
    for h in range(n_heads):
        sl = slice(h * dv, (h + 1) * dv)
        zs_ref[:, sl] = (_rms(o[:, sl], gnw) * _silu(og[:, sl])).astype(BF16)
    return _dot(zs_ref[...], w_out)


def _mix_out_kernel(n_heads, colmajor, of_ref, ob_ref, og_ref, x_ref, mod_ref, gnw_ref, w_ref,
                    o_ref, stage_ref, xs_ref, zs_ref):
    _load_chunks(x_ref, stage_ref, xs_ref, colmajor)
    y = _head_out(of_ref[...], ob_ref[...], og_ref[...], gnw_ref[...], w_ref[...], n_heads, zs_ref)
    xn = xs_ref[...] + mod_ref[2:3, :] * y
    _store_chunks(xn, stage_ref, o_ref, colmajor)


def _mix_out(of, ob, og, x2, mod_a, layer, gnw, w_out, n_heads, bsz, *, colmajor):
    seq = x2.shape[0]
    d = x2.shape[1] // bsz
    wv = of.shape[1] // bsz
    gnw = gnw.reshape(1, -1)
    xv, xspec, tm = _x_stream(x2, bsz, colmajor)
    ospec = pl.BlockSpec((tm, wv), lambda t, b: (t, b))
    out = pl.pallas_call(
        functools.partial(_mix_out_kernel, n_heads, colmajor),
        grid=(seq // tm, bsz),
        in_specs=[ospec, ospec, ospec, xspec,
                  pl.BlockSpec((None, None, None, 3, d), lambda t, b: (layer, b, 1, 0, 0)),
                  pl.BlockSpec(gnw.shape, lambda t, b: (0, 0)),
                  pl.BlockSpec(w_out.shape, lambda t, b: (0, 0))],
        out_specs=xspec,
        out_shape=jax.ShapeDtypeStruct(xv.shape, F32),
        scratch_shapes=_chunk_scratch(tm, d) + [pltpu.VMEM((tm, wv), BF16)],
        compiler_params=_cparams("parallel", "parallel"),
        name="mix_out",
    )(of, ob, og, xv, mod_a, gnw, w_out)
    return out.reshape(seq, bsz * d)


def _gla_mixer(xc2, x2, mods, norm_w4, layer, prm, bsz, need_ctx):
    (modc_a, _), (modx_a, _) = mods
    w_in, wgate, bgate, gnw, w_out, dkt, dvt = prm
    qc, kc, vc, ogc, gfc, gbc = _gla_project(xc2, modc_a, norm_w4, layer, w_in, wgate, bgate, bsz,
                                             dkt, dvt, colmajor=False)
    qx, kx, vx, ogx, gfx, gbx = _gla_project(x2, modx_a, norm_w4, layer, w_in, wgate, bgate, bsz,
                                             dkt, dvt, colmajor=True)
    oc, ox = _bidir_scan((qc, kc, kc, vc, gfc, gbc), (qx, kx, kx, vx, gfx, gbx), GLA_HEADS, bsz, need_ctx)
    x_new = _mix_out(ox[0], ox[1], ogx, x2, modx_a, layer, gnw, w_out, GLA_HEADS, bsz, colmajor=True)
    xc_new = None
    if need_ctx:
        xc_new = _mix_out(oc[0], oc[1], ogc, xc2, modc_a, layer, gnw, w_out, GLA_HEADS, bsz, colmajor=False)
    return xc_new, x_new


def _hgrn_mixer(xc2, x2, mods, norm_w4, layer, prm, bsz, need_ctx):
    (modc_a, _), (modx_a, _) = mods
    w_in, lb, gnw, w_out, n_heads = prm
    qc, kfc, kbc, vc, gfc, gbc, ogc = _hgrn_project(xc2, modc_a, norm_w4, layer, w_in, lb, bsz)
    qx, kfx, kbx, vx, gfx, gbx, ogx = _hgrn_project(x2, modx_a, norm_w4, layer, w_in, lb, bsz)
    oc, ox = _bidir_scan((qc, kfc, kbc, vc, gfc, gbc), (qx, kfx, kbx, vx, gfx, gbx), n_heads, bsz, need_ctx)
    x_new = _mix_out(ox[0], ox[1], ogx, x2, modx_a, layer, gnw, w_out, n_heads, bsz, colmajor=False)
    xc_new = None
    if need_ctx:
        xc_new = _mix_out(oc[0], oc[1], ogc, xc2, modc_a, layer, gnw, w_out, n_heads, bsz, colmajor=False)
    return xc_new, x_new


def _forward(x, c, ctx, c_ctx, ada_w, ada_b, norm_w, ffn_w_in, ffn_w_out,
             s5_a_re, s5_a_im, s5_log_step, s5_b_re, s5_b_im, s5_c_re, s5_c_im, s5_d,
             s5_w_glu, s5_b_glu, gla_w_in, gla_w_gate, gla_b_gate, gla_norm_w, gla_w_out,
             hgrn_w_in, hgrn_lb_logits, hgrn_norm_w, hgrn_w_out, final_norm_w, depth=None):
    depth = ada_w.shape[0] if depth is None else depth
    bsz, seq, d = x.shape
    mods_x, mods_c = _ada(c, c_ctx, ada_w, ada_b)
    norm_w4 = norm_w.reshape(norm_w.shape[0], 3, 1, d)
    w_in = ffn_w_in.astype(BF16)
    w_out = ffn_w_out.astype(BF16)
    lb_soft = jax.nn.softmax(hgrn_lb_logits.astype(F32), axis=0)
    lb_all = jnp.cumsum(lb_soft, axis=0) - lb_soft[0]

    xc2 = jnp.transpose(ctx, (1, 0, 2)).reshape(ctx.shape[1], bsz * d)
    x2 = x
    for i in range(depth):
        last = i == depth - 1
        kind, j = i % N_MIXERS, i // N_MIXERS
        x2 = _ffn(x2, mods_x[0], norm_w4, w_in, w_out, i, 0, bsz, in_bld=(i == 0))
        xc2 = _ffn(xc2, mods_c[0], norm_w4, w_in, w_out, i, 0, bsz)
        mods = (mods_c, mods_x)
        if kind == 0:
            are, aim, wb, wc = _s5_prep(s5_a_re[j], s5_a_im[j], s5_log_step[j], s5_b_re[j], s5_b_im[j],
                                        s5_c_re[j], s5_c_im[j], bsz)
            prm = (are, aim, wb, wc, s5_d[j], s5_w_glu[j].astype(BF16), s5_b_glu[j])
            yc, x2 = _s5_mixer(xc2, x2, (mods_c[1], mods_x[1]), norm_w4, i, prm, bsz, not last)
        elif kind == 1:
            dkt = d // 2
            dvt = d
            n_low = 2 * GLA_RANK
            pad = LANES - n_low
            wi = jnp.pad(gla_w_in[j], ((0, 0), (0, pad))).astype(BF16)
            wg = jnp.zeros((LANES, 2 * dkt), F32)
            wg = wg.at[:GLA_RANK, :dkt].set(gla_w_gate[j, 0]).at[GLA_RANK:n_low, dkt:].set(gla_w_gate[j, 1])
            prm = (wi, wg.astype(BF16), gla_b_gate[j].reshape(1, 2 * dkt), gla_norm_w[j],
                   gla_w_out[j].astype(BF16), dkt, dvt)
            yc, x2 = _gla_mixer(xc2, x2, mods, norm_w4, i, prm, bsz, not last)
        else:
            prm = (hgrn_w_in[j].astype(BF16), lb_all[i], hgrn_norm_w[j], hgrn_w_out[j].astype(BF16),
                   d // HGRN_DK)
            yc, x2 = _hgrn_mixer(xc2, x2, mods, norm_w4, i, prm, bsz, not last)
        x2 = _ffn(x2, mods_x[0], norm_w4, w_in, w_out, i, 2, bsz,
                  final_w=final_norm_w.reshape(1, d) if last else None)
        if not last:
            xc2 = _ffn(yc, mods_c[0], norm_w4, w_in, w_out, i, 2, bsz)
    return x2


def kernel(x, c, ctx, c_ctx, ada_w, ada_b, norm_w, ffn_w_in, ffn_w_out, s5_a_re, s5_a_im, s5_log_step,
           s5_b_re, s5_b_im, s5_c_re, s5_c_im, s5_d, s5_w_glu, s5_b_glu, gla_w_in, gla_w_gate,
           gla_b_gate, gla_norm_w, gla_w_out, hgrn_w_in, hgrn_lb_logits, hgrn_norm_w, hgrn_w_out,
           final_norm_w):
    return _forward(x, c, ctx, c_ctx, ada_w, ada_b, norm_w, ffn_w_in, ffn_w_out, s5_a_re, s5_a_im,
                    s5_log_step, s5_b_re, s5_b_im, s5_c_re, s5_c_im, s5_d, s5_w_glu, s5_b_glu,
                    gla_w_in, gla_w_gate, gla_b_gate, gla_norm_w, gla_w_out, hgrn_w_in,
                    hgrn_lb_logits, hgrn_norm_w, hgrn_w_out, final_norm_w)
```

```python
import functools
import math

import numpy as np
import jax
import jax.numpy as jnp
from jax import lax
from jax.experimental import pallas as pl
from jax.experimental.pallas import tpu as pltpu

F32 = jnp.float32
BF16 = jnp.bfloat16

NORM_EPS = 1e-6
GRID_W = 64
N_MOD = 9
N_MIXERS = 3

S5_GROUP = 16
S5_STATE = 64
S5_LANE_GROUPS = 8
S5_TB = 64
S5_TS = 64

GLA_HEADS = 4
GLA_RANK = 16
GLA_TAU = 16.0
HGRN_DK = 128

SUBLANES = 8
LANES = 128
CHUNK = 128
GROUP = SUBLANES * SUBLANES
N_GROUPS = CHUNK // GROUP
SUB = CHUNK // SUBLANES
N_FINE = 3
N_COARSE = 3
N_LEVELS = 7
LOG2E = 1.4426950408889634

FFN_CHUNK = 768

VMEM_LIMIT_BYTES = 56 * 1024 * 1024


def _cparams(*sem):
    return pltpu.CompilerParams(dimension_semantics=sem, vmem_limit_bytes=VMEM_LIMIT_BYTES)


def _dot(a, b):
    return jnp.dot(a, b, preferred_element_type=F32)


def _dot_nt(a, b):
    return lax.dot_general(a, b, (((1,), (1,)), ((), ())), preferred_element_type=F32)


def _dot_tn(a, b):
    return lax.dot_general(a, b, (((0,), (0,)), ((), ())), preferred_element_type=F32)


def _sigmoid(x):
    return 1.0 / (1.0 + jnp.exp(-x))


def _silu(x):
    return x * _sigmoid(x)


def _gelu_tanh(x):
    c = math.sqrt(2.0 / math.pi)
    return 0.5 * x * (1.0 + jnp.tanh(c * (x + 0.044715 * (x * x * x))))


def _log_sigmoid(x):
    return jnp.minimum(x, 0.0) - jnp.log1p(jnp.exp(-jnp.abs(x)))


def _rms(x, w):
    ms = jnp.mean(x * x, axis=-1, keepdims=True)
    return (x * lax.rsqrt(ms + NORM_EPS)) * w


def _norm_mod(x, w, shift, scale):
    return _rms(x, w) * (1.0 + scale) + shift


def _ada_kernel(cc_ref, w_ref, b_ref, o_ref):
    a = _silu(cc_ref[...]).astype(BF16)
    o_ref[...] = _dot(a, w_ref[...].astype(BF16)) + b_ref[...]


def _ada(c, c_ctx, ada_w, ada_b):
    depth, d, nd = ada_w.shape
    bsz = c.shape[0]
    rows = ((bsz + 1 + SUBLANES - 1) // SUBLANES) * SUBLANES
    cc = jnp.zeros((rows, d), F32).at[:bsz].set(c).at[bsz].set(c_ctx)
    nblk = nd // d
    mod = pl.pallas_call(
        _ada_kernel,
        grid=(depth, nblk),
        in_specs=[
            pl.BlockSpec((rows, d), lambda i, n: (0, 0)),
            pl.BlockSpec((None, d, d), lambda i, n: (i, 0, n)),
            pl.BlockSpec((None, 1, d), lambda i, n: (i, 0, n)),
        ],
        out_specs=pl.BlockSpec((None, rows, d), lambda i, n: (i, 0, n)),
        out_shape=jax.ShapeDtypeStruct((depth, rows, nd), F32),
        compiler_params=_cparams("arbitrary", "arbitrary"),
        name="ada_mod",
    )(cc, ada_w, ada_b.reshape(depth, 1, nd))
    mx = mod[:, :bsz].reshape(depth, bsz, 3, 3, d)
    mc = jnp.broadcast_to(mod[:, bsz:bsz + 1], (depth, bsz, nd)).reshape(depth, bsz, 3, 3, d)
    return (mx, jnp.transpose(mx, (0, 2, 3, 1, 4))), (mc, jnp.transpose(mc, (0, 2, 3, 1, 4)))


def _ffn_kernel(chunks, final, x_ref, mod_ref, nw_ref, wi_ref, wo_ref, *rest):
    if final:
        fw_ref, o_ref = rest
    else:
        (o_ref,) = rest
    ff = wo_ref.shape[0]
    x = x_ref[...]
    h = _norm_mod(x, nw_ref[...], mod_ref[0:1, :], mod_ref[1:2, :]).astype(BF16)
    acc = None
    for lo, hi in chunks:
        g = _dot(h, wi_ref[:, lo:hi])
        u = _dot(h, wi_ref[:, ff + lo:ff + hi])
        part = _dot((_silu(g) * u).astype(BF16), wo_ref[lo:hi, :])
        acc = part if acc is None else acc + part
    y = x + (0.5 * mod_ref[2:3, :]) * acc
    if final:
        y = _rms(y, fw_ref[...])
    o_ref[...] = y


def _ff_chunks(ff, width):
    assert ff % LANES == 0 and width % LANES == 0
    return tuple((lo, min(lo + width, ff)) for lo in range(0, ff, width))


def _ffn(x, mod_a, norm_w4, w_in, w_out, layer, stage, bsz, *, in_bld=False, final_w=None):
    d = norm_w4.shape[-1]
    seq = x.shape[1] if in_bld else x.shape[0]
    ff = w_out.shape[2]
    tm = 512 if seq % 512 == 0 else 256
    s = 0 if stage == 0 else 1
    final = final_w is not None
    resident = pl.Buffered(1)
    if in_bld:
        x_spec = pl.BlockSpec((None, tm, d), lambda t, b: (b, t, 0))
    else:
        x_spec = pl.BlockSpec((tm, d), lambda t, b: (t, b))
    in_specs = [
        x_spec,
        pl.BlockSpec((None, None, None, 3, d), lambda t, b: (layer, b, stage, 0, 0)),
        pl.BlockSpec((None, None, 1, d), lambda t, b: (layer, stage, 0, 0)),
        pl.BlockSpec((None, None, d, 2 * ff), lambda t, b: (layer, s, 0, 0), pipeline_mode=resident),
        pl.BlockSpec((None, None, ff, d), lambda t, b: (layer, s, 0, 0), pipeline_mode=resident),
    ]
    args = [x, mod_a, norm_w4, w_in, w_out]
    if final:
        in_specs.append(pl.BlockSpec((1, d), lambda t, b: (0, 0)))
        args.append(final_w)
        out_spec = pl.BlockSpec((None, tm, d), lambda t, b: (b, t, 0))
        out_shape = jax.ShapeDtypeStruct((bsz, seq, d), F32)
    else:
        out_spec = pl.BlockSpec((tm, d), lambda t, b: (t, b))
        out_shape = jax.ShapeDtypeStruct((seq, bsz * d), F32)
    return pl.pallas_call(
        functools.partial(_ffn_kernel, _ff_chunks(ff, FFN_CHUNK), final),
        grid=(seq // tm, bsz),
        in_specs=in_specs,
        out_specs=out_spec,
        out_shape=out_shape,
        compiler_params=_cparams("parallel", "parallel"),
        name="ffn_half",
    )(*args)


def _s5_prep(a_re, a_im, log_step, b_re, b_im, c_re, c_im, bsz):
    dt = jnp.exp(log_step.astype(F32))[..., None]
    a_re = a_re.astype(F32)
    a_im = a_im.astype(F32)
    mag = jnp.exp(a_re * dt)
    abar_re = mag * jnp.cos(a_im * dt)
    abar_im = mag * jnp.sin(a_im * dt)
    den = a_re * a_re + a_im * a_im
    zr = abar_re - 1.0
    zi = abar_im
    coef_re = ((zr * a_re + zi * a_im) / den)[..., None]
    coef_im = ((zi * a_re - zr * a_im) / den)[..., None]
    b_re = b_re.astype(F32)
    b_im = b_im.astype(F32)
    bbar_re = coef_re * b_re - coef_im * b_im
    bbar_im = coef_re * b_im + coef_im * b_re
    n_dir, g, p = abar_re.shape
    gc = b_re.shape[-1]
    nj = g // S5_LANE_GROUPS
    lg = S5_LANE_GROUPS
    are = jnp.broadcast_to(abar_re.reshape(n_dir, 1, g * p), (n_dir, bsz, g * p))
    aim = jnp.broadcast_to(abar_im.reshape(n_dir, 1, g * p), (n_dir, bsz, g * p))
    eye = jnp.eye(lg, dtype=F32)
    bb = jnp.stack([bbar_re, bbar_im], axis=1).reshape(n_dir, 2, nj, lg, p, gc)
    bb = jnp.transpose(bb, (0, 2, 3, 5, 1, 4))
    wb = bb[:, :, :, :, :, None, :] * eye[None, None, :, None, None, :, None]
    wb = wb.reshape(n_dir, nj, lg * gc, 2 * lg * p).astype(BF16)
    cc = jnp.stack([c_re.astype(F32), -c_im.astype(F32)], axis=1).reshape(n_dir, 2, nj, lg, gc, p)
    cc = jnp.transpose(cc, (0, 2, 1, 3, 5, 4))
    wc = cc[:, :, :, :, :, None, :] * eye[None, None, None, :, None, :, None]
    wc = wc.reshape(n_dir, nj, 2 * lg * p, lg * gc).astype(BF16)
    return are, aim, wb, wc


def _s5_kernel(tb, rev, finish, x_ref, mod_ref, nw_ref, are_ref, aim_ref, wb_ref, wc_ref,
               h0re_ref, h0im_ref, *rest):
    n_io = 7 if finish else 3
    io, scratch = rest[:n_io], rest[n_io:]
    if finish:
        yf_ref, dsk_ref, wglu_ref, bglu_ref, o_ref, hre_ref, him_ref = io
        h_s = scratch[12]
    else:
        o_ref, hre_ref, him_ref = io
    bu = (scratch[0:2], scratch[2:4])
    hs = (scratch[4:6], scratch[6:8])
    hs_re, hs_im, y_s, hb_s = scratch[8:12]
    bsz = x_ref.shape[1]
    d = x_ref.shape[2]
    nj = wb_ref.shape[0]
    sw = wb_ref.shape[2] // 2
    ri = S5_TS * bsz
    nit = tb // S5_TS

    @pl.when(pl.program_id(0) == 0)
    def _():
        hs_re[...] = h0re_ref[...]
        hs_im[...] = h0im_ref[...]

    x3 = x_ref[...]
    ms = jnp.mean(x3 * x3, axis=-1, keepdims=True)
    h3 = (x3 * lax.rsqrt(ms + NORM_EPS)) * nw_ref[...] * (1.0 + mod_ref[1]) + mod_ref[0]
    h2 = h3.reshape(tb * bsz, d)
    if finish:
        h_s[...] = h2
    hb_s[...] = h2.astype(BF16)

    def lanes(j):
        return slice(j * LANES, (j + 1) * LANES)

    def drive(j, rows):
        r = _dot(hb_s[rows, lanes(j)], wb_ref[j])
        bu[j % 2][0][rows, :] = r[:, :sw]
        bu[j % 2][1][rows, :] = r[:, sw:]

    def readout(j, rows):
        y_s[rows, lanes(j)] = (_dot(hs[j % 2][0][rows, :].astype(BF16), wc_ref[j, :sw, :])
                               + _dot(hs[j % 2][1][rows, :].astype(BF16), wc_ref[j, sw:, :]))

    drive(0, slice(None))
    for j in range(nj):
        cur = j % 2
        ar = are_ref[:, j * sw:(j + 1) * sw]
        ai = aim_ref[:, j * sw:(j + 1) * sw]

        def body(i, carry, j=j, cur=cur, ar=ar, ai=ai):
            hr, hi = carry
            rows = pl.ds(pl.multiple_of(i * ri, ri), ri)
            if j >= 1:
                readout(j - 1, rows)
            if j + 1 < nj:
                drive(j + 1, rows)
            for s in range(S5_TS):
                t = i * S5_TS + s
                tt = (tb - 1 - t) if rev else t
                sl = pl.ds(pl.multiple_of(tt * bsz, bsz), bsz)
                hr, hi = (ar * hr - ai * hi + bu[cur][0][sl, :], ar * hi + ai * hr + bu[cur][1][sl, :])
                hs[cur][0][sl, :] = hr
                hs[cur][1][sl, :] = hi
            return hr, hi

        hr, hi = lax.fori_loop(
            0, nit, body, (hs_re[:, j * sw:(j + 1) * sw], hs_im[:, j * sw:(j + 1) * sw]))
        hs_re[:, j * sw:(j + 1) * sw] = hr
        hs_im[:, j * sw:(j + 1) * sw] = hi
    readout(nj - 1, slice(None))

    hre_ref[...] = hs_re[...]
    him_ref[...] = hs_im[...]
    if not finish:
        o_ref[...] = y_s[...].reshape(tb, bsz, d)
    else:
        y = y_s[...] + yf_ref[...].reshape(tb * bsz, d) + dsk_ref[...] * h_s[...]
        z = _dot(_gelu_tanh(y).astype(BF16), wglu_ref[...]) + bglu_ref[...]
        out = z[:, :d] * _sigmoid(z[:, d:])
        o_ref[...] = x3 + mod_ref[2] * out.reshape(tb, bsz, d)


def _s5_pass(x2, mod_b, norm_w4, layer, are, aim, wb, wc, h0, bsz, *, rev, fin=None):
    seq = x2.shape[0]
    d = x2.shape[1] // bsz
    tb = S5_TB
    nblk = seq // tb
    dirn = 1 if rev else 0
    nstate = are.shape[-1]
    sw = wb.shape[3] // 2
    x3 = x2.reshape(seq, bsz, d)
    blk = (lambda i: (nblk - 1 - i, 0, 0)) if rev else (lambda i: (i, 0, 0))
    x_spec = pl.BlockSpec((tb, bsz, d), blk)
    const2 = lambda i: (0, 0)
    in_specs = [
        x_spec,
        pl.BlockSpec((None, None, 3, bsz, d), lambda i: (layer, 1, 0, 0, 0)),
        pl.BlockSpec((None, None, 1, d), lambda i: (layer, 1, 0, 0)),
        pl.BlockSpec((None, bsz, nstate), lambda i: (dirn, 0, 0)),
        pl.BlockSpec((None, bsz, nstate), lambda i: (dirn, 0, 0)),
        pl.BlockSpec((None,) + wb.shape[1:], lambda i: (dirn, 0, 0, 0)),
        pl.BlockSpec((None,) + wc.shape[1:], lambda i: (dirn, 0, 0, 0)),
        pl.BlockSpec((bsz, nstate), const2),
        pl.BlockSpec((bsz, nstate), const2),
    ]
    args = [x3, mod_b, norm_w4, are, aim, wb, wc, h0[0], h0[1]]
    scratch = [pltpu.VMEM((tb * bsz, sw), F32) for _ in range(8)] + [
        pltpu.VMEM((bsz, nstate), F32), pltpu.VMEM((bsz, nstate), F32),
        pltpu.VMEM((tb * bsz, d), F32), pltpu.VMEM((tb * bsz, d), BF16),
    ]
    finish = fin is not None
    if finish:
        y_fwd, d_skip, w_glu, b_glu = fin
        in_specs += [
            x_spec,
            pl.BlockSpec((1, d), const2),
            pl.BlockSpec(w_glu.shape, const2),
            pl.BlockSpec((1, 2 * d), const2),
        ]
        args += [y_fwd.reshape(seq, bsz, d), d_skip.reshape(1, d), w_glu, b_glu.reshape(1, 2 * d)]
        scratch.append(pltpu.VMEM((tb * bsz, d), F32))
    out, hre, him = pl.pallas_call(
        functools.partial(_s5_kernel, tb, rev, finish),
        grid=(nblk,),
        in_specs=in_specs,
        out_specs=[x_spec, pl.BlockSpec((bsz, nstate), const2), pl.BlockSpec((bsz, nstate), const2)],
        out_shape=[jax.ShapeDtypeStruct((seq, bsz, d), F32),
                   jax.ShapeDtypeStruct((bsz, nstate), F32),
                   jax.ShapeDtypeStruct((bsz, nstate), F32)],
        scratch_shapes=scratch,
        compiler_params=_cparams("arbitrary"),
        name="s5_bwd_finish" if finish else "s5_fwd",
    )(*args)
    return out.reshape(seq, bsz * d), (hre, him)


def _s5_mixer(xc2, x2, mods_b, norm_w4, layer, prm, bsz, need_ctx):
    (modc_b, modx_b) = mods_b
    are, aim, wb, wc, d_skip, w_glu, b_glu = prm
    nstate = are.shape[-1]
    zero = (jnp.zeros((bsz, nstate), F32), jnp.zeros((bsz, nstate), F32))
    ycf, hf = _s5_pass(xc2, modc_b, norm_w4, layer, are, aim, wb, wc, zero, bsz, rev=False)
    yxf, _ = _s5_pass(x2, modx_b, norm_w4, layer, are, aim, wb, wc, hf, bsz, rev=False)
    xc_new, hb = _s5_pass(xc2, modc_b, norm_w4, layer, are, aim, wb, wc, zero, bsz, rev=True,
                          fin=(ycf, d_skip, w_glu, b_glu))
    x_new, _ = _s5_pass(x2, modx_b, norm_w4, layer, are, aim, wb, wc, hb, bsz, rev=True,
                        fin=(yxf, d_skip, w_glu, b_glu))
    return (xc_new if need_ctx else None), x_new


S5_T = 16
S5C_TB = 128
SLOT = LANES // SUBLANES


def _cmul(ar, ai, br, bi):
    return ar * br - ai * bi, ar * bi + ai * br


def _s5c_prep(a_re, a_im, log_step, b_re, b_im, c_re, c_im):
    dt = jnp.exp(log_step.astype(F32))[..., None]
    a_re = a_re.astype(F32)
    a_im = a_im.astype(F32)
    mag = jnp.exp(a_re * dt)
    abar_re = mag * jnp.cos(a_im * dt)
    abar_im = mag * jnp.sin(a_im * dt)
    den = a_re * a_re + a_im * a_im
    zr = abar_re - 1.0
    zi = abar_im
    coef_re = ((zr * a_re + zi * a_im) / den)[..., None]
    coef_im = ((zi * a_re - zr * a_im) / den)[..., None]
    b_re = b_re.astype(F32)
    b_im = b_im.astype(F32)
    bb_re = coef_re * b_re - coef_im * b_im
    bb_im = coef_re * b_im + coef_im * b_re
    cc_re = c_re.astype(F32)
    cc_im = c_im.astype(F32)
    t = S5_T
    pr, pi = [jnp.ones_like(abar_re)], [jnp.zeros_like(abar_im)]
    for _ in range(t):
        nr, ni = _cmul(pr[-1], pi[-1], abar_re, abar_im)
        pr.append(nr)
        pi.append(ni)
    pr, pi = jnp.stack(pr), jnp.stack(pi)
    hi = lax.Precision.HIGHEST
    e_re, e_im = _cmul(pr[..., None], pi[..., None], bb_re[None], bb_im[None])
    taps = (jnp.einsum('dgop,kdgpc->kdgoc', cc_re, e_re, precision=hi)
            - jnp.einsum('dgop,kdgpc->kdgoc', cc_im, e_im, precision=hi))
    g, p, gc = abar_re.shape[1], abar_re.shape[2], b_re.shape[-1]
    lag = np.arange(t)[None, :] - np.arange(t)[:, None]
    kf = taps[:t, 0]
    kb = taps[:t, 1]
    full = jnp.where((lag > 0)[:, :, None, None, None], kf[np.abs(lag)], 0.0)
    full = full + jnp.where((lag < 0)[:, :, None, None, None], kb[np.abs(lag)], 0.0)
    full = full + jnp.where((lag == 0)[:, :, None, None, None], (kf[0] + kb[0])[None, None], 0.0)
    m = jnp.transpose(full, (2, 0, 4, 1, 3)).reshape(g, t * gc, t * gc)
    def wst(d, ks):
        er = jnp.transpose(e_re[ks, d], (1, 0, 3, 2))
        ei = jnp.transpose(e_im[ks, d], (1, 0, 3, 2))
        return jnp.concatenate([er, ei], axis=-1).reshape(g, t * gc, 2 * p)
    wst_f = wst(0, np.arange(t)[::-1].copy())
    wst_b = wst(1, np.arange(t))
    def wout(d, ks):
        gr, gi = _cmul(cc_re[d][None], cc_im[d][None], pr[ks, d][:, :, None, :], pi[ks, d][:, :, None, :])
        gr = jnp.transpose(gr, (1, 3, 0, 2))
        gi = jnp.transpose(gi, (1, 3, 0, 2))
        return jnp.concatenate([gr, -gi], axis=1).reshape(g, 2 * p, t * gc)
    wout_f = wout(0, np.arange(1, t + 1))
    wout_b = wout(1, np.arange(t, 0, -1))
    a1 = jnp.concatenate([pr[t], pr[t]], axis=-1)
    a2 = jnp.concatenate([-pi[t], pi[t]], axis=-1)
    bc = lambda a: jnp.broadcast_to(a[:, :, None, :], a.shape[:2] + (SUBLANES, a.shape[-1]))
    return dict(m=m.astype(BF16), wst=(wst_f.astype(BF16), wst_b.astype(BF16)),
                wout=(wout_f.astype(BF16), wout_b.astype(BF16)), a1=bc(a1), a2=bc(a2))


def _slot_transpose(x):
    m, n, bsz, _ = x.shape
    slot = lax.broadcasted_iota(jnp.int32, (bsz, LANES), 1) // SLOT

    def rot(v, shift):
        return pltpu.roll(v.reshape(-1, LANES), shift, 1).reshape(v.shape)

    d = n // 2
    while d >= 1:
        keep = (slot & d) == 0
        x = x.reshape(m, n // (2 * d), 2, d, bsz, LANES)
        lo, hi = x[:, :, 0], x[:, :, 1]
        new_lo = jnp.where(keep, lo, rot(hi, SLOT * d))
        new_hi = jnp.where(keep, rot(lo, LANES - SLOT * d), hi)
        x = jnp.stack([new_lo, new_hi], axis=2).reshape(m, n, bsz, LANES)
        d //= 2
    return x


def _s5c_gather(h_s, a_s, nch):
    nj = h_s.shape[1] // LANES
    nh = S5_T // SUBLANES
    for j in range(nj):
        x = h_s[:, j * LANES:(j + 1) * LANES].reshape(nch * nh, SUBLANES, SUBLANES, LANES)
        r = _slot_transpose(x).reshape(nch, nh, SUBLANES, SUBLANES, LANES)
        for gl in range(S5_LANE_GROUPS):
            for hh in range(nh):
                a_s[S5_LANE_GROUPS * j + gl, :, hh * LANES:(hh + 1) * LANES] = (
                    r[:, hh, gl].reshape(nch * SUBLANES, LANES))


def _s5c_state_scan(v, st_ref, sin_ref, a1, a2, g, nch, rev):
    gl = slice(g * LANES, (g + 1) * LANES)
    s = st_ref[:, gl]
    w = pltpu.roll(s, LANES // 2, 1)
    for n in (range(nch - 1, -1, -1) if rev else range(nch)):
        rows = slice(SUBLANES * n, SUBLANES * (n + 1))
        sin_ref[rows, gl] = s
        vn = v[rows, :]
        s, w = a1 * s + a2 * w + vn, a1 * w - a2 * s + pltpu.roll(vn, LANES // 2, 1)
    st_ref[:, gl] = s


def _s5c_norm(x_ref, mod_ref, nw_ref, h_s):
    x3 = x_ref[...]
    ms = jnp.mean(x3 * x3, axis=-1, keepdims=True)
    h3 = (x3 * lax.rsqrt(ms + NORM_EPS)) * nw_ref[...] * (1.0 + mod_ref[1]) + mod_ref[0]
    h_s[...] = h3.reshape(h_s.shape)


def _s5c_fwd_kernel(x_ref, mod_ref, nw_ref, wst_ref, a1_ref, a2_ref, h0_ref,
                    sin_ref, hfin_ref, st_ref, h_s, a_s):
    ng = wst_ref.shape[0]
    nch = a_s.shape[1] // SUBLANES

    @pl.when(pl.program_id(0) == 0)
    def _():
        st_ref[...] = h0_ref[...]

    _s5c_norm(x_ref, mod_ref, nw_ref, h_s)
    _s5c_gather(h_s, a_s, nch)
    for g in range(ng):
        v = _dot(a_s[g].astype(BF16), wst_ref[g])
        _s5c_state_scan(v, st_ref, sin_ref, a1_ref[g], a2_ref[g], g, nch, False)
    hfin_ref[...] = st_ref[...]


def _s5c_bwd_kernel(x_ref, mod_ref, nw_ref, sinf_ref, m_ref, wst_ref, woutf_ref, woutb_ref,
                    a1_ref, a2_ref, h0_ref, dsk_ref, y_ref, hfin_ref, st_ref, h_s, a_s, sinb_s):
    ng = m_ref.shape[0]
    nch = a_s.shape[1] // SUBLANES

    @pl.when(pl.program_id(0) == 0)
    def _():
        st_ref[...] = h0_ref[...]

    _s5c_norm(x_ref, mod_ref, nw_ref, h_s)
    _s5c_gather(h_s, a_s, nch)
    for j in range(ng // S5_LANE_GROUPS):
        ys = []
        for gl in range(S5_LANE_GROUPS):
            g = S5_LANE_GROUPS * j + gl
            lanes = slice(g * LANES, (g + 1) * LANES)
            a = a_s[g].astype(BF16)
            _s5c_state_scan(_dot(a, wst_ref[g]), st_ref, sinb_s, a1_ref[g], a2_ref[g], g, nch, True)
            ys.append(_dot(a, m_ref[g])
                      + _dot(sinf_ref[:, lanes].astype(BF16), woutf_ref[g])
                      + _dot(sinb_s[:, lanes].astype(BF16), woutb_ref[g]))
        cols = slice(j * LANES, (j + 1) * LANES)
        for hh in range(S5_T // SUBLANES):
            r = jnp.stack([y[:, hh * LANES:(hh + 1) * LANES].reshape(nch, SUBLANES, LANES) for y in ys],
                          axis=1)
            out = _slot_transpose(r)
            for n in range(nch):
                t0 = S5_T * n + SUBLANES * hh
                skip = dsk_ref[:, cols] * h_s[t0 * SUBLANES:(t0 + SUBLANES) * SUBLANES, cols]
                y_ref[t0:t0 + SUBLANES, :, cols] = out[n] + skip.reshape(SUBLANES, SUBLANES, LANES)
    hfin_ref[...] = st_ref[...]


def _s5c_glu_kernel(y_ref, x_ref, mod_ref, w_ref, b_ref, o_ref):
    d = x_ref.shape[1]
    z = _dot(_gelu_tanh(y_ref[...]).astype(BF16), w_ref[...]) + b_ref[...]
    o_ref[...] = x_ref[...] + mod_ref[2:3, :] * (z[:, :d] * _sigmoid(z[:, d:]))


def _s5c_stream(x2, mod_b, norm_w4, layer, prm, h0, bsz, *, rev, sin_f=None, d_skip=None):
    seq = x2.shape[0]
    d = x2.shape[1] // bsz
    tb = S5C_TB
    nblk = seq // tb
    nch = tb // S5_T
    dirn = 1 if rev else 0
    ng = prm["m"].shape[0]
    ns = ng * LANES
    x3 = x2.reshape(seq, bsz, d)
    blk = (lambda i: (nblk - 1 - i, 0, 0)) if rev else (lambda i: (i, 0, 0))
    blk2 = (lambda i: (nblk - 1 - i, 0)) if rev else (lambda i: (i, 0))
    x_spec = pl.BlockSpec((tb, bsz, d), blk)
    sin_spec = pl.BlockSpec((nch * bsz, ns), blk2)
    st_spec = pl.BlockSpec((bsz, ns), lambda i: (0, 0))
    res = pl.Buffered(1)
    wspec = lambda w: pl.BlockSpec(w.shape, lambda i: (0,) * w.ndim, pipeline_mode=res)
    aspec = pl.BlockSpec((None,) + prm["a1"].shape[1:], lambda i: (dirn, 0, 0, 0), pipeline_mode=res)
    common = [x_spec,
              pl.BlockSpec((None, None, 3, bsz, d), lambda i: (layer, 1, 0, 0, 0)),
              pl.BlockSpec((None, None, 1, d), lambda i: (layer, 1, 0, 0))]
    scratch = [pltpu.VMEM((bsz, ns), F32), pltpu.VMEM((tb * bsz, d), F32),
               pltpu.VMEM((ng, nch * bsz, S5_T * S5_GROUP), F32)]
    if not rev:
        return pl.pallas_call(
            _s5c_fwd_kernel,
            grid=(nblk,),
            in_specs=common + [wspec(prm["wst"][0]), aspec, aspec, st_spec],
            out_specs=[sin_spec, st_spec],
            out_shape=[jax.ShapeDtypeStruct((seq // S5_T * bsz, ns), F32),
                       jax.ShapeDtypeStruct((bsz, ns), F32)],
            scratch_shapes=scratch,
            compiler_params=_cparams("arbitrary"),
            name="s5_fwd_states",
        )(x3, mod_b, norm_w4, prm["wst"][0], prm["a1"], prm["a2"], h0)
    y, hfin = pl.pallas_call(
        _s5c_bwd_kernel,
        grid=(nblk,),
        in_specs=common + [sin_spec, wspec(prm["m"]), wspec(prm["wst"][1]), wspec(prm["wout"][0]),
                           wspec(prm["wout"][1]), aspec, aspec, st_spec,
                           pl.BlockSpec((1, d), lambda i: (0, 0))],
        out_specs=[x_spec, st_spec],
        out_shape=[jax.ShapeDtypeStruct((seq, bsz, d), F32), jax.ShapeDtypeStruct((bsz, ns), F32)],
        scratch_shapes=scratch + [pltpu.VMEM((nch * bsz, ns), F32)],
        compiler_params=_cparams("arbitrary"),
        name="s5_bwd_readout",
    )(x3, mod_b, norm_w4, sin_f, prm["m"], prm["wst"][1], prm["wout"][0], prm["wout"][1],
      prm["a1"], prm["a2"], h0, d_skip.reshape(1, d))
    return y.reshape(seq, bsz * d), hfin


def _s5c_glu(y2, x2, mod_a, layer, w_glu, b_glu, bsz):
    seq = x2.shape[0]
    d = x2.shape[1] // bsz
    tm = 512 if seq % 512 == 0 else 256
    row = pl.BlockSpec((tm, d), lambda t, b: (t, b))
    return pl.pallas_call(
        _s5c_glu_kernel,
        grid=(seq // tm, bsz),
        in_specs=[row, row,
                  pl.BlockSpec((None, None, None, 3, d), lambda t, b: (layer, b, 1, 0, 0)),
                  pl.BlockSpec(w_glu.shape, lambda t, b: (0, 0), pipeline_mode=pl.Buffered(1)),
                  pl.BlockSpec((1, 2 * d), lambda t, b: (0, 0))],
        out_specs=row,
        out_shape=jax.ShapeDtypeStruct((seq, bsz * d), F32),
        compiler_params=_cparams("parallel", "parallel"),
        name="s5_glu",
    )(y2, x2, mod_a, w_glu, b_glu.reshape(1, 2 * d))


def _s5c_mixer(xc2, x2, mods, norm_w4, layer, prm, extra, bsz, need_ctx):
    (modc_a, modc_b), (modx_a, modx_b) = mods
    d_skip, w_glu, b_glu = extra
    ns = prm["m"].shape[0] * LANES
    zero = jnp.zeros((bsz, ns), F32)
    sc_f, hf = _s5c_stream(xc2, modc_b, norm_w4, layer, prm, zero, bsz, rev=False)
    sx_f, _ = _s5c_stream(x2, modx_b, norm_w4, layer, prm, hf, bsz, rev=False)
    yc, hb = _s5c_stream(xc2, modc_b, norm_w4, layer, prm, zero, bsz, rev=True, sin_f=sc_f, d_skip=d_skip)
    yx, _ = _s5c_stream(x2, modx_b, norm_w4, layer, prm, hb, bsz, rev=True, sin_f=sx_f, d_skip=d_skip)
    x_new = _s5c_glu(yx, x2, modx_a, layer, w_glu, b_glu, bsz)
    xc_new = _s5c_glu(yc, xc2, modc_a, layer, w_glu, b_glu, bsz) if need_ctx else None
    return xc_new, x_new


def _level_matrix(rev):
    p = np.arange(CHUNK)
    v, r = p // SUBLANES, p % SUBLANES
    tau = GROUP * (v // SUBLANES) + SUBLANES * r + v % SUBLANES
    if rev:
        tau = CHUNK - 1 - tau
    ti, tj = tau[:, None], tau[None, :]
    x = ti ^ tj
    lvl = np.zeros((CHUNK, CHUNK), np.int32)
    for b in range(N_LEVELS):
        lvl = np.where((x >> b) & 1, b + 1, lvl)
    lvl = np.where(tj > ti, -1, lvl)
    return lvl.astype(np.int32)


def _scan_chunk_head(q, k, vb, g, st_t, masks, rev):
    dk = q.shape[1]
    ng, nv = N_GROUPS, SUBLANES
    ea = (lambda e: SUB - 1 - e) if rev else (lambda e: e)
    ksub = (lambda e: SUBLANES - 1 - e) if rev else (lambda e: e)

    def vregs(x, scale=None):
        out = [x[SUBLANES * ea(e):SUBLANES * ea(e) + SUBLANES, :] for e in range(SUB)]
        if scale is not None:
            out = [o * scale for o in out]
        return [out[nv * gi:nv * (gi + 1)] for gi in range(ng)]

    def assemble(vs):
        flat = [vs[gi][a] for gi in range(ng) for a in range(nv)]
        return jnp.concatenate([flat[ea(a)] for a in range(SUB)], axis=0)

    def zeros():
        return [[zero] * nv for _ in range(ng)]

    qv, kv, gv = vregs(q), vregs(k), vregs(g, LOG2E)
    zero = jnp.zeros((SUBLANES, dk), F32)
    ninf = jnp.full((SUBLANES, dk), -jnp.inf, F32)

    sub_i = lax.broadcasted_iota(jnp.int32, (SUBLANES, dk), 0)
    re = (SUBLANES - 1 - sub_i) if rev else sub_i

    def row(x, e):
        kk = ksub(e)
        return jnp.broadcast_to(x[kk:kk + 1, :], (SUBLANES, dk))

    pf, tot, ct, xcl, gtot = [], [], [], [], []
    for gi in range(ng):
        p = [gv[gi][0]]
        for a in range(1, nv):
            p.append(p[-1] + gv[gi][a])
        t = p[nv - 1]
        c = zero
        for e in range(SUBLANES):
            c = c + jnp.where(re >= e, row(t, e), 0.0)
        pf.append(p)
        tot.append(t)
        ct.append(c)
        xcl.append(c - t)
        gtot.append(row(c, SUBLANES - 1))

    q_lv, k_lv = [], []
    for lvl in range(1, N_FINE + 1):
        s = 1 << lvl
        qs, ks = zeros(), zeros()
        for gi in range(ng):
            for bs in range(0, nv, s):
                m = bs + s // 2
                ref = pf[gi][m - 1]
                for a in range(bs, m - 1):
                    ks[gi][a] = kv[gi][a] * jnp.exp2(ref - pf[gi][a])
                ks[gi][m - 1] = kv[gi][m - 1]
                for a in range(m, bs + s):
                    qs[gi][a] = qv[gi][a] * jnp.exp2(pf[gi][a] - ref)
        q_lv.append(assemble(qs))
        k_lv.append(assemble(ks))
    for lvl in range(N_FINE + 1, N_FINE + N_COARSE + 1):
        w = 1 << (lvl - N_FINE)
        right = (re & (w // 2)) != 0
        qs, ks = zeros(), zeros()
        for gi in range(ng):
            xm = zero
            for bs in range(0, SUBLANES, w):
                inblk = (re >= bs) & (re < bs + w)
                xm = xm + jnp.where(inblk, row(xcl[gi], bs + w // 2), 0.0)
            drq = jnp.where(right, xcl[gi] - xm, ninf)
            dlk = jnp.where(right, ninf, tot[gi] + xm - ct[gi])
            for a in range(nv):
                qs[gi][a] = qv[gi][a] * jnp.exp2(pf[gi][a] + drq)
                ks[gi][a] = kv[gi][a] * jnp.exp2(dlk - pf[gi][a])
        q_lv.append(assemble(qs))
        k_lv.append(assemble(ks))
    qs, ks = zeros(), zeros()
    for a in range(nv):
        ks[0][a] = kv[0][a] * jnp.exp2(tot[0] + (gtot[0] - ct[0]) - pf[0][a])
        qs[1][a] = qv[1][a] * jnp.exp2(pf[1][a] + xcl[1])
    q_lv.append(assemble(qs))
    k_lv.append(assemble(ks))

    sc = jnp.where(masks[0], jnp.sum(q * k, axis=-1, keepdims=True), 0.0)
    for lvl in range(1, N_LEVELS + 1):
        s_l = _dot_nt(q_lv[lvl - 1].astype(BF16), k_lv[lvl - 1].astype(BF16))
        sc = jnp.where(masks[lvl], s_l, sc)

    e0, e1 = jnp.exp2(gtot[0]), jnp.exp2(gtot[1])
    qc = zeros()
    kd = zeros()
    for a in range(nv):
        qc[0][a] = qv[0][a] * jnp.exp2(pf[0][a] + xcl[0])
        qc[1][a] = qs[1][a] * e0
        kd[0][a] = ks[0][a] * e1
        kd[1][a] = kv[1][a] * jnp.exp2(tot[1] + (gtot[1] - ct[1]) - pf[1][a])
    o = _dot(sc.astype(BF16), vb) + _dot_nt(assemble(qc).astype(BF16), st_t.astype(BF16))
    st_new = (e0 * e1)[0:1, :] * st_t + _dot_tn(vb, assemble(kd).astype(BF16))
    return o, st_new


def _scan_kernel(n_heads, dk, dv, qf_ref, kf_ref, vf_ref, gf_ref, qb_ref, kb_ref, vb_ref, gb_ref,
                 lv_ref, s0_ref, of_ref, ob_ref, sf_ref, st_ref):
    @pl.when(pl.program_id(1) == 0)
    def _():
        st_ref[...] = s0_ref[...]

    dirs = ((qf_ref, kf_ref, vf_ref, gf_ref, of_ref, False), (qb_ref, kb_ref, vb_ref, gb_ref, ob_ref, True))
    masks = [[lv_ref[d] == lvl for lvl in range(N_LEVELS + 1)] for d in range(2)]
    for h in range(n_heads):
        ks = slice(h * dk, (h + 1) * dk)
        vs = slice(h * dv, (h + 1) * dv)
        for d, (q_ref, k_ref, v_ref, g_ref, o_ref, rev) in enumerate(dirs):
            o, st_new = _scan_chunk_head(q_ref[:, ks], k_ref[:, ks], v_ref[:, vs], g_ref[:, ks],
                                         st_ref[d, vs, :], masks[d], rev)
            o_ref[:, vs] = o
            st_ref[d, vs, :] = st_new
    sf_ref[...] = st_ref[...]


def _gated_scan(q, kf, kb, v, gf, gb, s0, n_heads, bsz):
    seq = q.shape[0]
    wk = q.shape[1] // bsz
    wv = v.shape[1] // bsz
    dk, dv = wk // n_heads, wv // n_heads
    nchunk = seq // CHUNK
    fwd = lambda b, n: (n, b)
    bwd = lambda b, n: (nchunk - 1 - n, b)
    kspec = lambda im: pl.BlockSpec((CHUNK, wk), im)
    vspec = lambda im: pl.BlockSpec((CHUNK, wv), im)
    sspec = pl.BlockSpec((2, None, wv, dk), lambda b, n: (0, b, 0, 0))
    lv = jnp.asarray(np.stack([_level_matrix(False), _level_matrix(True)]))
    return pl.pallas_call(
        functools.partial(_scan_kernel, n_heads, dk, dv),
        grid=(bsz, nchunk),
        in_specs=[kspec(fwd), kspec(fwd), vspec(fwd), kspec(fwd),
                  kspec(bwd), kspec(bwd), vspec(bwd), kspec(bwd),
                  pl.BlockSpec((2, CHUNK, CHUNK), lambda b, n: (0, 0, 0)), sspec],
        out_specs=[vspec(fwd), vspec(bwd), sspec],
        out_shape=[jax.ShapeDtypeStruct((seq, bsz * wv), F32),
                   jax.ShapeDtypeStruct((seq, bsz * wv), F32),
                   jax.ShapeDtypeStruct((2, bsz, wv, dk), F32)],
        scratch_shapes=[pltpu.VMEM((2, wv, dk), F32)],
        compiler_params=_cparams("parallel", "arbitrary"),
        name="gated_scan",
    )(q, kf, v, gf, q, kb, v, gb, lv, s0)


def _bidir_scan(ctx_in, lat_in, n_heads, bsz, need_ctx):
    qc, kfc, kbc, vc, gfc, gbc = ctx_in
    qx, kfx, kbx, vx, gfx, gbx = lat_in
    wk = qc.shape[1] // bsz
    wv = vc.shape[1] // bsz
    s0 = jnp.zeros((2, bsz, wv, wk // n_heads), F32)
    oc_f, oc_b, s_c = _gated_scan(qc, kfc, kbc, vc, gfc, gbc, s0, n_heads, bsz)
    ox_f, ox_b, _ = _gated_scan(qx, kfx, kbx, vx, gfx, gbx, s_c, n_heads, bsz)
    return ((oc_f, oc_b) if need_ctx else None), (ox_f, ox_b)


def _chunk_rows(c, colmajor):
    if colmajor:
        return lambda v: (SUBLANES * (v % SUBLANES) + N_GROUPS * c + v // SUBLANES, GROUP)
    return lambda v: (c * CHUNK + GROUP * (v // SUBLANES) + v % SUBLANES, SUBLANES)


def _permute_chunk_in(stage_ref, dst_ref, c, colmajor):
    rows = _chunk_rows(c, colmajor)
    for jl in range(stage_ref.shape[0]):
        for v in range(SUB):
            start, stride = rows(v)
            dst_ref[c * CHUNK + SUBLANES * v:c * CHUNK + SUBLANES * (v + 1), jl * LANES:(jl + 1) * LANES] = (
                stage_ref[jl, pl.ds(start, SUBLANES, stride=stride), :])


def _permute_chunk_out(val, stage_ref, c, colmajor):
    rows = _chunk_rows(c, colmajor)
    for jl in range(stage_ref.shape[0]):
        for v in range(SUB):
            start, stride = rows(v)
            stage_ref[jl, pl.ds(start, SUBLANES, stride=stride), :] = (
                val[SUBLANES * v:SUBLANES * (v + 1), jl * LANES:(jl + 1) * LANES])


def _stage_block(x_ref, stage_ref):
    n = stage_ref.shape[1]
    for jl in range(stage_ref.shape[0]):
        blk = x_ref[..., jl * LANES:(jl + 1) * LANES]
        stage_ref[jl] = blk.reshape(n, LANES)


def _unstage_block(stage_ref, o_ref):
    for jl in range(stage_ref.shape[0]):
        o_ref[..., jl * LANES:(jl + 1) * LANES] = stage_ref[jl].reshape(o_ref.shape[:-1] + (LANES,))


def _hgrn_epilogue(p, lb, d):
    q = _silu(p[:, :d])
    v = p[:, d:2 * d]
    zf = p[:, 2 * d:3 * d]
    zb = p[:, 3 * d:4 * d]
    og = p[:, 4 * d:]
    lf, lbk = lb[0:1, :], lb[1:2, :]
    gf = jnp.log(lf + (1.0 - lf) * _sigmoid(zf))
    gb = jnp.log(lbk + (1.0 - lbk) * _sigmoid(zb))
    kf = (1.0 - lf) * _sigmoid(-zf)
    kb = (1.0 - lbk) * _sigmoid(-zb)
    return q, kf, kb, v, gf, gb, og


def _x_stream(x2, bsz, colmajor):
    seq, d = x2.shape[0], x2.shape[1] // bsz
    if colmajor:
        rows = seq // GRID_W
        assert rows == GROUP, "column-major chunks assume one grid column per 64-step group"
        return (x2.reshape(rows, GRID_W, bsz * d),
                pl.BlockSpec((rows, SUBLANES, d), lambda t, b: (0, t, b)), SUBLANES * rows)
    tm = 2 * CHUNK
    return x2, pl.BlockSpec((tm, d), lambda t, b: (t, b)), tm


def _load_chunks(x_ref, stage_ref, xs_ref, colmajor):
    _stage_block(x_ref, stage_ref)
    for c in range(xs_ref.shape[0] // CHUNK):
        _permute_chunk_in(stage_ref, xs_ref, c, colmajor)


def _store_chunks(xn, stage_ref, o_ref, colmajor):
    for c in range(xn.shape[0] // CHUNK):
        _permute_chunk_out(xn[c * CHUNK:(c + 1) * CHUNK, :], stage_ref, c, colmajor)
    _unstage_block(stage_ref, o_ref)


def _chunk_scratch(tm, d):
    return [pltpu.VMEM((d // LANES, tm, LANES), F32), pltpu.VMEM((tm, d), F32)]


def _hgrn_proj_kernel(x_ref, mod_ref, nw_ref, w_ref, lb_ref,
                      q_ref, kf_ref, kb_ref, v_ref, gf_ref, gb_ref, og_ref, stage_ref, xs_ref):
    d = xs_ref.shape[1]
    _load_chunks(x_ref, stage_ref, xs_ref, False)
    h = _norm_mod(xs_ref[...], nw_ref[...], mod_ref[0:1, :], mod_ref[1:2, :]).astype(BF16)
    outs = _hgrn_epilogue(_dot(h, w_ref[...]), lb_ref[...], d)
    for ref, val in zip((q_ref, kf_ref, kb_ref, v_ref, gf_ref, gb_ref, og_ref), outs):
        ref[...] = val.astype(ref.dtype)


def _hgrn_project(x2, mod_a, norm_w4, layer, w_in, lb, bsz):
    seq = x2.shape[0]
    d = x2.shape[1] // bsz
    xv, xspec, tm = _x_stream(x2, bsz, False)
    row = pl.BlockSpec((tm, d), lambda t, b: (t, b))
    return pl.pallas_call(
        _hgrn_proj_kernel,
        grid=(seq // tm, bsz),
        in_specs=[
            xspec,
            pl.BlockSpec((None, None, None, 3, d), lambda t, b: (layer, b, 1, 0, 0)),
            pl.BlockSpec((None, None, 1, d), lambda t, b: (layer, 1, 0, 0)),
            pl.BlockSpec(w_in.shape, lambda t, b: (0, 0), pipeline_mode=pl.Buffered(1)),
            pl.BlockSpec(lb.shape, lambda t, b: (0, 0)),
        ],
        out_specs=[row] * 7,
        out_shape=[jax.ShapeDtypeStruct((seq, bsz * d), BF16 if i == 3 else F32) for i in range(7)],
        scratch_shapes=_chunk_scratch(tm, d),
        compiler_params=_cparams("parallel", "parallel"),
        name="hgrn_project",
    )(xv, mod_a, norm_w4, w_in, lb)


def _gla_epilogue(p, wgate, bgate, dkt, dvt):
    q = p[:, :dkt] * ((dkt // GLA_HEADS) ** -0.5)
    k = p[:, dkt:2 * dkt]
    v = p[:, 2 * dkt:2 * dkt + dvt]
    og = p[:, 2 * dkt + dvt:2 * dkt + 2 * dvt]
    low = p[:, 2 * dkt + 2 * dvt:]
    lg = _log_sigmoid(_dot(low.astype(BF16), wgate) + bgate) * (1.0 / GLA_TAU)
    return q, k, v, og, lg[:, :dkt], lg[:, dkt:]


def _gla_proj_kernel(colmajor, x_ref, mod_ref, nw_ref, w_ref, wg_ref, bg_ref,
                     q_ref, k_ref, v_ref, og_ref, gf_ref, gb_ref, stage_ref, xs_ref):
    dkt = q_ref.shape[1]
    dvt = v_ref.shape[1]
    _load_chunks(x_ref, stage_ref, xs_ref, colmajor)
    h = _norm_mod(xs_ref[...], nw_ref[...], mod_ref[0:1, :], mod_ref[1:2, :]).astype(BF16)
    outs = _gla_epilogue(_dot(h, w_ref[...]), wg_ref[...], bg_ref[...], dkt, dvt)
    for ref, val in zip((q_ref, k_ref, v_ref, og_ref, gf_ref, gb_ref), outs):
        ref[...] = val.astype(ref.dtype)


def _gla_project(x2, mod_a, norm_w4, layer, w_in, wgate, bgate, bsz, dkt, dvt, *, colmajor):
    seq = x2.shape[0]
    d = x2.shape[1] // bsz
    widths = (dkt, dkt, dvt, dvt, dkt, dkt)
    xv, xspec, tm = _x_stream(x2, bsz, colmajor)
    const = lambda t, b: (0, 0)
    return pl.pallas_call(
        functools.partial(_gla_proj_kernel, colmajor),
        grid=(seq // tm, bsz),
        in_specs=[
            xspec,
            pl.BlockSpec((None, None, None, 3, d), lambda t, b: (layer, b, 1, 0, 0)),
            pl.BlockSpec((None, None, 1, d), lambda t, b: (layer, 1, 0, 0)),
            pl.BlockSpec(w_in.shape, const, pipeline_mode=pl.Buffered(1)),
            pl.BlockSpec(wgate.shape, const),
            pl.BlockSpec(bgate.shape, const),
        ],
        out_specs=[pl.BlockSpec((tm, w), lambda t, b: (t, b)) for w in widths],
        out_shape=[jax.ShapeDtypeStruct((seq, bsz * w), BF16 if i == 2 else F32)
                   for i, w in enumerate(widths)],
        scratch_shapes=_chunk_scratch(tm, d),
        compiler_params=_cparams("parallel", "parallel"),
        name="gla_project",
    )(xv, mod_a, norm_w4, w_in, wgate, bgate)


def _head_out(of, ob, og, gnw, w_out, n_heads, zs_ref):
    dv = of.shape[1] // n_heads
    o = of + ob
    for h in range(n_heads):
        sl = slice(h * dv, (h + 1) * dv)
        zs_ref[:, sl] = (_rms(o[:, sl], gnw) * _silu(og[:, sl])).astype(BF16)
    return _dot(zs_ref[...], w_out)


def _mix_out_kernel(n_heads, colmajor, of_ref, ob_ref, og_ref, x_ref, mod_ref, gnw_ref, w_ref,
                    o_ref, stage_ref, xs_ref, zs_ref):
    _load_chunks(x_ref, stage_ref, xs_ref, colmajor)
    y = _head_out(of_ref[...], ob_ref[...], og_ref[...], gnw_ref[...], w_ref[...], n_heads, zs_ref)
    xn = xs_ref[...] + mod_ref[2:3, :] * y
    _store_chunks(xn, stage_ref, o_ref, colmajor)


def _mix_out(of, ob, og, x2, mod_a, layer, gnw, w_out, n_heads, bsz, *, colmajor):
    seq = x2.shape[0]
    d = x2.shape[1] // bsz
    wv = of.shape[1] // bsz
    gnw = gnw.reshape(1, -1)
    xv, xspec, tm = _x_stream(x2, bsz, colmajor)
    ospec = pl.BlockSpec((tm, wv), lambda t, b: (t, b))
    out = pl.pallas_call(
        functools.partial(_mix_out_kernel, n_heads, colmajor),
        grid=(seq // tm, bsz),
        in_specs=[ospec, ospec, ospec, xspec,
                  pl.BlockSpec((None, None, None, 3, d), lambda t, b: (layer, b, 1, 0, 0)),
                  pl.BlockSpec(gnw.shape, lambda t, b: (0, 0)),
                  pl.BlockSpec(w_out.shape, lambda t, b: (0, 0))],
        out_specs=xspec,
        out_shape=jax.ShapeDtypeStruct(xv.shape, F32),
        scratch_shapes=_chunk_scratch(tm, d) + [pltpu.VMEM((tm, wv), BF16)],
        compiler_params=_cparams("parallel", "parallel"),
        name="mix_out",
    )(of, ob, og, xv, mod_a, gnw, w_out)
    return out.reshape(seq, bsz * d)


def _gla_mixer(xc2, x2, mods, norm_w4, layer, prm, bsz, need_ctx):
    (modc_a, _), (modx_a, _) = mods
    w_in, wgate, bgate, gnw, w_out, dkt, dvt = prm
    qc, kc, vc, ogc, gfc, gbc = _gla_project(xc2, modc_a, norm_w4, layer, w_in, wgate, bgate, bsz,
                                             dkt, dvt, colmajor=False)
    qx, kx, vx, ogx, gfx, gbx = _gla_project(x2, modx_a, norm_w4, layer, w_in, wgate, bgate, bsz,
                                             dkt, dvt, colmajor=True)
    oc, ox = _bidir_scan((qc, kc, kc, vc, gfc, gbc), (qx, kx, kx, vx, gfx, gbx), GLA_HEADS, bsz, need_ctx)
    x_new = _mix_out(ox[0], ox[1], ogx, x2, modx_a, layer, gnw, w_out, GLA_HEADS, bsz, colmajor=True)
    xc_new = None
    if need_ctx:
        xc_new = _mix_out(oc[0], oc[1], ogc, xc2, modc_a, layer, gnw, w_out, GLA_HEADS, bsz, colmajor=False)
    return xc_new, x_new


def _hgrn_mixer(xc2, x2, mods, norm_w4, layer, prm, bsz, need_ctx):
    (modc_a, _), (modx_a, _) = mods
    w_in, lb, gnw, w_out, n_heads = prm
    qc, kfc, kbc, vc, gfc, gbc, ogc = _hgrn_project(xc2, modc_a, norm_w4, layer, w_in, lb, bsz)
    qx, kfx, kbx, vx, gfx, gbx, ogx = _hgrn_project(x2, modx_a, norm_w4, layer, w_in, lb, bsz)
    oc, ox = _bidir_scan((qc, kfc, kbc, vc, gfc, gbc), (qx, kfx, kbx, vx, gfx, gbx), n_heads, bsz, need_ctx)
    x_new = _mix_out(ox[0], ox[1], ogx, x2, modx_a, layer, gnw, w_out, n_heads, bsz, colmajor=False)
    xc_new = None
    if need_ctx:
        xc_new = _mix_out(oc[0], oc[1], ogc, xc2, modc_a, layer, gnw, w_out, n_heads, bsz, colmajor=False)
    return xc_new, x_new


def _forward(x, c, ctx, c_ctx, ada_w, ada_b, norm_w, ffn_w_in, ffn_w_out,
             s5_a_re, s5_a_im, s5_log_step, s5_b_re, s5_b_im, s5_c_re, s5_c_im, s5_d,
             s5_w_glu, s5_b_glu, gla_w_in, gla_w_gate, gla_b_gate, gla_norm_w, gla_w_out,
             hgrn_w_in, hgrn_lb_logits, hgrn_norm_w, hgrn_w_out, final_norm_w, depth=None):
    depth = ada_w.shape[0] if depth is None else depth
    bsz, seq, d = x.shape
    mods_x, mods_c = _ada(c, c_ctx, ada_w, ada_b)
    norm_w4 = norm_w.reshape(norm_w.shape[0], 3, 1, d)
    w_in = ffn_w_in.astype(BF16)
    w_out = ffn_w_out.astype(BF16)
    lb_soft = jax.nn.softmax(hgrn_lb_logits.astype(F32), axis=0)
    lb_all = jnp.cumsum(lb_soft, axis=0) - lb_soft[0]

    xc2 = jnp.transpose(ctx, (1, 0, 2)).reshape(ctx.shape[1], bsz * d)
    x2 = x
    for i in range(depth):
        last = i == depth - 1
        kind, j = i % N_MIXERS, i // N_MIXERS
        x2 = _ffn(x2, mods_x[0], norm_w4, w_in, w_out, i, 0, bsz, in_bld=(i == 0))
        xc2 = _ffn(xc2, mods_c[0], norm_w4, w_in, w_out, i, 0, bsz)
        mods = (mods_c, mods_x)
        if kind == 0:
            prm = _s5c_prep(s5_a_re[j], s5_a_im[j], s5_log_step[j], s5_b_re[j], s5_b_im[j],
                            s5_c_re[j], s5_c_im[j])
            extra = (s5_d[j], s5_w_glu[j].astype(BF16), s5_b_glu[j])
            yc, x2 = _s5c_mixer(xc2, x2, mods, norm_w4, i, prm, extra, bsz, not last)
        elif kind == 1:
            dkt = d // 2
            dvt = d
            n_low = 2 * GLA_RANK
            pad = LANES - n_low
            wi = jnp.pad(gla_w_in[j], ((0, 0), (0, pad))).astype(BF16)
            wg = jnp.zeros((LANES, 2 * dkt), F32)
            wg = wg.at[:GLA_RANK, :dkt].set(gla_w_gate[j, 0]).at[GLA_RANK:n_low, dkt:].set(gla_w_gate[j, 1])
            prm = (wi, wg.astype(BF16), gla_b_gate[j].reshape(1, 2 * dkt), gla_norm_w[j],
                   gla_w_out[j].astype(BF16), dkt, dvt)
            yc, x2 = _gla_mixer(xc2, x2, mods, norm_w4, i, prm, bsz, not last)
        else:
            prm = (hgrn_w_in[j].astype(BF16), lb_all[i], hgrn_norm_w[j], hgrn_w_out[j].astype(BF16),
                   d // HGRN_DK)
            yc, x2 = _hgrn_mixer(xc2, x2, mods, norm_w4, i, prm, bsz, not last)
        x2 = _ffn(x2, mods_x[0], norm_w4, w_in, w_out, i, 2, bsz,
                  final_w=final_norm_w.reshape(1, d) if last else None)
        if not last:
            xc2 = _ffn(yc, mods_c[0], norm_w4, w_in, w_out, i, 2, bsz)
    return x2


def kernel(x, c, ctx, c_ctx, ada_w, ada_b, norm_w, ffn_w_in, ffn_w_out, s5_a_re, s5_a_im, s5_log_step,
           s5_b_re, s5_b_im, s5_c_re, s5_c_im, s5_d, s5_w_glu, s5_b_glu, gla_w_in, gla_w_gate,
           gla_b_gate, gla_norm_w, gla_w_out, hgrn_w_in, hgrn_lb_logits, hgrn_norm_w, hgrn_w_out,
           final_norm_w):
    return _forward(x, c, ctx, c_ctx, ada_w, ada_b, norm_w, ffn_w_in, ffn_w_out, s5_a_re, s5_a_im,
                    s5_log_step, s5_b_re, s5_b_im, s5_c_re, s5_c_im, s5_d, s5_w_glu, s5_b_glu,
                    gla_w_in, gla_w_gate, gla_b_gate, gla_norm_w, gla_w_out, hgrn_w_in,
                    hgrn_lb_logits, hgrn_norm_w, hgrn_w_out, final_norm_w)
```

```python
import functools
import math

import numpy as np
import jax
import jax.numpy as jnp
from jax import lax
from jax.experimental import pallas as pl
from jax.experimental.pallas import tpu as pltpu

F32 = jnp.float32
BF16 = jnp.bfloat16

NORM_EPS = 1e-6
GRID_W = 64
N_MOD = 9
N_MIXERS = 3

S5_GROUP = 16
S5_STATE = 64
S5_LANE_GROUPS = 8
S5_TB = 64
S5_TS = 64

GLA_HEADS = 4
GLA_RANK = 16
GLA_TAU = 16.0
HGRN_DK = 128

SUBLANES = 8
LANES = 128
CHUNK = 128
GROUP = SUBLANES * SUBLANES
N_GROUPS = CHUNK // GROUP
SUB = CHUNK // SUBLANES
N_FINE = 3
N_COARSE = 3
N_LEVELS = 7
LOG2E = 1.4426950408889634

FFN_CHUNK = 768

VMEM_LIMIT_BYTES = 56 * 1024 * 1024


def _cparams(*sem):
    return pltpu.CompilerParams(dimension_semantics=sem, vmem_limit_bytes=VMEM_LIMIT_BYTES)


def _dot(a, b):
    return jnp.dot(a, b, preferred_element_type=F32)


def _dot_nt(a, b):
    return lax.dot_general(a, b, (((1,), (1,)), ((), ())), preferred_element_type=F32)


def _dot_tn(a, b):
    return lax.dot_general(a, b, (((0,), (0,)), ((), ())), preferred_element_type=F32)


def _sigmoid(x):
    return 1.0 / (1.0 + jnp.exp(-x))


def _silu(x):
    return x * _sigmoid(x)


def _gelu_tanh(x):
    c = math.sqrt(2.0 / math.pi)
    return 0.5 * x * (1.0 + jnp.tanh(c * (x + 0.044715 * (x * x * x))))


def _log_sigmoid(x):
    return jnp.minimum(x, 0.0) - jnp.log1p(jnp.exp(-jnp.abs(x)))


def _rms(x, w):
    ms = jnp.mean(x * x, axis=-1, keepdims=True)
    return (x * lax.rsqrt(ms + NORM_EPS)) * w


def _norm_mod(x, w, shift, scale):
    return _rms(x, w) * (1.0 + scale) + shift


def _ada_kernel(cc_ref, w_ref, b_ref, o_ref):
    a = _silu(cc_ref[...]).astype(BF16)
    o_ref[...] = _dot(a, w_ref[...].astype(BF16)) + b_ref[...]


def _ada(c, c_ctx, ada_w, ada_b):
    depth, d, nd = ada_w.shape
    bsz = c.shape[0]
    rows = ((bsz + 1 + SUBLANES - 1) // SUBLANES) * SUBLANES
    cc = jnp.zeros((rows, d), F32).at[:bsz].set(c).at[bsz].set(c_ctx)
    nblk = nd // d
    mod = pl.pallas_call(
        _ada_kernel,
        grid=(depth, nblk),
        in_specs=[
            pl.BlockSpec((rows, d), lambda i, n: (0, 0)),
            pl.BlockSpec((None, d, d), lambda i, n: (i, 0, n)),
            pl.BlockSpec((None, 1, d), lambda i, n: (i, 0, n)),
        ],
        out_specs=pl.BlockSpec((None, rows, d), lambda i, n: (i, 0, n)),
        out_shape=jax.ShapeDtypeStruct((depth, rows, nd), F32),
        compiler_params=_cparams("arbitrary", "arbitrary"),
        name="ada_mod",
    )(cc, ada_w, ada_b.reshape(depth, 1, nd))
    mx = mod[:, :bsz].reshape(depth, bsz, 3, 3, d)
    mc = jnp.broadcast_to(mod[:, bsz:bsz + 1], (depth, bsz, nd)).reshape(depth, bsz, 3, 3, d)
    return (mx, jnp.transpose(mx, (0, 2, 3, 1, 4))), (mc, jnp.transpose(mc, (0, 2, 3, 1, 4)))


def _ffn_kernel(chunks, final, x_ref, mod_ref, nw_ref, wi_ref, wo_ref, *rest):
    if final:
        fw_ref, o_ref = rest
    else:
        (o_ref,) = rest
    ff = wo_ref.shape[0]
    x = x_ref[...]
    h = _norm_mod(x, nw_ref[...], mod_ref[0:1, :], mod_ref[1:2, :]).astype(BF16)
    acc = None
    for lo, hi in chunks:
        g = _dot(h, wi_ref[:, lo:hi])
        u = _dot(h, wi_ref[:, ff + lo:ff + hi])
        part = _dot((_silu(g) * u).astype(BF16), wo_ref[lo:hi, :])
        acc = part if acc is None else acc + part
    y = x + (0.5 * mod_ref[2:3, :]) * acc
    if final:
        y = _rms(y, fw_ref[...])
    o_ref[...] = y


def _ff_chunks(ff, width):
    assert ff % LANES == 0 and width % LANES == 0
    return tuple((lo, min(lo + width, ff)) for lo in range(0, ff, width))


def _ffn(x, mod_a, norm_w4, w_in, w_out, layer, stage, bsz, *, in_bld=False, final_w=None):
    d = norm_w4.shape[-1]
    seq = x.shape[1] if in_bld else x.shape[0]
    ff = w_out.shape[2]
    tm = 512 if seq % 512 == 0 else 256
    s = 0 if stage == 0 else 1
    final = final_w is not None
    resident = pl.Buffered(1)
    if in_bld:
        x_spec = pl.BlockSpec((None, tm, d), lambda t, b: (b, t, 0))
    else:
        x_spec = pl.BlockSpec((tm, d), lambda t, b: (t, b))
    in_specs = [
        x_spec,
        pl.BlockSpec((None, None, None, 3, d), lambda t, b: (layer, b, stage, 0, 0)),
        pl.BlockSpec((None, None, 1, d), lambda t, b: (layer, stage, 0, 0)),
        pl.BlockSpec((None, None, d, 2 * ff), lambda t, b: (layer, s, 0, 0), pipeline_mode=resident),
        pl.BlockSpec((None, None, ff, d), lambda t, b: (layer, s, 0, 0), pipeline_mode=resident),
    ]
    args = [x, mod_a, norm_w4, w_in, w_out]
    if final:
        in_specs.append(pl.BlockSpec((1, d), lambda t, b: (0, 0)))
        args.append(final_w)
        out_spec = pl.BlockSpec((None, tm, d), lambda t, b: (b, t, 0))
        out_shape = jax.ShapeDtypeStruct((bsz, seq, d), F32)
    else:
        out_spec = pl.BlockSpec((tm, d), lambda t, b: (t, b))
        out_shape = jax.ShapeDtypeStruct((seq, bsz * d), F32)
    return pl.pallas_call(
        functools.partial(_ffn_kernel, _ff_chunks(ff, FFN_CHUNK), final),
        grid=(seq // tm, bsz),
        in_specs=in_specs,
        out_specs=out_spec,
        out_shape=out_shape,
        compiler_params=_cparams("parallel", "parallel"),
        name="ffn_half",
    )(*args)


def _s5_prep(a_re, a_im, log_step, b_re, b_im, c_re, c_im, bsz):
    dt = jnp.exp(log_step.astype(F32))[..., None]
    a_re = a_re.astype(F32)
    a_im = a_im.astype(F32)
    mag = jnp.exp(a_re * dt)
    abar_re = mag * jnp.cos(a_im * dt)
    abar_im = mag * jnp.sin(a_im * dt)
    den = a_re * a_re + a_im * a_im
    zr = abar_re - 1.0
    zi = abar_im
    coef_re = ((zr * a_re + zi * a_im) / den)[..., None]
    coef_im = ((zi * a_re - zr * a_im) / den)[..., None]
    b_re = b_re.astype(F32)
    b_im = b_im.astype(F32)
    bbar_re = coef_re * b_re - coef_im * b_im
    bbar_im = coef_re * b_im + coef_im * b_re
    n_dir, g, p = abar_re.shape
    gc = b_re.shape[-1]
    nj = g // S5_LANE_GROUPS
    lg = S5_LANE_GROUPS
    are = jnp.broadcast_to(abar_re.reshape(n_dir, 1, g * p), (n_dir, bsz, g * p))
    aim = jnp.broadcast_to(abar_im.reshape(n_dir, 1, g * p), (n_dir, bsz, g * p))
    eye = jnp.eye(lg, dtype=F32)
    bb = jnp.stack([bbar_re, bbar_im], axis=1).reshape(n_dir, 2, nj, lg, p, gc)
    bb = jnp.transpose(bb, (0, 2, 3, 5, 1, 4))
    wb = bb[:, :, :, :, :, None, :] * eye[None, None, :, None, None, :, None]
    wb = wb.reshape(n_dir, nj, lg * gc, 2 * lg * p).astype(BF16)
    cc = jnp.stack([c_re.astype(F32), -c_im.astype(F32)], axis=1).reshape(n_dir, 2, nj, lg, gc, p)
    cc = jnp.transpose(cc, (0, 2, 1, 3, 5, 4))
    wc = cc[:, :, :, :, :, None, :] * eye[None, None, None, :, None, :, None]
    wc = wc.reshape(n_dir, nj, 2 * lg * p, lg * gc).astype(BF16)
    return are, aim, wb, wc


def _s5_kernel(tb, rev, finish, x_ref, mod_ref, nw_ref, are_ref, aim_ref, wb_ref, wc_ref,
               h0re_ref, h0im_ref, *rest):
    n_io = 7 if finish else 3
    io, scratch = rest[:n_io], rest[n_io:]
    if finish:
        yf_ref, dsk_ref, wglu_ref, bglu_ref, o_ref, hre_ref, him_ref = io
        h_s = scratch[12]
    else:
        o_ref, hre_ref, him_ref = io
    bu = (scratch[0:2], scratch[2:4])
    hs = (scratch[4:6], scratch[6:8])
    hs_re, hs_im, y_s, hb_s = scratch[8:12]
    bsz = x_ref.shape[1]
    d = x_ref.shape[2]
    nj = wb_ref.shape[0]
    sw = wb_ref.shape[2] // 2
    ri = S5_TS * bsz
    nit = tb // S5_TS

    @pl.when(pl.program_id(0) == 0)
    def _():
        hs_re[...] = h0re_ref[...]
        hs_im[...] = h0im_ref[...]

    x3 = x_ref[...]
    ms = jnp.mean(x3 * x3, axis=-1, keepdims=True)
    h3 = (x3 * lax.rsqrt(ms + NORM_EPS)) * nw_ref[...] * (1.0 + mod_ref[1]) + mod_ref[0]
    h2 = h3.reshape(tb * bsz, d)
    if finish:
        h_s[...] = h2
    hb_s[...] = h2.astype(BF16)

    def lanes(j):
        return slice(j * LANES, (j + 1) * LANES)

    def drive(j, rows):
        r = _dot(hb_s[rows, lanes(j)], wb_ref[j])
        bu[j % 2][0][rows, :] = r[:, :sw]
        bu[j % 2][1][rows, :] = r[:, sw:]

    def readout(j, rows):
        y_s[rows, lanes(j)] = (_dot(hs[j % 2][0][rows, :].astype(BF16), wc_ref[j, :sw, :])
                               + _dot(hs[j % 2][1][rows, :].astype(BF16), wc_ref[j, sw:, :]))

    drive(0, slice(None))
    for j in range(nj):
        cur = j % 2
        ar = are_ref[:, j * sw:(j + 1) * sw]
        ai = aim_ref[:, j * sw:(j + 1) * sw]

        def body(i, carry, j=j, cur=cur, ar=ar, ai=ai):
            hr, hi = carry
            rows = pl.ds(pl.multiple_of(i * ri, ri), ri)
            if j >= 1:
                readout(j - 1, rows)
            if j + 1 < nj:
                drive(j + 1, rows)
            for s in range(S5_TS):
                t = i * S5_TS + s
                tt = (tb - 1 - t) if rev else t
                sl = pl.ds(pl.multiple_of(tt * bsz, bsz), bsz)
                hr, hi = (ar * hr - ai * hi + bu[cur][0][sl, :], ar * hi + ai * hr + bu[cur][1][sl, :])
                hs[cur][0][sl, :] = hr
                hs[cur][1][sl, :] = hi
            return hr, hi

        hr, hi = lax.fori_loop(
            0, nit, body, (hs_re[:, j * sw:(j + 1) * sw], hs_im[:, j * sw:(j + 1) * sw]))
        hs_re[:, j * sw:(j + 1) * sw] = hr
        hs_im[:, j * sw:(j + 1) * sw] = hi
    readout(nj - 1, slice(None))

    hre_ref[...] = hs_re[...]
    him_ref[...] = hs_im[...]
    if not finish:
        o_ref[...] = y_s[...].reshape(tb, bsz, d)
    else:
        y = y_s[...] + yf_ref[...].reshape(tb * bsz, d) + dsk_ref[...] * h_s[...]
        z = _dot(_gelu_tanh(y).astype(BF16), wglu_ref[...]) + bglu_ref[...]
        out = z[:, :d] * _sigmoid(z[:, d:])
        o_ref[...] = x3 + mod_ref[2] * out.reshape(tb, bsz, d)


def _s5_pass(x2, mod_b, norm_w4, layer, are, aim, wb, wc, h0, bsz, *, rev, fin=None):
    seq = x2.shape[0]
    d = x2.shape[1] // bsz
    tb = S5_TB
    nblk = seq // tb
    dirn = 1 if rev else 0
    nstate = are.shape[-1]
    sw = wb.shape[3] // 2
    x3 = x2.reshape(seq, bsz, d)
    blk = (lambda i: (nblk - 1 - i, 0, 0)) if rev else (lambda i: (i, 0, 0))
    x_spec = pl.BlockSpec((tb, bsz, d), blk)
    const2 = lambda i: (0, 0)
    in_specs = [
        x_spec,
        pl.BlockSpec((None, None, 3, bsz, d), lambda i: (layer, 1, 0, 0, 0)),
        pl.BlockSpec((None, None, 1, d), lambda i: (layer, 1, 0, 0)),
        pl.BlockSpec((None, bsz, nstate), lambda i: (dirn, 0, 0)),
        pl.BlockSpec((None, bsz, nstate), lambda i: (dirn, 0, 0)),
        pl.BlockSpec((None,) + wb.shape[1:], lambda i: (dirn, 0, 0, 0)),
        pl.BlockSpec((None,) + wc.shape[1:], lambda i: (dirn, 0, 0, 0)),
        pl.BlockSpec((bsz, nstate), const2),
        pl.BlockSpec((bsz, nstate), const2),
    ]
    args = [x3, mod_b, norm_w4, are, aim, wb, wc, h0[0], h0[1]]
    scratch = [pltpu.VMEM((tb * bsz, sw), F32) for _ in range(8)] + [
        pltpu.VMEM((bsz, nstate), F32), pltpu.VMEM((bsz, nstate), F32),
        pltpu.VMEM((tb * bsz, d), F32), pltpu.VMEM((tb * bsz, d), BF16),
    ]
    finish = fin is not None
    if finish:
        y_fwd, d_skip, w_glu, b_glu = fin
        in_specs += [
            x_spec,
            pl.BlockSpec((1, d), const2),
            pl.BlockSpec(w_glu.shape, const2),
            pl.BlockSpec((1, 2 * d), const2),
        ]
        args += [y_fwd.reshape(seq, bsz, d), d_skip.reshape(1, d), w_glu, b_glu.reshape(1, 2 * d)]
        scratch.append(pltpu.VMEM((tb * bsz, d), F32))
    out, hre, him = pl.pallas_call(
        functools.partial(_s5_kernel, tb, rev, finish),
        grid=(nblk,),
        in_specs=in_specs,
        out_specs=[x_spec, pl.BlockSpec((bsz, nstate), const2), pl.BlockSpec((bsz, nstate), const2)],
        out_shape=[jax.ShapeDtypeStruct((seq, bsz, d), F32),
                   jax.ShapeDtypeStruct((bsz, nstate), F32),
                   jax.ShapeDtypeStruct((bsz, nstate), F32)],
        scratch_shapes=scratch,
        compiler_params=_cparams("arbitrary"),
        name="s5_bwd_finish" if finish else "s5_fwd",
    )(*args)
    return out.reshape(seq, bsz * d), (hre, him)


def _s5_mixer(xc2, x2, mods_b, norm_w4, layer, prm, bsz, need_ctx):
    (modc_b, modx_b) = mods_b
    are, aim, wb, wc, d_skip, w_glu, b_glu = prm
    nstate = are.shape[-1]
    zero = (jnp.zeros((bsz, nstate), F32), jnp.zeros((bsz, nstate), F32))
    ycf, hf = _s5_pass(xc2, modc_b, norm_w4, layer, are, aim, wb, wc, zero, bsz, rev=False)
    yxf, _ = _s5_pass(x2, modx_b, norm_w4, layer, are, aim, wb, wc, hf, bsz, rev=False)
    xc_new, hb = _s5_pass(xc2, modc_b, norm_w4, layer, are, aim, wb, wc, zero, bsz, rev=True,
                          fin=(ycf, d_skip, w_glu, b_glu))
    x_new, _ = _s5_pass(x2, modx_b, norm_w4, layer, are, aim, wb, wc, hb, bsz, rev=True,
                        fin=(yxf, d_skip, w_glu, b_glu))
    return (xc_new if need_ctx else None), x_new


S5_T = 16
S5C_TB = 128
SLOT = LANES // SUBLANES


def _cmul(ar, ai, br, bi):
    return ar * br - ai * bi, ar * bi + ai * br


def _s5c_prep(a_re, a_im, log_step, b_re, b_im, c_re, c_im):
    dt = jnp.exp(log_step.astype(F32))[..., None]
    a_re = a_re.astype(F32)
    a_im = a_im.astype(F32)
    mag = jnp.exp(a_re * dt)
    abar_re = mag * jnp.cos(a_im * dt)
    abar_im = mag * jnp.sin(a_im * dt)
    den = a_re * a_re + a_im * a_im
    zr = abar_re - 1.0
    zi = abar_im
    coef_re = ((zr * a_re + zi * a_im) / den)[..., None]
    coef_im = ((zi * a_re - zr * a_im) / den)[..., None]
    b_re = b_re.astype(F32)
    b_im = b_im.astype(F32)
    bb_re = coef_re * b_re - coef_im * b_im
    bb_im = coef_re * b_im + coef_im * b_re
    cc_re = c_re.astype(F32)
    cc_im = c_im.astype(F32)
    t = S5_T
    pr, pi = [jnp.ones_like(abar_re)], [jnp.zeros_like(abar_im)]
    for _ in range(t):
        nr, ni = _cmul(pr[-1], pi[-1], abar_re, abar_im)
        pr.append(nr)
        pi.append(ni)
    pr, pi = jnp.stack(pr), jnp.stack(pi)
    hi = lax.Precision.HIGHEST
    g, p, gc = abar_re.shape[1], abar_re.shape[2], b_re.shape[-1]
    prk = jnp.transpose(pr, (1, 2, 3, 0))
    pik = jnp.transpose(pi, (1, 2, 3, 0))
    e_re, e_im = _cmul(prk[..., None], pik[..., None], bb_re[:, :, :, None, :], bb_im[:, :, :, None, :])
    nkc = (t + 1) * gc
    taps = (jnp.einsum('dgop,dgpn->dgon', cc_re, e_re.reshape(2, g, p, nkc), precision=hi)
            - jnp.einsum('dgop,dgpn->dgon', cc_im, e_im.reshape(2, g, p, nkc), precision=hi))
    taps = taps.reshape(2, g, gc, t + 1, gc)
    kf, kb = taps[0], taps[1]
    seq = jnp.concatenate([jnp.flip(kb[:, :, 1:t], axis=2), kf[:, :, :1] + kb[:, :, :1], kf[:, :, 1:t]],
                          axis=2)
    tab = jnp.transpose(seq, (0, 3, 2, 1)).reshape(g, gc, (2 * t - 1) * gc)
    m = jnp.stack([tab[:, :, gc * (t - 1 - s):gc * (t - 1 - s) + t * gc] for s in range(t)], axis=1)
    m = m.reshape(g, t * gc, t * gc)

    def wst(d, flip):
        er, ei = e_re[d, :, :, :t], e_im[d, :, :, :t]
        if flip:
            er, ei = jnp.flip(er, 2), jnp.flip(ei, 2)
        w = jnp.concatenate([jnp.transpose(er, (0, 2, 3, 1)), jnp.transpose(ei, (0, 2, 3, 1))], axis=-1)
        return w.reshape(g, t * gc, 2 * p)

    wst_f = wst(0, True)
    wst_b = wst(1, False)

    def wout(d, flip):
        qr, qi = prk[d, :, :, 1:t + 1], pik[d, :, :, 1:t + 1]
        if flip:
            qr, qi = jnp.flip(qr, 2), jnp.flip(qi, 2)
        cr = jnp.transpose(cc_re[d], (0, 2, 1))
        ci = jnp.transpose(cc_im[d], (0, 2, 1))
        gr, gi = _cmul(cr[:, :, None, :], ci[:, :, None, :], qr[..., None], qi[..., None])
        return jnp.concatenate([gr, -gi], axis=1).reshape(g, 2 * p, t * gc)

    wout_f = wout(0, False)
    wout_b = wout(1, True)
    a1 = jnp.concatenate([pr[t], pr[t]], axis=-1)
    a2 = jnp.concatenate([-pi[t], pi[t]], axis=-1)
    bc = lambda a: jnp.broadcast_to(a[:, :, None, :], a.shape[:2] + (SUBLANES, a.shape[-1]))
    return dict(m=m.astype(BF16), wst=(wst_f.astype(BF16), wst_b.astype(BF16)),
                wout=(wout_f.astype(BF16), wout_b.astype(BF16)), a1=bc(a1), a2=bc(a2))


def _slot_transpose(x):
    m, n, bsz, _ = x.shape
    slot = lax.broadcasted_iota(jnp.int32, (bsz, LANES), 1) // SLOT

    def rot(v, shift):
        return pltpu.roll(v.reshape(-1, LANES), shift, 1).reshape(v.shape)

    d = n // 2
    while d >= 1:
        keep = (slot & d) == 0
        x = x.reshape(m, n // (2 * d), 2, d, bsz, LANES)
        lo, hi = x[:, :, 0], x[:, :, 1]
        new_lo = jnp.where(keep, lo, rot(hi, SLOT * d))
        new_hi = jnp.where(keep, rot(lo, LANES - SLOT * d), hi)
        x = jnp.stack([new_lo, new_hi], axis=2).reshape(m, n, bsz, LANES)
        d //= 2
    return x


def _s5c_gather(h_s, a_s, nch):
    nj = h_s.shape[1] // LANES
    nh = S5_T // SUBLANES
    for j in range(nj):
        x = h_s[:, j * LANES:(j + 1) * LANES].reshape(nch * nh, SUBLANES, SUBLANES, LANES)
        r = _slot_transpose(x).reshape(nch, nh, SUBLANES, SUBLANES, LANES)
        for gl in range(S5_LANE_GROUPS):
            for hh in range(nh):
                a_s[S5_LANE_GROUPS * j + gl, :, hh * LANES:(hh + 1) * LANES] = (
                    r[:, hh, gl].reshape(nch * SUBLANES, LANES))


def _s5c_state_scan(v, st_ref, sin_ref, a1, a2, g, nch, rev):
    gl = slice(g * LANES, (g + 1) * LANES)
    s = st_ref[:, gl]
    w = pltpu.roll(s, LANES // 2, 1)
    for n in (range(nch - 1, -1, -1) if rev else range(nch)):
        rows = slice(SUBLANES * n, SUBLANES * (n + 1))
        sin_ref[rows, gl] = s
        vn = v[rows, :]
        s, w = a1 * s + a2 * w + vn, a1 * w - a2 * s + pltpu.roll(vn, LANES // 2, 1)
    st_ref[:, gl] = s


def _s5c_norm(x_ref, mod_ref, nw_ref, h_s):
    x3 = x_ref[...]
    ms = jnp.mean(x3 * x3, axis=-1, keepdims=True)
    h3 = (x3 * lax.rsqrt(ms + NORM_EPS)) * nw_ref[...] * (1.0 + mod_ref[1]) + mod_ref[0]
    h_s[...] = h3.reshape(h_s.shape)


def _s5c_fwd_kernel(x_ref, mod_ref, nw_ref, wst_ref, a1_ref, a2_ref, h0_ref,
                    sin_ref, hfin_ref, st_ref, h_s, a_s):
    ng = wst_ref.shape[0]
    nch = a_s.shape[1] // SUBLANES

    @pl.when(pl.program_id(0) == 0)
    def _():
        st_ref[...] = h0_ref[...]

    _s5c_norm(x_ref, mod_ref, nw_ref, h_s)
    _s5c_gather(h_s, a_s, nch)
    for g in range(ng):
        v = _dot(a_s[g].astype(BF16), wst_ref[g])
        _s5c_state_scan(v, st_ref, sin_ref, a1_ref[g], a2_ref[g], g, nch, False)
    hfin_ref[...] = st_ref[...]


def _s5c_bwd_kernel(x_ref, mod_ref, nw_ref, sinf_ref, m_ref, wst_ref, woutf_ref, woutb_ref,
                    a1_ref, a2_ref, h0_ref, dsk_ref, y_ref, hfin_ref, st_ref, h_s, a_s, sinb_s):
    ng = m_ref.shape[0]
    nch = a_s.shape[1] // SUBLANES

    @pl.when(pl.program_id(0) == 0)
    def _():
        st_ref[...] = h0_ref[...]

    _s5c_norm(x_ref, mod_ref, nw_ref, h_s)
    _s5c_gather(h_s, a_s, nch)
    for j in range(ng // S5_LANE_GROUPS):
        ys = []
        for gl in range(S5_LANE_GROUPS):
            g = S5_LANE_GROUPS * j + gl
            lanes = slice(g * LANES, (g + 1) * LANES)
            a = a_s[g].astype(BF16)
            _s5c_state_scan(_dot(a, wst_ref[g]), st_ref, sinb_s, a1_ref[g], a2_ref[g], g, nch, True)
            ys.append(_dot(a, m_ref[g])
                      + _dot(sinf_ref[:, lanes].astype(BF16), woutf_ref[g])
                      + _dot(sinb_s[:, lanes].astype(BF16), woutb_ref[g]))
        cols = slice(j * LANES, (j + 1) * LANES)
        for hh in range(S5_T // SUBLANES):
            r = jnp.stack([y[:, hh * LANES:(hh + 1) * LANES].reshape(nch, SUBLANES, LANES) for y in ys],
                          axis=1)
            out = _slot_transpose(r)
            for n in range(nch):
                t0 = S5_T * n + SUBLANES * hh
                skip = dsk_ref[:, cols] * h_s[t0 * SUBLANES:(t0 + SUBLANES) * SUBLANES, cols]
                y_ref[t0:t0 + SUBLANES, :, cols] = out[n] + skip.reshape(SUBLANES, SUBLANES, LANES)
    hfin_ref[...] = st_ref[...]


def _s5c_glu_kernel(y_ref, x_ref, mod_ref, w_ref, b_ref, o_ref):
    d = x_ref.shape[1]
    z = _dot(_gelu_tanh(y_ref[...]).astype(BF16), w_ref[...]) + b_ref[...]
    o_ref[...] = x_ref[...] + mod_ref[2:3, :] * (z[:, :d] * _sigmoid(z[:, d:]))


def _s5c_stream(x2, mod_b, norm_w4, layer, prm, h0, bsz, *, rev, sin_f=None, d_skip=None):
    seq = x2.shape[0]
    d = x2.shape[1] // bsz
    tb = S5C_TB
    nblk = seq // tb
    nch = tb // S5_T
    dirn = 1 if rev else 0
    ng = prm["m"].shape[0]
    ns = ng * LANES
    x3 = x2.reshape(seq, bsz, d)
    blk = (lambda i: (nblk - 1 - i, 0, 0)) if rev else (lambda i: (i, 0, 0))
    blk2 = (lambda i: (nblk - 1 - i, 0)) if rev else (lambda i: (i, 0))
    x_spec = pl.BlockSpec((tb, bsz, d), blk)
    sin_spec = pl.BlockSpec((nch * bsz, ns), blk2)
    st_spec = pl.BlockSpec((bsz, ns), lambda i: (0, 0))
    res = pl.Buffered(1)
    wspec = lambda w: pl.BlockSpec(w.shape, lambda i: (0,) * w.ndim, pipeline_mode=res)
    aspec = pl.BlockSpec((None,) + prm["a1"].shape[1:], lambda i: (dirn, 0, 0, 0), pipeline_mode=res)
    common = [x_spec,
              pl.BlockSpec((None, None, 3, bsz, d), lambda i: (layer, 1, 0, 0, 0)),
              pl.BlockSpec((None, None, 1, d), lambda i: (layer, 1, 0, 0))]
    scratch = [pltpu.VMEM((bsz, ns), F32), pltpu.VMEM((tb * bsz, d), F32),
               pltpu.VMEM((ng, nch * bsz, S5_T * S5_GROUP), F32)]
    if not rev:
        return pl.pallas_call(
            _s5c_fwd_kernel,
            grid=(nblk,),
            in_specs=common + [wspec(prm["wst"][0]), aspec, aspec, st_spec],
            out_specs=[sin_spec, st_spec],
            out_shape=[jax.ShapeDtypeStruct((seq // S5_T * bsz, ns), F32),
                       jax.ShapeDtypeStruct((bsz, ns), F32)],
            scratch_shapes=scratch,
            compiler_params=_cparams("arbitrary"),
            name="s5_fwd_states",
        )(x3, mod_b, norm_w4, prm["wst"][0], prm["a1"], prm["a2"], h0)
    y, hfin = pl.pallas_call(
        _s5c_bwd_kernel,
        grid=(nblk,),
        in_specs=common + [sin_spec, wspec(prm["m"]), wspec(prm["wst"][1]), wspec(prm["wout"][0]),
                           wspec(prm["wout"][1]), aspec, aspec, st_spec,
                           pl.BlockSpec((1, d), lambda i: (0, 0))],
        out_specs=[x_spec, st_spec],
        out_shape=[jax.ShapeDtypeStruct((seq, bsz, d), F32), jax.ShapeDtypeStruct((bsz, ns), F32)],
        scratch_shapes=scratch + [pltpu.VMEM((nch * bsz, ns), F32)],
        compiler_params=_cparams("arbitrary"),
        name="s5_bwd_readout",
    )(x3, mod_b, norm_w4, sin_f, prm["m"], prm["wst"][1], prm["wout"][0], prm["wout"][1],
      prm["a1"], prm["a2"], h0, d_skip.reshape(1, d))
    return y.reshape(seq, bsz * d), hfin


def _s5c_glu(y2, x2, mod_a, layer, w_glu, b_glu, bsz):
    seq = x2.shape[0]
    d = x2.shape[1] // bsz
    tm = 512 if seq % 512 == 0 else 256
    row = pl.BlockSpec((tm, d), lambda t, b: (t, b))
    return pl.pallas_call(
        _s5c_glu_kernel,
        grid=(seq // tm, bsz),
        in_specs=[row, row,
                  pl.BlockSpec((None, None, None, 3, d), lambda t, b: (layer, b, 1, 0, 0)),
                  pl.BlockSpec(w_glu.shape, lambda t, b: (0, 0), pipeline_mode=pl.Buffered(1)),
                  pl.BlockSpec((1, 2 * d), lambda t, b: (0, 0))],
        out_specs=row,
        out_shape=jax.ShapeDtypeStruct((seq, bsz * d), F32),
        compiler_params=_cparams("parallel", "parallel"),
        name="s5_glu",
    )(y2, x2, mod_a, w_glu, b_glu.reshape(1, 2 * d))


def _s5c_mixer(xc2, x2, mods, norm_w4, layer, prm, extra, bsz, need_ctx):
    (modc_a, modc_b), (modx_a, modx_b) = mods
    d_skip, w_glu, b_glu = extra
    ns = prm["m"].shape[0] * LANES
    zero = jnp.zeros((bsz, ns), F32)
    sc_f, hf = _s5c_stream(xc2, modc_b, norm_w4, layer, prm, zero, bsz, rev=False)
    sx_f, _ = _s5c_stream(x2, modx_b, norm_w4, layer, prm, hf, bsz, rev=False)
    yc, hb = _s5c_stream(xc2, modc_b, norm_w4, layer, prm, zero, bsz, rev=True, sin_f=sc_f, d_skip=d_skip)
    yx, _ = _s5c_stream(x2, modx_b, norm_w4, layer, prm, hb, bsz, rev=True, sin_f=sx_f, d_skip=d_skip)
    x_new = _s5c_glu(yx, x2, modx_a, layer, w_glu, b_glu, bsz)
    xc_new = _s5c_glu(yc, xc2, modc_a, layer, w_glu, b_glu, bsz) if need_ctx else None
    return xc_new, x_new


def _level_matrix(rev):
    p = np.arange(CHUNK)
    v, r = p // SUBLANES, p % SUBLANES
    tau = GROUP * (v // SUBLANES) + SUBLANES * r + v % SUBLANES
    if rev:
        tau = CHUNK - 1 - tau
    ti, tj = tau[:, None], tau[None, :]
    x = ti ^ tj
    lvl = np.zeros((CHUNK, CHUNK), np.int32)
    for b in range(N_LEVELS):
        lvl = np.where((x >> b) & 1, b + 1, lvl)
    lvl = np.where(tj > ti, -1, lvl)
    return lvl.astype(np.int32)


def _scan_chunk_head(q, k, vb, g, st_t, masks, rev):
    dk = q.shape[1]
    ng, nv = N_GROUPS, SUBLANES
    ea = (lambda e: SUB - 1 - e) if rev else (lambda e: e)
    ksub = (lambda e: SUBLANES - 1 - e) if rev else (lambda e: e)

    def vregs(x, scale=None):
        out = [x[SUBLANES * ea(e):SUBLANES * ea(e) + SUBLANES, :] for e in range(SUB)]
        if scale is not None:
            out = [o * scale for o in out]
        return [out[nv * gi:nv * (gi + 1)] for gi in range(ng)]

    def assemble(vs):
        flat = [vs[gi][a] for gi in range(ng) for a in range(nv)]
        return jnp.concatenate([flat[ea(a)] for a in range(SUB)], axis=0)

    def zeros():
        return [[zero] * nv for _ in range(ng)]

    qv, kv, gv = vregs(q), vregs(k), vregs(g, LOG2E)
    zero = jnp.zeros((SUBLANES, dk), F32)
    ninf = jnp.full((SUBLANES, dk), -jnp.inf, F32)

    sub_i = lax.broadcasted_iota(jnp.int32, (SUBLANES, dk), 0)
    re = (SUBLANES - 1 - sub_i) if rev else sub_i

    def row(x, e):
        kk = ksub(e)
        return jnp.broadcast_to(x[kk:kk + 1, :], (SUBLANES, dk))

    pf, tot, ct, xcl, gtot = [], [], [], [], []
    for gi in range(ng):
        p = [gv[gi][0]]
        for a in range(1, nv):
            p.append(p[-1] + gv[gi][a])
        t = p[nv - 1]
        c = zero
        for e in range(SUBLANES):
            c = c + jnp.where(re >= e, row(t, e), 0.0)
        pf.append(p)
        tot.append(t)
        ct.append(c)
        xcl.append(c - t)
        gtot.append(row(c, SUBLANES - 1))

    q_lv, k_lv = [], []
    for lvl in range(1, N_FINE + 1):
        s = 1 << lvl
        qs, ks = zeros(), zeros()
        for gi in range(ng):
            for bs in range(0, nv, s):
                m = bs + s // 2
                ref = pf[gi][m - 1]
                for a in range(bs, m - 1):
                    ks[gi][a] = kv[gi][a] * jnp.exp2(ref - pf[gi][a])
                ks[gi][m - 1] = kv[gi][m - 1]
                for a in range(m, bs + s):
                    qs[gi][a] = qv[gi][a] * jnp.exp2(pf[gi][a] - ref)
        q_lv.append(assemble(qs))
        k_lv.append(assemble(ks))
    for lvl in range(N_FINE + 1, N_FINE + N_COARSE + 1):
        w = 1 << (lvl - N_FINE)
        right = (re & (w // 2)) != 0
        qs, ks = zeros(), zeros()
        for gi in range(ng):
            xm = zero
            for bs in range(0, SUBLANES, w):
                inblk = (re >= bs) & (re < bs + w)
                xm = xm + jnp.where(inblk, row(xcl[gi], bs + w // 2), 0.0)
            drq = jnp.where(right, xcl[gi] - xm, ninf)
            dlk = jnp.where(right, ninf, tot[gi] + xm - ct[gi])
            for a in range(nv):
                qs[gi][a] = qv[gi][a] * jnp.exp2(pf[gi][a] + drq)
                ks[gi][a] = kv[gi][a] * jnp.exp2(dlk - pf[gi][a])
        q_lv.append(assemble(qs))
        k_lv.append(assemble(ks))
    qs, ks = zeros(), zeros()
    for a in range(nv):
        ks[0][a] = kv[0][a] * jnp.exp2(tot[0] + (gtot[0] - ct[0]) - pf[0][a])
        qs[1][a] = qv[1][a] * jnp.exp2(pf[1][a] + xcl[1])
    q_lv.append(assemble(qs))
    k_lv.append(assemble(ks))

    sc = jnp.where(masks[0], jnp.sum(q * k, axis=-1, keepdims=True), 0.0)
    for lvl in range(1, N_LEVELS + 1):
        s_l = _dot_nt(q_lv[lvl - 1].astype(BF16), k_lv[lvl - 1].astype(BF16))
        sc = jnp.where(masks[lvl], s_l, sc)

    e0, e1 = jnp.exp2(gtot[0]), jnp.exp2(gtot[1])
    qc = zeros()
    kd = zeros()
    for a in range(nv):
        qc[0][a] = qv[0][a] * jnp.exp2(pf[0][a] + xcl[0])
        qc[1][a] = qs[1][a] * e0
        kd[0][a] = ks[0][a] * e1
        kd[1][a] = kv[1][a] * jnp.exp2(tot[1] + (gtot[1] - ct[1]) - pf[1][a])
    o = _dot(sc.astype(BF16), vb) + _dot_nt(assemble(qc).astype(BF16), st_t.astype(BF16))
    st_new = (e0 * e1)[0:1, :] * st_t + _dot_tn(vb, assemble(kd).astype(BF16))
    return o, st_new


def _scan_kernel(n_heads, dk, dv, qf_ref, kf_ref, vf_ref, gf_ref, qb_ref, kb_ref, vb_ref, gb_ref,
                 lv_ref, s0_ref, of_ref, ob_ref, sf_ref, st_ref):
    @pl.when(pl.program_id(1) == 0)
    def _():
        st_ref[...] = s0_ref[...]

    dirs = ((qf_ref, kf_ref, vf_ref, gf_ref, of_ref, False), (qb_ref, kb_ref, vb_ref, gb_ref, ob_ref, True))
    masks = [[lv_ref[d] == lvl for lvl in range(N_LEVELS + 1)] for d in range(2)]
    for h in range(n_heads):
        ks = slice(h * dk, (h + 1) * dk)
        vs = slice(h * dv, (h + 1) * dv)
        for d, (q_ref, k_ref, v_ref, g_ref, o_ref, rev) in enumerate(dirs):
            o, st_new = _scan_chunk_head(q_ref[:, ks], k_ref[:, ks], v_ref[:, vs], g_ref[:, ks],
                                         st_ref[d, vs, :], masks[d], rev)
            o_ref[:, vs] = o
            st_ref[d, vs, :] = st_new
    sf_ref[...] = st_ref[...]


def _gated_scan(q, kf, kb, v, gf, gb, s0, n_heads, bsz):
    seq = q.shape[0]
    wk = q.shape[1] // bsz
    wv = v.shape[1] // bsz
    dk, dv = wk // n_heads, wv // n_heads
    nchunk = seq // CHUNK
    fwd = lambda b, n: (n, b)
    bwd = lambda b, n: (nchunk - 1 - n, b)
    kspec = lambda im: pl.BlockSpec((CHUNK, wk), im)
    vspec = lambda im: pl.BlockSpec((CHUNK, wv), im)
    sspec = pl.BlockSpec((2, None, wv, dk), lambda b, n: (0, b, 0, 0))
    lv = jnp.asarray(np.stack([_level_matrix(False), _level_matrix(True)]))
    return pl.pallas_call(
        functools.partial(_scan_kernel, n_heads, dk, dv),
        grid=(bsz, nchunk),
        in_specs=[kspec(fwd), kspec(fwd), vspec(fwd), kspec(fwd),
                  kspec(bwd), kspec(bwd), vspec(bwd), kspec(bwd),
                  pl.BlockSpec((2, CHUNK, CHUNK), lambda b, n: (0, 0, 0)), sspec],
        out_specs=[vspec(fwd), vspec(bwd), sspec],
        out_shape=[jax.ShapeDtypeStruct((seq, bsz * wv), F32),
                   jax.ShapeDtypeStruct((seq, bsz * wv), F32),
                   jax.ShapeDtypeStruct((2, bsz, wv, dk), F32)],
        scratch_shapes=[pltpu.VMEM((2, wv, dk), F32)],
        compiler_params=_cparams("parallel", "arbitrary"),
        name="gated_scan",
    )(q, kf, v, gf, q, kb, v, gb, lv, s0)


def _bidir_scan(ctx_in, lat_in, n_heads, bsz, need_ctx):
    qc, kfc, kbc, vc, gfc, gbc = ctx_in
    qx, kfx, kbx, vx, gfx, gbx = lat_in
    wk = qc.shape[1] // bsz
    wv = vc.shape[1] // bsz
    s0 = jnp.zeros((2, bsz, wv, wk // n_heads), F32)
    oc_f, oc_b, s_c = _gated_scan(qc, kfc, kbc, vc, gfc, gbc, s0, n_heads, bsz)
    ox_f, ox_b, _ = _gated_scan(qx, kfx, kbx, vx, gfx, gbx, s_c, n_heads, bsz)
    return ((oc_f, oc_b) if need_ctx else None), (ox_f, ox_b)


def _chunk_rows(c, colmajor):
    if colmajor:
        return lambda v: (SUBLANES * (v % SUBLANES) + N_GROUPS * c + v // SUBLANES, GROUP)
    return lambda v: (c * CHUNK + GROUP * (v // SUBLANES) + v % SUBLANES, SUBLANES)


def _permute_chunk_in(stage_ref, dst_ref, c, colmajor):
    rows = _chunk_rows(c, colmajor)
    for jl in range(stage_ref.shape[0]):
        for v in range(SUB):
            start, stride = rows(v)
            dst_ref[c * CHUNK + SUBLANES * v:c * CHUNK + SUBLANES * (v + 1), jl * LANES:(jl + 1) * LANES] = (
                stage_ref[jl, pl.ds(start, SUBLANES, stride=stride), :])


def _permute_chunk_out(val, stage_ref, c, colmajor):
    rows = _chunk_rows(c, colmajor)
    for jl in range(stage_ref.shape[0]):
        for v in range(SUB):
            start, stride = rows(v)
            stage_ref[jl, pl.ds(start, SUBLANES, stride=stride), :] = (
                val[SUBLANES * v:SUBLANES * (v + 1), jl * LANES:(jl + 1) * LANES])


def _stage_block(x_ref, stage_ref):
    n = stage_ref.shape[1]
    for jl in range(stage_ref.shape[0]):
        blk = x_ref[..., jl * LANES:(jl + 1) * LANES]
        stage_ref[jl] = blk.reshape(n, LANES)


def _unstage_block(stage_ref, o_ref):
    for jl in range(stage_ref.shape[0]):
        o_ref[..., jl * LANES:(jl + 1) * LANES] = stage_ref[jl].reshape(o_ref.shape[:-1] + (LANES,))


def _hgrn_epilogue(p, lb, d):
    q = _silu(p[:, :d])
    v = p[:, d:2 * d]
    zf = p[:, 2 * d:3 * d]
    zb = p[:, 3 * d:4 * d]
    og = p[:, 4 * d:]
    lf, lbk = lb[0:1, :], lb[1:2, :]
    gf = jnp.log(lf + (1.0 - lf) * _sigmoid(zf))
    gb = jnp.log(lbk + (1.0 - lbk) * _sigmoid(zb))
    kf = (1.0 - lf) * _sigmoid(-zf)
    kb = (1.0 - lbk) * _sigmoid(-zb)
    return q, kf, kb, v, gf, gb, og


def _x_stream(x2, bsz, colmajor):
    seq, d = x2.shape[0], x2.shape[1] // bsz
    if colmajor:
        rows = seq // GRID_W
        assert rows == GROUP, "column-major chunks assume one grid column per 64-step group"
        return (x2.reshape(rows, GRID_W, bsz * d),
                pl.BlockSpec((rows, SUBLANES, d), lambda t, b: (0, t, b)), SUBLANES * rows)
    tm = 2 * CHUNK
    return x2, pl.BlockSpec((tm, d), lambda t, b: (t, b)), tm


def _load_chunks(x_ref, stage_ref, xs_ref, colmajor):
    _stage_block(x_ref, stage_ref)
    for c in range(xs_ref.shape[0] // CHUNK):
        _permute_chunk_in(stage_ref, xs_ref, c, colmajor)


def _store_chunks(xn, stage_ref, o_ref, colmajor):
    for c in range(xn.shape[0] // CHUNK):
        _permute_chunk_out(xn[c * CHUNK:(c + 1) * CHUNK, :], stage_ref, c, colmajor)
    _unstage_block(stage_ref, o_ref)


def _chunk_scratch(tm, d):
    return [pltpu.VMEM((d // LANES, tm, LANES), F32), pltpu.VMEM((tm, d), F32)]


def _hgrn_proj_kernel(x_ref, mod_ref, nw_ref, w_ref, lb_ref,
                      q_ref, kf_ref, kb_ref, v_ref, gf_ref, gb_ref, og_ref, stage_ref, xs_ref):
    d = xs_ref.shape[1]
    _load_chunks(x_ref, stage_ref, xs_ref, False)
    h = _norm_mod(xs_ref[...], nw_ref[...], mod_ref[0:1, :], mod_ref[1:2, :]).astype(BF16)
    outs = _hgrn_epilogue(_dot(h, w_ref[...]), lb_ref[...], d)
    for ref, val in zip((q_ref, kf_ref, kb_ref, v_ref, gf_ref, gb_ref, og_ref), outs):
        ref[...] = val.astype(ref.dtype)


def _hgrn_project(x2, mod_a, norm_w4, layer, w_in, lb, bsz):
    seq = x2.shape[0]
    d = x2.shape[1] // bsz
    xv, xspec, tm = _x_stream(x2, bsz, False)
    row = pl.BlockSpec((tm, d), lambda t, b: (t, b))
    return pl.pallas_call(
        _hgrn_proj_kernel,
        grid=(seq // tm, bsz),
        in_specs=[
            xspec,
            pl.BlockSpec((None, None, None, 3, d), lambda t, b: (layer, b, 1, 0, 0)),
            pl.BlockSpec((None, None, 1, d), lambda t, b: (layer, 1, 0, 0)),
            pl.BlockSpec(w_in.shape, lambda t, b: (0, 0), pipeline_mode=pl.Buffered(1)),
            pl.BlockSpec(lb.shape, lambda t, b: (0, 0)),
        ],
        out_specs=[row] * 7,
        out_shape=[jax.ShapeDtypeStruct((seq, bsz * d), BF16 if i == 3 else F32) for i in range(7)],
        scratch_shapes=_chunk_scratch(tm, d),
        compiler_params=_cparams("parallel", "parallel"),
        name="hgrn_project",
    )(xv, mod_a, norm_w4, w_in, lb)


def _gla_epilogue(p, wgate, bgate, dkt, dvt):
    q = p[:, :dkt] * ((dkt // GLA_HEADS) ** -0.5)
    k = p[:, dkt:2 * dkt]
    v = p[:, 2 * dkt:2 * dkt + dvt]
    og = p[:, 2 * dkt + dvt:2 * dkt + 2 * dvt]
    low = p[:, 2 * dkt + 2 * dvt:]
    lg = _log_sigmoid(_dot(low.astype(BF16), wgate) + bgate) * (1.0 / GLA_TAU)
    return q, k, v, og, lg[:, :dkt], lg[:, dkt:]


def _gla_proj_kernel(colmajor, x_ref, mod_ref, nw_ref, w_ref, wg_ref, bg_ref,
                     q_ref, k_ref, v_ref, og_ref, gf_ref, gb_ref, stage_ref, xs_ref):
    dkt = q_ref.shape[1]
    dvt = v_ref.shape[1]
    _load_chunks(x_ref, stage_ref, xs_ref, colmajor)
    h = _norm_mod(xs_ref[...], nw_ref[...], mod_ref[0:1, :], mod_ref[1:2, :]).astype(BF16)
    outs = _gla_epilogue(_dot(h, w_ref[...]), wg_ref[...], bg_ref[...], dkt, dvt)
    for ref, val in zip((q_ref, k_ref, v_ref, og_ref, gf_ref, gb_ref), outs):
        ref[...] = val.astype(ref.dtype)


def _gla_project(x2, mod_a, norm_w4, layer, w_in, wgate, bgate, bsz, dkt, dvt, *, colmajor):
    seq = x2.shape[0]
    d = x2.shape[1] // bsz
    widths = (dkt, dkt, dvt, dvt, dkt, dkt)
    xv, xspec, tm = _x_stream(x2, bsz, colmajor)
    const = lambda t, b: (0, 0)
    return pl.pallas_call(
        functools.partial(_gla_proj_kernel, colmajor),
        grid=(seq // tm, bsz),
        in_specs=[
            xspec,
            pl.BlockSpec((None, None, None, 3, d), lambda t, b: (layer, b, 1, 0, 0)),
            pl.BlockSpec((None, None, 1, d), lambda t, b: (layer, 1, 0, 0)),
            pl.BlockSpec(w_in.shape, const, pipeline_mode=pl.Buffered(1)),
            pl.BlockSpec(wgate.shape, const),
            pl.BlockSpec(bgate.shape, const),
        ],
        out_specs=[pl.BlockSpec((tm, w), lambda t, b: (t, b)) for w in widths],
        out_shape=[jax.ShapeDtypeStruct((seq, bsz * w), BF16 if i == 2 else F32)
                   for i, w in enumerate(widths)],
        scratch_shapes=_chunk_scratch(tm, d),
        compiler_params=_cparams("parallel", "parallel"),
        name="gla_project",
    )(xv, mod_a, norm_w4, w_in, wgate, bgate)


def _head_out(of, ob, og, gnw, w_out, n_heads, zs_ref):
    dv = of.shape[1] // n_heads
    o = of + ob
    for h in range(n_heads):
        sl = slice(h * dv, (h + 1) * dv)
        zs_ref[:, sl] = (_rms(o[:, sl], gnw) * _silu(og[:, sl])).astype(BF16)
    return _dot(zs_ref[...], w_out)


def _mix_out_kernel(n_heads, colmajor, of_ref, ob_ref, og_ref, x_ref, mod_ref, gnw_ref, w_ref,
                    o_ref, stage_ref, xs_ref, zs_ref):
    _load_chunks(x_ref, stage_ref, xs_ref, colmajor)
    y = _head_out(of_ref[...], ob_ref[...], og_ref[...], gnw_ref[...], w_ref[...], n_heads, zs_ref)
    xn = xs_ref[...] + mod_ref[2:3, :] * y
    _store_chunks(xn, stage_ref, o_ref, colmajor)


def _mix_out(of, ob, og, x2, mod_a, layer, gnw, w_out, n_heads, bsz, *, colmajor):
    seq = x2.shape[0]
    d = x2.shape[1] // bsz
    wv = of.shape[1] // bsz
    gnw = gnw.reshape(1, -1)
    xv, xspec, tm = _x_stream(x2, bsz, colmajor)
    ospec = pl.BlockSpec((tm, wv), lambda t, b: (t, b))
    out = pl.pallas_call(
        functools.partial(_mix_out_kernel, n_heads, colmajor),
        grid=(seq // tm, bsz),
        in_specs=[ospec, ospec, ospec, xspec,
                  pl.BlockSpec((None, None, None, 3, d), lambda t, b: (layer, b, 1, 0, 0)),
                  pl.BlockSpec(gnw.shape, lambda t, b: (0, 0)),
                  pl.BlockSpec(w_out.shape, lambda t, b: (0, 0))],
        out_specs=xspec,
        out_shape=jax.ShapeDtypeStruct(xv.shape, F32),
        scratch_shapes=_chunk_scratch(tm, d) + [pltpu.VMEM((tm, wv), BF16)],
        compiler_params=_cparams("parallel", "parallel"),
        name="mix_out",
    )(of, ob, og, xv, mod_a, gnw, w_out)
    return out.reshape(seq, bsz * d)


def _gla_mixer(xc2, x2, mods, norm_w4, layer, prm, bsz, need_ctx):
    (modc_a, _), (modx_a, _) = mods
    w_in, wgate, bgate, gnw, w_out, dkt, dvt = prm
    qc, kc, vc, ogc, gfc, gbc = _gla_project(xc2, modc_a, norm_w4, layer, w_in, wgate, bgate, bsz,
                                             dkt, dvt, colmajor=False)
    qx, kx, vx, ogx, gfx, gbx = _gla_project(x2, modx_a, norm_w4, layer, w_in, wgate, bgate, bsz,
                                             dkt, dvt, colmajor=True)
    oc, ox = _bidir_scan((qc, kc, kc, vc, gfc, gbc), (qx, kx, kx, vx, gfx, gbx), GLA_HEADS, bsz, need_ctx)
    x_new = _mix_out(ox[0], ox[1], ogx, x2, modx_a, layer, gnw, w_out, GLA_HEADS, bsz, colmajor=True)
    xc_new = None
    if need_ctx:
        xc_new = _mix_out(oc[0], oc[1], ogc, xc2, modc_a, layer, gnw, w_out, GLA_HEADS, bsz, colmajor=False)
    return xc_new, x_new


def _hgrn_mixer(xc2, x2, mods, norm_w4, layer, prm, bsz, need_ctx):
    (modc_a, _), (modx_a, _) = mods
    w_in, lb, gnw, w_out, n_heads = prm
    qc, kfc, kbc, vc, gfc, gbc, ogc = _hgrn_project(xc2, modc_a, norm_w4, layer, w_in, lb, bsz)
    qx, kfx, kbx, vx, gfx, gbx, ogx = _hgrn_project(x2, modx_a, norm_w4, layer, w_in, lb, bsz)
    oc, ox = _bidir_scan((qc, kfc, kbc, vc, gfc, gbc), (qx, kfx, kbx, vx, gfx, gbx), n_heads, bsz, need_ctx)
    x_new = _mix_out(ox[0], ox[1], ogx, x2, modx_a, layer, gnw, w_out, n_heads, bsz, colmajor=False)
    xc_new = None
    if need_ctx:
        xc_new = _mix_out(oc[0], oc[1], ogc, xc2, modc_a, layer, gnw, w_out, n_heads, bsz, colmajor=False)
    return xc_new, x_new


def _forward(x, c, ctx, c_ctx, ada_w, ada_b, norm_w, ffn_w_in, ffn_w_out,
             s5_a_re, s5_a_im, s5_log_step, s5_b_re, s5_b_im, s5_c_re, s5_c_im, s5_d,
             s5_w_glu, s5_b_glu, gla_w_in, gla_w_gate, gla_b_gate, gla_norm_w, gla_w_out,
             hgrn_w_in, hgrn_lb_logits, hgrn_norm_w, hgrn_w_out, final_norm_w, depth=None):
    depth = ada_w.shape[0] if depth is None else depth
    bsz, seq, d = x.shape
    mods_x, mods_c = _ada(c, c_ctx, ada_w, ada_b)
    norm_w4 = norm_w.reshape(norm_w.shape[0], 3, 1, d)
    w_in = ffn_w_in.astype(BF16)
    w_out = ffn_w_out.astype(BF16)
    lb_soft = jax.nn.softmax(hgrn_lb_logits.astype(F32), axis=0)
    lb_all = jnp.cumsum(lb_soft, axis=0) - lb_soft[0]

    xc2 = jnp.transpose(ctx, (1, 0, 2)).reshape(ctx.shape[1], bsz * d)
    x2 = x
    for i in range(depth):
        last = i == depth - 1
        kind, j = i % N_MIXERS, i // N_MIXERS
        x2 = _ffn(x2, mods_x[0], norm_w4, w_in, w_out, i, 0, bsz, in_bld=(i == 0))
        xc2 = _ffn(xc2, mods_c[0], norm_w4, w_in, w_out, i, 0, bsz)
        mods = (mods_c, mods_x)
        if kind == 0:
            prm = _s5c_prep(s5_a_re[j], s5_a_im[j], s5_log_step[j], s5_b_re[j], s5_b_im[j],
                            s5_c_re[j], s5_c_im[j])
            extra = (s5_d[j], s5_w_glu[j].astype(BF16), s5_b_glu[j])
            yc, x2 = _s5c_mixer(xc2, x2, mods, norm_w4, i, prm, extra, bsz, not last)
        elif kind == 1:
            dkt = d // 2
            dvt = d
            n_low = 2 * GLA_RANK
            pad = LANES - n_low
            wi = jnp.pad(gla_w_in[j], ((0, 0), (0, pad))).astype(BF16)
            wg = jnp.zeros((LANES, 2 * dkt), F32)
            wg = wg.at[:GLA_RANK, :dkt].set(gla_w_gate[j, 0]).at[GLA_RANK:n_low, dkt:].set(gla_w_gate[j, 1])
            prm = (wi, wg.astype(BF16), gla_b_gate[j].reshape(1, 2 * dkt), gla_norm_w[j],
                   gla_w_out[j].astype(BF16), dkt, dvt)
            yc, x2 = _gla_mixer(xc2, x2, mods, norm_w4, i, prm, bsz, not last)
        else:
            prm = (hgrn_w_in[j].astype(BF16), lb_all[i], hgrn_norm_w[j], hgrn_w_out[j].astype(BF16),
                   d // HGRN_DK)
            yc, x2 = _hgrn_mixer(xc2, x2, mods, norm_w4, i, prm, bsz, not last)
        x2 = _ffn(x2, mods_x[0], norm_w4, w_in, w_out, i, 2, bsz,
                  final_w=final_norm_w.reshape(1, d) if last else None)
        if not last:
            xc2 = _ffn(yc, mods_c[0], norm_w4, w_in, w_out, i, 2, bsz)
    return x2


def kernel(x, c, ctx, c_ctx, ada_w, ada_b, norm_w, ffn_w_in, ffn_w_out, s5_a_re, s5_a_im, s5_log_step,
           s5_b_re, s5_b_im, s5_c_re, s5_c_im, s5_d, s5_w_glu, s5_b_glu, gla_w_in, gla_w_gate,
           gla_b_gate, gla_norm_w, gla_w_out, hgrn_w_in, hgrn_lb_logits, hgrn_norm_w, hgrn_w_out,
           final_norm_w):
    return _forward(x, c, ctx, c_ctx, ada_w, ada_b, norm_w, ffn_w_in, ffn_w_out, s5_a_re, s5_a_im,
                    s5_log_step, s5_b_re, s5_b_im, s5_c_re, s5_c_im, s5_d, s5_w_glu, s5_b_glu,
                    gla_w_in, gla_w_gate, gla_b_gate, gla_norm_w, gla_w_out, hgrn_w_in,
                    hgrn_lb_logits, hgrn_norm_w, hgrn_w_out, final_norm_w)
```

```python
import functools
import math

import numpy as np
import jax
import jax.numpy as jnp
from jax import lax
from jax.experimental import pallas as pl
from jax.experimental.pallas import tpu as pltpu

F32 = jnp.float32
BF16 = jnp.bfloat16

NORM_EPS = 1e-6
GRID_W = 64
N_MOD = 9
N_MIXERS = 3

S5_GROUP = 16
S5_STATE = 64
S5_LANE_GROUPS = 8
S5_TB = 64
S5_TS = 64

GLA_HEADS = 4
GLA_RANK = 16
GLA_TAU = 16.0
HGRN_DK = 128

SUBLANES = 8
LANES = 128
CHUNK = 128
GROUP = SUBLANES * SUBLANES
N_GROUPS = CHUNK // GROUP
SUB = CHUNK // SUBLANES
N_FINE = 3
N_COARSE = 3
N_LEVELS = 7
LOG2E = 1.4426950408889634

FFN_CHUNK = 768

VMEM_LIMIT_BYTES = 56 * 1024 * 1024


def _cparams(*sem):
    return pltpu.CompilerParams(dimension_semantics=sem, vmem_limit_bytes=VMEM_LIMIT_BYTES)


def _dot(a, b):
    return jnp.dot(a, b, preferred_element_type=F32)


def _dot_nt(a, b):
    return lax.dot_general(a, b, (((1,), (1,)), ((), ())), preferred_element_type=F32)


def _dot_tn(a, b):
    return lax.dot_general(a, b, (((0,), (0,)), ((), ())), preferred_element_type=F32)


def _sigmoid(x):
    return 1.0 / (1.0 + jnp.exp(-x))


def _silu(x):
    return x * _sigmoid(x)


def _gelu_tanh(x):
    c = math.sqrt(2.0 / math.pi)
    return 0.5 * x * (1.0 + jnp.tanh(c * (x + 0.044715 * (x * x * x))))


def _log_sigmoid(x):
    return jnp.minimum(x, 0.0) - jnp.log1p(jnp.exp(-jnp.abs(x)))


def _rms(x, w):
    ms = jnp.mean(x * x, axis=-1, keepdims=True)
    return (x * lax.rsqrt(ms + NORM_EPS)) * w


def _norm_mod(x, w, shift, scale):
    return _rms(x, w) * (1.0 + scale) + shift


def _ada_kernel(cc_ref, w_ref, b_ref, o_ref):
    a = _silu(cc_ref[...]).astype(BF16)
    o_ref[...] = _dot(a, w_ref[...].astype(BF16)) + b_ref[...]


def _ada(c, c_ctx, ada_w, ada_b):
    depth, d, nd = ada_w.shape
    bsz = c.shape[0]
    rows = ((bsz + 1 + SUBLANES - 1) // SUBLANES) * SUBLANES
    cc = jnp.zeros((rows, d), F32).at[:bsz].set(c).at[bsz].set(c_ctx)
    nblk = nd // d
    mod = pl.pallas_call(
        _ada_kernel,
        grid=(depth, nblk),
        in_specs=[
            pl.BlockSpec((rows, d), lambda i, n: (0, 0)),
            pl.BlockSpec((None, d, d), lambda i, n: (i, 0, n)),
            pl.BlockSpec((None, 1, d), lambda i, n: (i, 0, n)),
        ],
        out_specs=pl.BlockSpec((None, rows, d), lambda i, n: (i, 0, n)),
        out_shape=jax.ShapeDtypeStruct((depth, rows, nd), F32),
        compiler_params=_cparams("arbitrary", "arbitrary"),
        name="ada_mod",
    )(cc, ada_w, ada_b.reshape(depth, 1, nd))
    mx = mod[:, :bsz].reshape(depth, bsz, 3, 3, d)
    mc = jnp.broadcast_to(mod[:, bsz:bsz + 1], (depth, bsz, nd)).reshape(depth, bsz, 3, 3, d)
    return (mx, jnp.transpose(mx, (0, 2, 3, 1, 4))), (mc, jnp.transpose(mc, (0, 2, 3, 1, 4)))


def _ffn_kernel(chunks, final, x_ref, mod_ref, nw_ref, wi_ref, wo_ref, *rest):
    if final:
        fw_ref, o_ref = rest
    else:
        (o_ref,) = rest
    ff = wo_ref.shape[0]
    x = x_ref[...]
    h = _norm_mod(x, nw_ref[...], mod_ref[0:1, :], mod_ref[1:2, :]).astype(BF16)
    acc = None
    for lo, hi in chunks:
        g = _dot(h, wi_ref[:, lo:hi])
        u = _dot(h, wi_ref[:, ff + lo:ff + hi])
        part = _dot((_silu(g) * u).astype(BF16), wo_ref[lo:hi, :])
        acc = part if acc is None else acc + part
    y = x + (0.5 * mod_ref[2:3, :]) * acc
    if final:
        y = _rms(y, fw_ref[...])
    o_ref[...] = y


def _ff_chunks(ff, width):
    assert ff % LANES == 0 and width % LANES == 0
    return tuple((lo, min(lo + width, ff)) for lo in range(0, ff, width))


def _ffn(x, mod_a, norm_w4, w_in, w_out, layer, stage, bsz, *, in_bld=False, final_w=None):
    d = norm_w4.shape[-1]
    seq = x.shape[1] if in_bld else x.shape[0]
    ff = w_out.shape[2]
    tm = 512 if seq % 512 == 0 else 256
    s = 0 if stage == 0 else 1
    final = final_w is not None
    resident = pl.Buffered(1)
    if in_bld:
        x_spec = pl.BlockSpec((None, tm, d), lambda t, b: (b, t, 0))
    else:
        x_spec = pl.BlockSpec((tm, d), lambda t, b: (t, b))
    in_specs = [
        x_spec,
        pl.BlockSpec((None, None, None, 3, d), lambda t, b: (layer, b, stage, 0, 0)),
        pl.BlockSpec((None, None, 1, d), lambda t, b: (layer, stage, 0, 0)),
        pl.BlockSpec((None, None, d, 2 * ff), lambda t, b: (layer, s, 0, 0), pipeline_mode=resident),
        pl.BlockSpec((None, None, ff, d), lambda t, b: (layer, s, 0, 0), pipeline_mode=resident),
    ]
    args = [x, mod_a, norm_w4, w_in, w_out]
    if final:
        in_specs.append(pl.BlockSpec((1, d), lambda t, b: (0, 0)))
        args.append(final_w)
        out_spec = pl.BlockSpec((None, tm, d), lambda t, b: (b, t, 0))
        out_shape = jax.ShapeDtypeStruct((bsz, seq, d), F32)
    else:
        out_spec = pl.BlockSpec((tm, d), lambda t, b: (t, b))
        out_shape = jax.ShapeDtypeStruct((seq, bsz * d), F32)
    return pl.pallas_call(
        functools.partial(_ffn_kernel, _ff_chunks(ff, FFN_CHUNK), final),
        grid=(seq // tm, bsz),
        in_specs=in_specs,
        out_specs=out_spec,
        out_shape=out_shape,
        compiler_params=_cparams("parallel", "parallel"),
        name="ffn_half",
    )(*args)


def _s5_prep(a_re, a_im, log_step, b_re, b_im, c_re, c_im, bsz):
    dt = jnp.exp(log_step.astype(F32))[..., None]
    a_re = a_re.astype(F32)
    a_im = a_im.astype(F32)
    mag = jnp.exp(a_re * dt)
    abar_re = mag * jnp.cos(a_im * dt)
    abar_im = mag * jnp.sin(a_im * dt)
    den = a_re * a_re + a_im * a_im
    zr = abar_re - 1.0
    zi = abar_im
    coef_re = ((zr * a_re + zi * a_im) / den)[..., None]
    coef_im = ((zi * a_re - zr * a_im) / den)[..., None]
    b_re = b_re.astype(F32)
    b_im = b_im.astype(F32)
    bbar_re = coef_re * b_re - coef_im * b_im
    bbar_im = coef_re * b_im + coef_im * b_re
    n_dir, g, p = abar_re.shape
    gc = b_re.shape[-1]
    nj = g // S5_LANE_GROUPS
    lg = S5_LANE_GROUPS
    are = jnp.broadcast_to(abar_re.reshape(n_dir, 1, g * p), (n_dir, bsz, g * p))
    aim = jnp.broadcast_to(abar_im.reshape(n_dir, 1, g * p), (n_dir, bsz, g * p))
    eye = jnp.eye(lg, dtype=F32)
    bb = jnp.stack([bbar_re, bbar_im], axis=1).reshape(n_dir, 2, nj, lg, p, gc)
    bb = jnp.transpose(bb, (0, 2, 3, 5, 1, 4))
    wb = bb[:, :, :, :, :, None, :] * eye[None, None, :, None, None, :, None]
    wb = wb.reshape(n_dir, nj, lg * gc, 2 * lg * p).astype(BF16)
    cc = jnp.stack([c_re.astype(F32), -c_im.astype(F32)], axis=1).reshape(n_dir, 2, nj, lg, gc, p)
    cc = jnp.transpose(cc, (0, 2, 1, 3, 5, 4))
    wc = cc[:, :, :, :, :, None, :] * eye[None, None, None, :, None, :, None]
    wc = wc.reshape(n_dir, nj, 2 * lg * p, lg * gc).astype(BF16)
    return are, aim, wb, wc


def _s5_kernel(tb, rev, finish, x_ref, mod_ref, nw_ref, are_ref, aim_ref, wb_ref, wc_ref,
               h0re_ref, h0im_ref, *rest):
    n_io = 7 if finish else 3
    io, scratch = rest[:n_io], rest[n_io:]
    if finish:
        yf_ref, dsk_ref, wglu_ref, bglu_ref, o_ref, hre_ref, him_ref = io
        h_s = scratch[12]
    else:
        o_ref, hre_ref, him_ref = io
    bu = (scratch[0:2], scratch[2:4])
    hs = (scratch[4:6], scratch[6:8])
    hs_re, hs_im, y_s, hb_s = scratch[8:12]
    bsz = x_ref.shape[1]
    d = x_ref.shape[2]
    nj = wb_ref.shape[0]
    sw = wb_ref.shape[2] // 2
    ri = S5_TS * bsz
    nit = tb // S5_TS

    @pl.when(pl.program_id(0) == 0)
    def _():
        hs_re[...] = h0re_ref[...]
        hs_im[...] = h0im_ref[...]

    x3 = x_ref[...]
    ms = jnp.mean(x3 * x3, axis=-1, keepdims=True)
    h3 = (x3 * lax.rsqrt(ms + NORM_EPS)) * nw_ref[...] * (1.0 + mod_ref[1]) + mod_ref[0]
    h2 = h3.reshape(tb * bsz, d)
    if finish:
        h_s[...] = h2
    hb_s[...] = h2.astype(BF16)

    def lanes(j):
        return slice(j * LANES, (j + 1) * LANES)

    def drive(j, rows):
        r = _dot(hb_s[rows, lanes(j)], wb_ref[j])
        bu[j % 2][0][rows, :] = r[:, :sw]
        bu[j % 2][1][rows, :] = r[:, sw:]

    def readout(j, rows):
        y_s[rows, lanes(j)] = (_dot(hs[j % 2][0][rows, :].astype(BF16), wc_ref[j, :sw, :])
                               + _dot(hs[j % 2][1][rows, :].astype(BF16), wc_ref[j, sw:, :]))

    drive(0, slice(None))
    for j in range(nj):
        cur = j % 2
        ar = are_ref[:, j * sw:(j + 1) * sw]
        ai = aim_ref[:, j * sw:(j + 1) * sw]

        def body(i, carry, j=j, cur=cur, ar=ar, ai=ai):
            hr, hi = carry
            rows = pl.ds(pl.multiple_of(i * ri, ri), ri)
            if j >= 1:
                readout(j - 1, rows)
            if j + 1 < nj:
                drive(j + 1, rows)
            for s in range(S5_TS):
                t = i * S5_TS + s
                tt = (tb - 1 - t) if rev else t
                sl = pl.ds(pl.multiple_of(tt * bsz, bsz), bsz)
                hr, hi = (ar * hr - ai * hi + bu[cur][0][sl, :], ar * hi + ai * hr + bu[cur][1][sl, :])
                hs[cur][0][sl, :] = hr
                hs[cur][1][sl, :] = hi
            return hr, hi

        hr, hi = lax.fori_loop(
            0, nit, body, (hs_re[:, j * sw:(j + 1) * sw], hs_im[:, j * sw:(j + 1) * sw]))
        hs_re[:, j * sw:(j + 1) * sw] = hr
        hs_im[:, j * sw:(j + 1) * sw] = hi
    readout(nj - 1, slice(None))

    hre_ref[...] = hs_re[...]
    him_ref[...] = hs_im[...]
    if not finish:
        o_ref[...] = y_s[...].reshape(tb, bsz, d)
    else:
        y = y_s[...] + yf_ref[...].reshape(tb * bsz, d) + dsk_ref[...] * h_s[...]
        z = _dot(_gelu_tanh(y).astype(BF16), wglu_ref[...]) + bglu_ref[...]
        out = z[:, :d] * _sigmoid(z[:, d:])
        o_ref[...] = x3 + mod_ref[2] * out.reshape(tb, bsz, d)


def _s5_pass(x2, mod_b, norm_w4, layer, are, aim, wb, wc, h0, bsz, *, rev, fin=None):
    seq = x2.shape[0]
    d = x2.shape[1] // bsz
    tb = S5_TB
    nblk = seq // tb
    dirn = 1 if rev else 0
    nstate = are.shape[-1]
    sw = wb.shape[3] // 2
    x3 = x2.reshape(seq, bsz, d)
    blk = (lambda i: (nblk - 1 - i, 0, 0)) if rev else (lambda i: (i, 0, 0))
    x_spec = pl.BlockSpec((tb, bsz, d), blk)
    const2 = lambda i: (0, 0)
    in_specs = [
        x_spec,
        pl.BlockSpec((None, None, 3, bsz, d), lambda i: (layer, 1, 0, 0, 0)),
        pl.BlockSpec((None, None, 1, d), lambda i: (layer, 1, 0, 0)),
        pl.BlockSpec((None, bsz, nstate), lambda i: (dirn, 0, 0)),
        pl.BlockSpec((None, bsz, nstate), lambda i: (dirn, 0, 0)),
        pl.BlockSpec((None,) + wb.shape[1:], lambda i: (dirn, 0, 0, 0)),
        pl.BlockSpec((None,) + wc.shape[1:], lambda i: (dirn, 0, 0, 0)),
        pl.BlockSpec((bsz, nstate), const2),
        pl.BlockSpec((bsz, nstate), const2),
    ]
    args = [x3, mod_b, norm_w4, are, aim, wb, wc, h0[0], h0[1]]
    scratch = [pltpu.VMEM((tb * bsz, sw), F32) for _ in range(8)] + [
        pltpu.VMEM((bsz, nstate), F32), pltpu.VMEM((bsz, nstate), F32),
        pltpu.VMEM((tb * bsz, d), F32), pltpu.VMEM((tb * bsz, d), BF16),
    ]
    finish = fin is not None
    if finish:
        y_fwd, d_skip, w_glu, b_glu = fin
        in_specs += [
            x_spec,
            pl.BlockSpec((1, d), const2),
            pl.BlockSpec(w_glu.shape, const2),
            pl.BlockSpec((1, 2 * d), const2),
        ]
        args += [y_fwd.reshape(seq, bsz, d), d_skip.reshape(1, d), w_glu, b_glu.reshape(1, 2 * d)]
        scratch.append(pltpu.VMEM((tb * bsz, d), F32))
    out, hre, him = pl.pallas_call(
        functools.partial(_s5_kernel, tb, rev, finish),
        grid=(nblk,),
        in_specs=in_specs,
        out_specs=[x_spec, pl.BlockSpec((bsz, nstate), const2), pl.BlockSpec((bsz, nstate), const2)],
        out_shape=[jax.ShapeDtypeStruct((seq, bsz, d), F32),
                   jax.ShapeDtypeStruct((bsz, nstate), F32),
                   jax.ShapeDtypeStruct((bsz, nstate), F32)],
        scratch_shapes=scratch,
        compiler_params=_cparams("arbitrary"),
        name="s5_bwd_finish" if finish else "s5_fwd",
    )(*args)
    return out.reshape(seq, bsz * d), (hre, him)


def _s5_mixer(xc2, x2, mods_b, norm_w4, layer, prm, bsz, need_ctx):
    (modc_b, modx_b) = mods_b
    are, aim, wb, wc, d_skip, w_glu, b_glu = prm
    nstate = are.shape[-1]
    zero = (jnp.zeros((bsz, nstate), F32), jnp.zeros((bsz, nstate), F32))
    ycf, hf = _s5_pass(xc2, modc_b, norm_w4, layer, are, aim, wb, wc, zero, bsz, rev=False)
    yxf, _ = _s5_pass(x2, modx_b, norm_w4, layer, are, aim, wb, wc, hf, bsz, rev=False)
    xc_new, hb = _s5_pass(xc2, modc_b, norm_w4, layer, are, aim, wb, wc, zero, bsz, rev=True,
                          fin=(ycf, d_skip, w_glu, b_glu))
    x_new, _ = _s5_pass(x2, modx_b, norm_w4, layer, are, aim, wb, wc, hb, bsz, rev=True,
                        fin=(yxf, d_skip, w_glu, b_glu))
    return (xc_new if need_ctx else None), x_new


S5_T = 16
S5C_TB = 128
SLOT = LANES // SUBLANES


def _cmul(ar, ai, br, bi):
    return ar * br - ai * bi, ar * bi + ai * br


def _s5c_prep(a_re, a_im, log_step, b_re, b_im, c_re, c_im):
    dt = jnp.exp(log_step.astype(F32))[..., None]
    a_re = a_re.astype(F32)
    a_im = a_im.astype(F32)
    mag = jnp.exp(a_re * dt)
    abar_re = mag * jnp.cos(a_im * dt)
    abar_im = mag * jnp.sin(a_im * dt)
    den = a_re * a_re + a_im * a_im
    zr = abar_re - 1.0
    zi = abar_im
    coef_re = ((zr * a_re + zi * a_im) / den)[..., None]
    coef_im = ((zi * a_re - zr * a_im) / den)[..., None]
    b_re = b_re.astype(F32)
    b_im = b_im.astype(F32)
    bb_re = coef_re * b_re - coef_im * b_im
    bb_im = coef_re * b_im + coef_im * b_re
    cc_re = c_re.astype(F32)
    cc_im = c_im.astype(F32)
    t = S5_T
    pr, pi = [jnp.ones_like(abar_re)], [jnp.zeros_like(abar_im)]
    for _ in range(t):
        nr, ni = _cmul(pr[-1], pi[-1], abar_re, abar_im)
        pr.append(nr)
        pi.append(ni)
    pr, pi = jnp.stack(pr), jnp.stack(pi)
    hi = lax.Precision.HIGHEST
    g, p, gc = abar_re.shape[1], abar_re.shape[2], b_re.shape[-1]
    prk = jnp.transpose(pr, (1, 2, 3, 0))
    pik = jnp.transpose(pi, (1, 2, 3, 0))
    e_re, e_im = _cmul(prk[..., None], pik[..., None], bb_re[:, :, :, None, :], bb_im[:, :, :, None, :])
    nkc = (t + 1) * gc
    taps = (jnp.einsum('dgop,dgpn->dgon', cc_re, e_re.reshape(2, g, p, nkc), precision=hi)
            - jnp.einsum('dgop,dgpn->dgon', cc_im, e_im.reshape(2, g, p, nkc), precision=hi))
    taps = taps.reshape(2, g, gc, t + 1, gc)
    kf, kb = taps[0], taps[1]
    seq = jnp.concatenate([jnp.flip(kb[:, :, 1:t], axis=2), kf[:, :, :1] + kb[:, :, :1], kf[:, :, 1:t]],
                          axis=2)
    tab = jnp.transpose(seq, (0, 3, 2, 1)).reshape(g, gc, (2 * t - 1) * gc)
    m = jnp.stack([tab[:, :, gc * (t - 1 - s):gc * (t - 1 - s) + t * gc] for s in range(t)], axis=1)
    m = m.reshape(g, t * gc, t * gc)

    def wst(d, flip):
        er, ei = e_re[d, :, :, :t], e_im[d, :, :, :t]
        if flip:
            er, ei = jnp.flip(er, 2), jnp.flip(ei, 2)
        w = jnp.concatenate([jnp.transpose(er, (0, 2, 3, 1)), jnp.transpose(ei, (0, 2, 3, 1))], axis=-1)
        return w.reshape(g, t * gc, 2 * p)

    wst_f = wst(0, True)
    wst_b = wst(1, False)

    def wout(d, flip):
        qr, qi = prk[d, :, :, 1:t + 1], pik[d, :, :, 1:t + 1]
        if flip:
            qr, qi = jnp.flip(qr, 2), jnp.flip(qi, 2)
        cr = jnp.transpose(cc_re[d], (0, 2, 1))
        ci = jnp.transpose(cc_im[d], (0, 2, 1))
        gr, gi = _cmul(cr[:, :, None, :], ci[:, :, None, :], qr[..., None], qi[..., None])
        return jnp.concatenate([gr, -gi], axis=1).reshape(g, 2 * p, t * gc)

    wout_f = wout(0, False)
    wout_b = wout(1, True)
    a1 = jnp.concatenate([pr[t], pr[t]], axis=-1)
    a2 = jnp.concatenate([-pi[t], pi[t]], axis=-1)
    bc = lambda a: jnp.broadcast_to(a[:, :, None, :], a.shape[:2] + (SUBLANES, a.shape[-1]))
    return dict(m=m.astype(BF16), wst=(wst_f.astype(BF16), wst_b.astype(BF16)),
                wout=(wout_f.astype(BF16), wout_b.astype(BF16)), a1=bc(a1), a2=bc(a2))


def _slot_transpose(x):
    m, n, bsz, _ = x.shape
    slot = lax.broadcasted_iota(jnp.int32, (bsz, LANES), 1) // SLOT

    def rot(v, shift):
        return pltpu.roll(v.reshape(-1, LANES), shift, 1).reshape(v.shape)

    d = n // 2
    while d >= 1:
        keep = (slot & d) == 0
        x = x.reshape(m, n // (2 * d), 2, d, bsz, LANES)
        lo, hi = x[:, :, 0], x[:, :, 1]
        new_lo = jnp.where(keep, lo, rot(hi, SLOT * d))
        new_hi = jnp.where(keep, rot(lo, LANES - SLOT * d), hi)
        x = jnp.stack([new_lo, new_hi], axis=2).reshape(m, n, bsz, LANES)
        d //= 2
    return x


def _s5c_gather(h_s, a_s, nch):
    nj = h_s.shape[1] // LANES
    nh = S5_T // SUBLANES
    for j in range(nj):
        x = h_s[:, j * LANES:(j + 1) * LANES].reshape(nch * nh, SUBLANES, SUBLANES, LANES)
        r = _slot_transpose(x).reshape(nch, nh, SUBLANES, SUBLANES, LANES)
        for gl in range(S5_LANE_GROUPS):
            for hh in range(nh):
                a_s[S5_LANE_GROUPS * j + gl, :, hh * LANES:(hh + 1) * LANES] = (
                    r[:, hh, gl].reshape(nch * SUBLANES, LANES))


def _s5c_state_scan(v, st_ref, sin_ref, a1, a2, g, nch, rev):
    gl = slice(g * LANES, (g + 1) * LANES)
    s = st_ref[:, gl]
    w = pltpu.roll(s, LANES // 2, 1)
    for n in (range(nch - 1, -1, -1) if rev else range(nch)):
        rows = slice(SUBLANES * n, SUBLANES * (n + 1))
        sin_ref[rows, gl] = s
        vn = v[rows, :]
        s, w = a1 * s + a2 * w + vn, a1 * w - a2 * s + pltpu.roll(vn, LANES // 2, 1)
    st_ref[:, gl] = s


def _s5c_norm(x_ref, mod_ref, nw_ref, h_s):
    x3 = x_ref[...]
    ms = jnp.mean(x3 * x3, axis=-1, keepdims=True)
    h3 = (x3 * lax.rsqrt(ms + NORM_EPS)) * nw_ref[...] * (1.0 + mod_ref[1]) + mod_ref[0]
    h_s[...] = h3.reshape(h_s.shape)


def _s5c_fwd_kernel(x_ref, mod_ref, nw_ref, wst_ref, a1_ref, a2_ref, h0_ref,
                    sin_ref, hfin_ref, st_ref, h_s, a_s):
    ng = wst_ref.shape[0]
    nch = a_s.shape[1] // SUBLANES

    @pl.when(pl.program_id(0) == 0)
    def _():
        st_ref[...] = h0_ref[...]

    _s5c_norm(x_ref, mod_ref, nw_ref, h_s)
    _s5c_gather(h_s, a_s, nch)
    for g in range(ng):
        v = _dot(a_s[g].astype(BF16), wst_ref[g])
        _s5c_state_scan(v, st_ref, sin_ref, a1_ref[g], a2_ref[g], g, nch, False)
    hfin_ref[...] = st_ref[...]


def _s5c_bwd_kernel(x_ref, mod_ref, nw_ref, sinf_ref, m_ref, wst_ref, woutf_ref, woutb_ref,
                    a1_ref, a2_ref, h0_ref, dsk_ref, y_ref, hfin_ref, st_ref, h_s, a_s, sinb_s):
    ng = m_ref.shape[0]
    nch = a_s.shape[1] // SUBLANES

    @pl.when(pl.program_id(0) == 0)
    def _():
        st_ref[...] = h0_ref[...]

    _s5c_norm(x_ref, mod_ref, nw_ref, h_s)
    _s5c_gather(h_s, a_s, nch)
    for j in range(ng // S5_LANE_GROUPS):
        ys = []
        for gl in range(S5_LANE_GROUPS):
            g = S5_LANE_GROUPS * j + gl
            lanes = slice(g * LANES, (g + 1) * LANES)
            a = a_s[g].astype(BF16)
            _s5c_state_scan(_dot(a, wst_ref[g]), st_ref, sinb_s, a1_ref[g], a2_ref[g], g, nch, True)
            ys.append(_dot(a, m_ref[g])
                      + _dot(sinf_ref[:, lanes].astype(BF16), woutf_ref[g])
                      + _dot(sinb_s[:, lanes].astype(BF16), woutb_ref[g]))
        cols = slice(j * LANES, (j + 1) * LANES)
        for hh in range(S5_T // SUBLANES):
            r = jnp.stack([y[:, hh * LANES:(hh + 1) * LANES].reshape(nch, SUBLANES, LANES) for y in ys],
                          axis=1)
            out = _slot_transpose(r)
            for n in range(nch):
                t0 = S5_T * n + SUBLANES * hh
                skip = dsk_ref[:, cols] * h_s[t0 * SUBLANES:(t0 + SUBLANES) * SUBLANES, cols]
                y_ref[t0:t0 + SUBLANES, :, cols] = out[n] + skip.reshape(SUBLANES, SUBLANES, LANES)
    hfin_ref[...] = st_ref[...]


def _s5c_glu_kernel(y_ref, x_ref, mod_ref, w_ref, b_ref, o_ref):
    d = x_ref.shape[1]
    z = _dot(_gelu_tanh(y_ref[...]).astype(BF16), w_ref[...]) + b_ref[...]
    o_ref[...] = x_ref[...] + mod_ref[2:3, :] * (z[:, :d] * _sigmoid(z[:, d:]))


def _s5c_stream(x2, mod_b, norm_w4, layer, prm, h0, bsz, *, rev, sin_f=None, d_skip=None):
    seq = x2.shape[0]
    d = x2.shape[1] // bsz
    tb = S5C_TB
    nblk = seq // tb
    nch = tb // S5_T
    dirn = 1 if rev else 0
    ng = prm["m"].shape[0]
    ns = ng * LANES
    x3 = x2.reshape(seq, bsz, d)
    blk = (lambda i: (nblk - 1 - i, 0, 0)) if rev else (lambda i: (i, 0, 0))
    blk2 = (lambda i: (nblk - 1 - i, 0)) if rev else (lambda i: (i, 0))
    x_spec = pl.BlockSpec((tb, bsz, d), blk)
    sin_spec = pl.BlockSpec((nch * bsz, ns), blk2)
    st_spec = pl.BlockSpec((bsz, ns), lambda i: (0, 0))
    res = pl.Buffered(1)
    wspec = lambda w: pl.BlockSpec(w.shape, lambda i: (0,) * w.ndim, pipeline_mode=res)
    aspec = pl.BlockSpec((None,) + prm["a1"].shape[1:], lambda i: (dirn, 0, 0, 0), pipeline_mode=res)
    common = [x_spec,
              pl.BlockSpec((None, None, 3, bsz, d), lambda i: (layer, 1, 0, 0, 0)),
              pl.BlockSpec((None, None, 1, d), lambda i: (layer, 1, 0, 0))]
    scratch = [pltpu.VMEM((bsz, ns), F32), pltpu.VMEM((tb * bsz, d), F32),
               pltpu.VMEM((ng, nch * bsz, S5_T * S5_GROUP), F32)]
    if not rev:
        return pl.pallas_call(
            _s5c_fwd_kernel,
            grid=(nblk,),
            in_specs=common + [wspec(prm["wst"][0]), aspec, aspec, st_spec],
            out_specs=[sin_spec, st_spec],
            out_shape=[jax.ShapeDtypeStruct((seq // S5_T * bsz, ns), F32),
                       jax.ShapeDtypeStruct((bsz, ns), F32)],
            scratch_shapes=scratch,
            compiler_params=_cparams("arbitrary"),
            name="s5_fwd_states",
        )(x3, mod_b, norm_w4, prm["wst"][0], prm["a1"], prm["a2"], h0)
    y, hfin = pl.pallas_call(
        _s5c_bwd_kernel,
        grid=(nblk,),
        in_specs=common + [sin_spec, wspec(prm["m"]), wspec(prm["wst"][1]), wspec(prm["wout"][0]),
                           wspec(prm["wout"][1]), aspec, aspec, st_spec,
                           pl.BlockSpec((1, d), lambda i: (0, 0))],
        out_specs=[x_spec, st_spec],
        out_shape=[jax.ShapeDtypeStruct((seq, bsz, d), F32), jax.ShapeDtypeStruct((bsz, ns), F32)],
        scratch_shapes=scratch + [pltpu.VMEM((nch * bsz, ns), F32)],
        compiler_params=_cparams("arbitrary"),
        name="s5_bwd_readout",
    )(x3, mod_b, norm_w4, sin_f, prm["m"], prm["wst"][1], prm["wout"][0], prm["wout"][1],
      prm["a1"], prm["a2"], h0, d_skip.reshape(1, d))
    return y.reshape(seq, bsz * d), hfin


def _s5c_glu(y2, x2, mod_a, layer, w_glu, b_glu, bsz):
    seq = x2.shape[0]
    d = x2.shape[1] // bsz
    tm = 512 if seq % 512 == 0 else 256
    row = pl.BlockSpec((tm, d), lambda t, b: (t, b))
    return pl.pallas_call(
        _s5c_glu_kernel,
        grid=(seq // tm, bsz),
        in_specs=[row, row,
                  pl.BlockSpec((None, None, None, 3, d), lambda t, b: (layer, b, 1, 0, 0)),
                  pl.BlockSpec(w_glu.shape, lambda t, b: (0, 0), pipeline_mode=pl.Buffered(1)),
                  pl.BlockSpec((1, 2 * d), lambda t, b: (0, 0))],
        out_specs=row,
        out_shape=jax.ShapeDtypeStruct((seq, bsz * d), F32),
        compiler_params=_cparams("parallel", "parallel"),
        name="s5_glu",
    )(y2, x2, mod_a, w_glu, b_glu.reshape(1, 2 * d))


def _s5c_mixer(xc2, x2, mods, norm_w4, layer, prm, extra, bsz, need_ctx):
    (modc_a, modc_b), (modx_a, modx_b) = mods
    d_skip, w_glu, b_glu = extra
    ns = prm["m"].shape[0] * LANES
    zero = jnp.zeros((bsz, ns), F32)
    sc_f, hf = _s5c_stream(xc2, modc_b, norm_w4, layer, prm, zero, bsz, rev=False)
    sx_f, _ = _s5c_stream(x2, modx_b, norm_w4, layer, prm, hf, bsz, rev=False)
    yc, hb = _s5c_stream(xc2, modc_b, norm_w4, layer, prm, zero, bsz, rev=True, sin_f=sc_f, d_skip=d_skip)
    yx, _ = _s5c_stream(x2, modx_b, norm_w4, layer, prm, hb, bsz, rev=True, sin_f=sx_f, d_skip=d_skip)
    x_new = _s5c_glu(yx, x2, modx_a, layer, w_glu, b_glu, bsz)
    xc_new = _s5c_glu(yc, xc2, modc_a, layer, w_glu, b_glu, bsz) if need_ctx else None
    return xc_new, x_new


def _level_matrix(rev):
    p = np.arange(CHUNK)
    v, r = p // SUBLANES, p % SUBLANES
    tau = GROUP * (v // SUBLANES) + SUBLANES * r + v % SUBLANES
    if rev:
        tau = CHUNK - 1 - tau
    ti, tj = tau[:, None], tau[None, :]
    x = ti ^ tj
    lvl = np.zeros((CHUNK, CHUNK), np.int32)
    for b in range(N_LEVELS):
        lvl = np.where((x >> b) & 1, b + 1, lvl)
    lvl = np.where(tj > ti, -1, lvl)
    return lvl.astype(np.int32)


def _scan_chunk_head(q, k, vb, g, st_t, masks, rev):
    dk = q.shape[1]
    ng, nv = N_GROUPS, SUBLANES
    ea = (lambda e: SUB - 1 - e) if rev else (lambda e: e)
    ksub = (lambda e: SUBLANES - 1 - e) if rev else (lambda e: e)

    def vregs(x, scale=None):
        out = [x[SUBLANES * ea(e):SUBLANES * ea(e) + SUBLANES, :] for e in range(SUB)]
        if scale is not None:
            out = [o * scale for o in out]
        return [out[nv * gi:nv * (gi + 1)] for gi in range(ng)]

    def assemble(vs):
        flat = [vs[gi][a] for gi in range(ng) for a in range(nv)]
        return jnp.concatenate([flat[ea(a)] for a in range(SUB)], axis=0)

    def zeros():
        return [[zero] * nv for _ in range(ng)]

    qv, kv, gv = vregs(q), vregs(k), vregs(g, LOG2E)
    zero = jnp.zeros((SUBLANES, dk), F32)
    ninf = jnp.full((SUBLANES, dk), -jnp.inf, F32)

    sub_i = lax.broadcasted_iota(jnp.int32, (SUBLANES, dk), 0)
    re = (SUBLANES - 1 - sub_i) if rev else sub_i

    def row(x, e):
        kk = ksub(e)
        return jnp.broadcast_to(x[kk:kk + 1, :], (SUBLANES, dk))

    pf, tot, ct, xcl, gtot = [], [], [], [], []
    for gi in range(ng):
        p = [gv[gi][0]]
        for a in range(1, nv):
            p.append(p[-1] + gv[gi][a])
        t = p[nv - 1]
        c = zero
        for e in range(SUBLANES):
            c = c + jnp.where(re >= e, row(t, e), 0.0)
        pf.append(p)
        tot.append(t)
        ct.append(c)
        xcl.append(c - t)
        gtot.append(row(c, SUBLANES - 1))

    q_lv, k_lv = [], []
    for lvl in range(1, N_FINE + 1):
        s = 1 << lvl
        qs, ks = zeros(), zeros()
        for gi in range(ng):
            for bs in range(0, nv, s):
                m = bs + s // 2
                ref = pf[gi][m - 1]
                for a in range(bs, m - 1):
                    ks[gi][a] = kv[gi][a] * jnp.exp2(ref - pf[gi][a])
                ks[gi][m - 1] = kv[gi][m - 1]
                for a in range(m, bs + s):
                    qs[gi][a] = qv[gi][a] * jnp.exp2(pf[gi][a] - ref)
        q_lv.append(assemble(qs))
        k_lv.append(assemble(ks))
    for lvl in range(N_FINE + 1, N_FINE + N_COARSE + 1):
        w = 1 << (lvl - N_FINE)
        right = (re & (w // 2)) != 0
        qs, ks = zeros(), zeros()
        for gi in range(ng):
            xm = zero
            for bs in range(0, SUBLANES, w):
                inblk = (re >= bs) & (re < bs + w)
                xm = xm + jnp.where(inblk, row(xcl[gi], bs + w // 2), 0.0)
            drq = jnp.where(right, xcl[gi] - xm, ninf)
            dlk = jnp.where(right, ninf, tot[gi] + xm - ct[gi])
            for a in range(nv):
                qs[gi][a] = qv[gi][a] * jnp.exp2(pf[gi][a] + drq)
                ks[gi][a] = kv[gi][a] * jnp.exp2(dlk - pf[gi][a])
        q_lv.append(assemble(qs))
        k_lv.append(assemble(ks))
    qs, ks = zeros(), zeros()
    for a in range(nv):
        ks[0][a] = kv[0][a] * jnp.exp2(tot[0] + (gtot[0] - ct[0]) - pf[0][a])
        qs[1][a] = qv[1][a] * jnp.exp2(pf[1][a] + xcl[1])
    q_lv.append(assemble(qs))
    k_lv.append(assemble(ks))

    sc = jnp.where(masks[0], jnp.sum(q * k, axis=-1, keepdims=True), 0.0)
    for lvl in range(1, N_LEVELS + 1):
        s_l = _dot_nt(q_lv[lvl - 1].astype(BF16), k_lv[lvl - 1].astype(BF16))
        sc = jnp.where(masks[lvl], s_l, sc)

    e0, e1 = jnp.exp2(gtot[0]), jnp.exp2(gtot[1])
    qc = zeros()
    kd = zeros()
    for a in range(nv):
        qc[0][a] = qv[0][a] * jnp.exp2(pf[0][a] + xcl[0])
        qc[1][a] = qs[1][a] * e0
        kd[0][a] = ks[0][a] * e1
        kd[1][a] = kv[1][a] * jnp.exp2(tot[1] + (gtot[1] - ct[1]) - pf[1][a])
    o = _dot(sc.astype(BF16), vb) + _dot_nt(assemble(qc).astype(BF16), st_t.astype(BF16))
    st_new = (e0 * e1)[0:1, :] * st_t + _dot_tn(vb, assemble(kd).astype(BF16))
    return o, st_new


def _scan_kernel(n_heads, dk, dv, qf_ref, kf_ref, vf_ref, gf_ref, qb_ref, kb_ref, vb_ref, gb_ref,
                 lv_ref, s0_ref, of_ref, ob_ref, sf_ref, st_ref):
    @pl.when(pl.program_id(1) == 0)
    def _():
        st_ref[...] = s0_ref[...]

    dirs = ((qf_ref, kf_ref, vf_ref, gf_ref, of_ref, False), (qb_ref, kb_ref, vb_ref, gb_ref, ob_ref, True))
    masks = [[lv_ref[d] == lvl for lvl in range(N_LEVELS + 1)] for d in range(2)]
    for h in range(n_heads):
        ks = slice(h * dk, (h + 1) * dk)
        vs = slice(h * dv, (h + 1) * dv)
        for d, (q_ref, k_ref, v_ref, g_ref, o_ref, rev) in enumerate(dirs):
            o, st_new = _scan_chunk_head(q_ref[:, ks], k_ref[:, ks], v_ref[:, vs], g_ref[:, ks],
                                         st_ref[d, vs, :], masks[d], rev)
            o_ref[:, vs] = o
            st_ref[d, vs, :] = st_new
    sf_ref[...] = st_ref[...]


def _gated_scan(q, kf, kb, v, gf, gb, s0, n_heads, bsz):
    seq = q.shape[0]
    wk = q.shape[1] // bsz
    wv = v.shape[1] // bsz
    dk, dv = wk // n_heads, wv // n_heads
    nchunk = seq // CHUNK
    fwd = lambda b, n: (n, b)
    bwd = lambda b, n: (nchunk - 1 - n, b)
    kspec = lambda im: pl.BlockSpec((CHUNK, wk), im)
    vspec = lambda im: pl.BlockSpec((CHUNK, wv), im)
    sspec = pl.BlockSpec((2, None, wv, dk), lambda b, n: (0, b, 0, 0))
    lv = jnp.asarray(np.stack([_level_matrix(False), _level_matrix(True)]))
    return pl.pallas_call(
        functools.partial(_scan_kernel, n_heads, dk, dv),
        grid=(bsz, nchunk),
        in_specs=[kspec(fwd), kspec(fwd), vspec(fwd), kspec(fwd),
                  kspec(bwd), kspec(bwd), vspec(bwd), kspec(bwd),
                  pl.BlockSpec((2, CHUNK, CHUNK), lambda b, n: (0, 0, 0)), sspec],
        out_specs=[vspec(fwd), vspec(bwd), sspec],
        out_shape=[jax.ShapeDtypeStruct((seq, bsz * wv), F32),
                   jax.ShapeDtypeStruct((seq, bsz * wv), F32),
                   jax.ShapeDtypeStruct((2, bsz, wv, dk), F32)],
        scratch_shapes=[pltpu.VMEM((2, wv, dk), F32)],
        compiler_params=_cparams("parallel", "arbitrary"),
        name="gated_scan",
    )(q, kf, v, gf, q, kb, v, gb, lv, s0)


def _bidir_scan(ctx_in, lat_in, n_heads, bsz, need_ctx):
    qc, kfc, kbc, vc, gfc, gbc = ctx_in
    qx, kfx, kbx, vx, gfx, gbx = lat_in
    wk = qc.shape[1] // bsz
    wv = vc.shape[1] // bsz
    s0 = jnp.zeros((2, bsz, wv, wk // n_heads), F32)
    oc_f, oc_b, s_c = _gated_scan(qc, kfc, kbc, vc, gfc, gbc, s0, n_heads, bsz)
    ox_f, ox_b, _ = _gated_scan(qx, kfx, kbx, vx, gfx, gbx, s_c, n_heads, bsz)
    return ((oc_f, oc_b) if need_ctx else None), (ox_f, ox_b)


def _chunk_rows(c, colmajor):
    if colmajor:
        return lambda v: (SUBLANES * (v % SUBLANES) + N_GROUPS * c + v // SUBLANES, GROUP)
    return lambda v: (c * CHUNK + GROUP * (v // SUBLANES) + v % SUBLANES, SUBLANES)


def _permute_chunk_in(stage_ref, dst_ref, c, colmajor):
    rows = _chunk_rows(c, colmajor)
    for jl in range(stage_ref.shape[0]):
        for v in range(SUB):
            start, stride = rows(v)
            dst_ref[c * CHUNK + SUBLANES * v:c * CHUNK + SUBLANES * (v + 1), jl * LANES:(jl + 1) * LANES] = (
                stage_ref[jl, pl.ds(start, SUBLANES, stride=stride), :])


def _permute_chunk_out(val, stage_ref, c, colmajor):
    rows = _chunk_rows(c, colmajor)
    for jl in range(stage_ref.shape[0]):
        for v in range(SUB):
            start, stride = rows(v)
            stage_ref[jl, pl.ds(start, SUBLANES, stride=stride), :] = (
                val[SUBLANES * v:SUBLANES * (v + 1), jl * LANES:(jl + 1) * LANES])


def _stage_block(x_ref, stage_ref):
    n = stage_ref.shape[1]
    for jl in range(stage_ref.shape[0]):
        blk = x_ref[..., jl * LANES:(jl + 1) * LANES]
        stage_ref[jl] = blk.reshape(n, LANES)


def _unstage_block(stage_ref, o_ref):
    for jl in range(stage_ref.shape[0]):
        o_ref[..., jl * LANES:(jl + 1) * LANES] = stage_ref[jl].reshape(o_ref.shape[:-1] + (LANES,))


def _hgrn_epilogue(h, w_ref, lb, d):
    q = _silu(_dot(h, w_ref[:, :d]))
    v = _dot(h, w_ref[:, d:2 * d])
    zf = _dot(h, w_ref[:, 2 * d:3 * d])
    zb = _dot(h, w_ref[:, 3 * d:4 * d])
    og = _dot(h, w_ref[:, 4 * d:])
    lf, lbk = lb[0:1, :], lb[1:2, :]
    gf = jnp.log(lf + (1.0 - lf) * _sigmoid(zf))
    gb = jnp.log(lbk + (1.0 - lbk) * _sigmoid(zb))
    kf = (1.0 - lf) * _sigmoid(-zf)
    kb = (1.0 - lbk) * _sigmoid(-zb)
    return q, kf, kb, v, gf, gb, og


def _x_stream(x2, bsz, colmajor):
    seq, d = x2.shape[0], x2.shape[1] // bsz
    if colmajor:
        rows = seq // GRID_W
        assert rows == GROUP, "column-major chunks assume one grid column per 64-step group"
        return (x2.reshape(rows, GRID_W, bsz * d),
                pl.BlockSpec((rows, SUBLANES, d), lambda t, b: (0, t, b)), SUBLANES * rows)
    tm = 2 * CHUNK
    return x2, pl.BlockSpec((tm, d), lambda t, b: (t, b)), tm


def _load_chunks(x_ref, stage_ref, xs_ref, colmajor):
    _stage_block(x_ref, stage_ref)
    for c in range(xs_ref.shape[0] // CHUNK):
        _permute_chunk_in(stage_ref, xs_ref, c, colmajor)


def _store_chunks(xn, stage_ref, o_ref, colmajor):
    for c in range(xn.shape[0] // CHUNK):
        _permute_chunk_out(xn[c * CHUNK:(c + 1) * CHUNK, :], stage_ref, c, colmajor)
    _unstage_block(stage_ref, o_ref)


def _chunk_scratch(tm, d):
    return [pltpu.VMEM((d // LANES, tm, LANES), F32), pltpu.VMEM((tm, d), F32)]


def _hgrn_proj_kernel(x_ref, mod_ref, nw_ref, w_ref, lb_ref,
                      q_ref, kf_ref, kb_ref, v_ref, gf_ref, gb_ref, og_ref, stage_ref, xs_ref):
    d = xs_ref.shape[1]
    _load_chunks(x_ref, stage_ref, xs_ref, False)
    h = _norm_mod(xs_ref[...], nw_ref[...], mod_ref[0:1, :], mod_ref[1:2, :]).astype(BF16)
    outs = _hgrn_epilogue(h, w_ref, lb_ref[...], d)
    for ref, val in zip((q_ref, kf_ref, kb_ref, v_ref, gf_ref, gb_ref, og_ref), outs):
        ref[...] = val.astype(ref.dtype)


def _hgrn_project(x2, mod_a, norm_w4, layer, w_in, lb, bsz):
    seq = x2.shape[0]
    d = x2.shape[1] // bsz
    xv, xspec, tm = _x_stream(x2, bsz, False)
    row = pl.BlockSpec((tm, d), lambda t, b: (t, b))
    return pl.pallas_call(
        _hgrn_proj_kernel,
        grid=(seq // tm, bsz),
        in_specs=[
            xspec,
            pl.BlockSpec((None, None, None, 3, d), lambda t, b: (layer, b, 1, 0, 0)),
            pl.BlockSpec((None, None, 1, d), lambda t, b: (layer, 1, 0, 0)),
            pl.BlockSpec(w_in.shape, lambda t, b: (0, 0), pipeline_mode=pl.Buffered(1)),
            pl.BlockSpec(lb.shape, lambda t, b: (0, 0)),
        ],
        out_specs=[row] * 7,
        out_shape=[jax.ShapeDtypeStruct((seq, bsz * d), BF16 if i == 3 else F32) for i in range(7)],
        scratch_shapes=_chunk_scratch(tm, d),
        compiler_params=_cparams("parallel", "parallel"),
        name="hgrn_project",
    )(xv, mod_a, norm_w4, w_in, lb)


def _gla_epilogue(h, w_ref, wgate, bgate, dkt, dvt):
    low = _dot(h, w_ref[:, 2 * dkt + 2 * dvt:])
    lg = _log_sigmoid(_dot(low.astype(BF16), wgate) + bgate) * (1.0 / GLA_TAU)
    q = _dot(h, w_ref[:, :dkt]) * ((dkt // GLA_HEADS) ** -0.5)
    k = _dot(h, w_ref[:, dkt:2 * dkt])
    v = _dot(h, w_ref[:, 2 * dkt:2 * dkt + dvt])
    og = _dot(h, w_ref[:, 2 * dkt + dvt:2 * dkt + 2 * dvt])
    return q, k, v, og, lg[:, :dkt], lg[:, dkt:]


def _gla_proj_kernel(colmajor, x_ref, mod_ref, nw_ref, w_ref, wg_ref, bg_ref,
                     q_ref, k_ref, v_ref, og_ref, gf_ref, gb_ref, stage_ref, xs_ref):
    dkt = q_ref.shape[1]
    dvt = v_ref.shape[1]
    _load_chunks(x_ref, stage_ref, xs_ref, colmajor)
    h = _norm_mod(xs_ref[...], nw_ref[...], mod_ref[0:1, :], mod_ref[1:2, :]).astype(BF16)
    outs = _gla_epilogue(h, w_ref, wg_ref[...], bg_ref[...], dkt, dvt)
    for ref, val in zip((q_ref, k_ref, v_ref, og_ref, gf_ref, gb_ref), outs):
        ref[...] = val.astype(ref.dtype)


def _gla_project(x2, mod_a, norm_w4, layer, w_in, wgate, bgate, bsz, dkt, dvt, *, colmajor):
    seq = x2.shape[0]
    d = x2.shape[1] // bsz
    widths = (dkt, dkt, dvt, dvt, dkt, dkt)
    xv, xspec, tm = _x_stream(x2, bsz, colmajor)
    const = lambda t, b: (0, 0)
    return pl.pallas_call(
        functools.partial(_gla_proj_kernel, colmajor),
        grid=(seq // tm, bsz),
        in_specs=[
            xspec,
            pl.BlockSpec((None, None, None, 3, d), lambda t, b: (layer, b, 1, 0, 0)),
            pl.BlockSpec((None, None, 1, d), lambda t, b: (layer, 1, 0, 0)),
            pl.BlockSpec(w_in.shape, const, pipeline_mode=pl.Buffered(1)),
            pl.BlockSpec(wgate.shape, const),
            pl.BlockSpec(bgate.shape, const),
        ],
        out_specs=[pl.BlockSpec((tm, w), lambda t, b: (t, b)) for w in widths],
        out_shape=[jax.ShapeDtypeStruct((seq, bsz * w), BF16 if i == 2 else F32)
                   for i, w in enumerate(widths)],
        scratch_shapes=_chunk_scratch(tm, d),
        compiler_params=_cparams("parallel", "parallel"),
        name="gla_project",
    )(xv, mod_a, norm_w4, w_in, wgate, bgate)


def _head_out(of, ob, og, gnw, w_out, n_heads, zs_ref):
    dv = of.shape[1] // n_heads
    o = of + ob
    for h in range(n_heads):
        sl = slice(h * dv, (h + 1) * dv)
        zs_ref[:, sl] = (_rms(o[:, sl], gnw) * _silu(og[:, sl])).astype(BF16)
    return _dot(zs_ref[...], w_out)


def _mix_out_kernel(n_heads, colmajor, of_ref, ob_ref, og_ref, x_ref, mod_ref, gnw_ref, w_ref,
                    o_ref, stage_ref, xs_ref, zs_ref):
    _load_chunks(x_ref, stage_ref, xs_ref, colmajor)
    y = _head_out(of_ref[...], ob_ref[...], og_ref[...], gnw_ref[...], w_ref[...], n_heads, zs_ref)
    xn = xs_ref[...] + mod_ref[2:3, :] * y
    _store_chunks(xn, stage_ref, o_ref, colmajor)


def _mix_out(of, ob, og, x2, mod_a, layer, gnw, w_out, n_heads, bsz, *, colmajor):
    seq = x2.shape[0]
    d = x2.shape[1] // bsz
    wv = of.shape[1] // bsz
    gnw = gnw.reshape(1, -1)
    xv, xspec, tm = _x_stream(x2, bsz, colmajor)
    ospec = pl.BlockSpec((tm, wv), lambda t, b: (t, b))
    out = pl.pallas_call(
        functools.partial(_mix_out_kernel, n_heads, colmajor),
        grid=(seq // tm, bsz),
        in_specs=[ospec, ospec, ospec, xspec,
                  pl.BlockSpec((None, None, None, 3, d), lambda t, b: (layer, b, 1, 0, 0)),
                  pl.BlockSpec(gnw.shape, lambda t, b: (0, 0)),
                  pl.BlockSpec(w_out.shape, lambda t, b: (0, 0))],
        out_specs=xspec,
        out_shape=jax.ShapeDtypeStruct(xv.shape, F32),
        scratch_shapes=_chunk_scratch(tm, d) + [pltpu.VMEM((tm, wv), BF16)],
        compiler_params=_cparams("parallel", "parallel"),
        name="mix_out",
    )(of, ob, og, xv, mod_a, gnw, w_out)
    return out.reshape(seq, bsz * d)


def _gla_mixer(xc2, x2, mods, norm_w4, layer, prm, bsz, need_ctx):
    (modc_a, _), (modx_a, _) = mods
    w_in, wgate, bgate, gnw, w_out, dkt, dvt = prm
    qc, kc, vc, ogc, gfc, gbc = _gla_project(xc2, modc_a, norm_w4, layer, w_in, wgate, bgate, bsz,
                                             dkt, dvt, colmajor=False)
    qx, kx, vx, ogx, gfx, gbx = _gla_project(x2, modx_a, norm_w4, layer, w_in, wgate, bgate, bsz,
                                             dkt, dvt, colmajor=True)
    oc, ox = _bidir_scan((qc, kc, kc, vc, gfc, gbc), (qx, kx, kx, vx, gfx, gbx), GLA_HEADS, bsz, need_ctx)
    x_new = _mix_out(ox[0], ox[1], ogx, x2, modx_a, layer, gnw, w_out, GLA_HEADS, bsz, colmajor=True)
    xc_new = None
    if need_ctx:
        xc_new = _mix_out(oc[0], oc[1], ogc, xc2, modc_a, layer, gnw, w_out, GLA_HEADS, bsz, colmajor=False)
    return xc_new, x_new


def _hgrn_mixer(xc2, x2, mods, norm_w4, layer, prm, bsz, need_ctx):
    (modc_a, _), (modx_a, _) = mods
    w_in, lb, gnw, w_out, n_heads = prm
    qc, kfc, kbc, vc, gfc, gbc, ogc = _hgrn_project(xc2, modc_a, norm_w4, layer, w_in, lb, bsz)
    qx, kfx, kbx, vx, gfx, gbx, ogx = _hgrn_project(x2, modx_a, norm_w4, layer, w_in, lb, bsz)
    oc, ox = _bidir_scan((qc, kfc, kbc, vc, gfc, gbc), (qx, kfx, kbx, vx, gfx, gbx), n_heads, bsz, need_ctx)
    x_new = _mix_out(ox[0], ox[1], ogx, x2, modx_a, layer, gnw, w_out, n_heads, bsz, colmajor=False)
    xc_new = None
    if need_ctx:
        xc_new = _mix_out(oc[0], oc[1], ogc, xc2, modc_a, layer, gnw, w_out, n_heads, bsz, colmajor=False)
    return xc_new, x_new


def _forward(x, c, ctx, c_ctx, ada_w, ada_b, norm_w, ffn_w_in, ffn_w_out,
             s5_a_re, s5_a_im, s5_log_step, s5_b_re, s5_b_im, s5_c_re, s5_c_im, s5_d,
             s5_w_glu, s5_b_glu, gla_w_in, gla_w_gate, gla_b_gate, gla_norm_w, gla_w_out,
             hgrn_w_in, hgrn_lb_logits, hgrn_norm_w, hgrn_w_out, final_norm_w, depth=None):
    depth = ada_w.shape[0] if depth is None else depth
    bsz, seq, d = x.shape
    mods_x, mods_c = _ada(c, c_ctx, ada_w, ada_b)
    norm_w4 = norm_w.reshape(norm_w.shape[0], 3, 1, d)
    w_in = ffn_w_in.astype(BF16)
    w_out = ffn_w_out.astype(BF16)
    lb_soft = jax.nn.softmax(hgrn_lb_logits.astype(F32), axis=0)
    lb_all = jnp.cumsum(lb_soft, axis=0) - lb_soft[0]

    xc2 = jnp.transpose(ctx, (1, 0, 2)).reshape(ctx.shape[1], bsz * d)
    x2 = x
    for i in range(depth):
        last = i == depth - 1
        kind, j = i % N_MIXERS, i // N_MIXERS
        x2 = _ffn(x2, mods_x[0], norm_w4, w_in, w_out, i, 0, bsz, in_bld=(i == 0))
        xc2 = _ffn(xc2, mods_c[0], norm_w4, w_in, w_out, i, 0, bsz)
        mods = (mods_c, mods_x)
        if kind == 0:
            prm = _s5c_prep(s5_a_re[j], s5_a_im[j], s5_log_step[j], s5_b_re[j], s5_b_im[j],
                            s5_c_re[j], s5_c_im[j])
            extra = (s5_d[j], s5_w_glu[j].astype(BF16), s5_b_glu[j])
            yc, x2 = _s5c_mixer(xc2, x2, mods, norm_w4, i, prm, extra, bsz, not last)
        elif kind == 1:
            dkt = d // 2
            dvt = d
            n_low = 2 * GLA_RANK
            pad = LANES - n_low
            wi = jnp.pad(gla_w_in[j], ((0, 0), (0, pad))).astype(BF16)
            wg = jnp.zeros((LANES, 2 * dkt), F32)
            wg = wg.at[:GLA_RANK, :dkt].set(gla_w_gate[j, 0]).at[GLA_RANK:n_low, dkt:].set(gla_w_gate[j, 1])
            prm = (wi, wg.astype(BF16), gla_b_gate[j].reshape(1, 2 * dkt), gla_norm_w[j],
                   gla_w_out[j].astype(BF16), dkt, dvt)
            yc, x2 = _gla_mixer(xc2, x2, mods, norm_w4, i, prm, bsz, not last)
        else:
            prm = (hgrn_w_in[j].astype(BF16), lb_all[i], hgrn_norm_w[j], hgrn_w_out[j].astype(BF16),
                   d // HGRN_DK)
            yc, x2 = _hgrn_mixer(xc2, x2, mods, norm_w4, i, prm, bsz, not last)
        x2 = _ffn(x2, mods_x[0], norm_w4, w_in, w_out, i, 2, bsz,
                  final_w=final_norm_w.reshape(1, d) if last else None)
        if not last:
            xc2 = _ffn(yc, mods_c[0], norm_w4, w_in, w_out, i, 2, bsz)
    return x2


def kernel(x, c, ctx, c_ctx, ada_w, ada_b, norm_w, ffn_w_in, ffn_w_out, s5_a_re, s5_a_im, s5_log_step,
           s5_b_re, s5_b_im, s5_c_re, s5_c_im, s5_d, s5_w_glu, s5_b_glu, gla_w_in, gla_w_gate,
           gla_b_gate, gla_norm_w, gla_w_out, hgrn_w_in, hgrn_lb_logits, hgrn_norm_w, hgrn_w_out,
           final_norm_w):
    return _forward(x, c, ctx, c_ctx, ada_w, ada_b, norm_w, ffn_w_in, ffn_w_out, s5_a_re, s5_a_im,
                    s5_log_step, s5_b_re, s5_b_im, s5_c_re, s5_c_im, s5_d, s5_w_glu, s5_b_glu,
                    gla_w_in, gla_w_gate, gla_b_gate, gla_norm_w, gla_w_out, hgrn_w_in,
                    hgrn_lb_logits, hgrn_norm_w, hgrn_w_out, final_norm_w)
```

```python
import functools
import math

import numpy as np
import jax
import jax.numpy as jnp
from jax import lax
from jax.experimental import pallas as pl
from jax.experimental.pallas import tpu as pltpu

F32 = jnp.float32
BF16 = jnp.bfloat16

NORM_EPS = 1e-6
GRID_W = 64
N_MOD = 9
N_MIXERS = 3

S5_GROUP = 16
S5_STATE = 64
S5_LANE_GROUPS = 8
S5_TB = 64
S5_TS = 64

GLA_HEADS = 4
GLA_RANK = 16
GLA_TAU = 16.0
HGRN_DK = 128

SUBLANES = 8
LANES = 128
CHUNK = 128
GROUP = SUBLANES * SUBLANES
N_GROUPS = CHUNK // GROUP
SUB = CHUNK // SUBLANES
N_FINE = 3
N_COARSE = 3
N_LEVELS = 7
LOG2E = 1.4426950408889634

FFN_CHUNK = 768

VMEM_LIMIT_BYTES = 56 * 1024 * 1024


def _cparams(*sem):
    return pltpu.CompilerParams(dimension_semantics=sem, vmem_limit_bytes=VMEM_LIMIT_BYTES)


def _dot(a, b):
    return jnp.dot(a, b, preferred_element_type=F32)


def _dot_nt(a, b):
    return lax.dot_general(a, b, (((1,), (1,)), ((), ())), preferred_element_type=F32)


def _dot_tn(a, b):
    return lax.dot_general(a, b, (((0,), (0,)), ((), ())), preferred_element_type=F32)


def _sigmoid(x):
    return 1.0 / (1.0 + jnp.exp(-x))


def _silu(x):
    return x * _sigmoid(x)


def _gelu_tanh(x):
    c = math.sqrt(2.0 / math.pi)
    return 0.5 * x * (1.0 + jnp.tanh(c * (x + 0.044715 * (x * x * x))))


def _log_sigmoid(x):
    return jnp.minimum(x, 0.0) - jnp.log1p(jnp.exp(-jnp.abs(x)))


def _rms(x, w):
    ms = jnp.mean(x * x, axis=-1, keepdims=True)
    return (x * lax.rsqrt(ms + NORM_EPS)) * w


def _norm_mod(x, w, shift, scale):
    return _rms(x, w) * (1.0 + scale) + shift


def _ada_kernel(cc_ref, w_ref, b_ref, o_ref):
    a = _silu(cc_ref[...]).astype(BF16)
    o_ref[...] = _dot(a, w_ref[...].astype(BF16)) + b_ref[...]


def _ada(c, c_ctx, ada_w, ada_b):
    depth, d, nd = ada_w.shape
    bsz = c.shape[0]
    rows = ((bsz + 1 + SUBLANES - 1) // SUBLANES) * SUBLANES
    cc = jnp.zeros((rows, d), F32).at[:bsz].set(c).at[bsz].set(c_ctx)
    nblk = nd // d
    mod = pl.pallas_call(
        _ada_kernel,
        grid=(depth, nblk),
        in_specs=[
            pl.BlockSpec((rows, d), lambda i, n: (0, 0)),
            pl.BlockSpec((None, d, d), lambda i, n: (i, 0, n)),
            pl.BlockSpec((None, 1, d), lambda i, n: (i, 0, n)),
        ],
        out_specs=pl.BlockSpec((None, rows, d), lambda i, n: (i, 0, n)),
        out_shape=jax.ShapeDtypeStruct((depth, rows, nd), F32),
        compiler_params=_cparams("arbitrary", "arbitrary"),
        name="ada_mod",
    )(cc, ada_w, ada_b.reshape(depth, 1, nd))
    mx = mod[:, :bsz].reshape(depth, bsz, 3, 3, d)
    mc = jnp.broadcast_to(mod[:, bsz:bsz + 1], (depth, bsz, nd)).reshape(depth, bsz, 3, 3, d)
    return (mx, jnp.transpose(mx, (0, 2, 3, 1, 4))), (mc, jnp.transpose(mc, (0, 2, 3, 1, 4)))


def _ffn_kernel(chunks, final, x_ref, mod_ref, nw_ref, wi_ref, wo_ref, *rest):
    if final:
        fw_ref, o_ref = rest
    else:
        (o_ref,) = rest
    ff = wo_ref.shape[0]
    x = x_ref[...]
    h = _norm_mod(x, nw_ref[...], mod_ref[0:1, :], mod_ref[1:2, :]).astype(BF16)
    acc = None
    for lo, hi in chunks:
        g = _dot(h, wi_ref[:, lo:hi])
        u = _dot(h, wi_ref[:, ff + lo:ff + hi])
        part = _dot((_silu(g) * u).astype(BF16), wo_ref[lo:hi, :])
        acc = part if acc is None else acc + part
    y = x + (0.5 * mod_ref[2:3, :]) * acc
    if final:
        y = _rms(y, fw_ref[...])
    o_ref[...] = y


def _ff_chunks(ff, width):
    assert ff % LANES == 0 and width % LANES == 0
    return tuple((lo, min(lo + width, ff)) for lo in range(0, ff, width))


def _ffn(x, mod_a, norm_w4, w_in, w_out, layer, stage, bsz, *, in_bld=False, final_w=None):
    d = norm_w4.shape[-1]
    seq = x.shape[1] if in_bld else x.shape[0]
    ff = w_out.shape[2]
    tm = 512 if seq % 512 == 0 else 256
    s = 0 if stage == 0 else 1
    final = final_w is not None
    resident = pl.Buffered(1)
    if in_bld:
        x_spec = pl.BlockSpec((None, tm, d), lambda t, b: (b, t, 0))
    else:
        x_spec = pl.BlockSpec((tm, d), lambda t, b: (t, b))
    in_specs = [
        x_spec,
        pl.BlockSpec((None, None, None, 3, d), lambda t, b: (layer, b, stage, 0, 0)),
        pl.BlockSpec((None, None, 1, d), lambda t, b: (layer, stage, 0, 0)),
        pl.BlockSpec((None, None, d, 2 * ff), lambda t, b: (layer, s, 0, 0), pipeline_mode=resident),
        pl.BlockSpec((None, None, ff, d), lambda t, b: (layer, s, 0, 0), pipeline_mode=resident),
    ]
    args = [x, mod_a, norm_w4, w_in, w_out]
    if final:
        in_specs.append(pl.BlockSpec((1, d), lambda t, b: (0, 0)))
        args.append(final_w)
        out_spec = pl.BlockSpec((None, tm, d), lambda t, b: (b, t, 0))
        out_shape = jax.ShapeDtypeStruct((bsz, seq, d), F32)
    else:
        out_spec = pl.BlockSpec((tm, d), lambda t, b: (t, b))
        out_shape = jax.ShapeDtypeStruct((seq, bsz * d), F32)
    return pl.pallas_call(
        functools.partial(_ffn_kernel, _ff_chunks(ff, FFN_CHUNK), final),
        grid=(seq // tm, bsz),
        in_specs=in_specs,
        out_specs=out_spec,
        out_shape=out_shape,
        compiler_params=_cparams("parallel", "parallel"),
        name="ffn_half",
    )(*args)


def _s5_prep(a_re, a_im, log_step, b_re, b_im, c_re, c_im, bsz):
    dt = jnp.exp(log_step.astype(F32))[..., None]
    a_re = a_re.astype(F32)
    a_im = a_im.astype(F32)
    mag = jnp.exp(a_re * dt)
    abar_re = mag * jnp.cos(a_im * dt)
    abar_im = mag * jnp.sin(a_im * dt)
    den = a_re * a_re + a_im * a_im
    zr = abar_re - 1.0
    zi = abar_im
    coef_re = ((zr * a_re + zi * a_im) / den)[..., None]
    coef_im = ((zi * a_re - zr * a_im) / den)[..., None]
    b_re = b_re.astype(F32)
    b_im = b_im.astype(F32)
    bbar_re = coef_re * b_re - coef_im * b_im
    bbar_im = coef_re * b_im + coef_im * b_re
    n_dir, g, p = abar_re.shape
    gc = b_re.shape[-1]
    nj = g // S5_LANE_GROUPS
    lg = S5_LANE_GROUPS
    are = jnp.broadcast_to(abar_re.reshape(n_dir, 1, g * p), (n_dir, bsz, g * p))
    aim = jnp.broadcast_to(abar_im.reshape(n_dir, 1, g * p), (n_dir, bsz, g * p))
    eye = jnp.eye(lg, dtype=F32)
    bb = jnp.stack([bbar_re, bbar_im], axis=1).reshape(n_dir, 2, nj, lg, p, gc)
    bb = jnp.transpose(bb, (0, 2, 3, 5, 1, 4))
    wb = bb[:, :, :, :, :, None, :] * eye[None, None, :, None, None, :, None]
    wb = wb.reshape(n_dir, nj, lg * gc, 2 * lg * p).astype(BF16)
    cc = jnp.stack([c_re.astype(F32), -c_im.astype(F32)], axis=1).reshape(n_dir, 2, nj, lg, gc, p)
    cc = jnp.transpose(cc, (0, 2, 1, 3, 5, 4))
    wc = cc[:, :, :, :, :, None, :] * eye[None, None, None, :, None, :, None]
    wc = wc.reshape(n_dir, nj, 2 * lg * p, lg * gc).astype(BF16)
    return are, aim, wb, wc


def _s5_kernel(tb, rev, finish, x_ref, mod_ref, nw_ref, are_ref, aim_ref, wb_ref, wc_ref,
               h0re_ref, h0im_ref, *rest):
    n_io = 7 if finish else 3
    io, scratch = rest[:n_io], rest[n_io:]
    if finish:
        yf_ref, dsk_ref, wglu_ref, bglu_ref, o_ref, hre_ref, him_ref = io
        h_s = scratch[12]
    else:
        o_ref, hre_ref, him_ref = io
    bu = (scratch[0:2], scratch[2:4])
    hs = (scratch[4:6], scratch[6:8])
    hs_re, hs_im, y_s, hb_s = scratch[8:12]
    bsz = x_ref.shape[1]
    d = x_ref.shape[2]
    nj = wb_ref.shape[0]
    sw = wb_ref.shape[2] // 2
    ri = S5_TS * bsz
    nit = tb // S5_TS

    @pl.when(pl.program_id(0) == 0)
    def _():
        hs_re[...] = h0re_ref[...]
        hs_im[...] = h0im_ref[...]

    x3 = x_ref[...]
    ms = jnp.mean(x3 * x3, axis=-1, keepdims=True)
    h3 = (x3 * lax.rsqrt(ms + NORM_EPS)) * nw_ref[...] * (1.0 + mod_ref[1]) + mod_ref[0]
    h2 = h3.reshape(tb * bsz, d)
    if finish:
        h_s[...] = h2
    hb_s[...] = h2.astype(BF16)

    def lanes(j):
        return slice(j * LANES, (j + 1) * LANES)

    def drive(j, rows):
        r = _dot(hb_s[rows, lanes(j)], wb_ref[j])
        bu[j % 2][0][rows, :] = r[:, :sw]
        bu[j % 2][1][rows, :] = r[:, sw:]

    def readout(j, rows):
        y_s[rows, lanes(j)] = (_dot(hs[j % 2][0][rows, :].astype(BF16), wc_ref[j, :sw, :])
                               + _dot(hs[j % 2][1][rows, :].astype(BF16), wc_ref[j, sw:, :]))

    drive(0, slice(None))
    for j in range(nj):
        cur = j % 2
        ar = are_ref[:, j * sw:(j + 1) * sw]
        ai = aim_ref[:, j * sw:(j + 1) * sw]

        def body(i, carry, j=j, cur=cur, ar=ar, ai=ai):
            hr, hi = carry
            rows = pl.ds(pl.multiple_of(i * ri, ri), ri)
            if j >= 1:
                readout(j - 1, rows)
            if j + 1 < nj:
                drive(j + 1, rows)
            for s in range(S5_TS):
                t = i * S5_TS + s
                tt = (tb - 1 - t) if rev else t
                sl = pl.ds(pl.multiple_of(tt * bsz, bsz), bsz)
                hr, hi = (ar * hr - ai * hi + bu[cur][0][sl, :], ar * hi + ai * hr + bu[cur][1][sl, :])
                hs[cur][0][sl, :] = hr
                hs[cur][1][sl, :] = hi
            return hr, hi

        hr, hi = lax.fori_loop(
            0, nit, body, (hs_re[:, j * sw:(j + 1) * sw], hs_im[:, j * sw:(j + 1) * sw]))
        hs_re[:, j * sw:(j + 1) * sw] = hr
        hs_im[:, j * sw:(j + 1) * sw] = hi
    readout(nj - 1, slice(None))

    hre_ref[...] = hs_re[...]
    him_ref[...] = hs_im[...]
    if not finish:
        o_ref[...] = y_s[...].reshape(tb, bsz, d)
    else:
        y = y_s[...] + yf_ref[...].reshape(tb * bsz, d) + dsk_ref[...] * h_s[...]
        z = _dot(_gelu_tanh(y).astype(BF16), wglu_ref[...]) + bglu_ref[...]
        out = z[:, :d] * _sigmoid(z[:, d:])
        o_ref[...] = x3 + mod_ref[2] * out.reshape(tb, bsz, d)


def _s5_pass(x2, mod_b, norm_w4, layer, are, aim, wb, wc, h0, bsz, *, rev, fin=None):
    seq = x2.shape[0]
    d = x2.shape[1] // bsz
    tb = S5_TB
    nblk = seq // tb
    dirn = 1 if rev else 0
    nstate = are.shape[-1]
    sw = wb.shape[3] // 2
    x3 = x2.reshape(seq, bsz, d)
    blk = (lambda i: (nblk - 1 - i, 0, 0)) if rev else (lambda i: (i, 0, 0))
    x_spec = pl.BlockSpec((tb, bsz, d), blk)
    const2 = lambda i: (0, 0)
    in_specs = [
        x_spec,
        pl.BlockSpec((None, None, 3, bsz, d), lambda i: (layer, 1, 0, 0, 0)),
        pl.BlockSpec((None, None, 1, d), lambda i: (layer, 1, 0, 0)),
        pl.BlockSpec((None, bsz, nstate), lambda i: (dirn, 0, 0)),
        pl.BlockSpec((None, bsz, nstate), lambda i: (dirn, 0, 0)),
        pl.BlockSpec((None,) + wb.shape[1:], lambda i: (dirn, 0, 0, 0)),
        pl.BlockSpec((None,) + wc.shape[1:], lambda i: (dirn, 0, 0, 0)),
        pl.BlockSpec((bsz, nstate), const2),
        pl.BlockSpec((bsz, nstate), const2),
    ]
    args = [x3, mod_b, norm_w4, are, aim, wb, wc, h0[0], h0[1]]
    scratch = [pltpu.VMEM((tb * bsz, sw), F32) for _ in range(8)] + [
        pltpu.VMEM((bsz, nstate), F32), pltpu.VMEM((bsz, nstate), F32),
        pltpu.VMEM((tb * bsz, d), F32), pltpu.VMEM((tb * bsz, d), BF16),
    ]
    finish = fin is not None
    if finish:
        y_fwd, d_skip, w_glu, b_glu = fin
        in_specs += [
            x_spec,
            pl.BlockSpec((1, d), const2),
            pl.BlockSpec(w_glu.shape, const2),
            pl.BlockSpec((1, 2 * d), const2),
        ]
        args += [y_fwd.reshape(seq, bsz, d), d_skip.reshape(1, d), w_glu, b_glu.reshape(1, 2 * d)]
        scratch.append(pltpu.VMEM((tb * bsz, d), F32))
    out, hre, him = pl.pallas_call(
        functools.partial(_s5_kernel, tb, rev, finish),
        grid=(nblk,),
        in_specs=in_specs,
        out_specs=[x_spec, pl.BlockSpec((bsz, nstate), const2), pl.BlockSpec((bsz, nstate), const2)],
        out_shape=[jax.ShapeDtypeStruct((seq, bsz, d), F32),
                   jax.ShapeDtypeStruct((bsz, nstate), F32),
                   jax.ShapeDtypeStruct((bsz, nstate), F32)],
        scratch_shapes=scratch,
        compiler_params=_cparams("arbitrary"),
        name="s5_bwd_finish" if finish else "s5_fwd",
    )(*args)
    return out.reshape(seq, bsz * d), (hre, him)


def _s5_mixer(xc2, x2, mods_b, norm_w4, layer, prm, bsz, need_ctx):
    (modc_b, modx_b) = mods_b
    are, aim, wb, wc, d_skip, w_glu, b_glu = prm
    nstate = are.shape[-1]
    zero = (jnp.zeros((bsz, nstate), F32), jnp.zeros((bsz, nstate), F32))
    ycf, hf = _s5_pass(xc2, modc_b, norm_w4, layer, are, aim, wb, wc, zero, bsz, rev=False)
    yxf, _ = _s5_pass(x2, modx_b, norm_w4, layer, are, aim, wb, wc, hf, bsz, rev=False)
    xc_new, hb = _s5_pass(xc2, modc_b, norm_w4, layer, are, aim, wb, wc, zero, bsz, rev=True,
                          fin=(ycf, d_skip, w_glu, b_glu))
    x_new, _ = _s5_pass(x2, modx_b, norm_w4, layer, are, aim, wb, wc, hb, bsz, rev=True,
                        fin=(yxf, d_skip, w_glu, b_glu))
    return (xc_new if need_ctx else None), x_new


S5_T = 16
S5C_TB = 128
S5C_TB_FWD = 256
SLOT = LANES // SUBLANES


def _cmul(ar, ai, br, bi):
    return ar * br - ai * bi, ar * bi + ai * br


def _s5c_prep(a_re, a_im, log_step, b_re, b_im, c_re, c_im):
    dt = jnp.exp(log_step.astype(F32))[..., None]
    a_re = a_re.astype(F32)
    a_im = a_im.astype(F32)
    mag = jnp.exp(a_re * dt)
    abar_re = mag * jnp.cos(a_im * dt)
    abar_im = mag * jnp.sin(a_im * dt)
    den = a_re * a_re + a_im * a_im
    zr = abar_re - 1.0
    zi = abar_im
    coef_re = ((zr * a_re + zi * a_im) / den)[..., None]
    coef_im = ((zi * a_re - zr * a_im) / den)[..., None]
    b_re = b_re.astype(F32)
    b_im = b_im.astype(F32)
    bb_re = coef_re * b_re - coef_im * b_im
    bb_im = coef_re * b_im + coef_im * b_re
    cc_re = c_re.astype(F32)
    cc_im = c_im.astype(F32)
    t = S5_T
    pr, pi = [jnp.ones_like(abar_re)], [jnp.zeros_like(abar_im)]
    for _ in range(t):
        nr, ni = _cmul(pr[-1], pi[-1], abar_re, abar_im)
        pr.append(nr)
        pi.append(ni)
    pr, pi = jnp.stack(pr), jnp.stack(pi)
    hi = lax.Precision.HIGHEST
    g, p, gc = abar_re.shape[1], abar_re.shape[2], b_re.shape[-1]
    prk = jnp.transpose(pr, (1, 2, 3, 0))
    pik = jnp.transpose(pi, (1, 2, 3, 0))
    e_re, e_im = _cmul(prk[..., None], pik[..., None], bb_re[:, :, :, None, :], bb_im[:, :, :, None, :])
    nkc = (t + 1) * gc
    taps = (jnp.einsum('dgop,dgpn->dgon', cc_re, e_re.reshape(2, g, p, nkc), precision=hi)
            - jnp.einsum('dgop,dgpn->dgon', cc_im, e_im.reshape(2, g, p, nkc), precision=hi))
    taps = taps.reshape(2, g, gc, t + 1, gc)
    kf, kb = taps[0], taps[1]
    seq = jnp.concatenate([jnp.flip(kb[:, :, 1:t], axis=2), kf[:, :, :1] + kb[:, :, :1], kf[:, :, 1:t]],
                          axis=2)
    tab = jnp.transpose(seq, (0, 3, 2, 1)).reshape(g, gc, (2 * t - 1) * gc)
    m = jnp.stack([tab[:, :, gc * (t - 1 - s):gc * (t - 1 - s) + t * gc] for s in range(t)], axis=1)
    m = m.reshape(g, t * gc, t * gc)

    def wst(d, flip):
        er, ei = e_re[d, :, :, :t], e_im[d, :, :, :t]
        if flip:
            er, ei = jnp.flip(er, 2), jnp.flip(ei, 2)
        w = jnp.concatenate([jnp.transpose(er, (0, 2, 3, 1)), jnp.transpose(ei, (0, 2, 3, 1))], axis=-1)
        return w.reshape(g, t * gc, 2 * p)

    wst_f = wst(0, True)
    wst_b = wst(1, False)

    def wout(d, flip):
        qr, qi = prk[d, :, :, 1:t + 1], pik[d, :, :, 1:t + 1]
        if flip:
            qr, qi = jnp.flip(qr, 2), jnp.flip(qi, 2)
        cr = jnp.transpose(cc_re[d], (0, 2, 1))
        ci = jnp.transpose(cc_im[d], (0, 2, 1))
        gr, gi = _cmul(cr[:, :, None, :], ci[:, :, None, :], qr[..., None], qi[..., None])
        return jnp.concatenate([gr, -gi], axis=1).reshape(g, 2 * p, t * gc)

    wout_f = wout(0, False)
    wout_b = wout(1, True)
    a1 = jnp.concatenate([pr[t], pr[t]], axis=-1)
    a2 = jnp.concatenate([-pi[t], pi[t]], axis=-1)
    bc = lambda a: jnp.broadcast_to(a[:, :, None, :], a.shape[:2] + (SUBLANES, a.shape[-1]))
    return dict(m=m.astype(BF16), wst=(wst_f.astype(BF16), wst_b.astype(BF16)),
                wout=(wout_f.astype(BF16), wout_b.astype(BF16)), a1=bc(a1), a2=bc(a2))


def _slot_transpose(x):
    m, n, bsz, _ = x.shape
    slot = lax.broadcasted_iota(jnp.int32, (bsz, LANES), 1) // SLOT

    def rot(v, shift):
        return pltpu.roll(v.reshape(-1, LANES), shift, 1).reshape(v.shape)

    d = n // 2
    while d >= 1:
        keep = (slot & d) == 0
        x = x.reshape(m, n // (2 * d), 2, d, bsz, LANES)
        lo, hi = x[:, :, 0], x[:, :, 1]
        new_lo = jnp.where(keep, lo, rot(hi, SLOT * d))
        new_hi = jnp.where(keep, rot(lo, LANES - SLOT * d), hi)
        x = jnp.stack([new_lo, new_hi], axis=2).reshape(m, n, bsz, LANES)
        d //= 2
    return x


def _s5c_gather(h_s, a_s, nch):
    nj = h_s.shape[1] // LANES
    nh = S5_T // SUBLANES
    for j in range(nj):
        x = h_s[:, j * LANES:(j + 1) * LANES].reshape(nch * nh, SUBLANES, SUBLANES, LANES)
        r = _slot_transpose(x).reshape(nch, nh, SUBLANES, SUBLANES, LANES)
        for gl in range(S5_LANE_GROUPS):
            for hh in range(nh):
                a_s[S5_LANE_GROUPS * j + gl, :, hh * LANES:(hh + 1) * LANES] = (
                    r[:, hh, gl].reshape(nch * SUBLANES, LANES))


def _s5c_state_scan(v, st_ref, sin_ref, a1, a2, g, nch, rev):
    gl = slice(g * LANES, (g + 1) * LANES)
    s = st_ref[:, gl]
    w = pltpu.roll(s, LANES // 2, 1)
    for n in (range(nch - 1, -1, -1) if rev else range(nch)):
        rows = slice(SUBLANES * n, SUBLANES * (n + 1))
        sin_ref[rows, gl] = s
        vn = v[rows, :]
        s, w = a1 * s + a2 * w + vn, a1 * w - a2 * s + pltpu.roll(vn, LANES // 2, 1)
    st_ref[:, gl] = s


def _s5c_norm(x_ref, mod_ref, nw_ref, h_s):
    x3 = x_ref[...]
    ms = jnp.mean(x3 * x3, axis=-1, keepdims=True)
    h3 = (x3 * lax.rsqrt(ms + NORM_EPS)) * nw_ref[...] * (1.0 + mod_ref[1]) + mod_ref[0]
    h_s[...] = h3.reshape(h_s.shape)


def _s5c_fwd_kernel(x_ref, mod_ref, nw_ref, wst_ref, a1_ref, a2_ref, h0_ref,
                    sin_ref, hfin_ref, st_ref, h_s, a_s):
    ng = wst_ref.shape[0]
    nch = a_s.shape[1] // SUBLANES

    @pl.when(pl.program_id(0) == 0)
    def _():
        st_ref[...] = h0_ref[...]

    _s5c_norm(x_ref, mod_ref, nw_ref, h_s)
    _s5c_gather(h_s, a_s, nch)
    for g in range(ng):
        v = _dot(a_s[g].astype(BF16), wst_ref[g])
        _s5c_state_scan(v, st_ref, sin_ref, a1_ref[g], a2_ref[g], g, nch, False)
    hfin_ref[...] = st_ref[...]


def _s5c_bwd_kernel(x_ref, mod_ref, nw_ref, sinf_ref, m_ref, wst_ref, woutf_ref, woutb_ref,
                    a1_ref, a2_ref, h0_ref, dsk_ref, y_ref, hfin_ref, st_ref, h_s, a_s, sinb_s):
    ng = m_ref.shape[0]
    nch = a_s.shape[1] // SUBLANES

    @pl.when(pl.program_id(0) == 0)
    def _():
        st_ref[...] = h0_ref[...]

    _s5c_norm(x_ref, mod_ref, nw_ref, h_s)
    _s5c_gather(h_s, a_s, nch)
    for j in range(ng // S5_LANE_GROUPS):
        ys = []
        for gl in range(S5_LANE_GROUPS):
            g = S5_LANE_GROUPS * j + gl
            lanes = slice(g * LANES, (g + 1) * LANES)
            a = a_s[g].astype(BF16)
            _s5c_state_scan(_dot(a, wst_ref[g]), st_ref, sinb_s, a1_ref[g], a2_ref[g], g, nch, True)
            ys.append(_dot(a, m_ref[g])
                      + _dot(sinf_ref[:, lanes].astype(BF16), woutf_ref[g])
                      + _dot(sinb_s[:, lanes].astype(BF16), woutb_ref[g]))
        cols = slice(j * LANES, (j + 1) * LANES)
        for hh in range(S5_T // SUBLANES):
            r = jnp.stack([y[:, hh * LANES:(hh + 1) * LANES].reshape(nch, SUBLANES, LANES) for y in ys],
                          axis=1)
            out = _slot_transpose(r)
            for n in range(nch):
                t0 = S5_T * n + SUBLANES * hh
                skip = dsk_ref[:, cols] * h_s[t0 * SUBLANES:(t0 + SUBLANES) * SUBLANES, cols]
                y_ref[t0:t0 + SUBLANES, :, cols] = out[n] + skip.reshape(SUBLANES, SUBLANES, LANES)
    hfin_ref[...] = st_ref[...]


def _s5c_glu_kernel(y_ref, x_ref, mod_ref, w_ref, b_ref, o_ref):
    d = x_ref.shape[1]
    z = _dot(_gelu_tanh(y_ref[...]).astype(BF16), w_ref[...]) + b_ref[...]
    o_ref[...] = x_ref[...] + mod_ref[2:3, :] * (z[:, :d] * _sigmoid(z[:, d:]))


def _s5c_stream(x2, mod_b, norm_w4, layer, prm, h0, bsz, *, rev, sin_f=None, d_skip=None):
    seq = x2.shape[0]
    d = x2.shape[1] // bsz
    tb = S5C_TB if rev else S5C_TB_FWD
    nblk = seq // tb
    nch = tb // S5_T
    dirn = 1 if rev else 0
    ng = prm["m"].shape[0]
    ns = ng * LANES
    x3 = x2.reshape(seq, bsz, d)
    blk = (lambda i: (nblk - 1 - i, 0, 0)) if rev else (lambda i: (i, 0, 0))
    blk2 = (lambda i: (nblk - 1 - i, 0)) if rev else (lambda i: (i, 0))
    x_spec = pl.BlockSpec((tb, bsz, d), blk)
    sin_spec = pl.BlockSpec((nch * bsz, ns), blk2)
    st_spec = pl.BlockSpec((bsz, ns), lambda i: (0, 0))
    res = pl.Buffered(1)
    wspec = lambda w: pl.BlockSpec(w.shape, lambda i: (0,) * w.ndim, pipeline_mode=res)
    aspec = pl.BlockSpec((None,) + prm["a1"].shape[1:], lambda i: (dirn, 0, 0, 0), pipeline_mode=res)
    common = [x_spec,
              pl.BlockSpec((None, None, 3, bsz, d), lambda i: (layer, 1, 0, 0, 0)),
              pl.BlockSpec((None, None, 1, d), lambda i: (layer, 1, 0, 0))]
    scratch = [pltpu.VMEM((bsz, ns), F32), pltpu.VMEM((tb * bsz, d), F32),
               pltpu.VMEM((ng, nch * bsz, S5_T * S5_GROUP), F32)]
    if not rev:
        return pl.pallas_call(
            _s5c_fwd_kernel,
            grid=(nblk,),
            in_specs=common + [wspec(prm["wst"][0]), aspec, aspec, st_spec],
            out_specs=[sin_spec, st_spec],
            out_shape=[jax.ShapeDtypeStruct((seq // S5_T * bsz, ns), F32),
                       jax.ShapeDtypeStruct((bsz, ns), F32)],
            scratch_shapes=scratch,
            compiler_params=_cparams("arbitrary"),
            name="s5_fwd_states",
        )(x3, mod_b, norm_w4, prm["wst"][0], prm["a1"], prm["a2"], h0)
    y, hfin = pl.pallas_call(
        _s5c_bwd_kernel,
        grid=(nblk,),
        in_specs=common + [sin_spec, wspec(prm["m"]), wspec(prm["wst"][1]), wspec(prm["wout"][0]),
                           wspec(prm["wout"][1]), aspec, aspec, st_spec,
                           pl.BlockSpec((1, d), lambda i: (0, 0))],
        out_specs=[x_spec, st_spec],
        out_shape=[jax.ShapeDtypeStruct((seq, bsz, d), F32), jax.ShapeDtypeStruct((bsz, ns), F32)],
        scratch_shapes=scratch + [pltpu.VMEM((nch * bsz, ns), F32)],
        compiler_params=_cparams("arbitrary"),
        name="s5_bwd_readout",
    )(x3, mod_b, norm_w4, sin_f, prm["m"], prm["wst"][1], prm["wout"][0], prm["wout"][1],
      prm["a1"], prm["a2"], h0, d_skip.reshape(1, d))
    return y.reshape(seq, bsz * d), hfin


def _s5c_glu(y2, x2, mod_a, layer, w_glu, b_glu, bsz):
    seq = x2.shape[0]
    d = x2.shape[1] // bsz
    tm = 512 if seq % 512 == 0 else 256
    row = pl.BlockSpec((tm, d), lambda t, b: (t, b))
    return pl.pallas_call(
        _s5c_glu_kernel,
        grid=(seq // tm, bsz),
        in_specs=[row, row,
                  pl.BlockSpec((None, None, None, 3, d), lambda t, b: (layer, b, 1, 0, 0)),
                  pl.BlockSpec(w_glu.shape, lambda t, b: (0, 0), pipeline_mode=pl.Buffered(1)),
                  pl.BlockSpec((1, 2 * d), lambda t, b: (0, 0))],
        out_specs=row,
        out_shape=jax.ShapeDtypeStruct((seq, bsz * d), F32),
        compiler_params=_cparams("parallel", "parallel"),
        name="s5_glu",
    )(y2, x2, mod_a, w_glu, b_glu.reshape(1, 2 * d))


def _s5c_mixer(xc2, x2, mods, norm_w4, layer, prm, extra, bsz, need_ctx):
    (modc_a, modc_b), (modx_a, modx_b) = mods
    d_skip, w_glu, b_glu = extra
    ns = prm["m"].shape[0] * LANES
    zero = jnp.zeros((bsz, ns), F32)
    sc_f, hf = _s5c_stream(xc2, modc_b, norm_w4, layer, prm, zero, bsz, rev=False)
    sx_f, _ = _s5c_stream(x2, modx_b, norm_w4, layer, prm, hf, bsz, rev=False)
    yc, hb = _s5c_stream(xc2, modc_b, norm_w4, layer, prm, zero, bsz, rev=True, sin_f=sc_f, d_skip=d_skip)
    yx, _ = _s5c_stream(x2, modx_b, norm_w4, layer, prm, hb, bsz, rev=True, sin_f=sx_f, d_skip=d_skip)
    x_new = _s5c_glu(yx, x2, modx_a, layer, w_glu, b_glu, bsz)
    xc_new = _s5c_glu(yc, xc2, modc_a, layer, w_glu, b_glu, bsz) if need_ctx else None
    return xc_new, x_new


def _level_matrix(rev):
    p = np.arange(CHUNK)
    v, r = p // SUBLANES, p % SUBLANES
    tau = GROUP * (v // SUBLANES) + SUBLANES * r + v % SUBLANES
    if rev:
        tau = CHUNK - 1 - tau
    ti, tj = tau[:, None], tau[None, :]
    x = ti ^ tj
    lvl = np.zeros((CHUNK, CHUNK), np.int32)
    for b in range(N_LEVELS):
        lvl = np.where((x >> b) & 1, b + 1, lvl)
    lvl = np.where(tj > ti, -1, lvl)
    return lvl.astype(np.int32)


def _scan_chunk_head(q, k, vb, g, st_t, masks, rev):
    dk = q.shape[1]
    ng, nv = N_GROUPS, SUBLANES
    ea = (lambda e: SUB - 1 - e) if rev else (lambda e: e)
    ksub = (lambda e: SUBLANES - 1 - e) if rev else (lambda e: e)

    def vregs(x, scale=None):
        out = [x[SUBLANES * ea(e):SUBLANES * ea(e) + SUBLANES, :] for e in range(SUB)]
        if scale is not None:
            out = [o * scale for o in out]
        return [out[nv * gi:nv * (gi + 1)] for gi in range(ng)]

    def assemble(vs):
        flat = [vs[gi][a] for gi in range(ng) for a in range(nv)]
        return jnp.concatenate([flat[ea(a)] for a in range(SUB)], axis=0)

    def zeros():
        return [[zero] * nv for _ in range(ng)]

    qv, kv, gv = vregs(q), vregs(k), vregs(g, LOG2E)
    zero = jnp.zeros((SUBLANES, dk), F32)
    ninf = jnp.full((SUBLANES, dk), -jnp.inf, F32)

    sub_i = lax.broadcasted_iota(jnp.int32, (SUBLANES, dk), 0)
    re = (SUBLANES - 1 - sub_i) if rev else sub_i

    def row(x, e):
        kk = ksub(e)
        return jnp.broadcast_to(x[kk:kk + 1, :], (SUBLANES, dk))

    pf, tot, ct, xcl, gtot = [], [], [], [], []
    for gi in range(ng):
        p = [gv[gi][0]]
        for a in range(1, nv):
            p.append(p[-1] + gv[gi][a])
        t = p[nv - 1]
        c = zero
        for e in range(SUBLANES):
            c = c + jnp.where(re >= e, row(t, e), 0.0)
        pf.append(p)
        tot.append(t)
        ct.append(c)
        xcl.append(c - t)
        gtot.append(row(c, SUBLANES - 1))

    q_lv, k_lv = [], []
    for lvl in range(1, N_FINE + 1):
        s = 1 << lvl
        qs, ks = zeros(), zeros()
        for gi in range(ng):
            for bs in range(0, nv, s):
                m = bs + s // 2
                ref = pf[gi][m - 1]
                for a in range(bs, m - 1):
                    ks[gi][a] = kv[gi][a] * jnp.exp2(ref - pf[gi][a])
                ks[gi][m - 1] = kv[gi][m - 1]
                for a in range(m, bs + s):
                    qs[gi][a] = qv[gi][a] * jnp.exp2(pf[gi][a] - ref)
        q_lv.append(assemble(qs))
        k_lv.append(assemble(ks))
    for lvl in range(N_FINE + 1, N_FINE + N_COARSE + 1):
        w = 1 << (lvl - N_FINE)
        right = (re & (w // 2)) != 0
        qs, ks = zeros(), zeros()
        for gi in range(ng):
            xm = zero
            for bs in range(0, SUBLANES, w):
                inblk = (re >= bs) & (re < bs + w)
                xm = xm + jnp.where(inblk, row(xcl[gi], bs + w // 2), 0.0)
            drq = jnp.where(right, xcl[gi] - xm, ninf)
            dlk = jnp.where(right, ninf, tot[gi] + xm - ct[gi])
            for a in range(nv):
                qs[gi][a] = qv[gi][a] * jnp.exp2(pf[gi][a] + drq)
                ks[gi][a] = kv[gi][a] * jnp.exp2(dlk - pf[gi][a])
        q_lv.append(assemble(qs))
        k_lv.append(assemble(ks))
    qs, ks = zeros(), zeros()
    for a in range(nv):
        ks[0][a] = kv[0][a] * jnp.exp2(tot[0] + (gtot[0] - ct[0]) - pf[0][a])
        qs[1][a] = qv[1][a] * jnp.exp2(pf[1][a] + xcl[1])
    q_lv.append(assemble(qs))
    k_lv.append(assemble(ks))

    sc = jnp.where(masks[0], jnp.sum(q * k, axis=-1, keepdims=True), 0.0)
    for lvl in range(1, N_LEVELS + 1):
        s_l = _dot_nt(q_lv[lvl - 1].astype(BF16), k_lv[lvl - 1].astype(BF16))
        sc = jnp.where(masks[lvl], s_l, sc)

    e0, e1 = jnp.exp2(gtot[0]), jnp.exp2(gtot[1])
    qc = zeros()
    kd = zeros()
    for a in range(nv):
        qc[0][a] = qv[0][a] * jnp.exp2(pf[0][a] + xcl[0])
        qc[1][a] = qs[1][a] * e0
        kd[0][a] = ks[0][a] * e1
        kd[1][a] = kv[1][a] * jnp.exp2(tot[1] + (gtot[1] - ct[1]) - pf[1][a])
    o = _dot(sc.astype(BF16), vb) + _dot_nt(assemble(qc).astype(BF16), st_t.astype(BF16))
    st_new = (e0 * e1)[0:1, :] * st_t + _dot_tn(vb, assemble(kd).astype(BF16))
    return o, st_new


def _scan_kernel(n_heads, dk, dv, qf_ref, kf_ref, vf_ref, gf_ref, qb_ref, kb_ref, vb_ref, gb_ref,
                 lv_ref, s0_ref, of_ref, ob_ref, sf_ref, st_ref):
    @pl.when(pl.program_id(1) == 0)
    def _():
        st_ref[...] = s0_ref[...]

    dirs = ((qf_ref, kf_ref, vf_ref, gf_ref, of_ref, False), (qb_ref, kb_ref, vb_ref, gb_ref, ob_ref, True))
    masks = [[lv_ref[d] == lvl for lvl in range(N_LEVELS + 1)] for d in range(2)]
    for h in range(n_heads):
        ks = slice(h * dk, (h + 1) * dk)
        vs = slice(h * dv, (h + 1) * dv)
        for d, (q_ref, k_ref, v_ref, g_ref, o_ref, rev) in enumerate(dirs):
            o, st_new = _scan_chunk_head(q_ref[:, ks], k_ref[:, ks], v_ref[:, vs], g_ref[:, ks],
                                         st_ref[d, vs, :], masks[d], rev)
            o_ref[:, vs] = o.astype(o_ref.dtype)
            st_ref[d, vs, :] = st_new
    sf_ref[...] = st_ref[...]


def _gated_scan(q, kf, kb, v, gf, gb, s0, n_heads, bsz):
    seq = q.shape[0]
    wk = q.shape[1] // bsz
    wv = v.shape[1] // bsz
    dk, dv = wk // n_heads, wv // n_heads
    nchunk = seq // CHUNK
    fwd = lambda b, n: (n, b)
    bwd = lambda b, n: (nchunk - 1 - n, b)
    kspec = lambda im: pl.BlockSpec((CHUNK, wk), im)
    vspec = lambda im: pl.BlockSpec((CHUNK, wv), im)
    sspec = pl.BlockSpec((2, None, wv, dk), lambda b, n: (0, b, 0, 0))
    lv = jnp.asarray(np.stack([_level_matrix(False), _level_matrix(True)]))
    return pl.pallas_call(
        functools.partial(_scan_kernel, n_heads, dk, dv),
        grid=(bsz, nchunk),
        in_specs=[kspec(fwd), kspec(fwd), vspec(fwd), kspec(fwd),
                  kspec(bwd), kspec(bwd), vspec(bwd), kspec(bwd),
                  pl.BlockSpec((2, CHUNK, CHUNK), lambda b, n: (0, 0, 0)), sspec],
        out_specs=[vspec(fwd), vspec(bwd), sspec],
        out_shape=[jax.ShapeDtypeStruct((seq, bsz * wv), BF16),
                   jax.ShapeDtypeStruct((seq, bsz * wv), BF16),
                   jax.ShapeDtypeStruct((2, bsz, wv, dk), F32)],
        scratch_shapes=[pltpu.VMEM((2, wv, dk), F32)],
        compiler_params=_cparams("parallel", "arbitrary"),
        name="gated_scan",
    )(q, kf, v, gf, q, kb, v, gb, lv, s0)


def _bidir_scan(ctx_in, lat_in, n_heads, bsz, need_ctx):
    qc, kfc, kbc, vc, gfc, gbc = ctx_in
    qx, kfx, kbx, vx, gfx, gbx = lat_in
    wk = qc.shape[1] // bsz
    wv = vc.shape[1] // bsz
    s0 = jnp.zeros((2, bsz, wv, wk // n_heads), F32)
    oc_f, oc_b, s_c = _gated_scan(qc, kfc, kbc, vc, gfc, gbc, s0, n_heads, bsz)
    ox_f, ox_b, _ = _gated_scan(qx, kfx, kbx, vx, gfx, gbx, s_c, n_heads, bsz)
    return ((oc_f, oc_b) if need_ctx else None), (ox_f, ox_b)


def _chunk_rows(c, colmajor):
    if colmajor:
        return lambda v: (SUBLANES * (v % SUBLANES) + N_GROUPS * c + v // SUBLANES, GROUP)
    return lambda v: (c * CHUNK + GROUP * (v // SUBLANES) + v % SUBLANES, SUBLANES)


def _permute_chunk_in(stage_ref, dst_ref, c, colmajor):
    rows = _chunk_rows(c, colmajor)
    for jl in range(stage_ref.shape[0]):
        for v in range(SUB):
            start, stride = rows(v)
            dst_ref[c * CHUNK + SUBLANES * v:c * CHUNK + SUBLANES * (v + 1), jl * LANES:(jl + 1) * LANES] = (
                stage_ref[jl, pl.ds(start, SUBLANES, stride=stride), :])


def _permute_chunk_out(val, stage_ref, c, colmajor):
    rows = _chunk_rows(c, colmajor)
    for jl in range(stage_ref.shape[0]):
        for v in range(SUB):
            start, stride = rows(v)
            stage_ref[jl, pl.ds(start, SUBLANES, stride=stride), :] = (
                val[SUBLANES * v:SUBLANES * (v + 1), jl * LANES:(jl + 1) * LANES])


def _stage_block(x_ref, stage_ref):
    n = stage_ref.shape[1]
    for jl in range(stage_ref.shape[0]):
        blk = x_ref[..., jl * LANES:(jl + 1) * LANES]
        stage_ref[jl] = blk.reshape(n, LANES)


def _unstage_block(stage_ref, o_ref):
    for jl in range(stage_ref.shape[0]):
        o_ref[..., jl * LANES:(jl + 1) * LANES] = stage_ref[jl].reshape(o_ref.shape[:-1] + (LANES,))


def _hgrn_epilogue(h, w_ref, lb, d):
    q = _silu(_dot(h, w_ref[:, :d]))
    v = _dot(h, w_ref[:, d:2 * d])
    zf = _dot(h, w_ref[:, 2 * d:3 * d])
    zb = _dot(h, w_ref[:, 3 * d:4 * d])
    og = _dot(h, w_ref[:, 4 * d:])
    lf, lbk = lb[0:1, :], lb[1:2, :]
    gf = jnp.log(lf + (1.0 - lf) * _sigmoid(zf))
    gb = jnp.log(lbk + (1.0 - lbk) * _sigmoid(zb))
    kf = (1.0 - lf) * _sigmoid(-zf)
    kb = (1.0 - lbk) * _sigmoid(-zb)
    return q, kf, kb, v, gf, gb, og


def _x_stream(x2, bsz, colmajor):
    seq, d = x2.shape[0], x2.shape[1] // bsz
    if colmajor:
        rows = seq // GRID_W
        assert rows == GROUP, "column-major chunks assume one grid column per 64-step group"
        return (x2.reshape(rows, GRID_W, bsz * d),
                pl.BlockSpec((rows, SUBLANES, d), lambda t, b: (0, t, b)), SUBLANES * rows)
    tm = 2 * CHUNK
    return x2, pl.BlockSpec((tm, d), lambda t, b: (t, b)), tm


def _load_chunks(x_ref, stage_ref, xs_ref, colmajor):
    _stage_block(x_ref, stage_ref)
    for c in range(xs_ref.shape[0] // CHUNK):
        _permute_chunk_in(stage_ref, xs_ref, c, colmajor)


def _store_chunks(xn, stage_ref, o_ref, colmajor):
    for c in range(xn.shape[0] // CHUNK):
        _permute_chunk_out(xn[c * CHUNK:(c + 1) * CHUNK, :], stage_ref, c, colmajor)
    _unstage_block(stage_ref, o_ref)


def _chunk_scratch(tm, d):
    return [pltpu.VMEM((d // LANES, tm, LANES), F32), pltpu.VMEM((tm, d), F32)]


def _hgrn_proj_kernel(x_ref, mod_ref, nw_ref, w_ref, lb_ref,
                      q_ref, kf_ref, kb_ref, v_ref, gf_ref, gb_ref, og_ref, stage_ref, xs_ref):
    d = xs_ref.shape[1]
    _load_chunks(x_ref, stage_ref, xs_ref, False)
    h = _norm_mod(xs_ref[...], nw_ref[...], mod_ref[0:1, :], mod_ref[1:2, :]).astype(BF16)
    outs = _hgrn_epilogue(h, w_ref, lb_ref[...], d)
    for ref, val in zip((q_ref, kf_ref, kb_ref, v_ref, gf_ref, gb_ref, og_ref), outs):
        ref[...] = val.astype(ref.dtype)


def _hgrn_project(x2, mod_a, norm_w4, layer, w_in, lb, bsz):
    seq = x2.shape[0]
    d = x2.shape[1] // bsz
    xv, xspec, tm = _x_stream(x2, bsz, False)
    row = pl.BlockSpec((tm, d), lambda t, b: (t, b))
    return pl.pallas_call(
        _hgrn_proj_kernel,
        grid=(seq // tm, bsz),
        in_specs=[
            xspec,
            pl.BlockSpec((None, None, None, 3, d), lambda t, b: (layer, b, 1, 0, 0)),
            pl.BlockSpec((None, None, 1, d), lambda t, b: (layer, 1, 0, 0)),
            pl.BlockSpec(w_in.shape, lambda t, b: (0, 0), pipeline_mode=pl.Buffered(1)),
            pl.BlockSpec(lb.shape, lambda t, b: (0, 0)),
        ],
        out_specs=[row] * 7,
        out_shape=[jax.ShapeDtypeStruct((seq, bsz * d), BF16 if i in (3, 6) else F32) for i in range(7)],
        scratch_shapes=_chunk_scratch(tm, d),
        compiler_params=_cparams("parallel", "parallel"),
        name="hgrn_project",
    )(xv, mod_a, norm_w4, w_in, lb)


def _gla_epilogue(h, w_ref, wgate, bgate, dkt, dvt):
    low = _dot(h, w_ref[:, 2 * dkt + 2 * dvt:])
    lg = _log_sigmoid(_dot(low.astype(BF16), wgate) + bgate) * (1.0 / GLA_TAU)
    q = _dot(h, w_ref[:, :dkt]) * ((dkt // GLA_HEADS) ** -0.5)
    k = _dot(h, w_ref[:, dkt:2 * dkt])
    v = _dot(h, w_ref[:, 2 * dkt:2 * dkt + dvt])
    og = _dot(h, w_ref[:, 2 * dkt + dvt:2 * dkt + 2 * dvt])
    return q, k, v, og, lg[:, :dkt], lg[:, dkt:]


def _gla_proj_kernel(colmajor, x_ref, mod_ref, nw_ref, w_ref, wg_ref, bg_ref,
                     q_ref, k_ref, v_ref, og_ref, gf_ref, gb_ref, stage_ref, xs_ref):
    dkt = q_ref.shape[1]
    dvt = v_ref.shape[1]
    _load_chunks(x_ref, stage_ref, xs_ref, colmajor)
    h = _norm_mod(xs_ref[...], nw_ref[...], mod_ref[0:1, :], mod_ref[1:2, :]).astype(BF16)
    outs = _gla_epilogue(h, w_ref, wg_ref[...], bg_ref[...], dkt, dvt)
    for ref, val in zip((q_ref, k_ref, v_ref, og_ref, gf_ref, gb_ref), outs):
        ref[...] = val.astype(ref.dtype)


def _gla_project(x2, mod_a, norm_w4, layer, w_in, wgate, bgate, bsz, dkt, dvt, *, colmajor):
    seq = x2.shape[0]
    d = x2.shape[1] // bsz
    widths = (dkt, dkt, dvt, dvt, dkt, dkt)
    xv, xspec, tm = _x_stream(x2, bsz, colmajor)
    const = lambda t, b: (0, 0)
    return pl.pallas_call(
        functools.partial(_gla_proj_kernel, colmajor),
        grid=(seq // tm, bsz),
        in_specs=[
            xspec,
            pl.BlockSpec((None, None, None, 3, d), lambda t, b: (layer, b, 1, 0, 0)),
            pl.BlockSpec((None, None, 1, d), lambda t, b: (layer, 1, 0, 0)),
            pl.BlockSpec(w_in.shape, const, pipeline_mode=pl.Buffered(1)),
            pl.BlockSpec(wgate.shape, const),
            pl.BlockSpec(bgate.shape, const),
        ],
        out_specs=[pl.BlockSpec((tm, w), lambda t, b: (t, b)) for w in widths],
        out_shape=[jax.ShapeDtypeStruct((seq, bsz * w), BF16 if i in (2, 3) else F32)
                   for i, w in enumerate(widths)],
        scratch_shapes=_chunk_scratch(tm, d),
        compiler_params=_cparams("parallel", "parallel"),
        name="gla_project",
    )(xv, mod_a, norm_w4, w_in, wgate, bgate)


def _head_out(of, ob, og, gnw, w_out, n_heads, zs_ref):
    dv = of.shape[1] // n_heads
    o = of.astype(F32) + ob.astype(F32)
    og = og.astype(F32)
    for h in range(n_heads):
        sl = slice(h * dv, (h + 1) * dv)
        zs_ref[:, sl] = (_rms(o[:, sl], gnw) * _silu(og[:, sl])).astype(BF16)
    return _dot(zs_ref[...], w_out)


def _mix_out_kernel(n_heads, colmajor, of_ref, ob_ref, og_ref, x_ref, mod_ref, gnw_ref, w_ref,
                    o_ref, stage_ref, xs_ref, zs_ref):
    _load_chunks(x_ref, stage_ref, xs_ref, colmajor)
    y = _head_out(of_ref[...], ob_ref[...], og_ref[...], gnw_ref[...], w_ref[...], n_heads, zs_ref)
    xn = xs_ref[...] + mod_ref[2:3, :] * y
    _store_chunks(xn, stage_ref, o_ref, colmajor)


def _mix_out(of, ob, og, x2, mod_a, layer, gnw, w_out, n_heads, bsz, *, colmajor):
    seq = x2.shape[0]
    d = x2.shape[1] // bsz
    wv = of.shape[1] // bsz
    gnw = gnw.reshape(1, -1)
    xv, xspec, tm = _x_stream(x2, bsz, colmajor)
    ospec = pl.BlockSpec((tm, wv), lambda t, b: (t, b))
    out = pl.pallas_call(
        functools.partial(_mix_out_kernel, n_heads, colmajor),
        grid=(seq // tm, bsz),
        in_specs=[ospec, ospec, ospec, xspec,
                  pl.BlockSpec((None, None, None, 3, d), lambda t, b: (layer, b, 1, 0, 0)),
                  pl.BlockSpec(gnw.shape, lambda t, b: (0, 0)),
                  pl.BlockSpec(w_out.shape, lambda t, b: (0, 0))],
        out_specs=xspec,
        out_shape=jax.ShapeDtypeStruct(xv.shape, F32),
        scratch_shapes=_chunk_scratch(tm, d) + [pltpu.VMEM((tm, wv), BF16)],
        compiler_params=_cparams("parallel", "parallel"),
        name="mix_out",
    )(of, ob, og, xv, mod_a, gnw, w_out)
    return out.reshape(seq, bsz * d)


def _gla_mixer(xc2, x2, mods, norm_w4, layer, prm, bsz, need_ctx):
    (modc_a, _), (modx_a, _) = mods
    w_in, wgate, bgate, gnw, w_out, dkt, dvt = prm
    qc, kc, vc, ogc, gfc, gbc = _gla_project(xc2, modc_a, norm_w4, layer, w_in, wgate, bgate, bsz,
                                             dkt, dvt, colmajor=False)
    qx, kx, vx, ogx, gfx, gbx = _gla_project(x2, modx_a, norm_w4, layer, w_in, wgate, bgate, bsz,
                                             dkt, dvt, colmajor=True)
    oc, ox = _bidir_scan((qc, kc, kc, vc, gfc, gbc), (qx, kx, kx, vx, gfx, gbx), GLA_HEADS, bsz, need_ctx)
    x_new = _mix_out(ox[0], ox[1], ogx, x2, modx_a, layer, gnw, w_out, GLA_HEADS, bsz, colmajor=True)
    xc_new = None
    if need_ctx:
        xc_new = _mix_out(oc[0], oc[1], ogc, xc2, modc_a, layer, gnw, w_out, GLA_HEADS, bsz, colmajor=False)
    return xc_new, x_new


def _hgrn_mixer(xc2, x2, mods, norm_w4, layer, prm, bsz, need_ctx):
    (modc_a, _), (modx_a, _) = mods
    w_in, lb, gnw, w_out, n_heads = prm
    qc, kfc, kbc, vc, gfc, gbc, ogc = _hgrn_project(xc2, modc_a, norm_w4, layer, w_in, lb, bsz)
    qx, kfx, kbx, vx, gfx, gbx, ogx = _hgrn_project(x2, modx_a, norm_w4, layer, w_in, lb, bsz)
    oc, ox = _bidir_scan((qc, kfc, kbc, vc, gfc, gbc), (qx, kfx, kbx, vx, gfx, gbx), n_heads, bsz, need_ctx)
    x_new = _mix_out(ox[0], ox[1], ogx, x2, modx_a, layer, gnw, w_out, n_heads, bsz, colmajor=False)
    xc_new = None
    if need_ctx:
        xc_new = _mix_out(oc[0], oc[1], ogc, xc2, modc_a, layer, gnw, w_out, n_heads, bsz, colmajor=False)
    return xc_new, x_new


def _forward(x, c, ctx, c_ctx, ada_w, ada_b, norm_w, ffn_w_in, ffn_w_out,
             s5_a_re, s5_a_im, s5_log_step, s5_b_re, s5_b_im, s5_c_re, s5_c_im, s5_d,
             s5_w_glu, s5_b_glu, gla_w_in, gla_w_gate, gla_b_gate, gla_norm_w, gla_w_out,
             hgrn_w_in, hgrn_lb_logits, hgrn_norm_w, hgrn_w_out, final_norm_w, depth=None):
    depth = ada_w.shape[0] if depth is None else depth
    bsz, seq, d = x.shape
    mods_x, mods_c = _ada(c, c_ctx, ada_w, ada_b)
    norm_w4 = norm_w.reshape(norm_w.shape[0], 3, 1, d)
    w_in = ffn_w_in.astype(BF16)
    w_out = ffn_w_out.astype(BF16)
    lb_soft = jax.nn.softmax(hgrn_lb_logits.astype(F32), axis=0)
    lb_all = jnp.cumsum(lb_soft, axis=0) - lb_soft[0]

    xc2 = jnp.transpose(ctx, (1, 0, 2)).reshape(ctx.shape[1], bsz * d)
    x2 = x
    for i in range(depth):
        last = i == depth - 1
        kind, j = i % N_MIXERS, i // N_MIXERS
        x2 = _ffn(x2, mods_x[0], norm_w4, w_in, w_out, i, 0, bsz, in_bld=(i == 0))
        xc2 = _ffn(xc2, mods_c[0], norm_w4, w_in, w_out, i, 0, bsz)
        mods = (mods_c, mods_x)
        if kind == 0:
            prm = _s5c_prep(s5_a_re[j], s5_a_im[j], s5_log_step[j], s5_b_re[j], s5_b_im[j],
                            s5_c_re[j], s5_c_im[j])
            extra = (s5_d[j], s5_w_glu[j].astype(BF16), s5_b_glu[j])
            yc, x2 = _s5c_mixer(xc2, x2, mods, norm_w4, i, prm, extra, bsz, not last)
        elif kind == 1:
            dkt = d // 2
            dvt = d
            n_low = 2 * GLA_RANK
            pad = LANES - n_low
            wi = jnp.pad(gla_w_in[j], ((0, 0), (0, pad))).astype(BF16)
            wg = jnp.zeros((LANES, 2 * dkt), F32)
            wg = wg.at[:GLA_RANK, :dkt].set(gla_w_gate[j, 0]).at[GLA_RANK:n_low, dkt:].set(gla_w_gate[j, 1])
            prm = (wi, wg.astype(BF16), gla_b_gate[j].reshape(1, 2 * dkt), gla_norm_w[j],
                   gla_w_out[j].astype(BF16), dkt, dvt)
            yc, x2 = _gla_mixer(xc2, x2, mods, norm_w4, i, prm, bsz, not last)
        else:
            prm = (hgrn_w_in[j].astype(BF16), lb_all[i], hgrn_norm_w[j], hgrn_w_out[j].astype(BF16),
                   d // HGRN_DK)
            yc, x2 = _hgrn_mixer(xc2, x2, mods, norm_w4, i, prm, bsz, not last)
        x2 = _ffn(x2, mods_x[0], norm_w4, w_in, w_out, i, 2, bsz,
                  final_w=final_norm_w.reshape(1, d) if last else None)
        if not last:
            xc2 = _ffn(yc, mods_c[0], norm_w4, w_in, w_out, i, 2, bsz)
    return x2


def kernel(x, c, ctx, c_ctx, ada_w, ada_b, norm_w, ffn_w_in, ffn_w_out, s5_a_re, s5_a_im, s5_log_step,
           s5_b_re, s5_b_im, s5_c_re, s5_c_im, s5_d, s5_w_glu, s5_b_glu, gla_w_in, gla_w_gate,
           gla_b_gate, gla_norm_w, gla_w_out, hgrn_w_in, hgrn_lb_logits, hgrn_norm_w, hgrn_w_out,
           final_norm_w):
    return _forward(x, c, ctx, c_ctx, ada_w, ada_b, norm_w, ffn_w_in, ffn_w_out, s5_a_re, s5_a_im,
                    s5_log_step, s5_b_re, s5_b_im, s5_c_re, s5_c_im, s5_d, s5_w_glu, s5_b_glu,
                    gla_w_in, gla_w_gate, gla_b_gate, gla_norm_w, gla_w_out, hgrn_w_in,
                    hgrn_lb_logits, hgrn_norm_w, hgrn_w_out, final_norm_w)
```

```python
import functools
import math

import numpy as np
import jax
import jax.numpy as jnp
from jax import lax
from jax.experimental import pallas as pl
from jax.experimental.pallas import tpu as pltpu

F32 = jnp.float32
BF16 = jnp.bfloat16

NORM_EPS = 1e-6
GRID_W = 64
N_MOD = 9
N_MIXERS = 3

S5_GROUP = 16
S5_STATE = 64
S5_LANE_GROUPS = 8

GLA_HEADS = 4
GLA_RANK = 16
GLA_TAU = 16.0
HGRN_DK = 128

SUBLANES = 8
LANES = 128
CHUNK = 128
GROUP = SUBLANES * SUBLANES
N_GROUPS = CHUNK // GROUP
SUB = CHUNK // SUBLANES
N_FINE = 3
N_COARSE = 3
N_LEVELS = 7
LOG2E = 1.4426950408889634

FFN_CHUNK = 768

VMEM_LIMIT_BYTES = 56 * 1024 * 1024


def _cparams(*sem):
    return pltpu.CompilerParams(dimension_semantics=sem, vmem_limit_bytes=VMEM_LIMIT_BYTES)


def _dot(a, b):
    return jnp.dot(a, b, preferred_element_type=F32)


def _dot_nt(a, b):
    return lax.dot_general(a, b, (((1,), (1,)), ((), ())), preferred_element_type=F32)


def _dot_tn(a, b):
    return lax.dot_general(a, b, (((0,), (0,)), ((), ())), preferred_element_type=F32)


def _sigmoid(x):
    return 0.5 * jnp.tanh(0.5 * x) + 0.5


def _sigmoid_pair(x):
    t = jnp.exp(-jnp.abs(x))
    r = 1.0 / (1.0 + t)
    tr = t * r
    pos = x >= 0.0
    return jnp.where(pos, r, tr), jnp.where(pos, tr, r)


def _silu(x):
    return x * _sigmoid(x)


def _gelu_tanh(x):
    c = math.sqrt(2.0 / math.pi)
    return 0.5 * x * (1.0 + jnp.tanh(c * (x + 0.044715 * (x * x * x))))


def _log_sigmoid(x):
    return jnp.minimum(x, 0.0) - jnp.log1p(jnp.exp(-jnp.abs(x)))


def _rms(x, w):
    ms = jnp.mean(x * x, axis=-1, keepdims=True)
    return (x * lax.rsqrt(ms + NORM_EPS)) * w


def _norm_mod(x, w, shift, scale):
    return _rms(x, w) * (1.0 + scale) + shift


def _ada_kernel(cc_ref, w_ref, b_ref, o_ref):
    a = _silu(cc_ref[...]).astype(BF16)
    o_ref[...] = _dot(a, w_ref[...].astype(BF16)) + b_ref[...]


def _ada(c, c_ctx, ada_w, ada_b):
    depth, d, nd = ada_w.shape
    bsz = c.shape[0]
    rows = ((bsz + 1 + SUBLANES - 1) // SUBLANES) * SUBLANES
    cc = jnp.zeros((rows, d), F32).at[:bsz].set(c).at[bsz].set(c_ctx)
    nblk = nd // d
    mod = pl.pallas_call(
        _ada_kernel,
        grid=(depth, nblk),
        in_specs=[
            pl.BlockSpec((rows, d), lambda i, n: (0, 0)),
            pl.BlockSpec((None, d, d), lambda i, n: (i, 0, n)),
            pl.BlockSpec((None, 1, d), lambda i, n: (i, 0, n)),
        ],
        out_specs=pl.BlockSpec((None, rows, d), lambda i, n: (i, 0, n)),
        out_shape=jax.ShapeDtypeStruct((depth, rows, nd), F32),
        compiler_params=_cparams("arbitrary", "arbitrary"),
        name="ada_mod",
    )(cc, ada_w, ada_b.reshape(depth, 1, nd))
    mx = mod[:, :bsz].reshape(depth, bsz, 3, 3, d)
    mc = jnp.broadcast_to(mod[:, bsz:bsz + 1], (depth, bsz, nd)).reshape(depth, bsz, 3, 3, d)
    return (mx, jnp.transpose(mx, (0, 2, 3, 1, 4))), (mc, jnp.transpose(mc, (0, 2, 3, 1, 4)))


def _ffn_kernel(chunks, final, x_ref, mod_ref, nw_ref, wi_ref, wo_ref, *rest):
    if final:
        fw_ref, o_ref = rest
    else:
        (o_ref,) = rest
    ff = wo_ref.shape[0]
    x = x_ref[...]
    h = _norm_mod(x, nw_ref[...], mod_ref[0:1, :], mod_ref[1:2, :]).astype(BF16)
    acc = None
    for lo, hi in chunks:
        g = _dot(h, wi_ref[:, lo:hi])
        u = _dot(h, wi_ref[:, ff + lo:ff + hi])
        part = _dot((_silu(g) * u).astype(BF16), wo_ref[lo:hi, :])
        acc = part if acc is None else acc + part
    y = x + (0.5 * mod_ref[2:3, :]) * acc
    if final:
        y = _rms(y, fw_ref[...])
    o_ref[...] = y


def _ff_chunks(ff, width):
    assert ff % LANES == 0 and width % LANES == 0
    return tuple((lo, min(lo + width, ff)) for lo in range(0, ff, width))


def _ffn(x, mod_a, norm_w4, w_in, w_out, layer, stage, bsz, *, in_bld=False, final_w=None):
    d = norm_w4.shape[-1]
    seq = x.shape[1] if in_bld else x.shape[0]
    ff = w_out.shape[2]
    tm = 512 if seq % 512 == 0 else 256
    s = 0 if stage == 0 else 1
    final = final_w is not None
    resident = pl.Buffered(1)
    if in_bld:
        x_spec = pl.BlockSpec((None, tm, d), lambda t, b: (b, t, 0))
    else:
        x_spec = pl.BlockSpec((tm, d), lambda t, b: (t, b))
    in_specs = [
        x_spec,
        pl.BlockSpec((None, None, None, 3, d), lambda t, b: (layer, b, stage, 0, 0)),
        pl.BlockSpec((None, None, 1, d), lambda t, b: (layer, stage, 0, 0)),
        pl.BlockSpec((None, None, d, 2 * ff), lambda t, b: (layer, s, 0, 0), pipeline_mode=resident),
        pl.BlockSpec((None, None, ff, d), lambda t, b: (layer, s, 0, 0), pipeline_mode=resident),
    ]
    args = [x, mod_a, norm_w4, w_in, w_out]
    if final:
        in_specs.append(pl.BlockSpec((1, d), lambda t, b: (0, 0)))
        args.append(final_w)
        out_spec = pl.BlockSpec((None, tm, d), lambda t, b: (b, t, 0))
        out_shape = jax.ShapeDtypeStruct((bsz, seq, d), F32)
    else:
        out_spec = pl.BlockSpec((tm, d), lambda t, b: (t, b))
        out_shape = jax.ShapeDtypeStruct((seq, bsz * d), F32)
    return pl.pallas_call(
        functools.partial(_ffn_kernel, _ff_chunks(ff, FFN_CHUNK), final),
        grid=(seq // tm, bsz),
        in_specs=in_specs,
        out_specs=out_spec,
        out_shape=out_shape,
        compiler_params=_cparams("parallel", "parallel"),
        name="ffn_half",
    )(*args)


S5_T = 16
S5C_TB = 128
S5C_TB_FWD = 256
SLOT = LANES // SUBLANES


def _cmul(ar, ai, br, bi):
    return ar * br - ai * bi, ar * bi + ai * br


def _s5c_prep(a_re, a_im, log_step, b_re, b_im, c_re, c_im):
    dt = jnp.exp(log_step.astype(F32))[..., None]
    a_re = a_re.astype(F32)
    a_im = a_im.astype(F32)
    mag = jnp.exp(a_re * dt)
    abar_re = mag * jnp.cos(a_im * dt)
    abar_im = mag * jnp.sin(a_im * dt)
    den = a_re * a_re + a_im * a_im
    zr = abar_re - 1.0
    zi = abar_im
    coef_re = ((zr * a_re + zi * a_im) / den)[..., None]
    coef_im = ((zi * a_re - zr * a_im) / den)[..., None]
    b_re = b_re.astype(F32)
    b_im = b_im.astype(F32)
    bb_re = coef_re * b_re - coef_im * b_im
    bb_im = coef_re * b_im + coef_im * b_re
    cc_re = c_re.astype(F32)
    cc_im = c_im.astype(F32)
    t = S5_T
    pr, pi = [jnp.ones_like(abar_re)], [jnp.zeros_like(abar_im)]
    for _ in range(t):
        nr, ni = _cmul(pr[-1], pi[-1], abar_re, abar_im)
        pr.append(nr)
        pi.append(ni)
    pr, pi = jnp.stack(pr), jnp.stack(pi)
    hi = lax.Precision.HIGHEST
    g, p, gc = abar_re.shape[1], abar_re.shape[2], b_re.shape[-1]
    prk = jnp.transpose(pr, (1, 2, 3, 0))
    pik = jnp.transpose(pi, (1, 2, 3, 0))
    e_re, e_im = _cmul(prk[..., None], pik[..., None], bb_re[:, :, :, None, :], bb_im[:, :, :, None, :])
    nkc = (t + 1) * gc
    taps = (jnp.einsum('dgop,dgpn->dgon', cc_re, e_re.reshape(2, g, p, nkc), precision=hi)
            - jnp.einsum('dgop,dgpn->dgon', cc_im, e_im.reshape(2, g, p, nkc), precision=hi))
    taps = taps.reshape(2, g, gc, t + 1, gc)
    kf, kb = taps[0], taps[1]
    seq = jnp.concatenate([jnp.flip(kb[:, :, 1:t], axis=2), kf[:, :, :1] + kb[:, :, :1], kf[:, :, 1:t]],
                          axis=2)
    tab = jnp.transpose(seq, (0, 3, 2, 1)).reshape(g, gc, (2 * t - 1) * gc)
    m = jnp.stack([tab[:, :, gc * (t - 1 - s):gc * (t - 1 - s) + t * gc] for s in range(t)], axis=1)
    m = m.reshape(g, t * gc, t * gc)

    def wst(d, flip):
        er, ei = e_re[d, :, :, :t], e_im[d, :, :, :t]
        if flip:
            er, ei = jnp.flip(er, 2), jnp.flip(ei, 2)
        w = jnp.concatenate([jnp.transpose(er, (0, 2, 3, 1)), jnp.transpose(ei, (0, 2, 3, 1))], axis=-1)
        return w.reshape(g, t * gc, 2 * p)

    wst_f = wst(0, True)
    wst_b = wst(1, False)

    def wout(d, flip):
        qr, qi = prk[d, :, :, 1:t + 1], pik[d, :, :, 1:t + 1]
        if flip:
            qr, qi = jnp.flip(qr, 2), jnp.flip(qi, 2)
        cr = jnp.transpose(cc_re[d], (0, 2, 1))
        ci = jnp.transpose(cc_im[d], (0, 2, 1))
        gr, gi = _cmul(cr[:, :, None, :], ci[:, :, None, :], qr[..., None], qi[..., None])
        return jnp.concatenate([gr, -gi], axis=1).reshape(g, 2 * p, t * gc)

    wout_f = wout(0, False)
    wout_b = wout(1, True)
    a1 = jnp.concatenate([pr[t], pr[t]], axis=-1)
    a2 = jnp.concatenate([-pi[t], pi[t]], axis=-1)
    bc = lambda a: jnp.broadcast_to(a[:, :, None, :], a.shape[:2] + (SUBLANES, a.shape[-1]))
    return dict(m=m.astype(BF16), wst=(wst_f.astype(BF16), wst_b.astype(BF16)),
                wout=(wout_f.astype(BF16), wout_b.astype(BF16)), a1=bc(a1), a2=bc(a2))


def _slot_transpose(x):
    m, n, bsz, _ = x.shape
    slot = lax.broadcasted_iota(jnp.int32, (bsz, LANES), 1) // SLOT

    def rot(v, shift):
        return pltpu.roll(v.reshape(-1, LANES), shift, 1).reshape(v.shape)

    d = n // 2
    while d >= 1:
        keep = (slot & d) == 0
        x = x.reshape(m, n // (2 * d), 2, d, bsz, LANES)
        lo, hi = x[:, :, 0], x[:, :, 1]
        new_lo = jnp.where(keep, lo, rot(hi, SLOT * d))
        new_hi = jnp.where(keep, rot(lo, LANES - SLOT * d), hi)
        x = jnp.stack([new_lo, new_hi], axis=2).reshape(m, n, bsz, LANES)
        d //= 2
    return x


def _s5c_gather(h_s, a_s, nch):
    nj = h_s.shape[1] // LANES
    nh = S5_T // SUBLANES
    for j in range(nj):
        x = h_s[:, j * LANES:(j + 1) * LANES].reshape(nch * nh, SUBLANES, SUBLANES, LANES)
        r = _slot_transpose(x).reshape(nch, nh, SUBLANES, SUBLANES, LANES)
        for gl in range(S5_LANE_GROUPS):
            for hh in range(nh):
                a_s[S5_LANE_GROUPS * j + gl, :, hh * LANES:(hh + 1) * LANES] = (
                    r[:, hh, gl].reshape(nch * SUBLANES, LANES))


def _s5c_state_scan(v, st_ref, sin_ref, a1, a2, g, nch, rev):
    gl = slice(g * LANES, (g + 1) * LANES)
    s = st_ref[:, gl]
    w = pltpu.roll(s, LANES // 2, 1)
    for n in (range(nch - 1, -1, -1) if rev else range(nch)):
        rows = slice(SUBLANES * n, SUBLANES * (n + 1))
        sin_ref[rows, gl] = s
        vn = v[rows, :]
        s, w = a1 * s + a2 * w + vn, a1 * w - a2 * s + pltpu.roll(vn, LANES // 2, 1)
    st_ref[:, gl] = s


def _s5c_norm(x_ref, mod_ref, nw_ref, h_s):
    x3 = x_ref[...]
    ms = jnp.mean(x3 * x3, axis=-1, keepdims=True)
    h3 = (x3 * lax.rsqrt(ms + NORM_EPS)) * nw_ref[...] * (1.0 + mod_ref[1]) + mod_ref[0]
    h_s[...] = h3.reshape(h_s.shape)


def _s5c_fwd_kernel(x_ref, mod_ref, nw_ref, wst_ref, a1_ref, a2_ref, h0_ref,
                    sin_ref, hfin_ref, st_ref, h_s, a_s):
    ng = wst_ref.shape[0]
    nch = a_s.shape[1] // SUBLANES

    @pl.when(pl.program_id(0) == 0)
    def _():
        st_ref[...] = h0_ref[...]

    _s5c_norm(x_ref, mod_ref, nw_ref, h_s)
    _s5c_gather(h_s, a_s, nch)
    for g in range(ng):
        v = _dot(a_s[g].astype(BF16), wst_ref[g])
        _s5c_state_scan(v, st_ref, sin_ref, a1_ref[g], a2_ref[g], g, nch, False)
    hfin_ref[...] = st_ref[...]


def _s5c_bwd_kernel(x_ref, mod_ref, nw_ref, sinf_ref, m_ref, wst_ref, woutf_ref, woutb_ref,
                    a1_ref, a2_ref, h0_ref, dsk_ref, y_ref, hfin_ref, st_ref, h_s, a_s, sinb_s):
    ng = m_ref.shape[0]
    nch = a_s.shape[1] // SUBLANES

    @pl.when(pl.program_id(0) == 0)
    def _():
        st_ref[...] = h0_ref[...]

    _s5c_norm(x_ref, mod_ref, nw_ref, h_s)
    _s5c_gather(h_s, a_s, nch)
    for j in range(ng // S5_LANE_GROUPS):
        ys = []
        for gl in range(S5_LANE_GROUPS):
            g = S5_LANE_GROUPS * j + gl
            lanes = slice(g * LANES, (g + 1) * LANES)
            a = a_s[g].astype(BF16)
            _s5c_state_scan(_dot(a, wst_ref[g]), st_ref, sinb_s, a1_ref[g], a2_ref[g], g, nch, True)
            ys.append(_dot(a, m_ref[g])
                      + _dot(sinf_ref[:, lanes].astype(BF16), woutf_ref[g])
                      + _dot(sinb_s[:, lanes].astype(BF16), woutb_ref[g]))
        cols = slice(j * LANES, (j + 1) * LANES)
        for hh in range(S5_T // SUBLANES):
            r = jnp.stack([y[:, hh * LANES:(hh + 1) * LANES].reshape(nch, SUBLANES, LANES) for y in ys],
                          axis=1)
            out = _slot_transpose(r)
            for n in range(nch):
                t0 = S5_T * n + SUBLANES * hh
                skip = dsk_ref[:, cols] * h_s[t0 * SUBLANES:(t0 + SUBLANES) * SUBLANES, cols]
                y_ref[t0:t0 + SUBLANES, :, cols] = out[n] + skip.reshape(SUBLANES, SUBLANES, LANES)
    hfin_ref[...] = st_ref[...]


def _s5c_glu_kernel(y_ref, x_ref, mod_ref, w_ref, b_ref, o_ref):
    d = x_ref.shape[1]
    z = _dot(_gelu_tanh(y_ref[...]).astype(BF16), w_ref[...]) + b_ref[...]
    o_ref[...] = x_ref[...] + mod_ref[2:3, :] * (z[:, :d] * _sigmoid(z[:, d:]))


def _s5c_stream(x2, mod_b, norm_w4, layer, prm, h0, bsz, *, rev, sin_f=None, d_skip=None):
    seq = x2.shape[0]
    d = x2.shape[1] // bsz
    tb = S5C_TB if rev else S5C_TB_FWD
    nblk = seq // tb
    nch = tb // S5_T
    dirn = 1 if rev else 0
    ng = prm["m"].shape[0]
    ns = ng * LANES
    x3 = x2.reshape(seq, bsz, d)
    blk = (lambda i: (nblk - 1 - i, 0, 0)) if rev else (lambda i: (i, 0, 0))
    blk2 = (lambda i: (nblk - 1 - i, 0)) if rev else (lambda i: (i, 0))
    x_spec = pl.BlockSpec((tb, bsz, d), blk)
    sin_spec = pl.BlockSpec((nch * bsz, ns), blk2)
    st_spec = pl.BlockSpec((bsz, ns), lambda i: (0, 0))
    res = pl.Buffered(1)
    wspec = lambda w: pl.BlockSpec(w.shape, lambda i: (0,) * w.ndim, pipeline_mode=res)
    aspec = pl.BlockSpec((None,) + prm["a1"].shape[1:], lambda i: (dirn, 0, 0, 0), pipeline_mode=res)
    common = [x_spec,
              pl.BlockSpec((None, None, 3, bsz, d), lambda i: (layer, 1, 0, 0, 0)),
              pl.BlockSpec((None, None, 1, d), lambda i: (layer, 1, 0, 0))]
    scratch = [pltpu.VMEM((bsz, ns), F32), pltpu.VMEM((tb * bsz, d), F32),
               pltpu.VMEM((ng, nch * bsz, S5_T * S5_GROUP), F32)]
    if not rev:
        return pl.pallas_call(
            _s5c_fwd_kernel,
            grid=(nblk,),
            in_specs=common + [wspec(prm["wst"][0]), aspec, aspec, st_spec],
            out_specs=[sin_spec, st_spec],
            out_shape=[jax.ShapeDtypeStruct((seq // S5_T * bsz, ns), F32),
                       jax.ShapeDtypeStruct((bsz, ns), F32)],
            scratch_shapes=scratch,
            compiler_params=_cparams("arbitrary"),
            name="s5_fwd_states",
        )(x3, mod_b, norm_w4, prm["wst"][0], prm["a1"], prm["a2"], h0)
    y, hfin = pl.pallas_call(
        _s5c_bwd_kernel,
        grid=(nblk,),
        in_specs=common + [sin_spec, wspec(prm["m"]), wspec(prm["wst"][1]), wspec(prm["wout"][0]),
                           wspec(prm["wout"][1]), aspec, aspec, st_spec,
                           pl.BlockSpec((1, d), lambda i: (0, 0))],
        out_specs=[x_spec, st_spec],
        out_shape=[jax.ShapeDtypeStruct((seq, bsz, d), F32), jax.ShapeDtypeStruct((bsz, ns), F32)],
        scratch_shapes=scratch + [pltpu.VMEM((nch * bsz, ns), F32)],
        compiler_params=_cparams("arbitrary"),
        name="s5_bwd_readout",
    )(x3, mod_b, norm_w4, sin_f, prm["m"], prm["wst"][1], prm["wout"][0], prm["wout"][1],
      prm["a1"], prm["a2"], h0, d_skip.reshape(1, d))
    return y.reshape(seq, bsz * d), hfin


def _s5c_glu(y2, x2, mod_a, layer, w_glu, b_glu, bsz):
    seq = x2.shape[0]
    d = x2.shape[1] // bsz
    tm = 512 if seq % 512 == 0 else 256
    row = pl.BlockSpec((tm, d), lambda t, b: (t, b))
    return pl.pallas_call(
        _s5c_glu_kernel,
        grid=(seq // tm, bsz),
        in_specs=[row, row,
                  pl.BlockSpec((None, None, None, 3, d), lambda t, b: (layer, b, 1, 0, 0)),
                  pl.BlockSpec(w_glu.shape, lambda t, b: (0, 0), pipeline_mode=pl.Buffered(1)),
                  pl.BlockSpec((1, 2 * d), lambda t, b: (0, 0))],
        out_specs=row,
        out_shape=jax.ShapeDtypeStruct((seq, bsz * d), F32),
        compiler_params=_cparams("parallel", "parallel"),
        name="s5_glu",
    )(y2, x2, mod_a, w_glu, b_glu.reshape(1, 2 * d))


def _s5c_mixer(xc2, x2, mods, norm_w4, layer, prm, extra, bsz, need_ctx):
    (modc_a, modc_b), (modx_a, modx_b) = mods
    d_skip, w_glu, b_glu = extra
    ns = prm["m"].shape[0] * LANES
    zero = jnp.zeros((bsz, ns), F32)
    sc_f, hf = _s5c_stream(xc2, modc_b, norm_w4, layer, prm, zero, bsz, rev=False)
    sx_f, _ = _s5c_stream(x2, modx_b, norm_w4, layer, prm, hf, bsz, rev=False)
    yc, hb = _s5c_stream(xc2, modc_b, norm_w4, layer, prm, zero, bsz, rev=True, sin_f=sc_f, d_skip=d_skip)
    yx, _ = _s5c_stream(x2, modx_b, norm_w4, layer, prm, hb, bsz, rev=True, sin_f=sx_f, d_skip=d_skip)
    x_new = _s5c_glu(yx, x2, modx_a, layer, w_glu, b_glu, bsz)
    xc_new = _s5c_glu(yc, xc2, modc_a, layer, w_glu, b_glu, bsz) if need_ctx else None
    return xc_new, x_new


def _level_matrix(rev):
    p = np.arange(CHUNK)
    v, r = p // SUBLANES, p % SUBLANES
    tau = GROUP * (v // SUBLANES) + SUBLANES * r + v % SUBLANES
    if rev:
        tau = CHUNK - 1 - tau
    ti, tj = tau[:, None], tau[None, :]
    x = ti ^ tj
    lvl = np.zeros((CHUNK, CHUNK), np.int32)
    for b in range(N_LEVELS):
        lvl = np.where((x >> b) & 1, b + 1, lvl)
    lvl = np.where(tj > ti, -1, lvl)
    return lvl.astype(np.int32)


def _scan_chunk_head(q, k, vb, g, st_t, masks, rev):
    dk = q.shape[1]
    ng, nv = N_GROUPS, SUBLANES
    ea = (lambda e: SUB - 1 - e) if rev else (lambda e: e)
    ksub = (lambda e: SUBLANES - 1 - e) if rev else (lambda e: e)

    def vregs(x, scale=None):
        out = [x[SUBLANES * ea(e):SUBLANES * ea(e) + SUBLANES, :] for e in range(SUB)]
        if scale is not None:
            out = [o * scale for o in out]
        return [out[nv * gi:nv * (gi + 1)] for gi in range(ng)]

    def assemble(vs):
        flat = [vs[gi][a] for gi in range(ng) for a in range(nv)]
        return jnp.concatenate([flat[ea(a)] for a in range(SUB)], axis=0)

    def zeros():
        return [[zero] * nv for _ in range(ng)]

    qv, kv, gv = vregs(q), vregs(k), vregs(g, LOG2E)
    zero = jnp.zeros((SUBLANES, dk), F32)
    ninf = jnp.full((SUBLANES, dk), -jnp.inf, F32)

    sub_i = lax.broadcasted_iota(jnp.int32, (SUBLANES, dk), 0)
    re = (SUBLANES - 1 - sub_i) if rev else sub_i

    def row(x, e):
        kk = ksub(e)
        return jnp.broadcast_to(x[kk:kk + 1, :], (SUBLANES, dk))

    pf, tot, ct, xcl, gtot = [], [], [], [], []
    for gi in range(ng):
        p = [gv[gi][0]]
        for a in range(1, nv):
            p.append(p[-1] + gv[gi][a])
        t = p[nv - 1]
        c = zero
        for e in range(SUBLANES):
            c = c + jnp.where(re >= e, row(t, e), 0.0)
        pf.append(p)
        tot.append(t)
        ct.append(c)
        xcl.append(c - t)
        gtot.append(row(c, SUBLANES - 1))

    q_lv, k_lv = [], []
    for lvl in range(1, N_FINE + 1):
        s = 1 << lvl
        qs, ks = zeros(), zeros()
        for gi in range(ng):
            for bs in range(0, nv, s):
                m = bs + s // 2
                ref = pf[gi][m - 1]
                for a in range(bs, m - 1):
                    ks[gi][a] = kv[gi][a] * jnp.exp2(ref - pf[gi][a])
                ks[gi][m - 1] = kv[gi][m - 1]
                for a in range(m, bs + s):
                    qs[gi][a] = qv[gi][a] * jnp.exp2(pf[gi][a] - ref)
        q_lv.append(assemble(qs))
        k_lv.append(assemble(ks))
    for lvl in range(N_FINE + 1, N_FINE + N_COARSE + 1):
        w = 1 << (lvl - N_FINE)
        right = (re & (w // 2)) != 0
        qs, ks = zeros(), zeros()
        for gi in range(ng):
            xm = zero
            for bs in range(0, SUBLANES, w):
                inblk = (re >= bs) & (re < bs + w)
                xm = xm + jnp.where(inblk, row(xcl[gi], bs + w // 2), 0.0)
            drq = jnp.where(right, xcl[gi] - xm, ninf)
            dlk = jnp.where(right, ninf, tot[gi] + xm - ct[gi])
            for a in range(nv):
                qs[gi][a] = qv[gi][a] * jnp.exp2(pf[gi][a] + drq)
                ks[gi][a] = kv[gi][a] * jnp.exp2(dlk - pf[gi][a])
        q_lv.append(assemble(qs))
        k_lv.append(assemble(ks))
    qs, ks = zeros(), zeros()
    for a in range(nv):
        ks[0][a] = kv[0][a] * jnp.exp2(tot[0] + (gtot[0] - ct[0]) - pf[0][a])
        qs[1][a] = qv[1][a] * jnp.exp2(pf[1][a] + xcl[1])
    q_lv.append(assemble(qs))
    k_lv.append(assemble(ks))

    sc = jnp.where(masks[0], jnp.sum(q * k, axis=-1, keepdims=True), 0.0)
    for lvl in range(1, N_LEVELS + 1):
        s_l = _dot_nt(q_lv[lvl - 1].astype(BF16), k_lv[lvl - 1].astype(BF16))
        sc = jnp.where(masks[lvl], s_l, sc)

    e0, e1 = jnp.exp2(gtot[0]), jnp.exp2(gtot[1])
    qc = zeros()
    kd = zeros()
    for a in range(nv):
        qc[0][a] = qv[0][a] * jnp.exp2(pf[0][a] + xcl[0])
        qc[1][a] = qs[1][a] * e0
        kd[0][a] = ks[0][a] * e1
        kd[1][a] = kv[1][a] * jnp.exp2(tot[1] + (gtot[1] - ct[1]) - pf[1][a])
    o = _dot(sc.astype(BF16), vb) + _dot_nt(assemble(qc).astype(BF16), st_t.astype(BF16))
    st_new = (e0 * e1)[0:1, :] * st_t + _dot_tn(vb, assemble(kd).astype(BF16))
    return o, st_new


def _scan_kernel(n_heads, dk, dv, qf_ref, kf_ref, vf_ref, gf_ref, qb_ref, kb_ref, vb_ref, gb_ref,
                 lv_ref, s0_ref, of_ref, ob_ref, sf_ref, st_ref):
    @pl.when(pl.program_id(1) == 0)
    def _():
        st_ref[...] = s0_ref[...]

    dirs = ((qf_ref, kf_ref, vf_ref, gf_ref, of_ref, False), (qb_ref, kb_ref, vb_ref, gb_ref, ob_ref, True))
    masks = [[lv_ref[d] == lvl for lvl in range(N_LEVELS + 1)] for d in range(2)]
    for h in range(n_heads):
        ks = slice(h * dk, (h + 1) * dk)
        vs = slice(h * dv, (h + 1) * dv)
        for d, (q_ref, k_ref, v_ref, g_ref, o_ref, rev) in enumerate(dirs):
            o, st_new = _scan_chunk_head(q_ref[:, ks], k_ref[:, ks], v_ref[:, vs], g_ref[:, ks],
                                         st_ref[d, vs, :], masks[d], rev)
            o_ref[:, vs] = o.astype(o_ref.dtype)
            st_ref[d, vs, :] = st_new
    sf_ref[...] = st_ref[...]


def _gated_scan(q, kf, kb, v, gf, gb, s0, n_heads, bsz):
    seq = q.shape[0]
    wk = q.shape[1] // bsz
    wv = v.shape[1] // bsz
    dk, dv = wk // n_heads, wv // n_heads
    nchunk = seq // CHUNK
    fwd = lambda b, n: (n, b)
    bwd = lambda b, n: (nchunk - 1 - n, b)
    kspec = lambda im: pl.BlockSpec((CHUNK, wk), im)
    vspec = lambda im: pl.BlockSpec((CHUNK, wv), im)
    sspec = pl.BlockSpec((2, None, wv, dk), lambda b, n: (0, b, 0, 0))
    lv = jnp.asarray(np.stack([_level_matrix(False), _level_matrix(True)]))
    return pl.pallas_call(
        functools.partial(_scan_kernel, n_heads, dk, dv),
        grid=(bsz, nchunk),
        in_specs=[kspec(fwd), kspec(fwd), vspec(fwd), kspec(fwd),
                  kspec(bwd), kspec(bwd), vspec(bwd), kspec(bwd),
                  pl.BlockSpec((2, CHUNK, CHUNK), lambda b, n: (0, 0, 0)), sspec],
        out_specs=[vspec(fwd), vspec(bwd), sspec],
        out_shape=[jax.ShapeDtypeStruct((seq, bsz * wv), F32),
                   jax.ShapeDtypeStruct((seq, bsz * wv), F32),
                   jax.ShapeDtypeStruct((2, bsz, wv, dk), F32)],
        scratch_shapes=[pltpu.VMEM((2, wv, dk), F32)],
        compiler_params=_cparams("parallel", "arbitrary"),
        name="gated_scan",
    )(q, kf, v, gf, q, kb, v, gb, lv, s0)


def _bidir_scan(ctx_in, lat_in, n_heads, bsz, need_ctx):
    qc, kfc, kbc, vc, gfc, gbc = ctx_in
    qx, kfx, kbx, vx, gfx, gbx = lat_in
    wk = qc.shape[1] // bsz
    wv = vc.shape[1] // bsz
    s0 = jnp.zeros((2, bsz, wv, wk // n_heads), F32)
    oc_f, oc_b, s_c = _gated_scan(qc, kfc, kbc, vc, gfc, gbc, s0, n_heads, bsz)
    ox_f, ox_b, _ = _gated_scan(qx, kfx, kbx, vx, gfx, gbx, s_c, n_heads, bsz)
    return ((oc_f, oc_b) if need_ctx else None), (ox_f, ox_b)


def _chunk_rows(c, colmajor):
    if colmajor:
        return lambda v: (SUBLANES * (v % SUBLANES) + N_GROUPS * c + v // SUBLANES, GROUP)
    return lambda v: (c * CHUNK + GROUP * (v // SUBLANES) + v % SUBLANES, SUBLANES)


def _permute_chunk_in(stage_ref, dst_ref, c, colmajor):
    rows = _chunk_rows(c, colmajor)
    for jl in range(stage_ref.shape[0]):
        for v in range(SUB):
            start, stride = rows(v)
            dst_ref[c * CHUNK + SUBLANES * v:c * CHUNK + SUBLANES * (v + 1), jl * LANES:(jl + 1) * LANES] = (
                stage_ref[jl, pl.ds(start, SUBLANES, stride=stride), :])


def _permute_chunk_out(val, stage_ref, c, colmajor):
    rows = _chunk_rows(c, colmajor)
    for jl in range(stage_ref.shape[0]):
        for v in range(SUB):
            start, stride = rows(v)
            stage_ref[jl, pl.ds(start, SUBLANES, stride=stride), :] = (
                val[SUBLANES * v:SUBLANES * (v + 1), jl * LANES:(jl + 1) * LANES])


def _stage_block(x_ref, stage_ref):
    n = stage_ref.shape[1]
    for jl in range(stage_ref.shape[0]):
        blk = x_ref[..., jl * LANES:(jl + 1) * LANES]
        stage_ref[jl] = blk.reshape(n, LANES)


def _unstage_block(stage_ref, o_ref):
    for jl in range(stage_ref.shape[0]):
        o_ref[..., jl * LANES:(jl + 1) * LANES] = stage_ref[jl].reshape(o_ref.shape[:-1] + (LANES,))


def _hgrn_epilogue(h, w_ref, lb, d):
    q = _silu(_dot(h, w_ref[:, :d]))
    v = _dot(h, w_ref[:, d:2 * d])
    zf = _dot(h, w_ref[:, 2 * d:3 * d])
    zb = _dot(h, w_ref[:, 3 * d:4 * d])
    og = _dot(h, w_ref[:, 4 * d:])
    lf, lbk = lb[0:1, :], lb[1:2, :]
    sf, snf = _sigmoid_pair(zf)
    sb, snb = _sigmoid_pair(zb)
    gf = jnp.log(lf + (1.0 - lf) * sf)
    gb = jnp.log(lbk + (1.0 - lbk) * sb)
    kf = (1.0 - lf) * snf
    kb = (1.0 - lbk) * snb
    return q, kf, kb, v, gf, gb, og


def _x_stream(x2, bsz, colmajor):
    seq, d = x2.shape[0], x2.shape[1] // bsz
    if colmajor:
        rows = seq // GRID_W
        assert rows == GROUP, "column-major chunks assume one grid column per 64-step group"
        return (x2.reshape(rows, GRID_W, bsz * d),
                pl.BlockSpec((rows, SUBLANES, d), lambda t, b: (0, t, b)), SUBLANES * rows)
    tm = 2 * CHUNK
    return x2, pl.BlockSpec((tm, d), lambda t, b: (t, b)), tm


def _load_chunks(x_ref, stage_ref, xs_ref, colmajor):
    _stage_block(x_ref, stage_ref)
    for c in range(xs_ref.shape[0] // CHUNK):
        _permute_chunk_in(stage_ref, xs_ref, c, colmajor)


def _store_chunks(xn, stage_ref, o_ref, colmajor):
    for c in range(xn.shape[0] // CHUNK):
        _permute_chunk_out(xn[c * CHUNK:(c + 1) * CHUNK, :], stage_ref, c, colmajor)
    _unstage_block(stage_ref, o_ref)


def _chunk_scratch(tm, d):
    return [pltpu.VMEM((d // LANES, tm, LANES), F32), pltpu.VMEM((tm, d), F32)]


def _hgrn_proj_kernel(x_ref, mod_ref, nw_ref, w_ref, lb_ref,
                      q_ref, kf_ref, kb_ref, v_ref, gf_ref, gb_ref, og_ref, stage_ref, xs_ref):
    d = xs_ref.shape[1]
    _load_chunks(x_ref, stage_ref, xs_ref, False)
    h = _norm_mod(xs_ref[...], nw_ref[...], mod_ref[0:1, :], mod_ref[1:2, :]).astype(BF16)
    outs = _hgrn_epilogue(h, w_ref, lb_ref[...], d)
    for ref, val in zip((q_ref, kf_ref, kb_ref, v_ref, gf_ref, gb_ref, og_ref), outs):
        ref[...] = val.astype(ref.dtype)


def _hgrn_project(x2, mod_a, norm_w4, layer, w_in, lb, bsz):
    seq = x2.shape[0]
    d = x2.shape[1] // bsz
    xv, xspec, tm = _x_stream(x2, bsz, False)
    row = pl.BlockSpec((tm, d), lambda t, b: (t, b))
    return pl.pallas_call(
        _hgrn_proj_kernel,
        grid=(seq // tm, bsz),
        in_specs=[
            xspec,
            pl.BlockSpec((None, None, None, 3, d), lambda t, b: (layer, b, 1, 0, 0)),
            pl.BlockSpec((None, None, 1, d), lambda t, b: (layer, 1, 0, 0)),
            pl.BlockSpec(w_in.shape, lambda t, b: (0, 0), pipeline_mode=pl.Buffered(1)),
            pl.BlockSpec(lb.shape, lambda t, b: (0, 0)),
        ],
        out_specs=[row] * 7,
        out_shape=[jax.ShapeDtypeStruct((seq, bsz * d), BF16 if i == 3 else F32) for i in range(7)],
        scratch_shapes=_chunk_scratch(tm, d),
        compiler_params=_cparams("parallel", "parallel"),
        name="hgrn_project",
    )(xv, mod_a, norm_w4, w_in, lb)


def _gla_epilogue(h, w_ref, wgate, bgate, dkt, dvt):
    low = _dot(h, w_ref[:, 2 * dkt + 2 * dvt:])
    lg = _log_sigmoid(_dot(low.astype(BF16), wgate) + bgate) * (1.0 / GLA_TAU)
    q = _dot(h, w_ref[:, :dkt]) * ((dkt // GLA_HEADS) ** -0.5)
    k = _dot(h, w_ref[:, dkt:2 * dkt])
    v = _dot(h, w_ref[:, 2 * dkt:2 * dkt + dvt])
    og = _dot(h, w_ref[:, 2 * dkt + dvt:2 * dkt + 2 * dvt])
    return q, k, v, og, lg[:, :dkt], lg[:, dkt:]


def _gla_proj_kernel(colmajor, x_ref, mod_ref, nw_ref, w_ref, wg_ref, bg_ref,
                     q_ref, k_ref, v_ref, og_ref, gf_ref, gb_ref, stage_ref, xs_ref):
    dkt = q_ref.shape[1]
    dvt = v_ref.shape[1]
    _load_chunks(x_ref, stage_ref, xs_ref, colmajor)
    h = _norm_mod(xs_ref[...], nw_ref[...], mod_ref[0:1, :], mod_ref[1:2, :]).astype(BF16)
    outs = _gla_epilogue(h, w_ref, wg_ref[...], bg_ref[...], dkt, dvt)
    for ref, val in zip((q_ref, k_ref, v_ref, og_ref, gf_ref, gb_ref), outs):
        ref[...] = val.astype(ref.dtype)


def _gla_project(x2, mod_a, norm_w4, layer, w_in, wgate, bgate, bsz, dkt, dvt, *, colmajor):
    seq = x2.shape[0]
    d = x2.shape[1] // bsz
    widths = (dkt, dkt, dvt, dvt, dkt, dkt)
    xv, xspec, tm = _x_stream(x2, bsz, colmajor)
    const = lambda t, b: (0, 0)
    return pl.pallas_call(
        functools.partial(_gla_proj_kernel, colmajor),
        grid=(seq // tm, bsz),
        in_specs=[
            xspec,
            pl.BlockSpec((None, None, None, 3, d), lambda t, b: (layer, b, 1, 0, 0)),
            pl.BlockSpec((None, None, 1, d), lambda t, b: (layer, 1, 0, 0)),
            pl.BlockSpec(w_in.shape, const, pipeline_mode=pl.Buffered(1)),
            pl.BlockSpec(wgate.shape, const),
            pl.BlockSpec(bgate.shape, const),
        ],
        out_specs=[pl.BlockSpec((tm, w), lambda t, b: (t, b)) for w in widths],
        out_shape=[jax.ShapeDtypeStruct((seq, bsz * w), BF16 if i == 2 else F32)
                   for i, w in enumerate(widths)],
        scratch_shapes=_chunk_scratch(tm, d),
        compiler_params=_cparams("parallel", "parallel"),
        name="gla_project",
    )(xv, mod_a, norm_w4, w_in, wgate, bgate)


def _head_out(of, ob, og, gnw, w_out, n_heads, zs_ref):
    dv = of.shape[1] // n_heads
    o = of + ob
    for h in range(n_heads):
        sl = slice(h * dv, (h + 1) * dv)
        zs_ref[:, sl] = (_rms(o[:, sl], gnw) * _silu(og[:, sl])).astype(BF16)
    return _dot(zs_ref[...], w_out)


def _mix_out_kernel(n_heads, colmajor, of_ref, ob_ref, og_ref, x_ref, mod_ref, gnw_ref, w_ref,
                    o_ref, stage_ref, xs_ref, zs_ref):
    _load_chunks(x_ref, stage_ref, xs_ref, colmajor)
    y = _head_out(of_ref[...], ob_ref[...], og_ref[...], gnw_ref[...], w_ref[...], n_heads, zs_ref)
    xn = xs_ref[...] + mod_ref[2:3, :] * y
    _store_chunks(xn, stage_ref, o_ref, colmajor)


def _mix_out(of, ob, og, x2, mod_a, layer, gnw, w_out, n_heads, bsz, *, colmajor):
    seq = x2.shape[0]
    d = x2.shape[1] // bsz
    wv = of.shape[1] // bsz
    gnw = gnw.reshape(1, -1)
    xv, xspec, tm = _x_stream(x2, bsz, colmajor)
    ospec = pl.BlockSpec((tm, wv), lambda t, b: (t, b))
    out = pl.pallas_call(
        functools.partial(_mix_out_kernel, n_heads, colmajor),
        grid=(seq // tm, bsz),
        in_specs=[ospec, ospec, ospec, xspec,
                  pl.BlockSpec((None, None, None, 3, d), lambda t, b: (layer, b, 1, 0, 0)),
                  pl.BlockSpec(gnw.shape, lambda t, b: (0, 0)),
                  pl.BlockSpec(w_out.shape, lambda t, b: (0, 0))],
        out_specs=xspec,
        out_shape=jax.ShapeDtypeStruct(xv.shape, F32),
        scratch_shapes=_chunk_scratch(tm, d) + [pltpu.VMEM((tm, wv), BF16)],
        compiler_params=_cparams("parallel", "parallel"),
        name="mix_out",
    )(of, ob, og, xv, mod_a, gnw, w_out)
    return out.reshape(seq, bsz * d)


def _gla_mixer(xc2, x2, mods, norm_w4, layer, prm, bsz, need_ctx):
    (modc_a, _), (modx_a, _) = mods
    w_in, wgate, bgate, gnw, w_out, dkt, dvt = prm
    qc, kc, vc, ogc, gfc, gbc = _gla_project(xc2, modc_a, norm_w4, layer, w_in, wgate, bgate, bsz,
                                             dkt, dvt, colmajor=False)
    qx, kx, vx, ogx, gfx, gbx = _gla_project(x2, modx_a, norm_w4, layer, w_in, wgate, bgate, bsz,
                                             dkt, dvt, colmajor=True)
    oc, ox = _bidir_scan((qc, kc, kc, vc, gfc, gbc), (qx, kx, kx, vx, gfx, gbx), GLA_HEADS, bsz, need_ctx)
    x_new = _mix_out(ox[0], ox[1], ogx, x2, modx_a, layer, gnw, w_out, GLA_HEADS, bsz, colmajor=True)
    xc_new = None
    if need_ctx:
        xc_new = _mix_out(oc[0], oc[1], ogc, xc2, modc_a, layer, gnw, w_out, GLA_HEADS, bsz, colmajor=False)
    return xc_new, x_new


def _hgrn_mixer(xc2, x2, mods, norm_w4, layer, prm, bsz, need_ctx):
    (modc_a, _), (modx_a, _) = mods
    w_in, lb, gnw, w_out, n_heads = prm
    qc, kfc, kbc, vc, gfc, gbc, ogc = _hgrn_project(xc2, modc_a, norm_w4, layer, w_in, lb, bsz)
    qx, kfx, kbx, vx, gfx, gbx, ogx = _hgrn_project(x2, modx_a, norm_w4, layer, w_in, lb, bsz)
    oc, ox = _bidir_scan((qc, kfc, kbc, vc, gfc, gbc), (qx, kfx, kbx, vx, gfx, gbx), n_heads, bsz, need_ctx)
    x_new = _mix_out(ox[0], ox[1], ogx, x2, modx_a, layer, gnw, w_out, n_heads, bsz, colmajor=False)
    xc_new = None
    if need_ctx:
        xc_new = _mix_out(oc[0], oc[1], ogc, xc2, modc_a, layer, gnw, w_out, n_heads, bsz, colmajor=False)
    return xc_new, x_new


def _forward(x, c, ctx, c_ctx, ada_w, ada_b, norm_w, ffn_w_in, ffn_w_out,
             s5_a_re, s5_a_im, s5_log_step, s5_b_re, s5_b_im, s5_c_re, s5_c_im, s5_d,
             s5_w_glu, s5_b_glu, gla_w_in, gla_w_gate, gla_b_gate, gla_norm_w, gla_w_out,
             hgrn_w_in, hgrn_lb_logits, hgrn_norm_w, hgrn_w_out, final_norm_w, depth=None):
    depth = ada_w.shape[0] if depth is None else depth
    bsz, seq, d = x.shape
    mods_x, mods_c = _ada(c, c_ctx, ada_w, ada_b)
    norm_w4 = norm_w.reshape(norm_w.shape[0], 3, 1, d)
    w_in = ffn_w_in.astype(BF16)
    w_out = ffn_w_out.astype(BF16)
    lb_soft = jax.nn.softmax(hgrn_lb_logits.astype(F32), axis=0)
    lb_all = jnp.cumsum(lb_soft, axis=0) - lb_soft[0]

    xc2 = jnp.transpose(ctx, (1, 0, 2)).reshape(ctx.shape[1], bsz * d)
    x2 = x
    for i in range(depth):
        last = i == depth - 1
        kind, j = i % N_MIXERS, i // N_MIXERS
        x2 = _ffn(x2, mods_x[0], norm_w4, w_in, w_out, i, 0, bsz, in_bld=(i == 0))
        xc2 = _ffn(xc2, mods_c[0], norm_w4, w_in, w_out, i, 0, bsz)
        mods = (mods_c, mods_x)
        if kind == 0:
            prm = _s5c_prep(s5_a_re[j], s5_a_im[j], s5_log_step[j], s5_b_re[j], s5_b_im[j],
                            s5_c_re[j], s5_c_im[j])
            extra = (s5_d[j], s5_w_glu[j].astype(BF16), s5_b_glu[j])
            yc, x2 = _s5c_mixer(xc2, x2, mods, norm_w4, i, prm, extra, bsz, not last)
        elif kind == 1:
            dkt = d // 2
            dvt = d
            n_low = 2 * GLA_RANK
            pad = LANES - n_low
            wi = jnp.pad(gla_w_in[j], ((0, 0), (0, pad))).astype(BF16)
            wg = jnp.zeros((LANES, 2 * dkt), F32)
            wg = wg.at[:GLA_RANK, :dkt].set(gla_w_gate[j, 0]).at[GLA_RANK:n_low, dkt:].set(gla_w_gate[j, 1])
            prm = (wi, wg.astype(BF16), gla_b_gate[j].reshape(1, 2 * dkt), gla_norm_w[j],
                   gla_w_out[j].astype(BF16), dkt, dvt)
            yc, x2 = _gla_mixer(xc2, x2, mods, norm_w4, i, prm, bsz, not last)
        else:
            prm = (hgrn_w_in[j].astype(BF16), lb_all[i], hgrn_norm_w[j], hgrn_w_out[j].astype(BF16),
                   d // HGRN_DK)
            yc, x2 = _hgrn_mixer(xc2, x2, mods, norm_w4, i, prm, bsz, not last)
        x2 = _ffn(x2, mods_x[0], norm_w4, w_in, w_out, i, 2, bsz,
                  final_w=final_norm_w.reshape(1, d) if last else None)
        if not last:
            xc2 = _ffn(yc, mods_c[0], norm_w4, w_in, w_out, i, 2, bsz)
    return x2


def kernel(x, c, ctx, c_ctx, ada_w, ada_b, norm_w, ffn_w_in, ffn_w_out, s5_a_re, s5_a_im, s5_log_step,
           s5_b_re, s5_b_im, s5_c_re, s5_c_im, s5_d, s5_w_glu, s5_b_glu, gla_w_in, gla_w_gate,
           gla_b_gate, gla_norm_w, gla_w_out, hgrn_w_in, hgrn_lb_logits, hgrn_norm_w, hgrn_w_out,
           final_norm_w):
    return _forward(x, c, ctx, c_ctx, ada_w, ada_b, norm_w, ffn_w_in, ffn_w_out, s5_a_re, s5_a_im,
                    s5_log_step, s5_b_re, s5_b_im, s5_c_re, s5_c_im, s5_d, s5_w_glu, s5_b_glu,
                    gla_w_in, gla_w_gate, gla_b_gate, gla_norm_w, gla_w_out, hgrn_w_in,
                    hgrn_lb_logits, hgrn_norm_w, hgrn_w_out, final_norm_w)
```

```python
import functools
import math

import numpy as np
import jax
import jax.numpy as jnp
from jax import lax
from jax.experimental import pallas as pl
from jax.experimental.pallas import tpu as pltpu

F32 = jnp.float32
BF16 = jnp.bfloat16

NORM_EPS = 1e-6
GRID_W = 64
N_MOD = 9
N_MIXERS = 3

S5_GROUP = 16
S5_STATE = 64
S5_LANE_GROUPS = 8

GLA_HEADS = 4
GLA_RANK = 16
GLA_TAU = 16.0
HGRN_DK = 128

SUBLANES = 8
LANES = 128
CHUNK = 128
GROUP = SUBLANES * SUBLANES
N_GROUPS = CHUNK // GROUP
SUB = CHUNK // SUBLANES
N_FINE = 3
N_COARSE = 3
N_LEVELS = 7
LOG2E = 1.4426950408889634
SCAN_CHUNKS_PER_STEP = 2

FFN_CHUNK = 768

VMEM_LIMIT_BYTES = 56 * 1024 * 1024


def _cparams(*sem):
    return pltpu.CompilerParams(dimension_semantics=sem, vmem_limit_bytes=VMEM_LIMIT_BYTES)


def _dot(a, b):
    return jnp.dot(a, b, preferred_element_type=F32)


def _dot_nt(a, b):
    return lax.dot_general(a, b, (((1,), (1,)), ((), ())), preferred_element_type=F32)


def _dot_tn(a, b):
    return lax.dot_general(a, b, (((0,), (0,)), ((), ())), preferred_element_type=F32)


def _sigmoid(x):
    return 0.5 * jnp.tanh(0.5 * x) + 0.5


def _sigmoid_pair(x):
    t = jnp.exp(-jnp.abs(x))
    r = 1.0 / (1.0 + t)
    tr = t * r
    pos = x >= 0.0
    return jnp.where(pos, r, tr), jnp.where(pos, tr, r)


def _silu(x):
    return x * _sigmoid(x)


def _gelu_tanh(x):
    c = math.sqrt(2.0 / math.pi)
    return 0.5 * x * (1.0 + jnp.tanh(c * (x + 0.044715 * (x * x * x))))


def _log_sigmoid(x):
    return jnp.minimum(x, 0.0) - jnp.log1p(jnp.exp(-jnp.abs(x)))


def _rms(x, w):
    ms = jnp.mean(x * x, axis=-1, keepdims=True)
    return (x * lax.rsqrt(ms + NORM_EPS)) * w


def _norm_mod(x, w, shift, scale):
    return _rms(x, w) * (1.0 + scale) + shift


def _ada_kernel(cc_ref, w_ref, b_ref, o_ref):
    a = _silu(cc_ref[...]).astype(BF16)
    o_ref[...] = _dot(a, w_ref[...].astype(BF16)) + b_ref[...]


def _ada(c, c_ctx, ada_w, ada_b):
    depth, d, nd = ada_w.shape
    bsz = c.shape[0]
    rows = ((bsz + 1 + SUBLANES - 1) // SUBLANES) * SUBLANES
    cc = jnp.zeros((rows, d), F32).at[:bsz].set(c).at[bsz].set(c_ctx)
    nblk = nd // d
    mod = pl.pallas_call(
        _ada_kernel,
        grid=(depth, nblk),
        in_specs=[
            pl.BlockSpec((rows, d), lambda i, n: (0, 0)),
            pl.BlockSpec((None, d, d), lambda i, n: (i, 0, n)),
            pl.BlockSpec((None, 1, d), lambda i, n: (i, 0, n)),
        ],
        out_specs=pl.BlockSpec((None, rows, d), lambda i, n: (i, 0, n)),
        out_shape=jax.ShapeDtypeStruct((depth, rows, nd), F32),
        compiler_params=_cparams("arbitrary", "arbitrary"),
        name="ada_mod",
    )(cc, ada_w, ada_b.reshape(depth, 1, nd))
    mx = mod[:, :bsz].reshape(depth, bsz, 3, 3, d)
    mc = jnp.broadcast_to(mod[:, bsz:bsz + 1], (depth, bsz, nd)).reshape(depth, bsz, 3, 3, d)
    return (mx, jnp.transpose(mx, (0, 2, 3, 1, 4))), (mc, jnp.transpose(mc, (0, 2, 3, 1, 4)))


def _ffn_kernel(chunks, final, x_ref, mod_ref, nw_ref, wi_ref, wo_ref, *rest):
    if final:
        fw_ref, o_ref = rest
    else:
        (o_ref,) = rest
    ff = wo_ref.shape[0]
    x = x_ref[...]
    h = _norm_mod(x, nw_ref[...], mod_ref[0:1, :], mod_ref[1:2, :]).astype(BF16)
    acc = None
    for lo, hi in chunks:
        g = _dot(h, wi_ref[:, lo:hi])
        u = _dot(h, wi_ref[:, ff + lo:ff + hi])
        part = _dot((_silu(g) * u).astype(BF16), wo_ref[lo:hi, :])
        acc = part if acc is None else acc + part
    y = x + (0.5 * mod_ref[2:3, :]) * acc
    if final:
        y = _rms(y, fw_ref[...])
    o_ref[...] = y


def _ff_chunks(ff, width):
    assert ff % LANES == 0 and width % LANES == 0
    return tuple((lo, min(lo + width, ff)) for lo in range(0, ff, width))


def _ffn(x, mod_a, norm_w4, w_in, w_out, layer, stage, bsz, *, in_bld=False, final_w=None):
    d = norm_w4.shape[-1]
    seq = x.shape[1] if in_bld else x.shape[0]
    ff = w_out.shape[2]
    tm = 512 if seq % 512 == 0 else 256
    s = 0 if stage == 0 else 1
    final = final_w is not None
    resident = pl.Buffered(1)
    if in_bld:
        x_spec = pl.BlockSpec((None, tm, d), lambda t, b: (b, t, 0))
    else:
        x_spec = pl.BlockSpec((tm, d), lambda t, b: (t, b))
    in_specs = [
        x_spec,
        pl.BlockSpec((None, None, None, 3, d), lambda t, b: (layer, b, stage, 0, 0)),
        pl.BlockSpec((None, None, 1, d), lambda t, b: (layer, stage, 0, 0)),
        pl.BlockSpec((None, None, d, 2 * ff), lambda t, b: (layer, s, 0, 0), pipeline_mode=resident),
        pl.BlockSpec((None, None, ff, d), lambda t, b: (layer, s, 0, 0), pipeline_mode=resident),
    ]
    args = [x, mod_a, norm_w4, w_in, w_out]
    if final:
        in_specs.append(pl.BlockSpec((1, d), lambda t, b: (0, 0)))
        args.append(final_w)
        out_spec = pl.BlockSpec((None, tm, d), lambda t, b: (b, t, 0))
        out_shape = jax.ShapeDtypeStruct((bsz, seq, d), F32)
    else:
        out_spec = pl.BlockSpec((tm, d), lambda t, b: (t, b))
        out_shape = jax.ShapeDtypeStruct((seq, bsz * d), F32)
    return pl.pallas_call(
        functools.partial(_ffn_kernel, _ff_chunks(ff, FFN_CHUNK), final),
        grid=(seq // tm, bsz),
        in_specs=in_specs,
        out_specs=out_spec,
        out_shape=out_shape,
        compiler_params=_cparams("parallel", "parallel"),
        name="ffn_half",
    )(*args)


S5_T = 16
S5C_TB = 128
S5C_TB_FWD = 256
SLOT = LANES // SUBLANES


def _cmul(ar, ai, br, bi):
    return ar * br - ai * bi, ar * bi + ai * br


def _s5c_prep(a_re, a_im, log_step, b_re, b_im, c_re, c_im):
    dt = jnp.exp(log_step.astype(F32))[..., None]
    a_re = a_re.astype(F32)
    a_im = a_im.astype(F32)
    mag = jnp.exp(a_re * dt)
    abar_re = mag * jnp.cos(a_im * dt)
    abar_im = mag * jnp.sin(a_im * dt)
    den = a_re * a_re + a_im * a_im
    zr = abar_re - 1.0
    zi = abar_im
    coef_re = ((zr * a_re + zi * a_im) / den)[..., None]
    coef_im = ((zi * a_re - zr * a_im) / den)[..., None]
    b_re = b_re.astype(F32)
    b_im = b_im.astype(F32)
    bb_re = coef_re * b_re - coef_im * b_im
    bb_im = coef_re * b_im + coef_im * b_re
    cc_re = c_re.astype(F32)
    cc_im = c_im.astype(F32)
    t = S5_T
    pr, pi = [jnp.ones_like(abar_re)], [jnp.zeros_like(abar_im)]
    for _ in range(t):
        nr, ni = _cmul(pr[-1], pi[-1], abar_re, abar_im)
        pr.append(nr)
        pi.append(ni)
    pr, pi = jnp.stack(pr), jnp.stack(pi)
    hi = lax.Precision.HIGHEST
    g, p, gc = abar_re.shape[1], abar_re.shape[2], b_re.shape[-1]
    prk = jnp.transpose(pr, (1, 2, 3, 0))
    pik = jnp.transpose(pi, (1, 2, 3, 0))
    e_re, e_im = _cmul(prk[..., None], pik[..., None], bb_re[:, :, :, None, :], bb_im[:, :, :, None, :])
    nkc = (t + 1) * gc
    taps = (jnp.einsum('dgop,dgpn->dgon', cc_re, e_re.reshape(2, g, p, nkc), precision=hi)
            - jnp.einsum('dgop,dgpn->dgon', cc_im, e_im.reshape(2, g, p, nkc), precision=hi))
    taps = taps.reshape(2, g, gc, t + 1, gc)
    kf, kb = taps[0], taps[1]
    seq = jnp.concatenate([jnp.flip(kb[:, :, 1:t], axis=2), kf[:, :, :1] + kb[:, :, :1], kf[:, :, 1:t]],
                          axis=2)
    tab = jnp.transpose(seq, (0, 3, 2, 1)).reshape(g, gc, (2 * t - 1) * gc)
    m = jnp.stack([tab[:, :, gc * (t - 1 - s):gc * (t - 1 - s) + t * gc] for s in range(t)], axis=1)
    m = m.reshape(g, t * gc, t * gc)

    def wst(d, flip):
        er, ei = e_re[d, :, :, :t], e_im[d, :, :, :t]
        if flip:
            er, ei = jnp.flip(er, 2), jnp.flip(ei, 2)
        w = jnp.concatenate([jnp.transpose(er, (0, 2, 3, 1)), jnp.transpose(ei, (0, 2, 3, 1))], axis=-1)
        return w.reshape(g, t * gc, 2 * p)

    wst_f = wst(0, True)
    wst_b = wst(1, False)

    def wout(d, flip):
        qr, qi = prk[d, :, :, 1:t + 1], pik[d, :, :, 1:t + 1]
        if flip:
            qr, qi = jnp.flip(qr, 2), jnp.flip(qi, 2)
        cr = jnp.transpose(cc_re[d], (0, 2, 1))
        ci = jnp.transpose(cc_im[d], (0, 2, 1))
        gr, gi = _cmul(cr[:, :, None, :], ci[:, :, None, :], qr[..., None], qi[..., None])
        return jnp.concatenate([gr, -gi], axis=1).reshape(g, 2 * p, t * gc)

    wout_f = wout(0, False)
    wout_b = wout(1, True)
    a1 = jnp.concatenate([pr[t], pr[t]], axis=-1)
    a2 = jnp.concatenate([-pi[t], pi[t]], axis=-1)
    bc = lambda a: jnp.broadcast_to(a[:, :, None, :], a.shape[:2] + (SUBLANES, a.shape[-1]))
    return dict(m=m.astype(BF16), wst=(wst_f.astype(BF16), wst_b.astype(BF16)),
                wout=(wout_f.astype(BF16), wout_b.astype(BF16)), a1=bc(a1), a2=bc(a2))


def _slot_transpose(x):
    m, n, bsz, _ = x.shape
    slot = lax.broadcasted_iota(jnp.int32, (bsz, LANES), 1) // SLOT

    def rot(v, shift):
        return pltpu.roll(v.reshape(-1, LANES), shift, 1).reshape(v.shape)

    d = n // 2
    while d >= 1:
        keep = (slot & d) == 0
        x = x.reshape(m, n // (2 * d), 2, d, bsz, LANES)
        lo, hi = x[:, :, 0], x[:, :, 1]
        new_lo = jnp.where(keep, lo, rot(hi, SLOT * d))
        new_hi = jnp.where(keep, rot(lo, LANES - SLOT * d), hi)
        x = jnp.stack([new_lo, new_hi], axis=2).reshape(m, n, bsz, LANES)
        d //= 2
    return x


def _s5c_gather(h_s, a_s, nch):
    nj = h_s.shape[1] // LANES
    nh = S5_T // SUBLANES
    for j in range(nj):
        x = h_s[:, j * LANES:(j + 1) * LANES].reshape(nch * nh, SUBLANES, SUBLANES, LANES)
        r = _slot_transpose(x).reshape(nch, nh, SUBLANES, SUBLANES, LANES)
        for gl in range(S5_LANE_GROUPS):
            for hh in range(nh):
                a_s[S5_LANE_GROUPS * j + gl, :, hh * LANES:(hh + 1) * LANES] = (
                    r[:, hh, gl].reshape(nch * SUBLANES, LANES))


def _s5c_state_scan(v, st_ref, sin_ref, a1, a2, g, nch, rev):
    gl = slice(g * LANES, (g + 1) * LANES)
    s = st_ref[:, gl]
    w = pltpu.roll(s, LANES // 2, 1)
    for n in (range(nch - 1, -1, -1) if rev else range(nch)):
        rows = slice(SUBLANES * n, SUBLANES * (n + 1))
        sin_ref[rows, gl] = s
        vn = v[rows, :]
        s, w = a1 * s + a2 * w + vn, a1 * w - a2 * s + pltpu.roll(vn, LANES // 2, 1)
    st_ref[:, gl] = s


def _s5c_norm(x_ref, mod_ref, nw_ref, h_s):
    x3 = x_ref[...]
    ms = jnp.mean(x3 * x3, axis=-1, keepdims=True)
    h3 = (x3 * lax.rsqrt(ms + NORM_EPS)) * nw_ref[...] * (1.0 + mod_ref[1]) + mod_ref[0]
    h_s[...] = h3.reshape(h_s.shape)


def _s5c_fwd_kernel(x_ref, mod_ref, nw_ref, wst_ref, a1_ref, a2_ref, h0_ref,
                    sin_ref, hfin_ref, st_ref, h_s, a_s):
    ng = wst_ref.shape[0]
    nch = a_s.shape[1] // SUBLANES

    @pl.when(pl.program_id(0) == 0)
    def _():
        st_ref[...] = h0_ref[...]

    _s5c_norm(x_ref, mod_ref, nw_ref, h_s)
    _s5c_gather(h_s, a_s, nch)
    for g in range(ng):
        v = _dot(a_s[g].astype(BF16), wst_ref[g])
        _s5c_state_scan(v, st_ref, sin_ref, a1_ref[g], a2_ref[g], g, nch, False)
    hfin_ref[...] = st_ref[...]


def _s5c_bwd_kernel(x_ref, mod_ref, nw_ref, sinf_ref, m_ref, wst_ref, woutf_ref, woutb_ref,
                    a1_ref, a2_ref, h0_ref, dsk_ref, y_ref, hfin_ref, st_ref, h_s, a_s, sinb_s):
    ng = m_ref.shape[0]
    nch = a_s.shape[1] // SUBLANES

    @pl.when(pl.program_id(0) == 0)
    def _():
        st_ref[...] = h0_ref[...]

    _s5c_norm(x_ref, mod_ref, nw_ref, h_s)
    _s5c_gather(h_s, a_s, nch)
    for j in range(ng // S5_LANE_GROUPS):
        ys = []
        for gl in range(S5_LANE_GROUPS):
            g = S5_LANE_GROUPS * j + gl
            lanes = slice(g * LANES, (g + 1) * LANES)
            a = a_s[g].astype(BF16)
            _s5c_state_scan(_dot(a, wst_ref[g]), st_ref, sinb_s, a1_ref[g], a2_ref[g], g, nch, True)
            ys.append(_dot(a, m_ref[g])
                      + _dot(sinf_ref[:, lanes].astype(BF16), woutf_ref[g])
                      + _dot(sinb_s[:, lanes].astype(BF16), woutb_ref[g]))
        cols = slice(j * LANES, (j + 1) * LANES)
        for hh in range(S5_T // SUBLANES):
            r = jnp.stack([y[:, hh * LANES:(hh + 1) * LANES].reshape(nch, SUBLANES, LANES) for y in ys],
                          axis=1)
            out = _slot_transpose(r)
            for n in range(nch):
                t0 = S5_T * n + SUBLANES * hh
                skip = dsk_ref[:, cols] * h_s[t0 * SUBLANES:(t0 + SUBLANES) * SUBLANES, cols]
                y_ref[t0:t0 + SUBLANES, :, cols] = out[n] + skip.reshape(SUBLANES, SUBLANES, LANES)
    hfin_ref[...] = st_ref[...]


def _s5c_glu_kernel(y_ref, x_ref, mod_ref, w_ref, b_ref, o_ref):
    d = x_ref.shape[1]
    z = _dot(_gelu_tanh(y_ref[...]).astype(BF16), w_ref[...]) + b_ref[...]
    o_ref[...] = x_ref[...] + mod_ref[2:3, :] * (z[:, :d] * _sigmoid(z[:, d:]))


def _s5c_stream(x2, mod_b, norm_w4, layer, prm, h0, bsz, *, rev, sin_f=None, d_skip=None):
    seq = x2.shape[0]
    d = x2.shape[1] // bsz
    tb = S5C_TB if rev else S5C_TB_FWD
    nblk = seq // tb
    nch = tb // S5_T
    dirn = 1 if rev else 0
    ng = prm["m"].shape[0]
    ns = ng * LANES
    x3 = x2.reshape(seq, bsz, d)
    blk = (lambda i: (nblk - 1 - i, 0, 0)) if rev else (lambda i: (i, 0, 0))
    blk2 = (lambda i: (nblk - 1 - i, 0)) if rev else (lambda i: (i, 0))
    x_spec = pl.BlockSpec((tb, bsz, d), blk)
    sin_spec = pl.BlockSpec((nch * bsz, ns), blk2)
    st_spec = pl.BlockSpec((bsz, ns), lambda i: (0, 0))
    res = pl.Buffered(1)
    wspec = lambda w: pl.BlockSpec(w.shape, lambda i: (0,) * w.ndim, pipeline_mode=res)
    aspec = pl.BlockSpec((None,) + prm["a1"].shape[1:], lambda i: (dirn, 0, 0, 0), pipeline_mode=res)
    common = [x_spec,
              pl.BlockSpec((None, None, 3, bsz, d), lambda i: (layer, 1, 0, 0, 0)),
              pl.BlockSpec((None, None, 1, d), lambda i: (layer, 1, 0, 0))]
    scratch = [pltpu.VMEM((bsz, ns), F32), pltpu.VMEM((tb * bsz, d), F32),
               pltpu.VMEM((ng, nch * bsz, S5_T * S5_GROUP), F32)]
    if not rev:
        return pl.pallas_call(
            _s5c_fwd_kernel,
            grid=(nblk,),
            in_specs=common + [wspec(prm["wst"][0]), aspec, aspec, st_spec],
            out_specs=[sin_spec, st_spec],
            out_shape=[jax.ShapeDtypeStruct((seq // S5_T * bsz, ns), F32),
                       jax.ShapeDtypeStruct((bsz, ns), F32)],
            scratch_shapes=scratch,
            compiler_params=_cparams("arbitrary"),
            name="s5_fwd_states",
        )(x3, mod_b, norm_w4, prm["wst"][0], prm["a1"], prm["a2"], h0)
    y, hfin = pl.pallas_call(
        _s5c_bwd_kernel,
        grid=(nblk,),
        in_specs=common + [sin_spec, wspec(prm["m"]), wspec(prm["wst"][1]), wspec(prm["wout"][0]),
                           wspec(prm["wout"][1]), aspec, aspec, st_spec,
                           pl.BlockSpec((1, d), lambda i: (0, 0))],
        out_specs=[x_spec, st_spec],
        out_shape=[jax.ShapeDtypeStruct((seq, bsz, d), F32), jax.ShapeDtypeStruct((bsz, ns), F32)],
        scratch_shapes=scratch + [pltpu.VMEM((nch * bsz, ns), F32)],
        compiler_params=_cparams("arbitrary"),
        name="s5_bwd_readout",
    )(x3, mod_b, norm_w4, sin_f, prm["m"], prm["wst"][1], prm["wout"][0], prm["wout"][1],
      prm["a1"], prm["a2"], h0, d_skip.reshape(1, d))
    return y.reshape(seq, bsz * d), hfin


def _s5c_glu(y2, x2, mod_a, layer, w_glu, b_glu, bsz):
    seq = x2.shape[0]
    d = x2.shape[1] // bsz
    tm = 512 if seq % 512 == 0 else 256
    row = pl.BlockSpec((tm, d), lambda t, b: (t, b))
    return pl.pallas_call(
        _s5c_glu_kernel,
        grid=(seq // tm, bsz),
        in_specs=[row, row,
                  pl.BlockSpec((None, None, None, 3, d), lambda t, b: (layer, b, 1, 0, 0)),
                  pl.BlockSpec(w_glu.shape, lambda t, b: (0, 0), pipeline_mode=pl.Buffered(1)),
                  pl.BlockSpec((1, 2 * d), lambda t, b: (0, 0))],
        out_specs=row,
        out_shape=jax.ShapeDtypeStruct((seq, bsz * d), F32),
        compiler_params=_cparams("parallel", "parallel"),
        name="s5_glu",
    )(y2, x2, mod_a, w_glu, b_glu.reshape(1, 2 * d))


def _s5c_mixer(xc2, x2, mods, norm_w4, layer, prm, extra, bsz, need_ctx):
    (modc_a, modc_b), (modx_a, modx_b) = mods
    d_skip, w_glu, b_glu = extra
    ns = prm["m"].shape[0] * LANES
    zero = jnp.zeros((bsz, ns), F32)
    sc_f, hf = _s5c_stream(xc2, modc_b, norm_w4, layer, prm, zero, bsz, rev=False)
    sx_f, _ = _s5c_stream(x2, modx_b, norm_w4, layer, prm, hf, bsz, rev=False)
    yc, hb = _s5c_stream(xc2, modc_b, norm_w4, layer, prm, zero, bsz, rev=True, sin_f=sc_f, d_skip=d_skip)
    yx, _ = _s5c_stream(x2, modx_b, norm_w4, layer, prm, hb, bsz, rev=True, sin_f=sx_f, d_skip=d_skip)
    x_new = _s5c_glu(yx, x2, modx_a, layer, w_glu, b_glu, bsz)
    xc_new = _s5c_glu(yc, xc2, modc_a, layer, w_glu, b_glu, bsz) if need_ctx else None
    return xc_new, x_new


def _level_matrix(rev):
    p = np.arange(CHUNK)
    v, r = p // SUBLANES, p % SUBLANES
    tau = GROUP * (v // SUBLANES) + SUBLANES * r + v % SUBLANES
    if rev:
        tau = CHUNK - 1 - tau
    ti, tj = tau[:, None], tau[None, :]
    x = ti ^ tj
    lvl = np.zeros((CHUNK, CHUNK), np.int32)
    for b in range(N_LEVELS):
        lvl = np.where((x >> b) & 1, b + 1, lvl)
    lvl = np.where(tj > ti, -1, lvl)
    return lvl.astype(np.int32)


def _scan_chunk_head(q, k, vb, g, st_t, masks, rev):
    dk = q.shape[1]
    ng, nv = N_GROUPS, SUBLANES
    ea = (lambda e: SUB - 1 - e) if rev else (lambda e: e)
    ksub = (lambda e: SUBLANES - 1 - e) if rev else (lambda e: e)

    def vregs(x, scale=None):
        out = [x[SUBLANES * ea(e):SUBLANES * ea(e) + SUBLANES, :] for e in range(SUB)]
        if scale is not None:
            out = [o * scale for o in out]
        return [out[nv * gi:nv * (gi + 1)] for gi in range(ng)]

    def assemble(vs):
        flat = [vs[gi][a] for gi in range(ng) for a in range(nv)]
        return jnp.concatenate([flat[ea(a)] for a in range(SUB)], axis=0)

    def zeros():
        return [[zero] * nv for _ in range(ng)]

    qv, kv, gv = vregs(q), vregs(k), vregs(g, LOG2E)
    zero = jnp.zeros((SUBLANES, dk), F32)
    ninf = jnp.full((SUBLANES, dk), -jnp.inf, F32)

    sub_i = lax.broadcasted_iota(jnp.int32, (SUBLANES, dk), 0)
    re = (SUBLANES - 1 - sub_i) if rev else sub_i

    def row(x, e):
        kk = ksub(e)
        return jnp.broadcast_to(x[kk:kk + 1, :], (SUBLANES, dk))

    pf, tot, ct, xcl, gtot = [], [], [], [], []
    for gi in range(ng):
        p = [gv[gi][0]]
        for a in range(1, nv):
            p.append(p[-1] + gv[gi][a])
        t = p[nv - 1]
        c = zero
        for e in range(SUBLANES):
            c = c + jnp.where(re >= e, row(t, e), 0.0)
        pf.append(p)
        tot.append(t)
        ct.append(c)
        xcl.append(c - t)
        gtot.append(row(c, SUBLANES - 1))

    q_lv, k_lv = [], []
    for lvl in range(1, N_FINE + 1):
        s = 1 << lvl
        qs, ks = zeros(), zeros()
        for gi in range(ng):
            for bs in range(0, nv, s):
                m = bs + s // 2
                ref = pf[gi][m - 1]
                for a in range(bs, m - 1):
                    ks[gi][a] = kv[gi][a] * jnp.exp2(ref - pf[gi][a])
                ks[gi][m - 1] = kv[gi][m - 1]
                for a in range(m, bs + s):
                    qs[gi][a] = qv[gi][a] * jnp.exp2(pf[gi][a] - ref)
        q_lv.append(assemble(qs))
        k_lv.append(assemble(ks))
    for lvl in range(N_FINE + 1, N_FINE + N_COARSE + 1):
        w = 1 << (lvl - N_FINE)
        right = (re & (w // 2)) != 0
        qs, ks = zeros(), zeros()
        for gi in range(ng):
            xm = zero
            for bs in range(0, SUBLANES, w):
                inblk = (re >= bs) & (re < bs + w)
                xm = xm + jnp.where(inblk, row(xcl[gi], bs + w // 2), 0.0)
            drq = jnp.where(right, xcl[gi] - xm, ninf)
            dlk = jnp.where(right, ninf, tot[gi] + xm - ct[gi])
            for a in range(nv):
                qs[gi][a] = qv[gi][a] * jnp.exp2(pf[gi][a] + drq)
                ks[gi][a] = kv[gi][a] * jnp.exp2(dlk - pf[gi][a])
        q_lv.append(assemble(qs))
        k_lv.append(assemble(ks))
    qs, ks = zeros(), zeros()
    for a in range(nv):
        ks[0][a] = kv[0][a] * jnp.exp2(tot[0] + (gtot[0] - ct[0]) - pf[0][a])
        qs[1][a] = qv[1][a] * jnp.exp2(pf[1][a] + xcl[1])
    q_lv.append(assemble(qs))
    k_lv.append(assemble(ks))

    sc = jnp.where(masks[0], jnp.sum(q * k, axis=-1, keepdims=True), 0.0)
    for lvl in range(1, N_LEVELS + 1):
        s_l = _dot_nt(q_lv[lvl - 1].astype(BF16), k_lv[lvl - 1].astype(BF16))
        sc = jnp.where(masks[lvl], s_l, sc)

    e0, e1 = jnp.exp2(gtot[0]), jnp.exp2(gtot[1])
    qc = zeros()
    kd = zeros()
    for a in range(nv):
        qc[0][a] = qv[0][a] * jnp.exp2(pf[0][a] + xcl[0])
        qc[1][a] = qs[1][a] * e0
        kd[0][a] = ks[0][a] * e1
        kd[1][a] = kv[1][a] * jnp.exp2(tot[1] + (gtot[1] - ct[1]) - pf[1][a])
    o = _dot(sc.astype(BF16), vb) + _dot_nt(assemble(qc).astype(BF16), st_t.astype(BF16))
    st_new = (e0 * e1)[0:1, :] * st_t + _dot_tn(vb, assemble(kd).astype(BF16))
    return o, st_new


def _scan_kernel(n_heads, dk, dv, qf_ref, kf_ref, vf_ref, gf_ref, qb_ref, kb_ref, vb_ref, gb_ref,
                 lv_ref, s0_ref, of_ref, ob_ref, sf_ref, st_ref):
    @pl.when(pl.program_id(1) == 0)
    def _():
        st_ref[...] = s0_ref[...]

    dirs = ((qf_ref, kf_ref, vf_ref, gf_ref, of_ref, False), (qb_ref, kb_ref, vb_ref, gb_ref, ob_ref, True))
    masks = [[lv_ref[d] == lvl for lvl in range(N_LEVELS + 1)] for d in range(2)]
    ncb = qf_ref.shape[0] // CHUNK
    for c in range(ncb):
        for h in range(n_heads):
            ks = slice(h * dk, (h + 1) * dk)
            vs = slice(h * dv, (h + 1) * dv)
            for d, (q_ref, k_ref, v_ref, g_ref, o_ref, rev) in enumerate(dirs):
                cc = ncb - 1 - c if rev else c
                rows = slice(cc * CHUNK, (cc + 1) * CHUNK)
                o, st_new = _scan_chunk_head(q_ref[rows, ks], k_ref[rows, ks], v_ref[rows, vs],
                                             g_ref[rows, ks], st_ref[d, vs, :], masks[d], rev)
                o_ref[rows, vs] = o.astype(o_ref.dtype)
                st_ref[d, vs, :] = st_new
    sf_ref[...] = st_ref[...]


def _gated_scan(q, kf, kb, v, gf, gb, s0, n_heads, bsz):
    seq = q.shape[0]
    wk = q.shape[1] // bsz
    wv = v.shape[1] // bsz
    dk, dv = wk // n_heads, wv // n_heads
    rows = SCAN_CHUNKS_PER_STEP * CHUNK
    nblk = seq // rows
    fwd = lambda b, n: (n, b)
    bwd = lambda b, n: (nblk - 1 - n, b)
    kspec = lambda im: pl.BlockSpec((rows, wk), im)
    vspec = lambda im: pl.BlockSpec((rows, wv), im)
    sspec = pl.BlockSpec((2, None, wv, dk), lambda b, n: (0, b, 0, 0))
    lv = jnp.asarray(np.stack([_level_matrix(False), _level_matrix(True)]))
    return pl.pallas_call(
        functools.partial(_scan_kernel, n_heads, dk, dv),
        grid=(bsz, nblk),
        in_specs=[kspec(fwd), kspec(fwd), vspec(fwd), kspec(fwd),
                  kspec(bwd), kspec(bwd), vspec(bwd), kspec(bwd),
                  pl.BlockSpec((2, CHUNK, CHUNK), lambda b, n: (0, 0, 0)), sspec],
        out_specs=[vspec(fwd), vspec(bwd), sspec],
        out_shape=[jax.ShapeDtypeStruct((seq, bsz * wv), F32),
                   jax.ShapeDtypeStruct((seq, bsz * wv), F32),
                   jax.ShapeDtypeStruct((2, bsz, wv, dk), F32)],
        scratch_shapes=[pltpu.VMEM((2, wv, dk), F32)],
        compiler_params=_cparams("parallel", "arbitrary"),
        name="gated_scan",
    )(q, kf, v, gf, q, kb, v, gb, lv, s0)


def _bidir_scan(ctx_in, lat_in, n_heads, bsz, need_ctx):
    qc, kfc, kbc, vc, gfc, gbc = ctx_in
    qx, kfx, kbx, vx, gfx, gbx = lat_in
    wk = qc.shape[1] // bsz
    wv = vc.shape[1] // bsz
    s0 = jnp.zeros((2, bsz, wv, wk // n_heads), F32)
    oc_f, oc_b, s_c = _gated_scan(qc, kfc, kbc, vc, gfc, gbc, s0, n_heads, bsz)
    ox_f, ox_b, _ = _gated_scan(qx, kfx, kbx, vx, gfx, gbx, s_c, n_heads, bsz)
    return ((oc_f, oc_b) if need_ctx else None), (ox_f, ox_b)


def _chunk_rows(c, colmajor):
    if colmajor:
        return lambda v: (SUBLANES * (v % SUBLANES) + N_GROUPS * c + v // SUBLANES, GROUP)
    return lambda v: (c * CHUNK + GROUP * (v // SUBLANES) + v % SUBLANES, SUBLANES)


def _permute_chunk_in(stage_ref, dst_ref, c, colmajor):
    rows = _chunk_rows(c, colmajor)
    for jl in range(stage_ref.shape[0]):
        for v in range(SUB):
            start, stride = rows(v)
            dst_ref[c * CHUNK + SUBLANES * v:c * CHUNK + SUBLANES * (v + 1), jl * LANES:(jl + 1) * LANES] = (
                stage_ref[jl, pl.ds(start, SUBLANES, stride=stride), :])


def _permute_chunk_out(val, stage_ref, c, colmajor):
    rows = _chunk_rows(c, colmajor)
    for jl in range(stage_ref.shape[0]):
        for v in range(SUB):
            start, stride = rows(v)
            stage_ref[jl, pl.ds(start, SUBLANES, stride=stride), :] = (
                val[SUBLANES * v:SUBLANES * (v + 1), jl * LANES:(jl + 1) * LANES])


def _stage_block(x_ref, stage_ref):
    n = stage_ref.shape[1]
    for jl in range(stage_ref.shape[0]):
        blk = x_ref[..., jl * LANES:(jl + 1) * LANES]
        stage_ref[jl] = blk.reshape(n, LANES)


def _unstage_block(stage_ref, o_ref):
    for jl in range(stage_ref.shape[0]):
        o_ref[..., jl * LANES:(jl + 1) * LANES] = stage_ref[jl].reshape(o_ref.shape[:-1] + (LANES,))


def _hgrn_epilogue(h, w_ref, lb, d):
    q = _silu(_dot(h, w_ref[:, :d]))
    v = _dot(h, w_ref[:, d:2 * d])
    zf = _dot(h, w_ref[:, 2 * d:3 * d])
    zb = _dot(h, w_ref[:, 3 * d:4 * d])
    og = _dot(h, w_ref[:, 4 * d:])
    lf, lbk = lb[0:1, :], lb[1:2, :]
    sf, snf = _sigmoid_pair(zf)
    sb, snb = _sigmoid_pair(zb)
    gf = jnp.log(lf + (1.0 - lf) * sf)
    gb = jnp.log(lbk + (1.0 - lbk) * sb)
    kf = (1.0 - lf) * snf
    kb = (1.0 - lbk) * snb
    return q, kf, kb, v, gf, gb, og


def _x_stream(x2, bsz, colmajor):
    seq, d = x2.shape[0], x2.shape[1] // bsz
    if colmajor:
        rows = seq // GRID_W
        assert rows == GROUP, "column-major chunks assume one grid column per 64-step group"
        return (x2.reshape(rows, GRID_W, bsz * d),
                pl.BlockSpec((rows, SUBLANES, d), lambda t, b: (0, t, b)), SUBLANES * rows)
    tm = 2 * CHUNK
    return x2, pl.BlockSpec((tm, d), lambda t, b: (t, b)), tm


def _load_chunks(x_ref, stage_ref, xs_ref, colmajor):
    _stage_block(x_ref, stage_ref)
    for c in range(xs_ref.shape[0] // CHUNK):
        _permute_chunk_in(stage_ref, xs_ref, c, colmajor)


def _store_chunks(xn, stage_ref, o_ref, colmajor):
    for c in range(xn.shape[0] // CHUNK):
        _permute_chunk_out(xn[c * CHUNK:(c + 1) * CHUNK, :], stage_ref, c, colmajor)
    _unstage_block(stage_ref, o_ref)


def _chunk_scratch(tm, d):
    return [pltpu.VMEM((d // LANES, tm, LANES), F32), pltpu.VMEM((tm, d), F32)]


def _hgrn_proj_kernel(x_ref, mod_ref, nw_ref, w_ref, lb_ref,
                      q_ref, kf_ref, kb_ref, v_ref, gf_ref, gb_ref, og_ref, stage_ref, xs_ref):
    d = xs_ref.shape[1]
    _load_chunks(x_ref, stage_ref, xs_ref, False)
    h = _norm_mod(xs_ref[...], nw_ref[...], mod_ref[0:1, :], mod_ref[1:2, :]).astype(BF16)
    outs = _hgrn_epilogue(h, w_ref, lb_ref[...], d)
    for ref, val in zip((q_ref, kf_ref, kb_ref, v_ref, gf_ref, gb_ref, og_ref), outs):
        ref[...] = val.astype(ref.dtype)


def _hgrn_project(x2, mod_a, norm_w4, layer, w_in, lb, bsz):
    seq = x2.shape[0]
    d = x2.shape[1] // bsz
    xv, xspec, tm = _x_stream(x2, bsz, False)
    row = pl.BlockSpec((tm, d), lambda t, b: (t, b))
    return pl.pallas_call(
        _hgrn_proj_kernel,
        grid=(seq // tm, bsz),
        in_specs=[
            xspec,
            pl.BlockSpec((None, None, None, 3, d), lambda t, b: (layer, b, 1, 0, 0)),
            pl.BlockSpec((None, None, 1, d), lambda t, b: (layer, 1, 0, 0)),
            pl.BlockSpec(w_in.shape, lambda t, b: (0, 0), pipeline_mode=pl.Buffered(1)),
            pl.BlockSpec(lb.shape, lambda t, b: (0, 0)),
        ],
        out_specs=[row] * 7,
        out_shape=[jax.ShapeDtypeStruct((seq, bsz * d), BF16 if i == 3 else F32) for i in range(7)],
        scratch_shapes=_chunk_scratch(tm, d),
        compiler_params=_cparams("parallel", "parallel"),
        name="hgrn_project",
    )(xv, mod_a, norm_w4, w_in, lb)


def _gla_epilogue(h, w_ref, wgate, bgate, dkt, dvt):
    low = _dot(h, w_ref[:, 2 * dkt + 2 * dvt:])
    lg = _log_sigmoid(_dot(low.astype(BF16), wgate) + bgate) * (1.0 / GLA_TAU)
    q = _dot(h, w_ref[:, :dkt]) * ((dkt // GLA_HEADS) ** -0.5)
    k = _dot(h, w_ref[:, dkt:2 * dkt])
    v = _dot(h, w_ref[:, 2 * dkt:2 * dkt + dvt])
    og = _dot(h, w_ref[:, 2 * dkt + dvt:2 * dkt + 2 * dvt])
    return q, k, v, og, lg[:, :dkt], lg[:, dkt:]


def _gla_proj_kernel(colmajor, x_ref, mod_ref, nw_ref, w_ref, wg_ref, bg_ref,
                     q_ref, k_ref, v_ref, og_ref, gf_ref, gb_ref, stage_ref, xs_ref):
    dkt = q_ref.shape[1]
    dvt = v_ref.shape[1]
    _load_chunks(x_ref, stage_ref, xs_ref, colmajor)
    h = _norm_mod(xs_ref[...], nw_ref[...], mod_ref[0:1, :], mod_ref[1:2, :]).astype(BF16)
    outs = _gla_epilogue(h, w_ref, wg_ref[...], bg_ref[...], dkt, dvt)
    for ref, val in zip((q_ref, k_ref, v_ref, og_ref, gf_ref, gb_ref), outs):
        ref[...] = val.astype(ref.dtype)


def _gla_project(x2, mod_a, norm_w4, layer, w_in, wgate, bgate, bsz, dkt, dvt, *, colmajor):
    seq = x2.shape[0]
    d = x2.shape[1] // bsz
    widths = (dkt, dkt, dvt, dvt, dkt, dkt)
    xv, xspec, tm = _x_stream(x2, bsz, colmajor)
    const = lambda t, b: (0, 0)
    return pl.pallas_call(
        functools.partial(_gla_proj_kernel, colmajor),
        grid=(seq // tm, bsz),
        in_specs=[
            xspec,
            pl.BlockSpec((None, None, None, 3, d), lambda t, b: (layer, b, 1, 0, 0)),
            pl.BlockSpec((None, None, 1, d), lambda t, b: (layer, 1, 0, 0)),
            pl.BlockSpec(w_in.shape, const, pipeline_mode=pl.Buffered(1)),
            pl.BlockSpec(wgate.shape, const),
            pl.BlockSpec(bgate.shape, const),
        ],
        out_specs=[pl.BlockSpec((tm, w), lambda t, b: (t, b)) for w in widths],
        out_shape=[jax.ShapeDtypeStruct((seq, bsz * w), BF16 if i == 2 else F32)
                   for i, w in enumerate(widths)],
        scratch_shapes=_chunk_scratch(tm, d),
        compiler_params=_cparams("parallel", "parallel"),
        name="gla_project",
    )(xv, mod_a, norm_w4, w_in, wgate, bgate)


def _head_out(of, ob, og, gnw, w_out, n_heads, zs_ref):
    dv = of.shape[1] // n_heads
    o = of + ob
    for h in range(n_heads):
        sl = slice(h * dv, (h + 1) * dv)
        zs_ref[:, sl] = (_rms(o[:, sl], gnw) * _silu(og[:, sl])).astype(BF16)
    return _dot(zs_ref[...], w_out)


def _mix_out_kernel(n_heads, colmajor, of_ref, ob_ref, og_ref, x_ref, mod_ref, gnw_ref, w_ref,
                    o_ref, stage_ref, xs_ref, zs_ref):
    _load_chunks(x_ref, stage_ref, xs_ref, colmajor)
    y = _head_out(of_ref[...], ob_ref[...], og_ref[...], gnw_ref[...], w_ref[...], n_heads, zs_ref)
    xn = xs_ref[...] + mod_ref[2:3, :] * y
    _store_chunks(xn, stage_ref, o_ref, colmajor)


def _mix_out(of, ob, og, x2, mod_a, layer, gnw, w_out, n_heads, bsz, *, colmajor):
    seq = x2.shape[0]
    d = x2.shape[1] // bsz
    wv = of.shape[1] // bsz
    gnw = gnw.reshape(1, -1)
    xv, xspec, tm = _x_stream(x2, bsz, colmajor)
    ospec = pl.BlockSpec((tm, wv), lambda t, b: (t, b))
    out = pl.pallas_call(
        functools.partial(_mix_out_kernel, n_heads, colmajor),
        grid=(seq // tm, bsz),
        in_specs=[ospec, ospec, ospec, xspec,
                  pl.BlockSpec((None, None, None, 3, d), lambda t, b: (layer, b, 1, 0, 0)),
                  pl.BlockSpec(gnw.shape, lambda t, b: (0, 0)),
                  pl.BlockSpec(w_out.shape, lambda t, b: (0, 0))],
        out_specs=xspec,
        out_shape=jax.ShapeDtypeStruct(xv.shape, F32),
        scratch_shapes=_chunk_scratch(tm, d) + [pltpu.VMEM((tm, wv), BF16)],
        compiler_params=_cparams("parallel", "parallel"),
        name="mix_out",
    )(of, ob, og, xv, mod_a, gnw, w_out)
    return out.reshape(seq, bsz * d)


def _gla_mixer(xc2, x2, mods, norm_w4, layer, prm, bsz, need_ctx):
    (modc_a, _), (modx_a, _) = mods
    w_in, wgate, bgate, gnw, w_out, dkt, dvt = prm
    qc, kc, vc, ogc, gfc, gbc = _gla_project(xc2, modc_a, norm_w4, layer, w_in, wgate, bgate, bsz,
                                             dkt, dvt, colmajor=False)
    qx, kx, vx, ogx, gfx, gbx = _gla_project(x2, modx_a, norm_w4, layer, w_in, wgate, bgate, bsz,
                                             dkt, dvt, colmajor=True)
    oc, ox = _bidir_scan((qc, kc, kc, vc, gfc, gbc), (qx, kx, kx, vx, gfx, gbx), GLA_HEADS, bsz, need_ctx)
    x_new = _mix_out(ox[0], ox[1], ogx, x2, modx_a, layer, gnw, w_out, GLA_HEADS, bsz, colmajor=True)
    xc_new = None
    if need_ctx:
        xc_new = _mix_out(oc[0], oc[1], ogc, xc2, modc_a, layer, gnw, w_out, GLA_HEADS, bsz, colmajor=False)
    return xc_new, x_new


def _hgrn_mixer(xc2, x2, mods, norm_w4, layer, prm, bsz, need_ctx):
    (modc_a, _), (modx_a, _) = mods
    w_in, lb, gnw, w_out, n_heads = prm
    qc, kfc, kbc, vc, gfc, gbc, ogc = _hgrn_project(xc2, modc_a, norm_w4, layer, w_in, lb, bsz)
    qx, kfx, kbx, vx, gfx, gbx, ogx = _hgrn_project(x2, modx_a, norm_w4, layer, w_in, lb, bsz)
    oc, ox = _bidir_scan((qc, kfc, kbc, vc, gfc, gbc), (qx, kfx, kbx, vx, gfx, gbx), n_heads, bsz, need_ctx)
    x_new = _mix_out(ox[0], ox[1], ogx, x2, modx_a, layer, gnw, w_out, n_heads, bsz, colmajor=False)
    xc_new = None
    if need_ctx:
        xc_new = _mix_out(oc[0], oc[1], ogc, xc2, modc_a, layer, gnw, w_out, n_heads, bsz, colmajor=False)
    return xc_new, x_new


def _forward(x, c, ctx, c_ctx, ada_w, ada_b, norm_w, ffn_w_in, ffn_w_out,
             s5_a_re, s5_a_im, s5_log_step, s5_b_re, s5_b_im, s5_c_re, s5_c_im, s5_d,
             s5_w_glu, s5_b_glu, gla_w_in, gla_w_gate, gla_b_gate, gla_norm_w, gla_w_out,
             hgrn_w_in, hgrn_lb_logits, hgrn_norm_w, hgrn_w_out, final_norm_w, depth=None):
    depth = ada_w.shape[0] if depth is None else depth
    bsz, seq, d = x.shape
    mods_x, mods_c = _ada(c, c_ctx, ada_w, ada_b)
    norm_w4 = norm_w.reshape(norm_w.shape[0], 3, 1, d)
    w_in = ffn_w_in.astype(BF16)
    w_out = ffn_w_out.astype(BF16)
    lb_soft = jax.nn.softmax(hgrn_lb_logits.astype(F32), axis=0)
    lb_all = jnp.cumsum(lb_soft, axis=0) - lb_soft[0]

    xc2 = jnp.transpose(ctx, (1, 0, 2)).reshape(ctx.shape[1], bsz * d)
    x2 = x
    for i in range(depth):
        last = i == depth - 1
        kind, j = i % N_MIXERS, i // N_MIXERS
        x2 = _ffn(x2, mods_x[0], norm_w4, w_in, w_out, i, 0, bsz, in_bld=(i == 0))
        xc2 = _ffn(xc2, mods_c[0], norm_w4, w_in, w_out, i, 0, bsz)
        mods = (mods_c, mods_x)
        if kind == 0:
            prm = _s5c_prep(s5_a_re[j], s5_a_im[j], s5_log_step[j], s5_b_re[j], s5_b_im[j],
                            s5_c_re[j], s5_c_im[j])
            extra = (s5_d[j], s5_w_glu[j].astype(BF16), s5_b_glu[j])
            yc, x2 = _s5c_mixer(xc2, x2, mods, norm_w4, i, prm, extra, bsz, not last)
        elif kind == 1:
            dkt = d // 2
            dvt = d
            n_low = 2 * GLA_RANK
            pad = LANES - n_low
            wi = jnp.pad(gla_w_in[j], ((0, 0), (0, pad))).astype(BF16)
            wg = jnp.zeros((LANES, 2 * dkt), F32)
            wg = wg.at[:GLA_RANK, :dkt].set(gla_w_gate[j, 0]).at[GLA_RANK:n_low, dkt:].set(gla_w_gate[j, 1])
            prm = (wi, wg.astype(BF16), gla_b_gate[j].reshape(1, 2 * dkt), gla_norm_w[j],
                   gla_w_out[j].astype(BF16), dkt, dvt)
            yc, x2 = _gla_mixer(xc2, x2, mods, norm_w4, i, prm, bsz, not last)
        else:
            prm = (hgrn_w_in[j].astype(BF16), lb_all[i], hgrn_norm_w[j], hgrn_w_out[j].astype(BF16),
                   d // HGRN_DK)
            yc, x2 = _hgrn_mixer(xc2, x2, mods, norm_w4, i, prm, bsz, not last)
        x2 = _ffn(x2, mods_x[0], norm_w4, w_in, w_out, i, 2, bsz,
                  final_w=final_norm_w.reshape(1, d) if last else None)
        if not last:
            xc2 = _ffn(yc, mods_c[0], norm_w4, w_in, w_out, i, 2, bsz)
    return x2


def kernel(x, c, ctx, c_ctx, ada_w, ada_b, norm_w, ffn_w_in, ffn_w_out, s5_a_re, s5_a_im, s5_log_step,
           s5_b_re, s5_b_im, s5_c_re, s5_c_im, s5_d, s5_w_glu, s5_b_glu, gla_w_in, gla_w_gate,
           gla_b_gate, gla_norm_w, gla_w_out, hgrn_w_in, hgrn_lb_logits, hgrn_norm_w, hgrn_w_out,
           final_norm_w):
    return _forward(x, c, ctx, c_ctx, ada_w, ada_b, norm_w, ffn_w_in, ffn_w_out, s5_a_re, s5_a_im,
                    s5_log_step, s5_b_re, s5_b_im, s5_c_re, s5_c_im, s5_d, s5_w_glu, s5_b_glu,
                    gla_w_in, gla_w_gate, gla_b_gate, gla_norm_w, gla_w_out, hgrn_w_in,
                    hgrn_lb_logits, hgrn_norm_w, hgrn_w_out, final_norm_w)
```

```python
import functools
import math

import numpy as np
import jax
import jax.numpy as jnp
from jax import lax
from jax.experimental import pallas as pl
from jax.experimental.pallas import tpu as pltpu

F32 = jnp.float32
BF16 = jnp.bfloat16

NORM_EPS = 1e-6
GRID_W = 64
N_MOD = 9
N_MIXERS = 3

S5_GROUP = 16
S5_STATE = 64
S5_LANE_GROUPS = 8

GLA_HEADS = 4
GLA_RANK = 16
GLA_TAU = 16.0
HGRN_DK = 128

SUBLANES = 8
LANES = 128
CHUNK = 128
GROUP = SUBLANES * SUBLANES
N_GROUPS = CHUNK // GROUP
SUB = CHUNK // SUBLANES
N_FINE = 3
N_COARSE = 3
N_LEVELS = 7
LOG2E = 1.4426950408889634

FFN_CHUNK = 768

VMEM_LIMIT_BYTES = 56 * 1024 * 1024


def _cparams(*sem):
    return pltpu.CompilerParams(dimension_semantics=sem, vmem_limit_bytes=VMEM_LIMIT_BYTES)


def _dot(a, b):
    return jnp.dot(a, b, preferred_element_type=F32)


def _dot_nt(a, b):
    return lax.dot_general(a, b, (((1,), (1,)), ((), ())), preferred_element_type=F32)


def _dot_tn(a, b):
    return lax.dot_general(a, b, (((0,), (0,)), ((), ())), preferred_element_type=F32)


def _sigmoid(x):
    return 0.5 * jnp.tanh(0.5 * x) + 0.5


def _sigmoid_pair(x):
    t = jnp.exp(-jnp.abs(x))
    r = 1.0 / (1.0 + t)
    tr = t * r
    pos = x >= 0.0
    return jnp.where(pos, r, tr), jnp.where(pos, tr, r)


def _silu(x):
    return x * _sigmoid(x)


def _gelu_tanh(x):
    c = math.sqrt(2.0 / math.pi)
    return 0.5 * x * (1.0 + jnp.tanh(c * (x + 0.044715 * (x * x * x))))


def _log_sigmoid(x):
    return jnp.minimum(x, 0.0) - jnp.log1p(jnp.exp(-jnp.abs(x)))


def _rms(x, w):
    ms = jnp.mean(x * x, axis=-1, keepdims=True)
    return (x * lax.rsqrt(ms + NORM_EPS)) * w


def _norm_mod(x, w, shift, scale):
    return _rms(x, w) * (1.0 + scale) + shift


def _ada_kernel(cc_ref, w_ref, b_ref, o_ref):
    a = _silu(cc_ref[...]).astype(BF16)
    o_ref[...] = _dot(a, w_ref[...].astype(BF16)) + b_ref[...]


def _ada(c, c_ctx, ada_w, ada_b):
    depth, d, nd = ada_w.shape
    bsz = c.shape[0]
    rows = ((bsz + 1 + SUBLANES - 1) // SUBLANES) * SUBLANES
    cc = jnp.zeros((rows, d), F32).at[:bsz].set(c).at[bsz].set(c_ctx)
    nblk = nd // d
    mod = pl.pallas_call(
        _ada_kernel,
        grid=(depth, nblk),
        in_specs=[
            pl.BlockSpec((rows, d), lambda i, n: (0, 0)),
            pl.BlockSpec((None, d, d), lambda i, n: (i, 0, n)),
            pl.BlockSpec((None, 1, d), lambda i, n: (i, 0, n)),
        ],
        out_specs=pl.BlockSpec((None, rows, d), lambda i, n: (i, 0, n)),
        out_shape=jax.ShapeDtypeStruct((depth, rows, nd), F32),
        compiler_params=_cparams("arbitrary", "arbitrary"),
        name="ada_mod",
    )(cc, ada_w, ada_b.reshape(depth, 1, nd))
    mx = mod[:, :bsz].reshape(depth, bsz, 3, 3, d)
    mc = jnp.broadcast_to(mod[:, bsz:bsz + 1], (depth, bsz, nd)).reshape(depth, bsz, 3, 3, d)
    return (mx, jnp.transpose(mx, (0, 2, 3, 1, 4))), (mc, jnp.transpose(mc, (0, 2, 3, 1, 4)))


def _ffn_kernel(chunks, final, x_ref, mod_ref, nw_ref, wi_ref, wo_ref, *rest):
    if final:
        fw_ref, o_ref = rest
    else:
        (o_ref,) = rest
    ff = wo_ref.shape[0]
    x = x_ref[...]
    h = _norm_mod(x, nw_ref[...], mod_ref[0:1, :], mod_ref[1:2, :]).astype(BF16)
    acc = None
    for lo, hi in chunks:
        g = _dot(h, wi_ref[:, lo:hi])
        u = _dot(h, wi_ref[:, ff + lo:ff + hi])
        part = _dot((_silu(g) * u).astype(BF16), wo_ref[lo:hi, :])
        acc = part if acc is None else acc + part
    y = x + (0.5 * mod_ref[2:3, :]) * acc
    if final:
        y = _rms(y, fw_ref[...])
    o_ref[...] = y


def _ff_chunks(ff, width):
    assert ff % LANES == 0 and width % LANES == 0
    return tuple((lo, min(lo + width, ff)) for lo in range(0, ff, width))


def _ffn(x, mod_a, norm_w4, w_in, w_out, layer, stage, bsz, *, in_bld=False, final_w=None):
    d = norm_w4.shape[-1]
    seq = x.shape[1] if in_bld else x.shape[0]
    ff = w_out.shape[2]
    tm = 512 if seq % 512 == 0 else 256
    s = 0 if stage == 0 else 1
    final = final_w is not None
    resident = pl.Buffered(1)
    if in_bld:
        x_spec = pl.BlockSpec((None, tm, d), lambda t, b: (b, t, 0))
    else:
        x_spec = pl.BlockSpec((tm, d), lambda t, b: (t, b))
    in_specs = [
        x_spec,
        pl.BlockSpec((None, None, None, 3, d), lambda t, b: (layer, b, stage, 0, 0)),
        pl.BlockSpec((None, None, 1, d), lambda t, b: (layer, stage, 0, 0)),
        pl.BlockSpec((None, None, d, 2 * ff), lambda t, b: (layer, s, 0, 0), pipeline_mode=resident),
        pl.BlockSpec((None, None, ff, d), lambda t, b: (layer, s, 0, 0), pipeline_mode=resident),
    ]
    args = [x, mod_a, norm_w4, w_in, w_out]
    if final:
        in_specs.append(pl.BlockSpec((1, d), lambda t, b: (0, 0)))
        args.append(final_w)
        out_spec = pl.BlockSpec((None, tm, d), lambda t, b: (b, t, 0))
        out_shape = jax.ShapeDtypeStruct((bsz, seq, d), F32)
    else:
        out_spec = pl.BlockSpec((tm, d), lambda t, b: (t, b))
        out_shape = jax.ShapeDtypeStruct((seq, bsz * d), F32)
    return pl.pallas_call(
        functools.partial(_ffn_kernel, _ff_chunks(ff, FFN_CHUNK), final),
        grid=(seq // tm, bsz),
        in_specs=in_specs,
        out_specs=out_spec,
        out_shape=out_shape,
        compiler_params=_cparams("parallel", "parallel"),
        name="ffn_half",
    )(*args)


S5_T = 16
S5C_TB = 128
S5C_TB_FWD = 256
SLOT = LANES // SUBLANES


def _cmul(ar, ai, br, bi):
    return ar * br - ai * bi, ar * bi + ai * br


def _s5c_prep(a_re, a_im, log_step, b_re, b_im, c_re, c_im):
    dt = jnp.exp(log_step.astype(F32))[..., None]
    a_re = a_re.astype(F32)
    a_im = a_im.astype(F32)
    mag = jnp.exp(a_re * dt)
    abar_re = mag * jnp.cos(a_im * dt)
    abar_im = mag * jnp.sin(a_im * dt)
    den = a_re * a_re + a_im * a_im
    zr = abar_re - 1.0
    zi = abar_im
    coef_re = ((zr * a_re + zi * a_im) / den)[..., None]
    coef_im = ((zi * a_re - zr * a_im) / den)[..., None]
    b_re = b_re.astype(F32)
    b_im = b_im.astype(F32)
    bb_re = coef_re * b_re - coef_im * b_im
    bb_im = coef_re * b_im + coef_im * b_re
    cc_re = c_re.astype(F32)
    cc_im = c_im.astype(F32)
    t = S5_T
    pr, pi = [jnp.ones_like(abar_re)], [jnp.zeros_like(abar_im)]
    for _ in range(t):
        nr, ni = _cmul(pr[-1], pi[-1], abar_re, abar_im)
        pr.append(nr)
        pi.append(ni)
    pr, pi = jnp.stack(pr), jnp.stack(pi)
    hi = lax.Precision.HIGHEST
    g, p, gc = abar_re.shape[1], abar_re.shape[2], b_re.shape[-1]
    prk = jnp.transpose(pr, (1, 2, 3, 0))
    pik = jnp.transpose(pi, (1, 2, 3, 0))
    e_re, e_im = _cmul(prk[..., None], pik[..., None], bb_re[:, :, :, None, :], bb_im[:, :, :, None, :])
    nkc = (t + 1) * gc
    taps = (jnp.einsum('dgop,dgpn->dgon', cc_re, e_re.reshape(2, g, p, nkc), precision=hi)
            - jnp.einsum('dgop,dgpn->dgon', cc_im, e_im.reshape(2, g, p, nkc), precision=hi))
    taps = taps.reshape(2, g, gc, t + 1, gc)
    kf, kb = taps[0], taps[1]
    seq = jnp.concatenate([jnp.flip(kb[:, :, 1:t], axis=2), kf[:, :, :1] + kb[:, :, :1], kf[:, :, 1:t]],
                          axis=2)
    tab = jnp.transpose(seq, (0, 3, 2, 1)).reshape(g, gc, (2 * t - 1) * gc)
    m = jnp.stack([tab[:, :, gc * (t - 1 - s):gc * (t - 1 - s) + t * gc] for s in range(t)], axis=1)
    m = m.reshape(g, t * gc, t * gc)

    def wst(d, flip):
        er, ei = e_re[d, :, :, :t], e_im[d, :, :, :t]
        if flip:
            er, ei = jnp.flip(er, 2), jnp.flip(ei, 2)
        w = jnp.concatenate([jnp.transpose(er, (0, 2, 3, 1)), jnp.transpose(ei, (0, 2, 3, 1))], axis=-1)
        return w.reshape(g, t * gc, 2 * p)

    wst_f = wst(0, True)
    wst_b = wst(1, False)

    def wout(d, flip):
        qr, qi = prk[d, :, :, 1:t + 1], pik[d, :, :, 1:t + 1]
        if flip:
            qr, qi = jnp.flip(qr, 2), jnp.flip(qi, 2)
        cr = jnp.transpose(cc_re[d], (0, 2, 1))
        ci = jnp.transpose(cc_im[d], (0, 2, 1))
        gr, gi = _cmul(cr[:, :, None, :], ci[:, :, None, :], qr[..., None], qi[..., None])
        return jnp.concatenate([gr, -gi], axis=1).reshape(g, 2 * p, t * gc)

    wout_f = wout(0, False)
    wout_b = wout(1, True)
    a1 = jnp.concatenate([pr[t], pr[t]], axis=-1)
    a2 = jnp.concatenate([-pi[t], pi[t]], axis=-1)
    bc = lambda a: jnp.broadcast_to(a[:, :, None, :], a.shape[:2] + (SUBLANES, a.shape[-1]))
    return dict(m=m.astype(BF16), wst=(wst_f.astype(BF16), wst_b.astype(BF16)),
                wout=(wout_f.astype(BF16), wout_b.astype(BF16)), a1=bc(a1), a2=bc(a2))


def _slot_transpose(x):
    m, n, bsz, _ = x.shape
    slot = lax.broadcasted_iota(jnp.int32, (bsz, LANES), 1) // SLOT

    def rot(v, shift):
        return pltpu.roll(v.reshape(-1, LANES), shift, 1).reshape(v.shape)

    d = n // 2
    while d >= 1:
        keep = (slot & d) == 0
        x = x.reshape(m, n // (2 * d), 2, d, bsz, LANES)
        lo, hi = x[:, :, 0], x[:, :, 1]
        new_lo = jnp.where(keep, lo, rot(hi, SLOT * d))
        new_hi = jnp.where(keep, rot(lo, LANES - SLOT * d), hi)
        x = jnp.stack([new_lo, new_hi], axis=2).reshape(m, n, bsz, LANES)
        d //= 2
    return x


def _s5c_gather(h_s, a_s, nch):
    nj = h_s.shape[1] // LANES
    nh = S5_T // SUBLANES
    for j in range(nj):
        x = h_s[:, j * LANES:(j + 1) * LANES].reshape(nch * nh, SUBLANES, SUBLANES, LANES)
        r = _slot_transpose(x).reshape(nch, nh, SUBLANES, SUBLANES, LANES)
        for gl in range(S5_LANE_GROUPS):
            for hh in range(nh):
                a_s[S5_LANE_GROUPS * j + gl, :, hh * LANES:(hh + 1) * LANES] = (
                    r[:, hh, gl].reshape(nch * SUBLANES, LANES))


def _s5c_state_scan(v, st_ref, sin_ref, a1, a2, g, nch, rev):
    gl = slice(g * LANES, (g + 1) * LANES)
    s = st_ref[:, gl]
    w = pltpu.roll(s, LANES // 2, 1)
    for n in (range(nch - 1, -1, -1) if rev else range(nch)):
        rows = slice(SUBLANES * n, SUBLANES * (n + 1))
        sin_ref[rows, gl] = s
        vn = v[rows, :]
        s, w = a1 * s + a2 * w + vn, a1 * w - a2 * s + pltpu.roll(vn, LANES // 2, 1)
    st_ref[:, gl] = s


def _s5c_norm(x_ref, mod_ref, nw_ref, h_s):
    x3 = x_ref[...]
    ms = jnp.mean(x3 * x3, axis=-1, keepdims=True)
    h3 = (x3 * lax.rsqrt(ms + NORM_EPS)) * nw_ref[...] * (1.0 + mod_ref[1]) + mod_ref[0]
    h_s[...] = h3.reshape(h_s.shape)


def _s5c_fwd_kernel(x_ref, mod_ref, nw_ref, wst_ref, a1_ref, a2_ref, h0_ref,
                    sin_ref, hfin_ref, st_ref, h_s, a_s):
    ng = wst_ref.shape[0]
    nch = a_s.shape[1] // SUBLANES

    @pl.when(pl.program_id(0) == 0)
    def _():
        st_ref[...] = h0_ref[...]

    _s5c_norm(x_ref, mod_ref, nw_ref, h_s)
    _s5c_gather(h_s, a_s, nch)
    for g in range(ng):
        v = _dot(a_s[g].astype(BF16), wst_ref[g])
        _s5c_state_scan(v, st_ref, sin_ref, a1_ref[g], a2_ref[g], g, nch, False)
    hfin_ref[...] = st_ref[...]


def _s5c_bwd_kernel(x_ref, mod_ref, nw_ref, sinf_ref, m_ref, wst_ref, woutf_ref, woutb_ref,
                    a1_ref, a2_ref, h0_ref, dsk_ref, y_ref, hfin_ref, st_ref, h_s, a_s, sinb_s):
    ng = m_ref.shape[0]
    nch = a_s.shape[1] // SUBLANES

    @pl.when(pl.program_id(0) == 0)
    def _():
        st_ref[...] = h0_ref[...]

    _s5c_norm(x_ref, mod_ref, nw_ref, h_s)
    _s5c_gather(h_s, a_s, nch)
    for j in range(ng // S5_LANE_GROUPS):
        ys = []
        for gl in range(S5_LANE_GROUPS):
            g = S5_LANE_GROUPS * j + gl
            lanes = slice(g * LANES, (g + 1) * LANES)
            a = a_s[g].astype(BF16)
            _s5c_state_scan(_dot(a, wst_ref[g]), st_ref, sinb_s, a1_ref[g], a2_ref[g], g, nch, True)
            ys.append(_dot(a, m_ref[g])
                      + _dot(sinf_ref[:, lanes].astype(BF16), woutf_ref[g])
                      + _dot(sinb_s[:, lanes].astype(BF16), woutb_ref[g]))
        cols = slice(j * LANES, (j + 1) * LANES)
        for hh in range(S5_T // SUBLANES):
            r = jnp.stack([y[:, hh * LANES:(hh + 1) * LANES].reshape(nch, SUBLANES, LANES) for y in ys],
                          axis=1)
            out = _slot_transpose(r)
            for n in range(nch):
                t0 = S5_T * n + SUBLANES * hh
                skip = dsk_ref[:, cols] * h_s[t0 * SUBLANES:(t0 + SUBLANES) * SUBLANES, cols]
                y_ref[t0:t0 + SUBLANES, :, cols] = out[n] + skip.reshape(SUBLANES, SUBLANES, LANES)
    hfin_ref[...] = st_ref[...]


def _s5c_glu_kernel(y_ref, x_ref, mod_ref, w_ref, b_ref, o_ref):
    d = x_ref.shape[1]
    z = _dot(_gelu_tanh(y_ref[...]).astype(BF16), w_ref[...]) + b_ref[...]
    o_ref[...] = x_ref[...] + mod_ref[2:3, :] * (z[:, :d] * _sigmoid(z[:, d:]))


def _s5c_stream(x2, mod_b, norm_w4, layer, prm, h0, bsz, *, rev, sin_f=None, d_skip=None):
    seq = x2.shape[0]
    d = x2.shape[1] // bsz
    tb = S5C_TB if rev else S5C_TB_FWD
    nblk = seq // tb
    nch = tb // S5_T
    dirn = 1 if rev else 0
    ng = prm["m"].shape[0]
    ns = ng * LANES
    x3 = x2.reshape(seq, bsz, d)
    blk = (lambda i: (nblk - 1 - i, 0, 0)) if rev else (lambda i: (i, 0, 0))
    blk2 = (lambda i: (nblk - 1 - i, 0)) if rev else (lambda i: (i, 0))
    x_spec = pl.BlockSpec((tb, bsz, d), blk)
    sin_spec = pl.BlockSpec((nch * bsz, ns), blk2)
    st_spec = pl.BlockSpec((bsz, ns), lambda i: (0, 0))
    res = pl.Buffered(1)
    wspec = lambda w: pl.BlockSpec(w.shape, lambda i: (0,) * w.ndim, pipeline_mode=res)
    aspec = pl.BlockSpec((None,) + prm["a1"].shape[1:], lambda i: (dirn, 0, 0, 0), pipeline_mode=res)
    common = [x_spec,
              pl.BlockSpec((None, None, 3, bsz, d), lambda i: (layer, 1, 0, 0, 0)),
              pl.BlockSpec((None, None, 1, d), lambda i: (layer, 1, 0, 0))]
    scratch = [pltpu.VMEM((bsz, ns), F32), pltpu.VMEM((tb * bsz, d), F32),
               pltpu.VMEM((ng, nch * bsz, S5_T * S5_GROUP), F32)]
    if not rev:
        return pl.pallas_call(
            _s5c_fwd_kernel,
            grid=(nblk,),
            in_specs=common + [wspec(prm["wst"][0]), aspec, aspec, st_spec],
            out_specs=[sin_spec, st_spec],
            out_shape=[jax.ShapeDtypeStruct((seq // S5_T * bsz, ns), F32),
                       jax.ShapeDtypeStruct((bsz, ns), F32)],
            scratch_shapes=scratch,
            compiler_params=_cparams("arbitrary"),
            name="s5_fwd_states",
        )(x3, mod_b, norm_w4, prm["wst"][0], prm["a1"], prm["a2"], h0)
    y, hfin = pl.pallas_call(
        _s5c_bwd_kernel,
        grid=(nblk,),
        in_specs=common + [sin_spec, wspec(prm["m"]), wspec(prm["wst"][1]), wspec(prm["wout"][0]),
                           wspec(prm["wout"][1]), aspec, aspec, st_spec,
                           pl.BlockSpec((1, d), lambda i: (0, 0))],
        out_specs=[x_spec, st_spec],
        out_shape=[jax.ShapeDtypeStruct((seq, bsz, d), F32), jax.ShapeDtypeStruct((bsz, ns), F32)],
        scratch_shapes=scratch + [pltpu.VMEM((nch * bsz, ns), F32)],
        compiler_params=_cparams("arbitrary"),
        name="s5_bwd_readout",
    )(x3, mod_b, norm_w4, sin_f, prm["m"], prm["wst"][1], prm["wout"][0], prm["wout"][1],
      prm["a1"], prm["a2"], h0, d_skip.reshape(1, d))
    return y.reshape(seq, bsz * d), hfin


def _s5c_glu(y2, x2, mod_a, layer, w_glu, b_glu, bsz):
    seq = x2.shape[0]
    d = x2.shape[1] // bsz
    tm = 512 if seq % 512 == 0 else 256
    row = pl.BlockSpec((tm, d), lambda t, b: (t, b))
    return pl.pallas_call(
        _s5c_glu_kernel,
        grid=(seq // tm, bsz),
        in_specs=[row, row,
                  pl.BlockSpec((None, None, None, 3, d), lambda t, b: (layer, b, 1, 0, 0)),
                  pl.BlockSpec(w_glu.shape, lambda t, b: (0, 0), pipeline_mode=pl.Buffered(1)),
                  pl.BlockSpec((1, 2 * d), lambda t, b: (0, 0))],
        out_specs=row,
        out_shape=jax.ShapeDtypeStruct((seq, bsz * d), F32),
        compiler_params=_cparams("parallel", "parallel"),
        name="s5_glu",
    )(y2, x2, mod_a, w_glu, b_glu.reshape(1, 2 * d))


def _s5c_mixer(xc2, x2, mods, norm_w4, layer, prm, extra, bsz, need_ctx):
    (modc_a, modc_b), (modx_a, modx_b) = mods
    d_skip, w_glu, b_glu = extra
    ns = prm["m"].shape[0] * LANES
    zero = jnp.zeros((bsz, ns), F32)
    sc_f, hf = _s5c_stream(xc2, modc_b, norm_w4, layer, prm, zero, bsz, rev=False)
    sx_f, _ = _s5c_stream(x2, modx_b, norm_w4, layer, prm, hf, bsz, rev=False)
    yc, hb = _s5c_stream(xc2, modc_b, norm_w4, layer, prm, zero, bsz, rev=True, sin_f=sc_f, d_skip=d_skip)
    yx, _ = _s5c_stream(x2, modx_b, norm_w4, layer, prm, hb, bsz, rev=True, sin_f=sx_f, d_skip=d_skip)
    x_new = _s5c_glu(yx, x2, modx_a, layer, w_glu, b_glu, bsz)
    xc_new = _s5c_glu(yc, xc2, modc_a, layer, w_glu, b_glu, bsz) if need_ctx else None
    return xc_new, x_new


def _level_matrix(rev):
    p = np.arange(CHUNK)
    v, r = p // SUBLANES, p % SUBLANES
    tau = GROUP * (v // SUBLANES) + SUBLANES * r + v % SUBLANES
    if rev:
        tau = CHUNK - 1 - tau
    ti, tj = tau[:, None], tau[None, :]
    x = ti ^ tj
    lvl = np.zeros((CHUNK, CHUNK), np.int32)
    for b in range(N_LEVELS):
        lvl = np.where((x >> b) & 1, b + 1, lvl)
    lvl = np.where(tj > ti, -1, lvl)
    return lvl.astype(np.int32)


def _scan_chunk_head(q, k, vb, g, st_t, masks, rev):
    dk = q.shape[1]
    ng, nv = N_GROUPS, SUBLANES
    ea = (lambda e: SUB - 1 - e) if rev else (lambda e: e)
    ksub = (lambda e: SUBLANES - 1 - e) if rev else (lambda e: e)

    def vregs(x, scale=None):
        out = [x[SUBLANES * ea(e):SUBLANES * ea(e) + SUBLANES, :] for e in range(SUB)]
        if scale is not None:
            out = [o * scale for o in out]
        return [out[nv * gi:nv * (gi + 1)] for gi in range(ng)]

    def assemble(vs):
        flat = [vs[gi][a] for gi in range(ng) for a in range(nv)]
        return jnp.concatenate([flat[ea(a)] for a in range(SUB)], axis=0)

    def zeros():
        return [[zero] * nv for _ in range(ng)]

    qv, kv, gv = vregs(q), vregs(k), vregs(g, LOG2E)
    zero = jnp.zeros((SUBLANES, dk), F32)
    ninf = jnp.full((SUBLANES, dk), -jnp.inf, F32)

    sub_i = lax.broadcasted_iota(jnp.int32, (SUBLANES, dk), 0)
    re = (SUBLANES - 1 - sub_i) if rev else sub_i

    def row(x, e):
        kk = ksub(e)
        return jnp.broadcast_to(x[kk:kk + 1, :], (SUBLANES, dk))

    pf, tot, ct, xcl, gtot = [], [], [], [], []
    for gi in range(ng):
        p = [gv[gi][0]]
        for a in range(1, nv):
            p.append(p[-1] + gv[gi][a])
        t = p[nv - 1]
        c = zero
        for e in range(SUBLANES):
            c = c + jnp.where(re >= e, row(t, e), 0.0)
        pf.append(p)
        tot.append(t)
        ct.append(c)
        xcl.append(c - t)
        gtot.append(row(c, SUBLANES - 1))

    q_lv, k_lv = [], []
    for lvl in range(1, N_FINE + 1):
        s = 1 << lvl
        qs, ks = zeros(), zeros()
        for gi in range(ng):
            for bs in range(0, nv, s):
                m = bs + s // 2
                ref = pf[gi][m - 1]
                for a in range(bs, m - 1):
                    ks[gi][a] = kv[gi][a] * jnp.exp2(ref - pf[gi][a])
                ks[gi][m - 1] = kv[gi][m - 1]
                for a in range(m, bs + s):
                    qs[gi][a] = qv[gi][a] * jnp.exp2(pf[gi][a] - ref)
        q_lv.append(assemble(qs))
        k_lv.append(assemble(ks))
    for lvl in range(N_FINE + 1, N_FINE + N_COARSE + 1):
        w = 1 << (lvl - N_FINE)
        right = (re & (w // 2)) != 0
        qs, ks = zeros(), zeros()
        for gi in range(ng):
            xm = zero
            for bs in range(0, SUBLANES, w):
                inblk = (re >= bs) & (re < bs + w)
                xm = xm + jnp.where(inblk, row(xcl[gi], bs + w // 2), 0.0)
            drq = jnp.where(right, xcl[gi] - xm, ninf)
            dlk = jnp.where(right, ninf, tot[gi] + xm - ct[gi])
            for a in range(nv):
                qs[gi][a] = qv[gi][a] * jnp.exp2(pf[gi][a] + drq)
                ks[gi][a] = kv[gi][a] * jnp.exp2(dlk - pf[gi][a])
        q_lv.append(assemble(qs))
        k_lv.append(assemble(ks))
    qs, ks = zeros(), zeros()
    for a in range(nv):
        ks[0][a] = kv[0][a] * jnp.exp2(tot[0] + (gtot[0] - ct[0]) - pf[0][a])
        qs[1][a] = qv[1][a] * jnp.exp2(pf[1][a] + xcl[1])
    q_lv.append(assemble(qs))
    k_lv.append(assemble(ks))

    sc = jnp.where(masks[0], jnp.sum(q * k, axis=-1, keepdims=True), 0.0)
    for lvl in range(1, N_LEVELS + 1):
        s_l = _dot_nt(q_lv[lvl - 1].astype(BF16), k_lv[lvl - 1].astype(BF16))
        sc = jnp.where(masks[lvl], s_l, sc)

    e0, e1 = jnp.exp2(gtot[0]), jnp.exp2(gtot[1])
    qc = zeros()
    kd = zeros()
    for a in range(nv):
        qc[0][a] = qv[0][a] * jnp.exp2(pf[0][a] + xcl[0])
        qc[1][a] = qs[1][a] * e0
        kd[0][a] = ks[0][a] * e1
        kd[1][a] = kv[1][a] * jnp.exp2(tot[1] + (gtot[1] - ct[1]) - pf[1][a])
    o = _dot(sc.astype(BF16), vb) + _dot_nt(assemble(qc).astype(BF16), st_t.astype(BF16))
    st_new = (e0 * e1)[0:1, :] * st_t + _dot_tn(vb, assemble(kd).astype(BF16))
    return o, st_new


def _scan_kernel(n_heads, dk, dv, qf_ref, kf_ref, vf_ref, gf_ref, qb_ref, kb_ref, vb_ref, gb_ref,
                 lv_ref, s0_ref, of_ref, ob_ref, sf_ref, st_ref):
    @pl.when(pl.program_id(1) == 0)
    def _():
        st_ref[...] = s0_ref[...]

    dirs = ((qf_ref, kf_ref, vf_ref, gf_ref, of_ref, False), (qb_ref, kb_ref, vb_ref, gb_ref, ob_ref, True))
    masks = [[lv_ref[d] == lvl for lvl in range(N_LEVELS + 1)] for d in range(2)]
    for h in range(n_heads):
        ks = slice(h * dk, (h + 1) * dk)
        vs = slice(h * dv, (h + 1) * dv)
        for d, (q_ref, k_ref, v_ref, g_ref, o_ref, rev) in enumerate(dirs):
            o, st_new = _scan_chunk_head(q_ref[:, ks], k_ref[:, ks], v_ref[:, vs], g_ref[:, ks],
                                         st_ref[d, vs, :], masks[d], rev)
            o_ref[:, vs] = o.astype(o_ref.dtype)
            st_ref[d, vs, :] = st_new
    sf_ref[...] = st_ref[...]


def _gated_scan(q, kf, kb, v, gf, gb, s0, n_heads, bsz):
    seq = q.shape[0]
    wk = q.shape[1] // bsz
    wv = v.shape[1] // bsz
    dk, dv = wk // n_heads, wv // n_heads
    nchunk = seq // CHUNK
    fwd = lambda b, n: (n, b)
    bwd = lambda b, n: (nchunk - 1 - n, b)
    kspec = lambda im: pl.BlockSpec((CHUNK, wk), im)
    vspec = lambda im: pl.BlockSpec((CHUNK, wv), im)
    sspec = pl.BlockSpec((2, None, wv, dk), lambda b, n: (0, b, 0, 0))
    lv = jnp.asarray(np.stack([_level_matrix(False), _level_matrix(True)]))
    return pl.pallas_call(
        functools.partial(_scan_kernel, n_heads, dk, dv),
        grid=(bsz, nchunk),
        in_specs=[kspec(fwd), kspec(fwd), vspec(fwd), kspec(fwd),
                  kspec(bwd), kspec(bwd), vspec(bwd), kspec(bwd),
                  pl.BlockSpec((2, CHUNK, CHUNK), lambda b, n: (0, 0, 0)), sspec],
        out_specs=[vspec(fwd), vspec(bwd), sspec],
        out_shape=[jax.ShapeDtypeStruct((seq, bsz * wv), F32),
                   jax.ShapeDtypeStruct((seq, bsz * wv), F32),
                   jax.ShapeDtypeStruct((2, bsz, wv, dk), F32)],
        scratch_shapes=[pltpu.VMEM((2, wv, dk), F32)],
        compiler_params=_cparams("parallel", "arbitrary"),
        name="gated_scan",
    )(q, kf, v, gf, q, kb, v, gb, lv, s0)


def _bidir_scan(ctx_in, lat_in, n_heads, bsz, need_ctx):
    qc, kfc, kbc, vc, gfc, gbc = ctx_in
    qx, kfx, kbx, vx, gfx, gbx = lat_in
    wk = qc.shape[1] // bsz
    wv = vc.shape[1] // bsz
    s0 = jnp.zeros((2, bsz, wv, wk // n_heads), F32)
    oc_f, oc_b, s_c = _gated_scan(qc, kfc, kbc, vc, gfc, gbc, s0, n_heads, bsz)
    ox_f, ox_b, _ = _gated_scan(qx, kfx, kbx, vx, gfx, gbx, s_c, n_heads, bsz)
    return ((oc_f, oc_b) if need_ctx else None), (ox_f, ox_b)


def _chunk_rows(c, colmajor):
    if colmajor:
        return lambda v: (SUBLANES * (v % SUBLANES) + N_GROUPS * c + v // SUBLANES, GROUP)
    return lambda v: (c * CHUNK + GROUP * (v // SUBLANES) + v % SUBLANES, SUBLANES)


def _permute_chunk_in(stage_ref, dst_ref, c, colmajor):
    rows = _chunk_rows(c, colmajor)
    for jl in range(stage_ref.shape[0]):
        for v in range(SUB):
            start, stride = rows(v)
            dst_ref[c * CHUNK + SUBLANES * v:c * CHUNK + SUBLANES * (v + 1), jl * LANES:(jl + 1) * LANES] = (
                stage_ref[jl, pl.ds(start, SUBLANES, stride=stride), :])


def _permute_chunk_out(val, stage_ref, c, colmajor):
    rows = _chunk_rows(c, colmajor)
    for jl in range(stage_ref.shape[0]):
        for v in range(SUB):
            start, stride = rows(v)
            stage_ref[jl, pl.ds(start, SUBLANES, stride=stride), :] = (
                val[SUBLANES * v:SUBLANES * (v + 1), jl * LANES:(jl + 1) * LANES])


def _stage_block(x_ref, stage_ref):
    n = stage_ref.shape[1]
    for jl in range(stage_ref.shape[0]):
        blk = x_ref[..., jl * LANES:(jl + 1) * LANES]
        stage_ref[jl] = blk.reshape(n, LANES)


def _unstage_block(stage_ref, o_ref):
    for jl in range(stage_ref.shape[0]):
        o_ref[..., jl * LANES:(jl + 1) * LANES] = stage_ref[jl].reshape(o_ref.shape[:-1] + (LANES,))


def _hgrn_epilogue(h, w_ref, lb, d):
    q = _silu(_dot(h, w_ref[:, :d]))
    v = _dot(h, w_ref[:, d:2 * d])
    zf = _dot(h, w_ref[:, 2 * d:3 * d])
    zb = _dot(h, w_ref[:, 3 * d:4 * d])
    og = _dot(h, w_ref[:, 4 * d:])
    lf, lbk = lb[0:1, :], lb[1:2, :]
    sf, snf = _sigmoid_pair(zf)
    sb, snb = _sigmoid_pair(zb)
    gf = jnp.log(lf + (1.0 - lf) * sf)
    gb = jnp.log(lbk + (1.0 - lbk) * sb)
    kf = (1.0 - lf) * snf
    kb = (1.0 - lbk) * snb
    return q, kf, kb, v, gf, gb, og


def _x_stream(x2, bsz, colmajor, tm=2 * CHUNK):
    seq, d = x2.shape[0], x2.shape[1] // bsz
    if colmajor:
        rows = seq // GRID_W
        assert rows == GROUP, "column-major chunks assume one grid column per 64-step group"
        return (x2.reshape(rows, GRID_W, bsz * d),
                pl.BlockSpec((rows, SUBLANES, d), lambda t, b: (0, t, b)), SUBLANES * rows)
    return x2, pl.BlockSpec((tm, d), lambda t, b: (t, b)), tm


def _load_chunks(x_ref, stage_ref, xs_ref, colmajor):
    _stage_block(x_ref, stage_ref)
    for c in range(xs_ref.shape[0] // CHUNK):
        _permute_chunk_in(stage_ref, xs_ref, c, colmajor)


def _store_chunks(xn, stage_ref, o_ref, colmajor):
    for c in range(xn.shape[0] // CHUNK):
        _permute_chunk_out(xn[c * CHUNK:(c + 1) * CHUNK, :], stage_ref, c, colmajor)
    _unstage_block(stage_ref, o_ref)


def _chunk_scratch(tm, d):
    return [pltpu.VMEM((d // LANES, tm, LANES), F32), pltpu.VMEM((tm, d), F32)]


def _hgrn_proj_kernel(x_ref, mod_ref, nw_ref, w_ref, lb_ref,
                      q_ref, kf_ref, kb_ref, v_ref, gf_ref, gb_ref, og_ref, stage_ref, xs_ref):
    d = xs_ref.shape[1]
    _load_chunks(x_ref, stage_ref, xs_ref, False)
    h = _norm_mod(xs_ref[...], nw_ref[...], mod_ref[0:1, :], mod_ref[1:2, :]).astype(BF16)
    outs = _hgrn_epilogue(h, w_ref, lb_ref[...], d)
    for ref, val in zip((q_ref, kf_ref, kb_ref, v_ref, gf_ref, gb_ref, og_ref), outs):
        ref[...] = val.astype(ref.dtype)


def _hgrn_project(x2, mod_a, norm_w4, layer, w_in, lb, bsz):
    seq = x2.shape[0]
    d = x2.shape[1] // bsz
    xv, xspec, tm = _x_stream(x2, bsz, False)
    row = pl.BlockSpec((tm, d), lambda t, b: (t, b))
    return pl.pallas_call(
        _hgrn_proj_kernel,
        grid=(seq // tm, bsz),
        in_specs=[
            xspec,
            pl.BlockSpec((None, None, None, 3, d), lambda t, b: (layer, b, 1, 0, 0)),
            pl.BlockSpec((None, None, 1, d), lambda t, b: (layer, 1, 0, 0)),
            pl.BlockSpec(w_in.shape, lambda t, b: (0, 0), pipeline_mode=pl.Buffered(1)),
            pl.BlockSpec(lb.shape, lambda t, b: (0, 0)),
        ],
        out_specs=[row] * 7,
        out_shape=[jax.ShapeDtypeStruct((seq, bsz * d), BF16 if i == 3 else F32) for i in range(7)],
        scratch_shapes=_chunk_scratch(tm, d),
        compiler_params=_cparams("parallel", "parallel"),
        name="hgrn_project",
    )(xv, mod_a, norm_w4, w_in, lb)


def _gla_epilogue(h, w_ref, wgate, bgate, dkt, dvt):
    low = _dot(h, w_ref[:, 2 * dkt + 2 * dvt:])
    lg = _log_sigmoid(_dot(low.astype(BF16), wgate) + bgate) * (1.0 / GLA_TAU)
    q = _dot(h, w_ref[:, :dkt]) * ((dkt // GLA_HEADS) ** -0.5)
    k = _dot(h, w_ref[:, dkt:2 * dkt])
    v = _dot(h, w_ref[:, 2 * dkt:2 * dkt + dvt])
    og = _dot(h, w_ref[:, 2 * dkt + dvt:2 * dkt + 2 * dvt])
    return q, k, v, og, lg[:, :dkt], lg[:, dkt:]


def _gla_proj_kernel(colmajor, x_ref, mod_ref, nw_ref, w_ref, wg_ref, bg_ref,
                     q_ref, k_ref, v_ref, og_ref, gf_ref, gb_ref, stage_ref, xs_ref):
    dkt = q_ref.shape[1]
    dvt = v_ref.shape[1]
    _load_chunks(x_ref, stage_ref, xs_ref, colmajor)
    h = _norm_mod(xs_ref[...], nw_ref[...], mod_ref[0:1, :], mod_ref[1:2, :]).astype(BF16)
    outs = _gla_epilogue(h, w_ref, wg_ref[...], bg_ref[...], dkt, dvt)
    for ref, val in zip((q_ref, k_ref, v_ref, og_ref, gf_ref, gb_ref), outs):
        ref[...] = val.astype(ref.dtype)


def _gla_project(x2, mod_a, norm_w4, layer, w_in, wgate, bgate, bsz, dkt, dvt, *, colmajor):
    seq = x2.shape[0]
    d = x2.shape[1] // bsz
    widths = (dkt, dkt, dvt, dvt, dkt, dkt)
    xv, xspec, tm = _x_stream(x2, bsz, colmajor)
    const = lambda t, b: (0, 0)
    return pl.pallas_call(
        functools.partial(_gla_proj_kernel, colmajor),
        grid=(seq // tm, bsz),
        in_specs=[
            xspec,
            pl.BlockSpec((None, None, None, 3, d), lambda t, b: (layer, b, 1, 0, 0)),
            pl.BlockSpec((None, None, 1, d), lambda t, b: (layer, 1, 0, 0)),
            pl.BlockSpec(w_in.shape, const, pipeline_mode=pl.Buffered(1)),
            pl.BlockSpec(wgate.shape, const),
            pl.BlockSpec(bgate.shape, const),
        ],
        out_specs=[pl.BlockSpec((tm, w), lambda t, b: (t, b)) for w in widths],
        out_shape=[jax.ShapeDtypeStruct((seq, bsz * w), BF16 if i == 2 else F32)
                   for i, w in enumerate(widths)],
        scratch_shapes=_chunk_scratch(tm, d),
        compiler_params=_cparams("parallel", "parallel"),
        name="gla_project",
    )(xv, mod_a, norm_w4, w_in, wgate, bgate)


def _head_out(of, ob, og, gnw, w_out, n_heads, zs_ref):
    dv = of.shape[1] // n_heads
    o = of + ob
    for h in range(n_heads):
        sl = slice(h * dv, (h + 1) * dv)
        zs_ref[:, sl] = (_rms(o[:, sl], gnw) * _silu(og[:, sl])).astype(BF16)
    return _dot(zs_ref[...], w_out)


def _mix_out_kernel(n_heads, colmajor, ff_chunks, of_ref, ob_ref, og_ref, x_ref, mod_ref, gnw_ref, w_ref,
                    *rest):
    if ff_chunks:
        modf_ref, nw_ref, wi_ref, wo_ref, o_ref, stage_ref, xs_ref, zs_ref = rest
    else:
        o_ref, stage_ref, xs_ref, zs_ref = rest
    _load_chunks(x_ref, stage_ref, xs_ref, colmajor)
    y = _head_out(of_ref[...], ob_ref[...], og_ref[...], gnw_ref[...], w_ref[...], n_heads, zs_ref)
    xn = xs_ref[...] + mod_ref[2:3, :] * y
    if ff_chunks:
        ff = wo_ref.shape[0]
        h = _norm_mod(xn, nw_ref[...], modf_ref[0:1, :], modf_ref[1:2, :]).astype(BF16)
        acc = None
        for lo, hi in ff_chunks:
            g = _dot(h, wi_ref[:, lo:hi])
            u = _dot(h, wi_ref[:, ff + lo:ff + hi])
            part = _dot((_silu(g) * u).astype(BF16), wo_ref[lo:hi, :])
            acc = part if acc is None else acc + part
        xn = xn + (0.5 * modf_ref[2:3, :]) * acc
    _store_chunks(xn, stage_ref, o_ref, colmajor)


def _mix_out(of, ob, og, x2, mod_a, layer, gnw, w_out, n_heads, bsz, *, colmajor, ffn=None):
    seq = x2.shape[0]
    d = x2.shape[1] // bsz
    wv = of.shape[1] // bsz
    gnw = gnw.reshape(1, -1)
    xv, xspec, tm = _x_stream(x2, bsz, colmajor, 4 * CHUNK if ffn else 2 * CHUNK)
    ospec = pl.BlockSpec((tm, wv), lambda t, b: (t, b))
    const = lambda t, b: (0, 0)
    in_specs = [ospec, ospec, ospec, xspec,
                pl.BlockSpec((None, None, None, 3, d), lambda t, b: (layer, b, 1, 0, 0)),
                pl.BlockSpec(gnw.shape, const),
                pl.BlockSpec(w_out.shape, const, pipeline_mode=pl.Buffered(1))]
    args = [of, ob, og, xv, mod_a, gnw, w_out]
    ff_chunks = None
    if ffn:
        norm_w4, w_in_f, w_out_f = ffn
        ff = w_out_f.shape[2]
        ff_chunks = _ff_chunks(ff, FFN_CHUNK)
        in_specs += [
            pl.BlockSpec((None, None, None, 3, d), lambda t, b: (layer, b, 2, 0, 0)),
            pl.BlockSpec((None, None, 1, d), lambda t, b: (layer, 2, 0, 0)),
            pl.BlockSpec((None, None, d, 2 * ff), lambda t, b: (layer, 1, 0, 0), pipeline_mode=pl.Buffered(1)),
            pl.BlockSpec((None, None, ff, d), lambda t, b: (layer, 1, 0, 0), pipeline_mode=pl.Buffered(1)),
        ]
        args += [mod_a, norm_w4, w_in_f, w_out_f]
    out = pl.pallas_call(
        functools.partial(_mix_out_kernel, n_heads, colmajor, ff_chunks),
        grid=(seq // tm, bsz),
        in_specs=in_specs,
        out_specs=xspec,
        out_shape=jax.ShapeDtypeStruct(xv.shape, F32),
        scratch_shapes=_chunk_scratch(tm, d) + [pltpu.VMEM((tm, wv), BF16)],
        compiler_params=_cparams("parallel", "parallel"),
        name="mix_out_ffn" if ffn else "mix_out",
    )(*args)
    return out.reshape(seq, bsz * d)


def _gla_mixer(xc2, x2, mods, norm_w4, layer, prm, bsz, need_ctx, ffn=None):
    (modc_a, _), (modx_a, _) = mods
    w_in, wgate, bgate, gnw, w_out, dkt, dvt = prm
    qc, kc, vc, ogc, gfc, gbc = _gla_project(xc2, modc_a, norm_w4, layer, w_in, wgate, bgate, bsz,
                                             dkt, dvt, colmajor=False)
    qx, kx, vx, ogx, gfx, gbx = _gla_project(x2, modx_a, norm_w4, layer, w_in, wgate, bgate, bsz,
                                             dkt, dvt, colmajor=True)
    oc, ox = _bidir_scan((qc, kc, kc, vc, gfc, gbc), (qx, kx, kx, vx, gfx, gbx), GLA_HEADS, bsz, need_ctx)
    x_new = _mix_out(ox[0], ox[1], ogx, x2, modx_a, layer, gnw, w_out, GLA_HEADS, bsz, colmajor=True, ffn=ffn)
    xc_new = None
    if need_ctx:
        xc_new = _mix_out(oc[0], oc[1], ogc, xc2, modc_a, layer, gnw, w_out, GLA_HEADS, bsz, colmajor=False)
    return xc_new, x_new


def _hgrn_mixer(xc2, x2, mods, norm_w4, layer, prm, bsz, need_ctx, ffn=None):
    (modc_a, _), (modx_a, _) = mods
    w_in, lb, gnw, w_out, n_heads = prm
    qc, kfc, kbc, vc, gfc, gbc, ogc = _hgrn_project(xc2, modc_a, norm_w4, layer, w_in, lb, bsz)
    qx, kfx, kbx, vx, gfx, gbx, ogx = _hgrn_project(x2, modx_a, norm_w4, layer, w_in, lb, bsz)
    oc, ox = _bidir_scan((qc, kfc, kbc, vc, gfc, gbc), (qx, kfx, kbx, vx, gfx, gbx), n_heads, bsz, need_ctx)
    x_new = _mix_out(ox[0], ox[1], ogx, x2, modx_a, layer, gnw, w_out, n_heads, bsz, colmajor=False, ffn=ffn)
    xc_new = None
    if need_ctx:
        xc_new = _mix_out(oc[0], oc[1], ogc, xc2, modc_a, layer, gnw, w_out, n_heads, bsz, colmajor=False)
    return xc_new, x_new


def _forward(x, c, ctx, c_ctx, ada_w, ada_b, norm_w, ffn_w_in, ffn_w_out,
             s5_a_re, s5_a_im, s5_log_step, s5_b_re, s5_b_im, s5_c_re, s5_c_im, s5_d,
             s5_w_glu, s5_b_glu, gla_w_in, gla_w_gate, gla_b_gate, gla_norm_w, gla_w_out,
             hgrn_w_in, hgrn_lb_logits, hgrn_norm_w, hgrn_w_out, final_norm_w, depth=None):
    depth = ada_w.shape[0] if depth is None else depth
    bsz, seq, d = x.shape
    mods_x, mods_c = _ada(c, c_ctx, ada_w, ada_b)
    norm_w4 = norm_w.reshape(norm_w.shape[0], 3, 1, d)
    w_in = ffn_w_in.astype(BF16)
    w_out = ffn_w_out.astype(BF16)
    lb_soft = jax.nn.softmax(hgrn_lb_logits.astype(F32), axis=0)
    lb_all = jnp.cumsum(lb_soft, axis=0) - lb_soft[0]

    xc2 = jnp.transpose(ctx, (1, 0, 2)).reshape(ctx.shape[1], bsz * d)
    x2 = x
    for i in range(depth):
        last = i == depth - 1
        kind, j = i % N_MIXERS, i // N_MIXERS
        x2 = _ffn(x2, mods_x[0], norm_w4, w_in, w_out, i, 0, bsz, in_bld=(i == 0))
        xc2 = _ffn(xc2, mods_c[0], norm_w4, w_in, w_out, i, 0, bsz)
        mods = (mods_c, mods_x)
        fused = None if last else (norm_w4, w_in, w_out)
        if kind == 0:
            prm = _s5c_prep(s5_a_re[j], s5_a_im[j], s5_log_step[j], s5_b_re[j], s5_b_im[j],
                            s5_c_re[j], s5_c_im[j])
            extra = (s5_d[j], s5_w_glu[j].astype(BF16), s5_b_glu[j])
            yc, x2 = _s5c_mixer(xc2, x2, mods, norm_w4, i, prm, extra, bsz, not last)
        elif kind == 1:
            dkt = d // 2
            dvt = d
            n_low = 2 * GLA_RANK
            pad = LANES - n_low
            wi = jnp.pad(gla_w_in[j], ((0, 0), (0, pad))).astype(BF16)
            wg = jnp.zeros((LANES, 2 * dkt), F32)
            wg = wg.at[:GLA_RANK, :dkt].set(gla_w_gate[j, 0]).at[GLA_RANK:n_low, dkt:].set(gla_w_gate[j, 1])
            prm = (wi, wg.astype(BF16), gla_b_gate[j].reshape(1, 2 * dkt), gla_norm_w[j],
                   gla_w_out[j].astype(BF16), dkt, dvt)
            yc, x2 = _gla_mixer(xc2, x2, mods, norm_w4, i, prm, bsz, not last, ffn=fused)
        else:
            prm = (hgrn_w_in[j].astype(BF16), lb_all[i], hgrn_norm_w[j], hgrn_w_out[j].astype(BF16),
                   d // HGRN_DK)
            yc, x2 = _hgrn_mixer(xc2, x2, mods, norm_w4, i, prm, bsz, not last, ffn=fused)
        if not (fused and kind != 0):
            x2 = _ffn(x2, mods_x[0], norm_w4, w_in, w_out, i, 2, bsz,
                      final_w=final_norm_w.reshape(1, d) if last else None)
        if not last:
            xc2 = _ffn(yc, mods_c[0], norm_w4, w_in, w_out, i, 2, bsz)
    return x2


def kernel(x, c, ctx, c_ctx, ada_w, ada_b, norm_w, ffn_w_in, ffn_w_out, s5_a_re, s5_a_im, s5_log_step,
           s5_b_re, s5_b_im, s5_c_re, s5_c_im, s5_d, s5_w_glu, s5_b_glu, gla_w_in, gla_w_gate,
           gla_b_gate, gla_norm_w, gla_w_out, hgrn_w_in, hgrn_lb_logits, hgrn_norm_w, hgrn_w_out,
           final_norm_w):
    return _forward(x, c, ctx, c_ctx, ada_w, ada_b, norm_w, ffn_w_in, ffn_w_out, s5_a_re, s5_a_im,
                    s5_log_step, s5_b_re, s5_b_im, s5_c_re, s5_c_im, s5_d, s5_w_glu, s5_b_glu,
                    gla_w_in, gla_w_gate, gla_b_gate, gla_norm_w, gla_w_out, hgrn_w_in,
                    hgrn_lb_logits, hgrn_norm_w, hgrn_w_out, final_norm_w)
```

```python
import functools
import math

import numpy as np
import jax
import jax.numpy as jnp
from jax import lax
from jax.experimental import pallas as pl
from jax.experimental.pallas import tpu as pltpu

F32 = jnp.float32
BF16 = jnp.bfloat16

NORM_EPS = 1e-6
GRID_W = 64
N_MOD = 9
N_MIXERS = 3

S5_GROUP = 16
S5_STATE = 64
S5_LANE_GROUPS = 8

GLA_HEADS = 4
GLA_RANK = 16
GLA_TAU = 16.0
HGRN_DK = 128

SUBLANES = 8
LANES = 128
CHUNK = 128
GROUP = SUBLANES * SUBLANES
N_GROUPS = CHUNK // GROUP
SUB = CHUNK // SUBLANES
N_FINE = 3
N_COARSE = 3
N_LEVELS = 7
LOG2E = 1.4426950408889634

FFN_CHUNK = 768

VMEM_LIMIT_BYTES = 56 * 1024 * 1024


def _cparams(*sem):
    return pltpu.CompilerParams(dimension_semantics=sem, vmem_limit_bytes=VMEM_LIMIT_BYTES)


def _dot(a, b):
    return jnp.dot(a, b, preferred_element_type=F32)


def _dot_nt(a, b):
    return lax.dot_general(a, b, (((1,), (1,)), ((), ())), preferred_element_type=F32)


def _dot_tn(a, b):
    return lax.dot_general(a, b, (((0,), (0,)), ((), ())), preferred_element_type=F32)


def _sigmoid(x):
    return 0.5 * jnp.tanh(0.5 * x) + 0.5


def _sigmoid_pair(x):
    t = jnp.exp(-jnp.abs(x))
    r = 1.0 / (1.0 + t)
    tr = t * r
    pos = x >= 0.0
    return jnp.where(pos, r, tr), jnp.where(pos, tr, r)


def _silu(x):
    return x * _sigmoid(x)


def _gelu_tanh(x):
    c = math.sqrt(2.0 / math.pi)
    return 0.5 * x * (1.0 + jnp.tanh(c * (x + 0.044715 * (x * x * x))))


def _log_sigmoid(x):
    return jnp.minimum(x, 0.0) - jnp.log1p(jnp.exp(-jnp.abs(x)))


def _rms(x, w):
    ms = jnp.mean(x * x, axis=-1, keepdims=True)
    return (x * lax.rsqrt(ms + NORM_EPS)) * w


def _norm_mod(x, w, shift, scale):
    return _rms(x, w) * (1.0 + scale) + shift


def _ada_kernel(cc_ref, w_ref, b_ref, o_ref):
    a = _silu(cc_ref[...]).astype(BF16)
    o_ref[...] = _dot(a, w_ref[...].astype(BF16)) + b_ref[...]


def _ada(c, c_ctx, ada_w, ada_b):
    depth, d, nd = ada_w.shape
    bsz = c.shape[0]
    rows = ((bsz + 1 + SUBLANES - 1) // SUBLANES) * SUBLANES
    cc = jnp.zeros((rows, d), F32).at[:bsz].set(c).at[bsz].set(c_ctx)
    nblk = nd // d
    mod = pl.pallas_call(
        _ada_kernel,
        grid=(depth, nblk),
        in_specs=[
            pl.BlockSpec((rows, d), lambda i, n: (0, 0)),
            pl.BlockSpec((None, d, d), lambda i, n: (i, 0, n)),
            pl.BlockSpec((None, 1, d), lambda i, n: (i, 0, n)),
        ],
        out_specs=pl.BlockSpec((None, rows, d), lambda i, n: (i, 0, n)),
        out_shape=jax.ShapeDtypeStruct((depth, rows, nd), F32),
        compiler_params=_cparams("arbitrary", "arbitrary"),
        name="ada_mod",
    )(cc, ada_w, ada_b.reshape(depth, 1, nd))
    mx = mod[:, :bsz].reshape(depth, bsz, 3, 3, d)
    mc = jnp.broadcast_to(mod[:, bsz:bsz + 1], (depth, bsz, nd)).reshape(depth, bsz, 3, 3, d)
    return (mx, jnp.transpose(mx, (0, 2, 3, 1, 4))), (mc, jnp.transpose(mc, (0, 2, 3, 1, 4)))


def _ffn_kernel(chunks, final, x_ref, mod_ref, nw_ref, wi_ref, wo_ref, *rest):
    if final:
        fw_ref, o_ref = rest
    else:
        (o_ref,) = rest
    ff = wo_ref.shape[0]
    x = x_ref[...]
    h = _norm_mod(x, nw_ref[...], mod_ref[0:1, :], mod_ref[1:2, :]).astype(BF16)
    acc = None
    for lo, hi in chunks:
        g = _dot(h, wi_ref[:, lo:hi])
        u = _dot(h, wi_ref[:, ff + lo:ff + hi])
        part = _dot((_silu(g) * u).astype(BF16), wo_ref[lo:hi, :])
        acc = part if acc is None else acc + part
    y = x + (0.5 * mod_ref[2:3, :]) * acc
    if final:
        y = _rms(y, fw_ref[...])
    o_ref[...] = y


def _ff_chunks(ff, width):
    assert ff % LANES == 0 and width % LANES == 0
    return tuple((lo, min(lo + width, ff)) for lo in range(0, ff, width))


def _ffn(x, mod_a, norm_w4, w_in, w_out, layer, stage, bsz, *, in_bld=False, final_w=None):
    d = norm_w4.shape[-1]
    seq = x.shape[1] if in_bld else x.shape[0]
    ff = w_out.shape[2]
    tm = 512 if seq % 512 == 0 else 256
    s = 0 if stage == 0 else 1
    final = final_w is not None
    resident = pl.Buffered(1)
    if in_bld:
        x_spec = pl.BlockSpec((None, tm, d), lambda t, b: (b, t, 0))
    else:
        x_spec = pl.BlockSpec((tm, d), lambda t, b: (t, b))
    in_specs = [
        x_spec,
        pl.BlockSpec((None, None, None, 3, d), lambda t, b: (layer, b, stage, 0, 0)),
        pl.BlockSpec((None, None, 1, d), lambda t, b: (layer, stage, 0, 0)),
        pl.BlockSpec((None, None, d, 2 * ff), lambda t, b: (layer, s, 0, 0), pipeline_mode=resident),
        pl.BlockSpec((None, None, ff, d), lambda t, b: (layer, s, 0, 0), pipeline_mode=resident),
    ]
    args = [x, mod_a, norm_w4, w_in, w_out]
    if final:
        in_specs.append(pl.BlockSpec((1, d), lambda t, b: (0, 0)))
        args.append(final_w)
        out_spec = pl.BlockSpec((None, tm, d), lambda t, b: (b, t, 0))
        out_shape = jax.ShapeDtypeStruct((bsz, seq, d), F32)
    else:
        out_spec = pl.BlockSpec((tm, d), lambda t, b: (t, b))
        out_shape = jax.ShapeDtypeStruct((seq, bsz * d), F32)
    return pl.pallas_call(
        functools.partial(_ffn_kernel, _ff_chunks(ff, FFN_CHUNK), final),
        grid=(seq // tm, bsz),
        in_specs=in_specs,
        out_specs=out_spec,
        out_shape=out_shape,
        compiler_params=_cparams("parallel", "parallel"),
        name="ffn_half",
    )(*args)


S5_T = 16
S5C_TB = 128
S5C_TB_FWD = 256
SLOT = LANES // SUBLANES


def _cmul(ar, ai, br, bi):
    return ar * br - ai * bi, ar * bi + ai * br


def _s5c_prep(a_re, a_im, log_step, b_re, b_im, c_re, c_im):
    dt = jnp.exp(log_step.astype(F32))[..., None]
    a_re = a_re.astype(F32)
    a_im = a_im.astype(F32)
    mag = jnp.exp(a_re * dt)
    abar_re = mag * jnp.cos(a_im * dt)
    abar_im = mag * jnp.sin(a_im * dt)
    den = a_re * a_re + a_im * a_im
    zr = abar_re - 1.0
    zi = abar_im
    coef_re = ((zr * a_re + zi * a_im) / den)[..., None]
    coef_im = ((zi * a_re - zr * a_im) / den)[..., None]
    b_re = b_re.astype(F32)
    b_im = b_im.astype(F32)
    bb_re = coef_re * b_re - coef_im * b_im
    bb_im = coef_re * b_im + coef_im * b_re
    cc_re = c_re.astype(F32)
    cc_im = c_im.astype(F32)
    t = S5_T
    pr, pi = [jnp.ones_like(abar_re)], [jnp.zeros_like(abar_im)]
    for _ in range(t):
        nr, ni = _cmul(pr[-1], pi[-1], abar_re, abar_im)
        pr.append(nr)
        pi.append(ni)
    pr, pi = jnp.stack(pr), jnp.stack(pi)
    hi = lax.Precision.HIGHEST
    g, p, gc = abar_re.shape[1], abar_re.shape[2], b_re.shape[-1]
    prk = jnp.transpose(pr, (1, 2, 3, 0))
    pik = jnp.transpose(pi, (1, 2, 3, 0))
    e_re, e_im = _cmul(prk[..., None], pik[..., None], bb_re[:, :, :, None, :], bb_im[:, :, :, None, :])
    nkc = (t + 1) * gc
    taps = (jnp.einsum('dgop,dgpn->dgon', cc_re, e_re.reshape(2, g, p, nkc), precision=hi)
            - jnp.einsum('dgop,dgpn->dgon', cc_im, e_im.reshape(2, g, p, nkc), precision=hi))
    taps = taps.reshape(2, g, gc, t + 1, gc)
    kf, kb = taps[0], taps[1]
    seq = jnp.concatenate([jnp.flip(kb[:, :, 1:t], axis=2), kf[:, :, :1] + kb[:, :, :1], kf[:, :, 1:t]],
                          axis=2)
    tab = jnp.transpose(seq, (0, 3, 2, 1)).reshape(g, gc, (2 * t - 1) * gc)
    m = jnp.stack([tab[:, :, gc * (t - 1 - s):gc * (t - 1 - s) + t * gc] for s in range(t)], axis=1)
    m = m.reshape(g, t * gc, t * gc)

    def wst(d, flip):
        er, ei = e_re[d, :, :, :t], e_im[d, :, :, :t]
        if flip:
            er, ei = jnp.flip(er, 2), jnp.flip(ei, 2)
        w = jnp.concatenate([jnp.transpose(er, (0, 2, 3, 1)), jnp.transpose(ei, (0, 2, 3, 1))], axis=-1)
        return w.reshape(g, t * gc, 2 * p)

    wst_f = wst(0, True)
    wst_b = wst(1, False)

    def wout(d, flip):
        qr, qi = prk[d, :, :, 1:t + 1], pik[d, :, :, 1:t + 1]
        if flip:
            qr, qi = jnp.flip(qr, 2), jnp.flip(qi, 2)
        cr = jnp.transpose(cc_re[d], (0, 2, 1))
        ci = jnp.transpose(cc_im[d], (0, 2, 1))
        gr, gi = _cmul(cr[:, :, None, :], ci[:, :, None, :], qr[..., None], qi[..., None])
        return jnp.concatenate([gr, -gi], axis=1).reshape(g, 2 * p, t * gc)

    wout_f = wout(0, False)
    wout_b = wout(1, True)
    a1 = jnp.concatenate([pr[t], pr[t]], axis=-1)
    a2 = jnp.concatenate([-pi[t], pi[t]], axis=-1)
    bc = lambda a: jnp.broadcast_to(a[:, :, None, :], a.shape[:2] + (SUBLANES, a.shape[-1]))
    return dict(m=m.astype(BF16), wst=(wst_f.astype(BF16), wst_b.astype(BF16)),
                wout=(wout_f.astype(BF16), wout_b.astype(BF16)), a1=bc(a1), a2=bc(a2))


def _slot_transpose(x):
    m, n, bsz, _ = x.shape
    slot = lax.broadcasted_iota(jnp.int32, (bsz, LANES), 1) // SLOT

    def rot(v, shift):
        return pltpu.roll(v.reshape(-1, LANES), shift, 1).reshape(v.shape)

    d = n // 2
    while d >= 1:
        keep = (slot & d) == 0
        x = x.reshape(m, n // (2 * d), 2, d, bsz, LANES)
        lo, hi = x[:, :, 0], x[:, :, 1]
        new_lo = jnp.where(keep, lo, rot(hi, SLOT * d))
        new_hi = jnp.where(keep, rot(lo, LANES - SLOT * d), hi)
        x = jnp.stack([new_lo, new_hi], axis=2).reshape(m, n, bsz, LANES)
        d //= 2
    return x


def _s5c_gather(h_s, a_s, nch):
    nj = h_s.shape[1] // LANES
    nh = S5_T // SUBLANES
    for j in range(nj):
        x = h_s[:, j * LANES:(j + 1) * LANES].reshape(nch * nh, SUBLANES, SUBLANES, LANES)
        r = _slot_transpose(x).reshape(nch, nh, SUBLANES, SUBLANES, LANES)
        for gl in range(S5_LANE_GROUPS):
            for hh in range(nh):
                a_s[S5_LANE_GROUPS * j + gl, :, hh * LANES:(hh + 1) * LANES] = (
                    r[:, hh, gl].reshape(nch * SUBLANES, LANES))


def _s5c_state_scan(v, st_ref, sin_ref, a1, a2, g, nch, rev):
    gl = slice(g * LANES, (g + 1) * LANES)
    s = st_ref[:, gl]
    w = pltpu.roll(s, LANES // 2, 1)
    for n in (range(nch - 1, -1, -1) if rev else range(nch)):
        rows = slice(SUBLANES * n, SUBLANES * (n + 1))
        sin_ref[rows, gl] = s
        vn = v[rows, :]
        s, w = a1 * s + a2 * w + vn, a1 * w - a2 * s + pltpu.roll(vn, LANES // 2, 1)
    st_ref[:, gl] = s


def _s5c_norm(x_ref, mod_ref, nw_ref, h_s):
    x3 = x_ref[...]
    ms = jnp.mean(x3 * x3, axis=-1, keepdims=True)
    h3 = (x3 * lax.rsqrt(ms + NORM_EPS)) * nw_ref[...] * (1.0 + mod_ref[1]) + mod_ref[0]
    h_s[...] = h3.reshape(h_s.shape)


def _s5c_fwd_kernel(x_ref, mod_ref, nw_ref, wst_ref, a1_ref, a2_ref, h0_ref,
                    sin_ref, hfin_ref, st_ref, h_s, a_s):
    ng = wst_ref.shape[0]
    nch = a_s.shape[1] // SUBLANES

    @pl.when(pl.program_id(0) == 0)
    def _():
        st_ref[...] = h0_ref[...]

    _s5c_norm(x_ref, mod_ref, nw_ref, h_s)
    _s5c_gather(h_s, a_s, nch)
    for g in range(ng):
        v = _dot(a_s[g].astype(BF16), wst_ref[g])
        _s5c_state_scan(v, st_ref, sin_ref, a1_ref[g], a2_ref[g], g, nch, False)
    hfin_ref[...] = st_ref[...]


def _s5c_bwd_kernel(x_ref, mod_ref, nw_ref, sinf_ref, m_ref, wst_ref, woutf_ref, woutb_ref,
                    a1_ref, a2_ref, h0_ref, dsk_ref, y_ref, hfin_ref, st_ref, h_s, a_s, sinb_s):
    ng = m_ref.shape[0]
    nch = a_s.shape[1] // SUBLANES

    @pl.when(pl.program_id(0) == 0)
    def _():
        st_ref[...] = h0_ref[...]

    _s5c_norm(x_ref, mod_ref, nw_ref, h_s)
    _s5c_gather(h_s, a_s, nch)
    for j in range(ng // S5_LANE_GROUPS):
        ys = []
        for gl in range(S5_LANE_GROUPS):
            g = S5_LANE_GROUPS * j + gl
            lanes = slice(g * LANES, (g + 1) * LANES)
            a = a_s[g].astype(BF16)
            _s5c_state_scan(_dot(a, wst_ref[g]), st_ref, sinb_s, a1_ref[g], a2_ref[g], g, nch, True)
            ys.append(_dot(a, m_ref[g])
                      + _dot(sinf_ref[:, lanes].astype(BF16), woutf_ref[g])
                      + _dot(sinb_s[:, lanes].astype(BF16), woutb_ref[g]))
        cols = slice(j * LANES, (j + 1) * LANES)
        for hh in range(S5_T // SUBLANES):
            r = jnp.stack([y[:, hh * LANES:(hh + 1) * LANES].reshape(nch, SUBLANES, LANES) for y in ys],
                          axis=1)
            out = _slot_transpose(r)
            for n in range(nch):
                t0 = S5_T * n + SUBLANES * hh
                skip = dsk_ref[:, cols] * h_s[t0 * SUBLANES:(t0 + SUBLANES) * SUBLANES, cols]
                y_ref[t0:t0 + SUBLANES, :, cols] = out[n] + skip.reshape(SUBLANES, SUBLANES, LANES)
    hfin_ref[...] = st_ref[...]


def _s5c_glu_kernel(y_ref, x_ref, mod_ref, w_ref, b_ref, o_ref):
    d = x_ref.shape[1]
    z = _dot(_gelu_tanh(y_ref[...]).astype(BF16), w_ref[...]) + b_ref[...]
    o_ref[...] = x_ref[...] + mod_ref[2:3, :] * (z[:, :d] * _sigmoid(z[:, d:]))


def _s5c_stream(x2, mod_b, norm_w4, layer, prm, h0, bsz, *, rev, sin_f=None, d_skip=None):
    seq = x2.shape[0]
    d = x2.shape[1] // bsz
    tb = S5C_TB if rev else S5C_TB_FWD
    nblk = seq // tb
    nch = tb // S5_T
    dirn = 1 if rev else 0
    ng = prm["m"].shape[0]
    ns = ng * LANES
    x3 = x2.reshape(seq, bsz, d)
    blk = (lambda i: (nblk - 1 - i, 0, 0)) if rev else (lambda i: (i, 0, 0))
    blk2 = (lambda i: (nblk - 1 - i, 0)) if rev else (lambda i: (i, 0))
    x_spec = pl.BlockSpec((tb, bsz, d), blk)
    sin_spec = pl.BlockSpec((nch * bsz, ns), blk2)
    st_spec = pl.BlockSpec((bsz, ns), lambda i: (0, 0))
    res = pl.Buffered(1)
    wspec = lambda w: pl.BlockSpec(w.shape, lambda i: (0,) * w.ndim, pipeline_mode=res)
    aspec = pl.BlockSpec((None,) + prm["a1"].shape[1:], lambda i: (dirn, 0, 0, 0), pipeline_mode=res)
    common = [x_spec,
              pl.BlockSpec((None, None, 3, bsz, d), lambda i: (layer, 1, 0, 0, 0)),
              pl.BlockSpec((None, None, 1, d), lambda i: (layer, 1, 0, 0))]
    scratch = [pltpu.VMEM((bsz, ns), F32), pltpu.VMEM((tb * bsz, d), F32),
               pltpu.VMEM((ng, nch * bsz, S5_T * S5_GROUP), F32)]
    if not rev:
        return pl.pallas_call(
            _s5c_fwd_kernel,
            grid=(nblk,),
            in_specs=common + [wspec(prm["wst"][0]), aspec, aspec, st_spec],
            out_specs=[sin_spec, st_spec],
            out_shape=[jax.ShapeDtypeStruct((seq // S5_T * bsz, ns), F32),
                       jax.ShapeDtypeStruct((bsz, ns), F32)],
            scratch_shapes=scratch,
            compiler_params=_cparams("arbitrary"),
            name="s5_fwd_states",
        )(x3, mod_b, norm_w4, prm["wst"][0], prm["a1"], prm["a2"], h0)
    y, hfin = pl.pallas_call(
        _s5c_bwd_kernel,
        grid=(nblk,),
        in_specs=common + [sin_spec, wspec(prm["m"]), wspec(prm["wst"][1]), wspec(prm["wout"][0]),
                           wspec(prm["wout"][1]), aspec, aspec, st_spec,
                           pl.BlockSpec((1, d), lambda i: (0, 0))],
        out_specs=[x_spec, st_spec],
        out_shape=[jax.ShapeDtypeStruct((seq, bsz, d), F32), jax.ShapeDtypeStruct((bsz, ns), F32)],
        scratch_shapes=scratch + [pltpu.VMEM((nch * bsz, ns), F32)],
        compiler_params=_cparams("arbitrary"),
        name="s5_bwd_readout",
    )(x3, mod_b, norm_w4, sin_f, prm["m"], prm["wst"][1], prm["wout"][0], prm["wout"][1],
      prm["a1"], prm["a2"], h0, d_skip.reshape(1, d))
    return y.reshape(seq, bsz * d), hfin


def _s5c_glu(y2, x2, mod_a, layer, w_glu, b_glu, bsz):
    seq = x2.shape[0]
    d = x2.shape[1] // bsz
    tm = 512 if seq % 512 == 0 else 256
    row = pl.BlockSpec((tm, d), lambda t, b: (t, b))
    return pl.pallas_call(
        _s5c_glu_kernel,
        grid=(seq // tm, bsz),
        in_specs=[row, row,
                  pl.BlockSpec((None, None, None, 3, d), lambda t, b: (layer, b, 1, 0, 0)),
                  pl.BlockSpec(w_glu.shape, lambda t, b: (0, 0), pipeline_mode=pl.Buffered(1)),
                  pl.BlockSpec((1, 2 * d), lambda t, b: (0, 0))],
        out_specs=row,
        out_shape=jax.ShapeDtypeStruct((seq, bsz * d), F32),
        compiler_params=_cparams("parallel", "parallel"),
        name="s5_glu",
    )(y2, x2, mod_a, w_glu, b_glu.reshape(1, 2 * d))


def _s5c_mixer(xc2, x2, mods, norm_w4, layer, prm, extra, bsz, need_ctx):
    (modc_a, modc_b), (modx_a, modx_b) = mods
    d_skip, w_glu, b_glu = extra
    ns = prm["m"].shape[0] * LANES
    zero = jnp.zeros((bsz, ns), F32)
    sc_f, hf = _s5c_stream(xc2, modc_b, norm_w4, layer, prm, zero, bsz, rev=False)
    sx_f, _ = _s5c_stream(x2, modx_b, norm_w4, layer, prm, hf, bsz, rev=False)
    yc, hb = _s5c_stream(xc2, modc_b, norm_w4, layer, prm, zero, bsz, rev=True, sin_f=sc_f, d_skip=d_skip)
    yx, _ = _s5c_stream(x2, modx_b, norm_w4, layer, prm, hb, bsz, rev=True, sin_f=sx_f, d_skip=d_skip)
    x_new = _s5c_glu(yx, x2, modx_a, layer, w_glu, b_glu, bsz)
    xc_new = _s5c_glu(yc, xc2, modc_a, layer, w_glu, b_glu, bsz) if need_ctx else None
    return xc_new, x_new


def _level_matrix(rev):
    p = np.arange(CHUNK)
    v, r = p // SUBLANES, p % SUBLANES
    tau = GROUP * (v // SUBLANES) + SUBLANES * r + v % SUBLANES
    if rev:
        tau = CHUNK - 1 - tau
    ti, tj = tau[:, None], tau[None, :]
    x = ti ^ tj
    lvl = np.zeros((CHUNK, CHUNK), np.int32)
    for b in range(N_LEVELS):
        lvl = np.where((x >> b) & 1, b + 1, lvl)
    lvl = np.where(tj > ti, -1, lvl)
    return lvl.astype(np.int32)


def _scan_chunk_head(q, k, vb, g, st_t, masks, rev):
    dk = q.shape[1]
    ng, nv = N_GROUPS, SUBLANES
    ea = (lambda e: SUB - 1 - e) if rev else (lambda e: e)
    ksub = (lambda e: SUBLANES - 1 - e) if rev else (lambda e: e)

    def vregs(x, scale=None):
        out = [x[SUBLANES * ea(e):SUBLANES * ea(e) + SUBLANES, :] for e in range(SUB)]
        if scale is not None:
            out = [o * scale for o in out]
        return [out[nv * gi:nv * (gi + 1)] for gi in range(ng)]

    def assemble(vs):
        flat = [vs[gi][a] for gi in range(ng) for a in range(nv)]
        return jnp.concatenate([flat[ea(a)] for a in range(SUB)], axis=0)

    def zeros():
        return [[zero] * nv for _ in range(ng)]

    qv, kv, gv = vregs(q), vregs(k), vregs(g, LOG2E)
    zero = jnp.zeros((SUBLANES, dk), F32)
    ninf = jnp.full((SUBLANES, dk), -jnp.inf, F32)

    sub_i = lax.broadcasted_iota(jnp.int32, (SUBLANES, dk), 0)
    re = (SUBLANES - 1 - sub_i) if rev else sub_i

    def row(x, e):
        kk = ksub(e)
        return jnp.broadcast_to(x[kk:kk + 1, :], (SUBLANES, dk))

    pf, tot, ct, xcl, gtot = [], [], [], [], []
    for gi in range(ng):
        p = [gv[gi][0]]
        for a in range(1, nv):
            p.append(p[-1] + gv[gi][a])
        t = p[nv - 1]
        c = zero
        for e in range(SUBLANES):
            c = c + jnp.where(re >= e, row(t, e), 0.0)
        pf.append(p)
        tot.append(t)
        ct.append(c)
        xcl.append(c - t)
        gtot.append(row(c, SUBLANES - 1))

    q_lv, k_lv = [], []
    for lvl in range(1, N_FINE + 1):
        s = 1 << lvl
        qs, ks = zeros(), zeros()
        for gi in range(ng):
            for bs in range(0, nv, s):
                m = bs + s // 2
                ref = pf[gi][m - 1]
                for a in range(bs, m - 1):
                    ks[gi][a] = kv[gi][a] * jnp.exp2(ref - pf[gi][a])
                ks[gi][m - 1] = kv[gi][m - 1]
                for a in range(m, bs + s):
                    qs[gi][a] = qv[gi][a] * jnp.exp2(pf[gi][a] - ref)
        q_lv.append(assemble(qs))
        k_lv.append(assemble(ks))
    for lvl in range(N_FINE + 1, N_FINE + N_COARSE + 1):
        w = 1 << (lvl - N_FINE)
        right = (re & (w // 2)) != 0
        qs, ks = zeros(), zeros()
        for gi in range(ng):
            xm = zero
            for bs in range(0, SUBLANES, w):
                inblk = (re >= bs) & (re < bs + w)
                xm = xm + jnp.where(inblk, row(xcl[gi], bs + w // 2), 0.0)
            drq = jnp.where(right, xcl[gi] - xm, ninf)
            dlk = jnp.where(right, ninf, tot[gi] + xm - ct[gi])
            for a in range(nv):
                qs[gi][a] = qv[gi][a] * jnp.exp2(pf[gi][a] + drq)
                ks[gi][a] = kv[gi][a] * jnp.exp2(dlk - pf[gi][a])
        q_lv.append(assemble(qs))
        k_lv.append(assemble(ks))
    qs, ks = zeros(), zeros()
    for a in range(nv):
        ks[0][a] = kv[0][a] * jnp.exp2(tot[0] + (gtot[0] - ct[0]) - pf[0][a])
        qs[1][a] = qv[1][a] * jnp.exp2(pf[1][a] + xcl[1])
    q_lv.append(assemble(qs))
    k_lv.append(assemble(ks))

    sc = jnp.where(masks[0], jnp.sum(q * k, axis=-1, keepdims=True), 0.0)
    for lvl in range(1, N_LEVELS + 1):
        s_l = _dot_nt(q_lv[lvl - 1].astype(BF16), k_lv[lvl - 1].astype(BF16))
        sc = jnp.where(masks[lvl], s_l, sc)

    e0, e1 = jnp.exp2(gtot[0]), jnp.exp2(gtot[1])
    qc = zeros()
    kd = zeros()
    for a in range(nv):
        qc[0][a] = qv[0][a] * jnp.exp2(pf[0][a] + xcl[0])
        qc[1][a] = qs[1][a] * e0
        kd[0][a] = ks[0][a] * e1
        kd[1][a] = kv[1][a] * jnp.exp2(tot[1] + (gtot[1] - ct[1]) - pf[1][a])
    o = _dot(sc.astype(BF16), vb) + _dot_nt(assemble(qc).astype(BF16), st_t.astype(BF16))
    st_new = (e0 * e1)[0:1, :] * st_t + _dot_tn(vb, assemble(kd).astype(BF16))
    return o, st_new


def _scan_kernel(n_heads, dk, dv, qf_ref, kf_ref, vf_ref, gf_ref, qb_ref, kb_ref, vb_ref, gb_ref,
                 lv_ref, s0_ref, of_ref, ob_ref, sf_ref, st_ref):
    @pl.when(pl.program_id(1) == 0)
    def _():
        st_ref[...] = s0_ref[...]

    dirs = ((qf_ref, kf_ref, vf_ref, gf_ref, of_ref, False), (qb_ref, kb_ref, vb_ref, gb_ref, ob_ref, True))
    masks = [[lv_ref[d] == lvl for lvl in range(N_LEVELS + 1)] for d in range(2)]
    for h in range(n_heads):
        ks = slice(h * dk, (h + 1) * dk)
        vs = slice(h * dv, (h + 1) * dv)
        for d, (q_ref, k_ref, v_ref, g_ref, o_ref, rev) in enumerate(dirs):
            o, st_new = _scan_chunk_head(q_ref[:, ks], k_ref[:, ks], v_ref[:, vs], g_ref[:, ks],
                                         st_ref[d, vs, :], masks[d], rev)
            o_ref[:, vs] = o.astype(o_ref.dtype)
            st_ref[d, vs, :] = st_new
    sf_ref[...] = st_ref[...]


def _gated_scan(q, kf, kb, v, gf, gb, s0, n_heads, bsz):
    seq = q.shape[0]
    wk = q.shape[1] // bsz
    wv = v.shape[1] // bsz
    dk, dv = wk // n_heads, wv // n_heads
    nchunk = seq // CHUNK
    fwd = lambda b, n: (n, b)
    bwd = lambda b, n: (nchunk - 1 - n, b)
    kspec = lambda im: pl.BlockSpec((CHUNK, wk), im)
    vspec = lambda im: pl.BlockSpec((CHUNK, wv), im)
    sspec = pl.BlockSpec((2, None, wv, dk), lambda b, n: (0, b, 0, 0))
    lv = jnp.asarray(np.stack([_level_matrix(False), _level_matrix(True)]))
    return pl.pallas_call(
        functools.partial(_scan_kernel, n_heads, dk, dv),
        grid=(bsz, nchunk),
        in_specs=[kspec(fwd), kspec(fwd), vspec(fwd), kspec(fwd),
                  kspec(bwd), kspec(bwd), vspec(bwd), kspec(bwd),
                  pl.BlockSpec((2, CHUNK, CHUNK), lambda b, n: (0, 0, 0)), sspec],
        out_specs=[vspec(fwd), vspec(bwd), sspec],
        out_shape=[jax.ShapeDtypeStruct((seq, bsz * wv), F32),
                   jax.ShapeDtypeStruct((seq, bsz * wv), F32),
                   jax.ShapeDtypeStruct((2, bsz, wv, dk), F32)],
        scratch_shapes=[pltpu.VMEM((2, wv, dk), F32)],
        compiler_params=_cparams("parallel", "arbitrary"),
        name="gated_scan",
    )(q, kf, v, gf, q, kb, v, gb, lv, s0)


def _bidir_scan(ctx_in, lat_in, n_heads, bsz, need_ctx):
    qc, kfc, kbc, vc, gfc, gbc = ctx_in
    qx, kfx, kbx, vx, gfx, gbx = lat_in
    wk = qc.shape[1] // bsz
    wv = vc.shape[1] // bsz
    s0 = jnp.zeros((2, bsz, wv, wk // n_heads), F32)
    oc_f, oc_b, s_c = _gated_scan(qc, kfc, kbc, vc, gfc, gbc, s0, n_heads, bsz)
    ox_f, ox_b, _ = _gated_scan(qx, kfx, kbx, vx, gfx, gbx, s_c, n_heads, bsz)
    return ((oc_f, oc_b) if need_ctx else None), (ox_f, ox_b)


def _chunk_rows(c, colmajor):
    if colmajor:
        return lambda v: (SUBLANES * (v % SUBLANES) + N_GROUPS * c + v // SUBLANES, GROUP + 1)
    return lambda v: ((c * CHUNK + GROUP * (v // SUBLANES)) // SUBLANES * (SUBLANES + 1) + v % SUBLANES,
                      SUBLANES + 1)


def _pad_period(colmajor):
    return GROUP if colmajor else SUBLANES


def _permute_chunk_in(stage_ref, dst_ref, c, colmajor):
    rows = _chunk_rows(c, colmajor)
    for jl in range(stage_ref.shape[0]):
        for v in range(SUB):
            start, stride = rows(v)
            dst_ref[c * CHUNK + SUBLANES * v:c * CHUNK + SUBLANES * (v + 1), jl * LANES:(jl + 1) * LANES] = (
                stage_ref[jl, pl.ds(start, SUBLANES, stride=stride), :])


def _permute_chunk_out(val, stage_ref, c, colmajor):
    rows = _chunk_rows(c, colmajor)
    for jl in range(stage_ref.shape[0]):
        for v in range(SUB):
            start, stride = rows(v)
            stage_ref[jl, pl.ds(start, SUBLANES, stride=stride), :] = (
                val[SUBLANES * v:SUBLANES * (v + 1), jl * LANES:(jl + 1) * LANES])


def _stage_block(x_ref, stage_ref, colmajor):
    p = _pad_period(colmajor)
    n = stage_ref.shape[1] * p // (p + 1)
    for jl in range(stage_ref.shape[0]):
        blk = x_ref[..., jl * LANES:(jl + 1) * LANES].reshape(n, LANES)
        for k in range(n // p):
            stage_ref[jl, k * (p + 1):k * (p + 1) + p, :] = blk[k * p:(k + 1) * p]


def _unstage_block(stage_ref, o_ref, colmajor):
    p = _pad_period(colmajor)
    n = stage_ref.shape[1] * p // (p + 1)
    for jl in range(stage_ref.shape[0]):
        blk = jnp.concatenate([stage_ref[jl, k * (p + 1):k * (p + 1) + p, :] for k in range(n // p)], axis=0)
        o_ref[..., jl * LANES:(jl + 1) * LANES] = blk.reshape(o_ref.shape[:-1] + (LANES,))


def _hgrn_epilogue(h, w_ref, lb, d):
    q = _silu(_dot(h, w_ref[:, :d]))
    v = _dot(h, w_ref[:, d:2 * d])
    zf = _dot(h, w_ref[:, 2 * d:3 * d])
    zb = _dot(h, w_ref[:, 3 * d:4 * d])
    og = _dot(h, w_ref[:, 4 * d:])
    lf, lbk = lb[0:1, :], lb[1:2, :]
    sf, snf = _sigmoid_pair(zf)
    sb, snb = _sigmoid_pair(zb)
    gf = jnp.log(lf + (1.0 - lf) * sf)
    gb = jnp.log(lbk + (1.0 - lbk) * sb)
    kf = (1.0 - lf) * snf
    kb = (1.0 - lbk) * snb
    return q, kf, kb, v, gf, gb, og


def _x_stream(x2, bsz, colmajor, tm=2 * CHUNK):
    seq, d = x2.shape[0], x2.shape[1] // bsz
    if colmajor:
        rows = seq // GRID_W
        assert rows == GROUP, "column-major chunks assume one grid column per 64-step group"
        return (x2.reshape(rows, GRID_W, bsz * d),
                pl.BlockSpec((rows, SUBLANES, d), lambda t, b: (0, t, b)), SUBLANES * rows)
    return x2, pl.BlockSpec((tm, d), lambda t, b: (t, b)), tm


def _load_chunks(x_ref, stage_ref, xs_ref, colmajor):
    _stage_block(x_ref, stage_ref, colmajor)
    for c in range(xs_ref.shape[0] // CHUNK):
        _permute_chunk_in(stage_ref, xs_ref, c, colmajor)


def _store_chunks(xn, stage_ref, o_ref, colmajor):
    for c in range(xn.shape[0] // CHUNK):
        _permute_chunk_out(xn[c * CHUNK:(c + 1) * CHUNK, :], stage_ref, c, colmajor)
    _unstage_block(stage_ref, o_ref, colmajor)


def _chunk_scratch(tm, d, colmajor=False):
    p = _pad_period(colmajor)
    return [pltpu.VMEM((d // LANES, tm + tm // p, LANES), F32), pltpu.VMEM((tm, d), F32)]


def _hgrn_proj_kernel(x_ref, mod_ref, nw_ref, w_ref, lb_ref,
                      q_ref, kf_ref, kb_ref, v_ref, gf_ref, gb_ref, og_ref, stage_ref, xs_ref):
    d = xs_ref.shape[1]
    _load_chunks(x_ref, stage_ref, xs_ref, False)
    h = _norm_mod(xs_ref[...], nw_ref[...], mod_ref[0:1, :], mod_ref[1:2, :]).astype(BF16)
    outs = _hgrn_epilogue(h, w_ref, lb_ref[...], d)
    for ref, val in zip((q_ref, kf_ref, kb_ref, v_ref, gf_ref, gb_ref, og_ref), outs):
        ref[...] = val.astype(ref.dtype)


def _hgrn_project(x2, mod_a, norm_w4, layer, w_in, lb, bsz):
    seq = x2.shape[0]
    d = x2.shape[1] // bsz
    xv, xspec, tm = _x_stream(x2, bsz, False)
    row = pl.BlockSpec((tm, d), lambda t, b: (t, b))
    return pl.pallas_call(
        _hgrn_proj_kernel,
        grid=(seq // tm, bsz),
        in_specs=[
            xspec,
            pl.BlockSpec((None, None, None, 3, d), lambda t, b: (layer, b, 1, 0, 0)),
            pl.BlockSpec((None, None, 1, d), lambda t, b: (layer, 1, 0, 0)),
            pl.BlockSpec(w_in.shape, lambda t, b: (0, 0), pipeline_mode=pl.Buffered(1)),
            pl.BlockSpec(lb.shape, lambda t, b: (0, 0)),
        ],
        out_specs=[row] * 7,
        out_shape=[jax.ShapeDtypeStruct((seq, bsz * d), BF16 if i == 3 else F32) for i in range(7)],
        scratch_shapes=_chunk_scratch(tm, d),
        compiler_params=_cparams("parallel", "parallel"),
        name="hgrn_project",
    )(xv, mod_a, norm_w4, w_in, lb)


def _gla_epilogue(h, w_ref, wgate, bgate, dkt, dvt):
    low = _dot(h, w_ref[:, 2 * dkt + 2 * dvt:])
    lg = _log_sigmoid(_dot(low.astype(BF16), wgate) + bgate) * (1.0 / GLA_TAU)
    q = _dot(h, w_ref[:, :dkt]) * ((dkt // GLA_HEADS) ** -0.5)
    k = _dot(h, w_ref[:, dkt:2 * dkt])
    v = _dot(h, w_ref[:, 2 * dkt:2 * dkt + dvt])
    og = _dot(h, w_ref[:, 2 * dkt + dvt:2 * dkt + 2 * dvt])
    return q, k, v, og, lg[:, :dkt], lg[:, dkt:]


def _gla_proj_kernel(colmajor, x_ref, mod_ref, nw_ref, w_ref, wg_ref, bg_ref,
                     q_ref, k_ref, v_ref, og_ref, gf_ref, gb_ref, stage_ref, xs_ref):
    dkt = q_ref.shape[1]
    dvt = v_ref.shape[1]
    _load_chunks(x_ref, stage_ref, xs_ref, colmajor)
    h = _norm_mod(xs_ref[...], nw_ref[...], mod_ref[0:1, :], mod_ref[1:2, :]).astype(BF16)
    outs = _gla_epilogue(h, w_ref, wg_ref[...], bg_ref[...], dkt, dvt)
    for ref, val in zip((q_ref, k_ref, v_ref, og_ref, gf_ref, gb_ref), outs):
        ref[...] = val.astype(ref.dtype)


def _gla_project(x2, mod_a, norm_w4, layer, w_in, wgate, bgate, bsz, dkt, dvt, *, colmajor):
    seq = x2.shape[0]
    d = x2.shape[1] // bsz
    widths = (dkt, dkt, dvt, dvt, dkt, dkt)
    xv, xspec, tm = _x_stream(x2, bsz, colmajor)
    const = lambda t, b: (0, 0)
    return pl.pallas_call(
        functools.partial(_gla_proj_kernel, colmajor),
        grid=(seq // tm, bsz),
        in_specs=[
            xspec,
            pl.BlockSpec((None, None, None, 3, d), lambda t, b: (layer, b, 1, 0, 0)),
            pl.BlockSpec((None, None, 1, d), lambda t, b: (layer, 1, 0, 0)),
            pl.BlockSpec(w_in.shape, const, pipeline_mode=pl.Buffered(1)),
            pl.BlockSpec(wgate.shape, const),
            pl.BlockSpec(bgate.shape, const),
        ],
        out_specs=[pl.BlockSpec((tm, w), lambda t, b: (t, b)) for w in widths],
        out_shape=[jax.ShapeDtypeStruct((seq, bsz * w), BF16 if i == 2 else F32)
                   for i, w in enumerate(widths)],
        scratch_shapes=_chunk_scratch(tm, d, colmajor),
        compiler_params=_cparams("parallel", "parallel"),
        name="gla_project",
    )(xv, mod_a, norm_w4, w_in, wgate, bgate)


def _head_out(of, ob, og, gnw, w_out, n_heads, zs_ref):
    dv = of.shape[1] // n_heads
    o = of + ob
    for h in range(n_heads):
        sl = slice(h * dv, (h + 1) * dv)
        zs_ref[:, sl] = (_rms(o[:, sl], gnw) * _silu(og[:, sl])).astype(BF16)
    return _dot(zs_ref[...], w_out)


def _mix_out_kernel(n_heads, colmajor, ff_chunks, of_ref, ob_ref, og_ref, x_ref, mod_ref, gnw_ref, w_ref,
                    *rest):
    if ff_chunks:
        modf_ref, nw_ref, wi_ref, wo_ref, o_ref, stage_ref, xs_ref, zs_ref = rest
    else:
        o_ref, stage_ref, xs_ref, zs_ref = rest
    _load_chunks(x_ref, stage_ref, xs_ref, colmajor)
    y = _head_out(of_ref[...], ob_ref[...], og_ref[...], gnw_ref[...], w_ref[...], n_heads, zs_ref)
    xn = xs_ref[...] + mod_ref[2:3, :] * y
    if ff_chunks:
        ff = wo_ref.shape[0]
        h = _norm_mod(xn, nw_ref[...], modf_ref[0:1, :], modf_ref[1:2, :]).astype(BF16)
        acc = None
        for lo, hi in ff_chunks:
            g = _dot(h, wi_ref[:, lo:hi])
            u = _dot(h, wi_ref[:, ff + lo:ff + hi])
            part = _dot((_silu(g) * u).astype(BF16), wo_ref[lo:hi, :])
            acc = part if acc is None else acc + part
        xn = xn + (0.5 * modf_ref[2:3, :]) * acc
    _store_chunks(xn, stage_ref, o_ref, colmajor)


def _mix_out(of, ob, og, x2, mod_a, layer, gnw, w_out, n_heads, bsz, *, colmajor, ffn=None):
    seq = x2.shape[0]
    d = x2.shape[1] // bsz
    wv = of.shape[1] // bsz
    gnw = gnw.reshape(1, -1)
    xv, xspec, tm = _x_stream(x2, bsz, colmajor, 4 * CHUNK if ffn else 2 * CHUNK)
    ospec = pl.BlockSpec((tm, wv), lambda t, b: (t, b))
    const = lambda t, b: (0, 0)
    in_specs = [ospec, ospec, ospec, xspec,
                pl.BlockSpec((None, None, None, 3, d), lambda t, b: (layer, b, 1, 0, 0)),
                pl.BlockSpec(gnw.shape, const),
                pl.BlockSpec(w_out.shape, const, pipeline_mode=pl.Buffered(1))]
    args = [of, ob, og, xv, mod_a, gnw, w_out]
    ff_chunks = None
    if ffn:
        norm_w4, w_in_f, w_out_f = ffn
        ff = w_out_f.shape[2]
        ff_chunks = _ff_chunks(ff, FFN_CHUNK)
        in_specs += [
            pl.BlockSpec((None, None, None, 3, d), lambda t, b: (layer, b, 2, 0, 0)),
            pl.BlockSpec((None, None, 1, d), lambda t, b: (layer, 2, 0, 0)),
            pl.BlockSpec((None, None, d, 2 * ff), lambda t, b: (layer, 1, 0, 0), pipeline_mode=pl.Buffered(1)),
            pl.BlockSpec((None, None, ff, d), lambda t, b: (layer, 1, 0, 0), pipeline_mode=pl.Buffered(1)),
        ]
        args += [mod_a, norm_w4, w_in_f, w_out_f]
    out = pl.pallas_call(
        functools.partial(_mix_out_kernel, n_heads, colmajor, ff_chunks),
        grid=(seq // tm, bsz),
        in_specs=in_specs,
        out_specs=xspec,
        out_shape=jax.ShapeDtypeStruct(xv.shape, F32),
        scratch_shapes=_chunk_scratch(tm, d, colmajor) + [pltpu.VMEM((tm, wv), BF16)],
        compiler_params=_cparams("parallel", "parallel"),
        name="mix_out_ffn" if ffn else "mix_out",
    )(*args)
    return out.reshape(seq, bsz * d)


def _gla_mixer(xc2, x2, mods, norm_w4, layer, prm, bsz, need_ctx, ffn=None):
    (modc_a, _), (modx_a, _) = mods
    w_in, wgate, bgate, gnw, w_out, dkt, dvt = prm
    qc, kc, vc, ogc, gfc, gbc = _gla_project(xc2, modc_a, norm_w4, layer, w_in, wgate, bgate, bsz,
                                             dkt, dvt, colmajor=False)
    qx, kx, vx, ogx, gfx, gbx = _gla_project(x2, modx_a, norm_w4, layer, w_in, wgate, bgate, bsz,
                                             dkt, dvt, colmajor=True)
    oc, ox = _bidir_scan((qc, kc, kc, vc, gfc, gbc), (qx, kx, kx, vx, gfx, gbx), GLA_HEADS, bsz, need_ctx)
    x_new = _mix_out(ox[0], ox[1], ogx, x2, modx_a, layer, gnw, w_out, GLA_HEADS, bsz, colmajor=True, ffn=ffn)
    xc_new = None
    if need_ctx:
        xc_new = _mix_out(oc[0], oc[1], ogc, xc2, modc_a, layer, gnw, w_out, GLA_HEADS, bsz, colmajor=False)
    return xc_new, x_new


def _hgrn_mixer(xc2, x2, mods, norm_w4, layer, prm, bsz, need_ctx, ffn=None):
    (modc_a, _), (modx_a, _) = mods
    w_in, lb, gnw, w_out, n_heads = prm
    qc, kfc, kbc, vc, gfc, gbc, ogc = _hgrn_project(xc2, modc_a, norm_w4, layer, w_in, lb, bsz)
    qx, kfx, kbx, vx, gfx, gbx, ogx = _hgrn_project(x2, modx_a, norm_w4, layer, w_in, lb, bsz)
    oc, ox = _bidir_scan((qc, kfc, kbc, vc, gfc, gbc), (qx, kfx, kbx, vx, gfx, gbx), n_heads, bsz, need_ctx)
    x_new = _mix_out(ox[0], ox[1], ogx, x2, modx_a, layer, gnw, w_out, n_heads, bsz, colmajor=False, ffn=ffn)
    xc_new = None
    if need_ctx:
        xc_new = _mix_out(oc[0], oc[1], ogc, xc2, modc_a, layer, gnw, w_out, n_heads, bsz, colmajor=False)
    return xc_new, x_new


def _forward(x, c, ctx, c_ctx, ada_w, ada_b, norm_w, ffn_w_in, ffn_w_out,
             s5_a_re, s5_a_im, s5_log_step, s5_b_re, s5_b_im, s5_c_re, s5_c_im, s5_d,
             s5_w_glu, s5_b_glu, gla_w_in, gla_w_gate, gla_b_gate, gla_norm_w, gla_w_out,
             hgrn_w_in, hgrn_lb_logits, hgrn_norm_w, hgrn_w_out, final_norm_w, depth=None):
    depth = ada_w.shape[0] if depth is None else depth
    bsz, seq, d = x.shape
    mods_x, mods_c = _ada(c, c_ctx, ada_w, ada_b)
    norm_w4 = norm_w.reshape(norm_w.shape[0], 3, 1, d)
    w_in = ffn_w_in.astype(BF16)
    w_out = ffn_w_out.astype(BF16)
    lb_soft = jax.nn.softmax(hgrn_lb_logits.astype(F32), axis=0)
    lb_all = jnp.cumsum(lb_soft, axis=0) - lb_soft[0]

    xc2 = jnp.transpose(ctx, (1, 0, 2)).reshape(ctx.shape[1], bsz * d)
    x2 = x
    for i in range(depth):
        last = i == depth - 1
        kind, j = i % N_MIXERS, i // N_MIXERS
        x2 = _ffn(x2, mods_x[0], norm_w4, w_in, w_out, i, 0, bsz, in_bld=(i == 0))
        xc2 = _ffn(xc2, mods_c[0], norm_w4, w_in, w_out, i, 0, bsz)
        mods = (mods_c, mods_x)
        fused = None if last else (norm_w4, w_in, w_out)
        if kind == 0:
            prm = _s5c_prep(s5_a_re[j], s5_a_im[j], s5_log_step[j], s5_b_re[j], s5_b_im[j],
                            s5_c_re[j], s5_c_im[j])
            extra = (s5_d[j], s5_w_glu[j].astype(BF16), s5_b_glu[j])
            yc, x2 = _s5c_mixer(xc2, x2, mods, norm_w4, i, prm, extra, bsz, not last)
        elif kind == 1:
            dkt = d // 2
            dvt = d
            n_low = 2 * GLA_RANK
            pad = LANES - n_low
            wi = jnp.pad(gla_w_in[j], ((0, 0), (0, pad))).astype(BF16)
            wg = jnp.zeros((LANES, 2 * dkt), F32)
            wg = wg.at[:GLA_RANK, :dkt].set(gla_w_gate[j, 0]).at[GLA_RANK:n_low, dkt:].set(gla_w_gate[j, 1])
            prm = (wi, wg.astype(BF16), gla_b_gate[j].reshape(1, 2 * dkt), gla_norm_w[j],
                   gla_w_out[j].astype(BF16), dkt, dvt)
            yc, x2 = _gla_mixer(xc2, x2, mods, norm_w4, i, prm, bsz, not last, ffn=fused)
        else:
            prm = (hgrn_w_in[j].astype(BF16), lb_all[i], hgrn_norm_w[j], hgrn_w_out[j].astype(BF16),
                   d // HGRN_DK)
            yc, x2 = _hgrn_mixer(xc2, x2, mods, norm_w4, i, prm, bsz, not last, ffn=fused)
        if not (fused and kind != 0):
            x2 = _ffn(x2, mods_x[0], norm_w4, w_in, w_out, i, 2, bsz,
                      final_w=final_norm_w.reshape(1, d) if last else None)
        if not last:
            xc2 = _ffn(yc, mods_c[0], norm_w4, w_in, w_out, i, 2, bsz)
    return x2


def kernel(x, c, ctx, c_ctx, ada_w, ada_b, norm_w, ffn_w_in, ffn_w_out, s5_a_re, s5_a_im, s5_log_step,
           s5_b_re, s5_b_im, s5_c_re, s5_c_im, s5_d, s5_w_glu, s5_b_glu, gla_w_in, gla_w_gate,
           gla_b_gate, gla_norm_w, gla_w_out, hgrn_w_in, hgrn_lb_logits, hgrn_norm_w, hgrn_w_out,
           final_norm_w):
    return _forward(x, c, ctx, c_ctx, ada_w, ada_b, norm_w, ffn_w_in, ffn_w_out, s5_a_re, s5_a_im,
                    s5_log_step, s5_b_re, s5_b_im, s5_c_re, s5_c_im, s5_d, s5_w_glu, s5_b_glu,
                    gla_w_in, gla_w_gate, gla_b_gate, gla_norm_w, gla_w_out, hgrn_w_in,
                    hgrn_lb_logits, hgrn_norm_w, hgrn_w_out, final_norm_w)
```
